```python
import math
import jax
import jax.numpy as jnp
from jax import lax
import numpy as np

D_MODEL = 2048
BATCH = 16
SEQ = 256
DEPTH = 4
DEC_BATCH = 2
DEC_SEQ = 1024
PAST_LEN = 512

GRID_W = 64
ROPE_THETA = 10000.0
Q_BLOCK = 128
N_EVEN = (DEPTH + 1) // 2
N_ODD = DEPTH // 2
DEEPNORM_ALPHA = (2.0 * DEPTH) ** 0.25
DEEPNORM_BETA = (8.0 * DEPTH) ** -0.25
LN_EPS = 1e-5
RMS_EPS = 1e-6

MLA_HEADS = 8
MLA_Q_RANK = 512
MLA_KV_RANK = 512
MLA_NOPE = 128
MLA_ROPE = 64
MLA_V = 128
DIFF_HEADS = 8
DIFF_QK = 64
DIFF_V = 2 * DIFF_QK
GQA_HEADS = 8
GQA_KV_HEADS = 2
GQA_HEAD_DIM = 128
WINDOW = 128
BAND_BLOCK = 128
SSD_HEADS = 16
SSD_HEAD_DIM = 64
SSD_D_INNER = SSD_HEADS * SSD_HEAD_DIM
SSD_GROUPS = 2
SSD_STATE = 128
SSD_CONV = 3
SSD_CHUNK = 128
SSD_CONV_DIM = SSD_D_INNER + 2 * SSD_GROUPS * SSD_STATE
MOE_GROUPS = 4
MOE_EXPERTS_PER_GROUP = 4
MOE_EXPERTS = MOE_GROUPS * MOE_EXPERTS_PER_GROUP
MOE_TOP_K = 2
MOE_HIDDEN = 512

EV_SPLITS = (MLA_Q_RANK, MLA_KV_RANK, MLA_ROPE, DIFF_HEADS * 2 * DIFF_QK, DIFF_HEADS * 2 * DIFF_QK, DIFF_HEADS * DIFF_V)
EV_IN = sum(EV_SPLITS)
EV_MIX = MLA_HEADS * MLA_V + DIFF_HEADS * DIFF_V
OD_SPLITS = (GQA_HEADS * GQA_HEAD_DIM, GQA_KV_HEADS * GQA_HEAD_DIM, GQA_KV_HEADS * GQA_HEAD_DIM, SSD_D_INNER, SSD_CONV_DIM, 2 * SSD_HEADS)
OD_IN = sum(OD_SPLITS)
OD_MIX = GQA_HEADS * GQA_HEAD_DIM + SSD_D_INNER

kernel_name = 'hybrid_diffusion_prefix_step'


def split_cols(x, sizes):
    return jnp.split(x, np.cumsum(sizes)[:-1].tolist(), axis=-1)


def rms_norm(x, g):
    xf = x.astype(jnp.float32)
    y = xf * lax.rsqrt(jnp.mean(xf * xf, axis=-1, keepdims=True) + RMS_EPS)
    return (y * g.astype(jnp.float32)).astype(x.dtype)


def layer_norm(x, g, b):
    xf = x.astype(jnp.float32)
    mu = jnp.mean(xf, axis=-1, keepdims=True)
    xc = xf - mu
    y = xc * lax.rsqrt(jnp.mean(xc * xc, axis=-1, keepdims=True) + LN_EPS)
    return (y * g.astype(jnp.float32) + b.astype(jnp.float32)).astype(x.dtype)


def adaln(cvec, w, b):
    m = jax.nn.silu(cvec) @ w + b
    return jnp.split(m[..., None, :], 6, axis=-1)


def modulate(x, shift, scale):
    return x * (1.0 + scale) + shift


def axial_rope(x):
    S, R = x.shape[1], x.shape[-1]
    rows = S // GRID_W
    row = jnp.repeat(jnp.arange(rows), GRID_W).astype(jnp.float32)
    col = (jnp.arange(S) % GRID_W).astype(jnp.float32)
    half = R // 2
    quarter = half // 2
    inv = ROPE_THETA ** (-jnp.arange(quarter, dtype=jnp.float32) * 2.0 / half)
    shape = (1, S) + (1,) * (x.ndim - 3) + (quarter,)

    def rot(xa, pos):
        ang = (pos[:, None] * inv[None, :]).reshape(shape)
        cos, sin = jnp.cos(ang), jnp.sin(ang)
        x1 = xa[..., :quarter].astype(jnp.float32)
        x2 = xa[..., quarter:].astype(jnp.float32)
        return jnp.concatenate([x1 * cos - x2 * sin, x2 * cos + x1 * sin], axis=-1)

    out = jnp.concatenate([rot(x[..., :half], row), rot(x[..., half:], col)], axis=-1)
    return out.astype(x.dtype)


def dense_attention(q, k, v, scale, sink=None):
    Bsz, Sq, H, dk = q.shape
    KVH, dv = k.shape[2], v.shape[-1]
    G = H // KVH
    nb = Sq // Q_BLOCK
    qb = q.reshape(Bsz, nb, Q_BLOCK, KVH, G, dk).transpose(1, 0, 2, 3, 4, 5)

    def one(qblk):
        s = jnp.einsum('bqkgd,bskd->bkgqs', qblk, k, preferred_element_type=jnp.float32) * scale
        if sink is not None:
            snk = jnp.broadcast_to(sink.astype(jnp.float32).reshape(1, KVH, G, 1, 1), s.shape[:-1] + (1,))
            s = jnp.concatenate([s, snk], axis=-1)
        p = jax.nn.softmax(s, axis=-1)
        if sink is not None:
            p = p[..., :-1]
        return jnp.einsum('bkgqs,bskd->bqkgd', p.astype(v.dtype), v)

    out = lax.map(one, qb)
    return out.transpose(1, 0, 2, 3, 4, 5).reshape(Bsz, Sq, H, dv)


def windowed_gqa(q, k, v, k_ctx, v_ctx, sink):
    Bsz, S, H, d = q.shape
    KVH = k.shape[2]
    G = H // KVH
    blk = BAND_BLOCK
    nb = S // blk
    scale = d ** -0.5
    qb = q.reshape(Bsz, nb, blk, KVH, G, d)
    idx = jnp.arange(nb)[:, None] * blk + jnp.arange(3 * blk)[None, :]
    pad = ((0, 0), (blk, blk), (0, 0), (0, 0))
    kb = jnp.pad(k, pad)[:, idx]
    vb = jnp.pad(v, pad)[:, idx]
    qpos = jnp.arange(nb)[:, None] * blk + jnp.arange(blk)[None, :]
    kpos = idx - blk
    valid = (jnp.abs(qpos[:, :, None] - kpos[:, None, :]) <= WINDOW) & (kpos[:, None, :] >= 0) & (kpos[:, None, :] < S)
    s_loc = jnp.einsum('bnqkgd,bnjkd->bnkgqj', qb, kb, preferred_element_type=jnp.float32) * scale
    s_loc = jnp.where(valid[None, :, None, None], s_loc, -jnp.inf)
    s_ctx = jnp.einsum('bnqkgd,bckd->bnkgqc', qb, k_ctx, preferred_element_type=jnp.float32) * scale
    s_snk = jnp.broadcast_to(sink.astype(jnp.float32).reshape(1, 1, KVH, G, 1, 1), s_loc.shape[:-1] + (1,))
    p = jax.nn.softmax(jnp.concatenate([s_loc, s_ctx, s_snk], axis=-1), axis=-1).astype(v.dtype)
    nloc, nctx = 3 * blk, k_ctx.shape[1]
    o = jnp.einsum('bnkgqj,bnjkd->bnqkgd', p[..., :nloc], vb) + jnp.einsum('bnkgqc,bckd->bnqkgd', p[..., nloc:nloc + nctx], v_ctx)
    return o.reshape(Bsz, S, H, d)


def centred_dwconv(x, w, b):
    kw = w.shape[0]
    y = lax.conv_general_dilated(x, w[:, None, :], window_strides=(1,), padding=[((kw - 1) // 2, kw // 2)],
                                 dimension_numbers=('NWC', 'WIO', 'NWC'), feature_group_count=x.shape[-1])
    return y + b


def ssd_scan(x, dt, a, bm, cm, h0):
    Bsz, S, H, P = x.shape
    N = bm.shape[-1]
    Q = SSD_CHUNK
    nc = S // Q
    xf = x.astype(jnp.float32).reshape(Bsz, nc, Q, H, P)
    dtc = dt.reshape(Bsz, nc, Q, H)
    bc = bm.astype(jnp.float32).reshape(Bsz, nc, Q, H, N)
    cc = cm.astype(jnp.float32).reshape(Bsz, nc, Q, H, N)
    cum = jnp.cumsum(dtc * a, axis=2)
    xdt = xf * dtc[..., None]
    tri = jnp.tril(jnp.ones((Q, Q), dtype=bool))[None, None, :, :, None]
    seg = cum[:, :, :, None, :] - cum[:, :, None, :, :]
    decay = jnp.exp(jnp.where(tri, seg, -jnp.inf))
    scores = jnp.einsum('bclhn,bcshn->bclsh', cc, bc) * decay
    y_diag = jnp.einsum('bclsh,bcshp->bclhp', scores, xdt)
    tail = jnp.exp(cum[:, :, -1:, :] - cum)
    chunk_states = jnp.einsum('bclhn,bclh,bclhp->bchpn', bc, tail, xdt)
    chunk_decay = jnp.exp(cum[:, :, -1, :])

    def step(hprev, inp):
        dec, st = inp
        return hprev * dec[:, :, None, None] + st, hprev

    h_final, h_in = lax.scan(step, h0.astype(jnp.float32),
                             (chunk_decay.transpose(1, 0, 2), chunk_states.transpose(1, 0, 2, 3, 4)))
    h_in = h_in.transpose(1, 0, 2, 3, 4)
    y_off = jnp.einsum('bclhn,bchpn,bclh->bclhp', cc, h_in, jnp.exp(cum))
    y = (y_diag + y_off).reshape(Bsz, S, H, P)
    return y.astype(x.dtype), h_final


def even_mixer(h, p, layer_idx, ctx):
    w_in, q_norm, kv_norm, wq_b, wkv_b, lam, subln, w_out = p
    Bsz, S, _ = h.shape
    q_lat, kv_lat, k_pe, dq, dk, dv = split_cols(h @ w_in, EV_SPLITS)
    q = (rms_norm(q_lat, q_norm) @ wq_b).reshape(Bsz, S, MLA_HEADS, MLA_NOPE + MLA_ROPE)
    c_kv = rms_norm(kv_lat, kv_norm)
    dq = dq.reshape(Bsz, S, DIFF_HEADS, 2, DIFF_QK)
    dk = dk.reshape(Bsz, S, DIFF_HEADS, 2, DIFF_QK)
    dv = dv.reshape(Bsz, S, DIFF_HEADS, DIFF_V)
    if ctx is None:
        cache = (c_kv, k_pe, dk.reshape(Bsz, S, DIFF_HEADS, 2 * DIFF_QK), dv)
        ckv_all, kpe_all, dk_all, dv_all = c_kv, k_pe, dk, dv
    else:
        ctx_ckv, ctx_kpe, ctx_dk, ctx_dv = ctx
        L = ctx_ckv.shape[1]
        q = jnp.concatenate([q[..., :MLA_NOPE], axial_rope(q[..., MLA_NOPE:])], axis=-1)
        k_pe = axial_rope(k_pe[:, :, None, :])[:, :, 0]
        dq, dk = axial_rope(dq), axial_rope(dk)
        ckv_all = jnp.concatenate([c_kv, ctx_ckv], axis=1)
        kpe_all = jnp.concatenate([k_pe, ctx_kpe], axis=1)
        dk_all = jnp.concatenate([dk, ctx_dk.reshape(Bsz, L, DIFF_HEADS, 2, DIFF_QK)], axis=1)
        dv_all = jnp.concatenate([dv, ctx_dv], axis=1)
        cache = None
    Sk = ckv_all.shape[1]
    kv = (ckv_all @ wkv_b).reshape(Bsz, Sk, MLA_HEADS, MLA_NOPE + MLA_V)
    k_mla = jnp.concatenate([kv[..., :MLA_NOPE], jnp.broadcast_to(kpe_all[:, :, None, :], (Bsz, Sk, MLA_HEADS, MLA_ROPE))], axis=-1)
    o_mla = dense_attention(q, k_mla, kv[..., MLA_NOPE:], (MLA_NOPE + MLA_ROPE) ** -0.5)
    lam_init = 0.8 - 0.6 * math.exp(-0.3 * layer_idx)
    lamf = lam.astype(jnp.float32)
    lam_full = jnp.exp(jnp.sum(lamf[0] * lamf[1])) - jnp.exp(jnp.sum(lamf[2] * lamf[3])) + lam_init
    a1 = dense_attention(dq[..., 0, :], dk_all[..., 0, :], dv_all, DIFF_QK ** -0.5)
    a2 = dense_attention(dq[..., 1, :], dk_all[..., 1, :], dv_all, DIFF_QK ** -0.5)
    o_diff = rms_norm(a1 - lam_full.astype(a1.dtype) * a2, subln) * (1.0 - lam_init)
    out = jnp.concatenate([o_mla.reshape(Bsz, S, -1), o_diff.reshape(Bsz, S, -1)], axis=-1) @ w_out
    return out, cache


def odd_mixer(h, p, ctx):
    w_in, sink, conv_w, conv_b, dt_bias, a_log, d_skip, norm_g, w_out = p
    Bsz, S, _ = h.shape
    q, k, v, z, xbc, dt = split_cols(h @ w_in, OD_SPLITS)
    q = q.reshape(Bsz, S, GQA_HEADS, GQA_HEAD_DIM)
    k = k.reshape(Bsz, S, GQA_KV_HEADS, GQA_HEAD_DIM)
    v = v.reshape(Bsz, S, GQA_KV_HEADS, GQA_HEAD_DIM)
    if ctx is None:
        o_att = dense_attention(q, k, v, GQA_HEAD_DIM ** -0.5, sink)
        h0_f = jnp.zeros((Bsz, SSD_HEADS, SSD_HEAD_DIM, SSD_STATE), jnp.float32)
        h0_b = h0_f
    else:
        k_ctx, v_ctx, h0_f, h0_b = ctx
        o_att = windowed_gqa(axial_rope(q), axial_rope(k), v, k_ctx, v_ctx, sink)
    xbc = jax.nn.silu(centred_dwconv(xbc, conv_w, conv_b))
    xs, bm, cm = split_cols(xbc, (SSD_D_INNER, SSD_GROUPS * SSD_STATE, SSD_GROUPS * SSD_STATE))
    xs = xs.reshape(Bsz, S, SSD_HEADS, SSD_HEAD_DIM)
    rep = SSD_HEADS // SSD_GROUPS
    bm = jnp.repeat(bm.reshape(Bsz, S, SSD_GROUPS, SSD_STATE), rep, axis=2)
    cm = jnp.repeat(cm.reshape(Bsz, S, SSD_GROUPS, SSD_STATE), rep, axis=2)
    dtp = jax.nn.softplus(dt.astype(jnp.float32).reshape(Bsz, S, 2, SSD_HEADS) + dt_bias.astype(jnp.float32))
    a = -jnp.exp(a_log.astype(jnp.float32))
    y_f, hf = ssd_scan(xs, dtp[:, :, 0], a[0], bm, cm, h0_f)
    y_b, hb = ssd_scan(jnp.flip(xs, 1), jnp.flip(dtp[:, :, 1], 1), a[1], jnp.flip(bm, 1), jnp.flip(cm, 1), h0_b)
    y = y_f + jnp.flip(y_b, 1) + xs * d_skip[:, None]
    y = y.reshape(Bsz, S, SSD_D_INNER) * jax.nn.silu(z)
    y = rms_norm(y.reshape(Bsz, S, SSD_GROUPS, -1), norm_g.reshape(SSD_GROUPS, -1)).reshape(Bsz, S, SSD_D_INNER)
    out = jnp.concatenate([o_att.reshape(Bsz, S, -1), y], axis=-1) @ w_out
    cache = (k, v, hf, hb) if ctx is None else None
    return out, cache


def hier_moe(h, w_rg, w_re, w_gate, w_up, w_down):
    Bsz, S, D = h.shape
    x = h.reshape(Bsz * S, D)
    pg = jax.nn.softmax(jnp.einsum('td,dg->tg', x, w_rg, preferred_element_type=jnp.float32), axis=-1)
    g_val, g_idx = lax.top_k(pg, 1)
    le = jnp.einsum('td,de->te', x, w_re, preferred_element_type=jnp.float32).reshape(-1, MOE_GROUPS, MOE_EXPERTS_PER_GROUP)
    le_sel = jnp.take_along_axis(le, g_idx[:, :, None], axis=1)[:, 0]
    e_val, e_idx = lax.top_k(jax.nn.softmax(le_sel, axis=-1), MOE_TOP_K)
    e_val = e_val / jnp.sum(e_val, axis=-1, keepdims=True)
    w_grp = jnp.sum(jax.nn.one_hot(e_idx, MOE_EXPERTS_PER_GROUP, dtype=jnp.float32) * e_val[..., None], axis=1)
    combine = (jax.nn.one_hot(g_idx[:, 0], MOE_GROUPS, dtype=jnp.float32)[:, :, None] * w_grp[:, None, :] * g_val[:, :, None]).reshape(-1, MOE_EXPERTS)
    hg = jnp.einsum('td,edf->tef', x, w_gate)
    hu = jnp.einsum('td,edf->tef', x, w_up)
    act = jax.nn.silu(hg) * hu * combine[..., None].astype(x.dtype)
    y = jnp.einsum('tef,efd->td', act, w_down)
    return y.reshape(Bsz, S, D)


def setup_inputs(seed: int = 0) -> dict:
    key = jax.random.key(seed)
    keys = jax.random.split(key, 64)
    counter = [0]

    def nk():
        k = keys[counter[0]]
        counter[0] += 1
        return k

    def nrm(shape, scale):
        return jax.random.normal(nk(), shape, jnp.float32) * scale

    D = D_MODEL
    L = PAST_LEN
    inp = {}
    inp['x_prompt'] = nrm((BATCH, SEQ, D), 1.0)
    inp['x_sample'] = nrm((DEC_BATCH, DEC_SEQ, D), 1.0)
    inp['cache_mla_ckv'] = nrm((DEC_BATCH, N_EVEN, L, MLA_KV_RANK), 1.0)
    inp['cache_mla_kpe'] = nrm((DEC_BATCH, N_EVEN, L, MLA_ROPE), 1.0)
    inp['cache_diff_k'] = nrm((DEC_BATCH, N_EVEN, L, DIFF_HEADS, 2 * DIFF_QK), 1.0)
    inp['cache_diff_v'] = nrm((DEC_BATCH, N_EVEN, L, DIFF_HEADS, DIFF_V), 1.0)
    inp['cache_gqa_k'] = nrm((DEC_BATCH, N_ODD, L, GQA_KV_HEADS, GQA_HEAD_DIM), 1.0)
    inp['cache_gqa_v'] = nrm((DEC_BATCH, N_ODD, L, GQA_KV_HEADS, GQA_HEAD_DIM), 1.0)
    inp['state_ssd_fwd'] = nrm((DEC_BATCH, N_ODD, SSD_HEADS, SSD_HEAD_DIM, SSD_STATE), 0.1)
    inp['state_ssd_bwd'] = nrm((DEC_BATCH, N_ODD, SSD_HEADS, SSD_HEAD_DIM, SSD_STATE), 0.1)
    inp['c'] = nrm((DEC_BATCH, D), 1.0)
    inp['c_ctx'] = nrm((D,), 1.0)
    inp['w_mod'] = nrm((DEPTH, D, 6 * D), 0.5 * D ** -0.5)
    inp['b_mod'] = nrm((DEPTH, 6 * D), 0.02)
    inp['ln1_g'] = 1.0 + nrm((DEPTH, D), 0.02)
    inp['ln1_b'] = nrm((DEPTH, D), 0.02)
    inp['ln2_g'] = 1.0 + nrm((DEPTH, D), 0.02)
    inp['ln2_b'] = nrm((DEPTH, D), 0.02)
    inp['ev_w_in'] = nrm((N_EVEN, D, EV_IN), D ** -0.5)
    inp['mla_q_norm'] = 1.0 + nrm((N_EVEN, MLA_Q_RANK), 0.02)
    inp['mla_kv_norm'] = 1.0 + nrm((N_EVEN, MLA_KV_RANK), 0.02)
    inp['mla_wq_b'] = nrm((N_EVEN, MLA_Q_RANK, MLA_HEADS * (MLA_NOPE + MLA_ROPE)), MLA_Q_RANK ** -0.5)
    inp['mla_wkv_b'] = nrm((N_EVEN, MLA_KV_RANK, MLA_HEADS * (MLA_NOPE + MLA_V)), MLA_KV_RANK ** -0.5)
    inp['diff_lambda'] = nrm((N_EVEN, 4, DIFF_QK), 0.1)
    inp['diff_subln'] = 1.0 + nrm((N_EVEN, DIFF_V), 0.02)
    inp['ev_w_out'] = nrm((N_EVEN, EV_MIX, D), DEEPNORM_BETA * EV_MIX ** -0.5)
    inp['od_w_in'] = nrm((N_ODD, D, OD_IN), D ** -0.5)
    inp['gqa_sink'] = nrm((N_ODD, GQA_HEADS), 0.5)
    inp['ssd_conv_w'] = nrm((N_ODD, SSD_CONV, SSD_CONV_DIM), SSD_CONV ** -0.5)
    inp['ssd_conv_b'] = nrm((N_ODD, SSD_CONV_DIM), 0.02)
    dt0 = jnp.exp(jax.random.uniform(nk(), (N_ODD, 2, SSD_HEADS), jnp.float32, math.log(1e-3), math.log(1e-1)))
    inp['ssd_dt_bias'] = dt0 + jnp.log(-jnp.expm1(-dt0))
    inp['ssd_a_log'] = jnp.log(jax.random.uniform(nk(), (N_ODD, 2, SSD_HEADS), jnp.float32, 1.0, 16.0))
    inp['ssd_d'] = 1.0 + nrm((N_ODD, SSD_HEADS), 0.02)
    inp['ssd_norm'] = 1.0 + nrm((N_ODD, SSD_D_INNER), 0.02)
    inp['od_w_out'] = nrm((N_ODD, OD_MIX, D), DEEPNORM_BETA * OD_MIX ** -0.5)
    inp['moe_router_group'] = nrm((DEPTH, D, MOE_GROUPS), D ** -0.5)
    inp['moe_router_expert'] = nrm((DEPTH, D, MOE_EXPERTS), D ** -0.5)
    inp['moe_w_gate'] = nrm((DEPTH, MOE_EXPERTS, D, MOE_HIDDEN), D ** -0.5)
    inp['moe_w_up'] = nrm((DEPTH, MOE_EXPERTS, D, MOE_HIDDEN), D ** -0.5)
    inp['moe_w_down'] = nrm((DEPTH, MOE_EXPERTS, MOE_HIDDEN, D), DEEPNORM_BETA * MOE_HIDDEN ** -0.5)
    return inp


def reference(x_prompt, x_sample, cache_mla_ckv, cache_mla_kpe, cache_diff_k, cache_diff_v, cache_gqa_k, cache_gqa_v,
              state_ssd_fwd, state_ssd_bwd, c, c_ctx, w_mod, b_mod, ln1_g, ln1_b, ln2_g, ln2_b,
              ev_w_in, mla_q_norm, mla_kv_norm, mla_wq_b, mla_wkv_b, diff_lambda, diff_subln, ev_w_out,
              od_w_in, gqa_sink, ssd_conv_w, ssd_conv_b, ssd_dt_bias, ssd_a_log, ssd_d, ssd_norm, od_w_out,
              moe_router_group, moe_router_expert, moe_w_gate, moe_w_up, moe_w_down):
    xp, xs = x_prompt, x_sample
    ev_caches, od_caches = [], []
    for l in range(DEPTH):
        i = l // 2
        mp = adaln(c_ctx, w_mod[l], b_mod[l])
        ms = adaln(c, w_mod[l], b_mod[l])
        hp = modulate(xp, mp[0], mp[1])
        hs = modulate(xs, ms[0], ms[1])
        if l % 2 == 0:
            p = (ev_w_in[i], mla_q_norm[i], mla_kv_norm[i], mla_wq_b[i], mla_wkv_b[i], diff_lambda[i], diff_subln[i], ev_w_out[i])
            op, cache = even_mixer(hp, p, l, None)
            os_, _ = even_mixer(hs, p, l, (cache_mla_ckv[:, i], cache_mla_kpe[:, i], cache_diff_k[:, i], cache_diff_v[:, i]))
            ev_caches.append(cache)
        else:
            p = (od_w_in[i], gqa_sink[i], ssd_conv_w[i], ssd_conv_b[i], ssd_dt_bias[i], ssd_a_log[i], ssd_d[i], ssd_norm[i], od_w_out[i])
            op, cache = odd_mixer(hp, p, None)
            os_, _ = odd_mixer(hs, p, (cache_gqa_k[:, i], cache_gqa_v[:, i], state_ssd_fwd[:, i], state_ssd_bwd[:, i]))
            od_caches.append(cache)
        xp = layer_norm(DEEPNORM_ALPHA * xp + mp[2] * op, ln1_g[l], ln1_b[l])
        xs = layer_norm(DEEPNORM_ALPHA * xs + ms[2] * os_, ln1_g[l], ln1_b[l])
        moe_p = (moe_router_group[l], moe_router_expert[l], moe_w_gate[l], moe_w_up[l], moe_w_down[l])
        xp = layer_norm(DEEPNORM_ALPHA * xp + mp[5] * hier_moe(modulate(xp, mp[3], mp[4]), *moe_p), ln2_g[l], ln2_b[l])
        xs = layer_norm(DEEPNORM_ALPHA * xs + ms[5] * hier_moe(modulate(xs, ms[3], ms[4]), *moe_p), ln2_g[l], ln2_b[l])
    new_mla_ckv = jnp.stack([t[0] for t in ev_caches], axis=1)
    new_mla_kpe = jnp.stack([t[1] for t in ev_caches], axis=1)
    new_diff_k = jnp.stack([t[2] for t in ev_caches], axis=1)
    new_diff_v = jnp.stack([t[3] for t in ev_caches], axis=1)
    new_gqa_k = jnp.stack([t[0] for t in od_caches], axis=1)
    new_gqa_v = jnp.stack([t[1] for t in od_caches], axis=1)
    new_ssd_fwd = jnp.stack([t[2] for t in od_caches], axis=1)
    new_ssd_bwd = jnp.stack([t[3] for t in od_caches], axis=1)
    return (xp, xs, new_mla_ckv, new_mla_kpe, new_diff_k, new_diff_v, new_gqa_k, new_gqa_v, new_ssd_fwd, new_ssd_bwd)
```

```python
import functools
import math

import jax
import jax.numpy as jnp
from jax import lax
from jax.experimental import pallas as pl
from jax.experimental.pallas import tpu as pltpu

F32 = jnp.float32
BF16 = jnp.bfloat16

D_MODEL = 2048
BATCH = 16
SEQ = 256
DEPTH = 4
DEC_BATCH = 2
DEC_SEQ = 1024
PAST_LEN = 512
GRID_W = 64
ROPE_THETA = 10000.0
DEEPNORM_ALPHA = (2.0 * DEPTH) ** 0.25
LN_EPS = 1e-5
RMS_EPS = 1e-6

MLA_HEADS = 8
MLA_Q_RANK = 512
MLA_KV_RANK = 512
MLA_NOPE = 128
MLA_ROPE = 64
MLA_V = 128
DIFF_HEADS = 8
DIFF_QK = 64
DIFF_V = 128
GQA_HEADS = 8
GQA_KV_HEADS = 2
GQA_HEAD_DIM = 128
WINDOW = 128
SSD_HEADS = 16
SSD_HEAD_DIM = 64
SSD_D_INNER = SSD_HEADS * SSD_HEAD_DIM
SSD_GROUPS = 2
SSD_STATE = 128
SSD_CHUNK = 128
MOE_GROUPS = 4
MOE_EPG = 4
MOE_EXPERTS = 16
MOE_HIDDEN = 512

N_PROMPT = BATCH * SEQ
N_SAMPLE = DEC_BATCH * DEC_SEQ
N_TOK = N_PROMPT + N_SAMPLE
PROJ_N = 4224
LANES = 128
VMEM_LIMIT = 56 * 1024 * 1024


def _cparams(sem):
    return pltpu.CompilerParams(dimension_semantics=sem, vmem_limit_bytes=VMEM_LIMIT)


def _mod_group(row0):
    return jnp.maximum(row0 // DEC_SEQ - (N_PROMPT // DEC_SEQ - 1), 0)


def _silu(x):
    return x / (1.0 + jnp.exp(-x))


def _adaln_body(c_ref, w_ref, b_ref, o_ref):
    a = _silu(c_ref[...])
    o_ref[0] = jnp.dot(a, w_ref[0], preferred_element_type=F32) + b_ref[0]


def _adaln(cs, w_mod, b_mod):
    tn = 1024
    n = w_mod.shape[-1]
    return pl.pallas_call(
        _adaln_body,
        grid=(DEPTH, n // tn),
        in_specs=[
            pl.BlockSpec((8, D_MODEL), lambda l, j: (0, 0)),
            pl.BlockSpec((1, D_MODEL, tn), lambda l, j: (l, 0, j)),
            pl.BlockSpec((1, 1, tn), lambda l, j: (l, 0, j)),
        ],
        out_specs=pl.BlockSpec((1, 8, tn), lambda l, j: (l, 0, j)),
        out_shape=jax.ShapeDtypeStruct((DEPTH, 8, n), F32),
        compiler_params=_cparams(("parallel", "parallel")),
        name="adaln",
    )(cs, w_mod, b_mod.reshape(DEPTH, 1, n))


def _proj_body(x_ref, sh_ref, sc_ref, w_ref, o_ref, h_scr):
    @pl.when(pl.program_id(1) == 0)
    def _():
        h = x_ref[...] * (1.0 + sc_ref[0]) + sh_ref[0]
        h_scr[...] = h.astype(BF16)

    o_ref[...] = jnp.dot(h_scr[...], w_ref[...], preferred_element_type=F32)


def _proj(x, shift, scale, w):
    tm, tn = 1024, 384
    n = w.shape[1]
    mod_spec = pl.BlockSpec((1, 1, D_MODEL), lambda i, j: (_mod_group(i * tm), 0, 0))
    return pl.pallas_call(
        _proj_body,
        grid=(N_TOK // tm, n // tn),
        in_specs=[
            pl.BlockSpec((tm, D_MODEL), lambda i, j: (i, 0)),
            mod_spec,
            mod_spec,
            pl.BlockSpec((D_MODEL, tn), lambda i, j: (0, j)),
        ],
        out_specs=pl.BlockSpec((tm, tn), lambda i, j: (i, j)),
        out_shape=jax.ShapeDtypeStruct((N_TOK, n), F32),
        scratch_shapes=[pltpu.VMEM((tm, D_MODEL), BF16)],
        compiler_params=_cparams(("parallel", "arbitrary")),
        name="in_proj",
    )(x, shift, scale, w)


def _mm_body(a_ref, b_ref, o_ref):
    o_ref[...] = jnp.dot(a_ref[...], b_ref[...], preferred_element_type=F32).astype(o_ref.dtype)


def _mm(a, b, out_dtype, tm=512):
    m, k = a.shape
    n = b.shape[1]
    return pl.pallas_call(
        _mm_body,
        grid=(m // tm,),
        in_specs=[pl.BlockSpec((tm, k), lambda i: (i, 0)), pl.BlockSpec((k, n), lambda i: (0, 0))],
        out_specs=pl.BlockSpec((tm, n), lambda i: (i, 0)),
        out_shape=jax.ShapeDtypeStruct((m, n), out_dtype),
        compiler_params=_cparams(("parallel",)),
        name="mm_resident",
    )(a, b)


def _rope(x, cos, sin_signed, quarter):
    rows, w = x.shape
    reps = w // LANES
    if reps > 1:
        cos = jnp.concatenate([cos] * reps, axis=1)
        sin_signed = jnp.concatenate([sin_signed] * reps, axis=1)
    lane = lax.broadcasted_iota(jnp.int32, x.shape, 1)
    first = (lane % (2 * quarter)) < quarter
    up = pltpu.roll(x, w - quarter, 1)
    dn = pltpu.roll(x, quarter, 1)
    return x * cos + jnp.where(first, up, dn) * sin_signed


def _rope_tables(rdim):
    half = rdim // 2
    quarter = half // 2
    pos = jnp.arange(DEC_SEQ)
    row = (pos // GRID_W).astype(F32)
    col = (pos % GRID_W).astype(F32)
    inv = ROPE_THETA ** (-jnp.arange(quarter, dtype=F32) * 2.0 / half)
    lane = jnp.arange(LANES)
    r = lane % rdim
    use_col = (r // half) == 1
    j = r % quarter
    p = jnp.where(use_col[None, :], col[:, None], row[:, None])
    ang = p * inv[j][None, :]
    sign = jnp.where((r % half) < quarter, -1.0, 1.0).astype(F32)
    cos = jnp.cos(ang)
    sin = jnp.sin(ang) * sign[None, :]
    cos = jnp.concatenate([jnp.ones((N_PROMPT, LANES), F32), cos, cos], axis=0)
    sin = jnp.concatenate([jnp.zeros((N_PROMPT, LANES), F32), sin, sin], axis=0)
    return cos, sin


def _rms(x, g):
    return x * lax.rsqrt(jnp.mean(x * x, axis=-1, keepdims=True) + RMS_EPS) * g


def _ev_post_body(ql_ref, kvl_ref, dq_ref, dk_ref, kpe_ref, cos_ref, sin_ref, gq_ref, gkv_ref,
                  qn_ref, ckv_ref, dqr_ref, dkr_ref, kper_ref):
    cos = cos_ref[...]
    sin = sin_ref[...]
    qn_ref[...] = _rms(ql_ref[...], gq_ref[...]).astype(BF16)
    ckv_ref[...] = _rms(kvl_ref[...], gkv_ref[...])
    dqr_ref[...] = _rope(dq_ref[...], cos, sin, DIFF_QK // 4).astype(BF16)
    dkr_ref[...] = _rope(dk_ref[...], cos, sin, DIFF_QK // 4)
    kper_ref[...] = _rope(kpe_ref[...], cos, sin, MLA_ROPE // 4)


def _ev_post(proj, cos, sin, gq, gkv):
    tm = 256
    row = lambda c: (lambda i: (i, c))
    return pl.pallas_call(
        _ev_post_body,
        grid=(N_TOK // tm,),
        in_specs=[
            pl.BlockSpec((tm, 512), row(0)),
            pl.BlockSpec((tm, 512), row(1)),
            pl.BlockSpec((tm, 1024), row(1)),
            pl.BlockSpec((tm, 1024), row(2)),
            pl.BlockSpec((tm, LANES), row(32)),
            pl.BlockSpec((tm, LANES), row(0)),
            pl.BlockSpec((tm, LANES), row(0)),
            pl.BlockSpec((1, 512), lambda i: (0, 0)),
            pl.BlockSpec((1, 512), lambda i: (0, 0)),
        ],
        out_specs=[
            pl.BlockSpec((tm, 512), row(0)),
            pl.BlockSpec((tm, 512), row(0)),
            pl.BlockSpec((tm, 1024), row(0)),
            pl.BlockSpec((tm, 1024), row(0)),
            pl.BlockSpec((tm, LANES), row(0)),
        ],
        out_shape=[
            jax.ShapeDtypeStruct((N_TOK, 512), BF16),
            jax.ShapeDtypeStruct((N_TOK, 512), F32),
            jax.ShapeDtypeStruct((N_TOK, 1024), BF16),
            jax.ShapeDtypeStruct((N_TOK, 1024), F32),
            jax.ShapeDtypeStruct((N_TOK, LANES), F32),
        ],
        compiler_params=_cparams(("parallel",)),
        name="even_post",
    )(proj, proj, proj, proj, proj, cos, sin, gq, gkv)


def _softmax_av(s, v):
    m = jnp.max(s, axis=-1, keepdims=True)
    e = jnp.exp(s - m)
    l = jnp.sum(e, axis=-1, keepdims=True)
    return jnp.dot(e.astype(BF16), v, preferred_element_type=F32) / l


def _nt(a, b):
    return lax.dot_general(a, b, (((1,), (1,)), ((), ())), preferred_element_type=F32)


def _even_attn_body(lam_init, q_ref, cos_ref, sin_ref, kv_ref, kpe_ref, dq_ref, dk_ref, dv_ref,
                    lam_ref, sub_ref, omla_ref, odiff_ref):
    q = q_ref[0]
    qn = q[:, :MLA_NOPE]
    qr = _rope(q[:, MLA_NOPE:].astype(F32), cos_ref[...], sin_ref[...], MLA_ROPE // 4).astype(BF16)
    kn = kv_ref[0, :, :MLA_NOPE]
    v = kv_ref[0, :, MLA_NOPE:]
    s = (_nt(qn, kn) + _nt(qr, kpe_ref[0])) * (MLA_NOPE + MLA_ROPE) ** -0.5
    omla_ref[0] = _softmax_av(s, v).astype(BF16)

    lam = lam_ref[...]
    lam_full = (jnp.exp(jnp.sum(lam[0:1] * lam[1:2], axis=-1, keepdims=True))
                - jnp.exp(jnp.sum(lam[2:3] * lam[3:4], axis=-1, keepdims=True)) + lam_init)
    dq = dq_ref[0]
    dk = dk_ref[0]
    dv = dv_ref[0]
    lane = lax.broadcasted_iota(jnp.int32, dq.shape, 1)
    zero = jnp.zeros_like(dq)
    diff_scale = DIFF_QK ** -0.5
    a1 = _softmax_av(_nt(jnp.where(lane < DIFF_QK, dq, zero), dk) * diff_scale, dv)
    a2 = _softmax_av(_nt(jnp.where(lane >= DIFF_QK, dq, zero), dk) * diff_scale, dv)
    odiff_ref[0] = (_rms(a1 - lam_full * a2, sub_ref[...]) * (1.0 - lam_init)).astype(BF16)


def _even_attn(lam_init, q, cos, sin, kv, kpe, dq, dk, dv, lam, subln, row_off):
    b, sq, _ = q.shape
    sk = kv.shape[1]
    tq = 256
    nq = sq // tq
    off = row_off // tq
    qspec = lambda w: pl.BlockSpec((1, tq, w), lambda i, j, h: (i, j, h))
    kspec = lambda w: pl.BlockSpec((1, sk, w), lambda i, j, h: (i, 0, h))
    tspec = pl.BlockSpec((tq, LANES), lambda i, j, h: (off + i * nq + j, 0))
    return pl.pallas_call(
        functools.partial(_even_attn_body, lam_init),
        grid=(b, nq, MLA_HEADS),
        in_specs=[qspec(256), tspec, tspec, kspec(256),
                  pl.BlockSpec((1, sk, LANES), lambda i, j, h: (i, 0, 0)),
                  qspec(LANES), kspec(LANES), kspec(LANES),
                  pl.BlockSpec((4, DIFF_QK), lambda i, j, h: (0, 0)),
                  pl.BlockSpec((1, DIFF_V), lambda i, j, h: (0, 0))],
        out_specs=[qspec(LANES), qspec(LANES)],
        out_shape=[jax.ShapeDtypeStruct((b, sq, 1024), BF16), jax.ShapeDtypeStruct((b, sq, 1024), BF16)],
        compiler_params=_cparams(("parallel", "parallel", "parallel")),
        name="even_attn",
    )(q, cos, sin, kv, kpe, dq, dk, dv, lam, subln)


def _layer_norm(y, g, b):
    mu = jnp.mean(y, axis=-1, keepdims=True)
    yc = y - mu
    return yc * lax.rsqrt(jnp.mean(yc * yc, axis=-1, keepdims=True) + LN_EPS) * g + b


def _route(logits):
    lane = lax.broadcasted_iota(jnp.int32, logits.shape, 1).astype(F32)
    neg = -jnp.inf
    none = float(LANES)
    is_g = lane < MOE_GROUPS
    lg = jnp.where(is_g, logits, neg)
    mg = jnp.max(lg, axis=-1, keepdims=True)
    g_val = 1.0 / jnp.sum(jnp.exp(lg - mg), axis=-1, keepdims=True)
    g_idx = jnp.min(jnp.where(is_g & (lg == mg), lane, none), axis=-1, keepdims=True)
    e_lo = MOE_GROUPS + g_idx * MOE_EPG
    is_e = (lane >= e_lo) & (lane < e_lo + MOE_EPG)
    le = jnp.where(is_e, logits, neg)
    p = jnp.exp(le - jnp.max(le, axis=-1, keepdims=True))
    p1 = jnp.max(p, axis=-1, keepdims=True)
    i1 = jnp.min(jnp.where(is_e & (p == p1), lane, none), axis=-1, keepdims=True)
    rest = is_e & (lane != i1)
    p2 = jnp.max(jnp.where(rest, p, neg), axis=-1, keepdims=True)
    i2 = jnp.min(jnp.where(rest & (p == p2), lane, none), axis=-1, keepdims=True)
    chosen = (lane == i1) | (lane == i2)
    return jnp.where(chosen, p, 0.0) / (p1 + p2) * g_val


def _out_proj_body(a1_ref, a2_ref, w_ref, x_ref, gate_ref, g_ref, b_ref, sh_ref, sc_ref, wr_ref,
                   xo_ref, h_ref, comb_ref):
    k1 = a1_ref.shape[1]
    acc = jnp.dot(a1_ref[...], w_ref[:k1, :], preferred_element_type=F32)
    acc = acc + jnp.dot(a2_ref[...], w_ref[k1:, :], preferred_element_type=F32)
    xn = _layer_norm(DEEPNORM_ALPHA * x_ref[...] + gate_ref[0] * acc, g_ref[...], b_ref[...])
    xo_ref[...] = xn
    h = xn * (1.0 + sc_ref[0]) + sh_ref[0]
    h_ref[...] = h.astype(BF16)
    logits = jnp.dot(h, wr_ref[...], preferred_element_type=F32, precision=lax.Precision.HIGHEST)
    comb_ref[...] = _route(logits)


def _out_proj(a1, a2, w, x, gate, ln_g, ln_b, shift, scale, w_router):
    tm = 256
    k1, k2 = a1.shape[1], a2.shape[1]
    mod_spec = pl.BlockSpec((1, 1, D_MODEL), lambda i: (_mod_group(i * tm), 0, 0))
    vec_spec = pl.BlockSpec((1, D_MODEL), lambda i: (0, 0))
    row_spec = pl.BlockSpec((tm, D_MODEL), lambda i: (i, 0))
    return pl.pallas_call(
        _out_proj_body,
        grid=(N_TOK // tm,),
        in_specs=[
            pl.BlockSpec((tm, k1), lambda i: (i, 0)),
            pl.BlockSpec((tm, k2), lambda i: (i, 0)),
            pl.BlockSpec((k1 + k2, D_MODEL), lambda i: (0, 0)),
            row_spec, mod_spec, vec_spec, vec_spec, mod_spec, mod_spec,
            pl.BlockSpec((D_MODEL, LANES), lambda i: (0, 0)),
        ],
        out_specs=[row_spec, row_spec, pl.BlockSpec((tm, LANES), lambda i: (i, 0))],
        out_shape=[
            jax.ShapeDtypeStruct((N_TOK, D_MODEL), F32),
            jax.ShapeDtypeStruct((N_TOK, D_MODEL), BF16),
            jax.ShapeDtypeStruct((N_TOK, LANES), F32),
        ],
        compiler_params=_cparams(("parallel",)),
        name="out_proj_ln_router",
    )(a1, a2, w, x, gate, ln_g, ln_b, shift, scale, w_router)


def _moe_body(h_ref, comb_ref, wg_ref, wu_ref, wd_ref, y_ref):
    e = pl.program_id(1)
    h = h_ref[...]
    hg = jnp.dot(h, wg_ref[0].astype(BF16), preferred_element_type=F32)
    hu = jnp.dot(h, wu_ref[0].astype(BF16), preferred_element_type=F32)
    lane = lax.broadcasted_iota(jnp.int32, comb_ref.shape, 1)
    c = jnp.sum(jnp.where(lane == MOE_GROUPS + e, comb_ref[...], 0.0), axis=-1, keepdims=True)
    act = (_silu(hg) * hu * c).astype(BF16)
    y = jnp.dot(act, wd_ref[0].astype(BF16), preferred_element_type=F32)
    first = (e == 0) & (pl.program_id(2) == 0)

    @pl.when(first)
    def _():
        y_ref[...] = y

    @pl.when(jnp.logical_not(first))
    def _():
        y_ref[...] += y


def _moe(h, comb, wg, wu, wd):
    tm = 1024
    th = MOE_HIDDEN // 2
    return pl.pallas_call(
        _moe_body,
        grid=(N_TOK // tm, MOE_EXPERTS, MOE_HIDDEN // th),
        in_specs=[
            pl.BlockSpec((tm, D_MODEL), lambda i, e, f: (i, 0)),
            pl.BlockSpec((tm, LANES), lambda i, e, f: (i, 0)),
            pl.BlockSpec((1, D_MODEL, th), lambda i, e, f: (e, 0, f)),
            pl.BlockSpec((1, D_MODEL, th), lambda i, e, f: (e, 0, f)),
            pl.BlockSpec((1, th, D_MODEL), lambda i, e, f: (e, f, 0)),
        ],
        out_specs=pl.BlockSpec((tm, D_MODEL), lambda i, e, f: (i, 0)),
        out_shape=jax.ShapeDtypeStruct((N_TOK, D_MODEL), F32),
        compiler_params=_cparams(("parallel", "arbitrary", "arbitrary")),
        name="moe_experts",
    )(h, comb, wg, wu, wd)


def _res_ln_body(x_ref, y_ref, gate_ref, g_ref, b_ref, o_ref):
    o_ref[...] = _layer_norm(DEEPNORM_ALPHA * x_ref[...] + gate_ref[0] * y_ref[...], g_ref[...], b_ref[...])


def _res_ln(x, y, gate, ln_g, ln_b):
    tm = 512
    row_spec = pl.BlockSpec((tm, D_MODEL), lambda i: (i, 0))
    vec_spec = pl.BlockSpec((1, D_MODEL), lambda i: (0, 0))
    return pl.pallas_call(
        _res_ln_body,
        grid=(N_TOK // tm,),
        in_specs=[row_spec, row_spec,
                  pl.BlockSpec((1, 1, D_MODEL), lambda i: (_mod_group(i * tm), 0, 0)), vec_spec, vec_spec],
        out_specs=row_spec,
        out_shape=jax.ShapeDtypeStruct((N_TOK, D_MODEL), F32),
        compiler_params=_cparams(("parallel",)),
        name="residual_ln",
    )(x, y, gate, ln_g, ln_b)


def _od_post_body(q_ref, k_ref, xs_ref, bc_ref, dt_ref, cos_ref, sin_ref, cw_ref, cb_ref, dtb_ref,
                  qr_ref, kr_ref, xc_ref, bcc_ref, dtp_ref):
    cos = cos_ref[...]
    sin = sin_ref[...]
    qr_ref[...] = _rope(q_ref[...], cos, sin, GQA_HEAD_DIM // 4).astype(BF16)
    kr_ref[...] = _rope(k_ref[...], cos, sin, GQA_HEAD_DIM // 4).astype(BF16)

    def conv_silu(x, w, b):
        rows = x.shape[0]
        r = lax.broadcasted_iota(jnp.int32, x.shape, 0)
        prev = jnp.where(r == 0, 0.0, pltpu.roll(x, 1, 0))
        nxt = jnp.where(r == rows - 1, 0.0, pltpu.roll(x, rows - 1, 0))
        return _silu(prev * w[0:1] + x * w[1:2] + nxt * w[2:3] + b)

    xc_ref[...] = conv_silu(xs_ref[...], cw_ref[:, :SSD_D_INNER], cb_ref[:, :SSD_D_INNER])
    bcc_ref[...] = conv_silu(bc_ref[...], cw_ref[:, SSD_D_INNER:], cb_ref[:, SSD_D_INNER:])
    t = dt_ref[...] + dtb_ref[...]
    dtp_ref[...] = jnp.maximum(t, 0.0) + jnp.log1p(jnp.exp(-jnp.abs(t)))


def _od_post(proj, cos, sin, conv_w, conv_b, dt_bias, rows, row_off, n_rows):
    off = row_off // rows
    blk = lambda w, c: pl.BlockSpec((rows, w), lambda i: (off + i, c))
    full = lambda a: pl.BlockSpec(a.shape, lambda i: (0, 0))
    oblk = lambda w: pl.BlockSpec((rows, w), lambda i: (i, 0))
    return pl.pallas_call(
        _od_post_body,
        grid=(n_rows // rows,),
        in_specs=[blk(1024, 0), blk(256, 12), blk(1024, 1), blk(512, 7), blk(LANES, 32),
                  blk(LANES, 0), blk(LANES, 0), full(conv_w), full(conv_b), full(dt_bias)],
        out_specs=[oblk(1024), oblk(256), oblk(1024), oblk(512), oblk(LANES)],
        out_shape=[
            jax.ShapeDtypeStruct((n_rows, 1024), BF16),
            jax.ShapeDtypeStruct((n_rows, 256), BF16),
            jax.ShapeDtypeStruct((n_rows, 1024), F32),
            jax.ShapeDtypeStruct((n_rows, 512), F32),
            jax.ShapeDtypeStruct((n_rows, LANES), F32),
        ],
        compiler_params=_cparams(("parallel",)),
        name="odd_post",
    )(proj, proj, proj, proj, proj, cos, sin, conv_w, conv_b, dt_bias)


def _sink_column(sink_ref, kh, rows):
    g = GQA_HEADS // GQA_KV_HEADS
    return jnp.concatenate(
        [jnp.broadcast_to(sink_ref[kh * g + j:kh * g + j + 1, 0:1], (rows, 1)) for j in range(g)], axis=0)


def _gqa_prompt_body(q_ref, k_ref, v_ref, sink_ref, o_ref):
    g = GQA_HEADS // GQA_KV_HEADS
    d = GQA_HEAD_DIM
    rows = q_ref.shape[1]
    scale = d ** -0.5
    outs = []
    for kh in range(GQA_KV_HEADS):
        qg = jnp.concatenate([q_ref[0, :, (kh * g + j) * d:(kh * g + j + 1) * d] for j in range(g)], axis=0)
        k = k_ref[0, :, kh * d:(kh + 1) * d]
        v = v_ref[0, :, kh * d:(kh + 1) * d]
        s = _nt(qg, k) * scale
        snk = _sink_column(sink_ref, kh, rows)
        m = jnp.maximum(jnp.max(s, axis=-1, keepdims=True), snk)
        e = jnp.exp(s - m)
        l = jnp.sum(e, axis=-1, keepdims=True) + jnp.exp(snk - m)
        o = jnp.dot(e.astype(BF16), v, preferred_element_type=F32) / l
        outs.extend(o[j * rows:(j + 1) * rows] for j in range(g))
    o_ref[0] = jnp.concatenate(outs, axis=1).astype(BF16)


def _gqa_prompt(q, k, v, sink):
    b, s, _ = q.shape
    spec = lambda w: pl.BlockSpec((1, s, w), lambda i: (i, 0, 0))
    return pl.pallas_call(
        _gqa_prompt_body,
        grid=(b,),
        in_specs=[spec(1024), spec(256), spec(256), pl.BlockSpec((8, LANES), lambda i: (0, 0))],
        out_specs=spec(1024),
        out_shape=jax.ShapeDtypeStruct((b, s, 1024), BF16),
        compiler_params=_cparams(("parallel",)),
        name="gqa_dense",
    )(q, k, v, sink)


def _gqa_window_body(q_ref, k_ref, v_ref, kc_ref, vc_ref, sink_ref, o_ref):
    g = GQA_HEADS // GQA_KV_HEADS
    d = GQA_HEAD_DIM
    blk = q_ref.shape[1]
    span = 3 * blk
    n = pl.program_id(1)
    start = pl.multiple_of(jnp.clip((n - 1) * blk, 0, DEC_SEQ - span), blk)
    scale = d ** -0.5
    qpos = n * blk + lax.broadcasted_iota(jnp.int32, (blk, span), 0)
    kpos = start + lax.broadcasted_iota(jnp.int32, (blk, span), 1)
    valid1 = jnp.abs(qpos - kpos) <= WINDOW
    valid = jnp.concatenate([valid1] * g, axis=0)
    outs = []
    for kh in range(GQA_KV_HEADS):
        qg = jnp.concatenate([q_ref[0, :, (kh * g + j) * d:(kh * g + j + 1) * d] for j in range(g)], axis=0)
        kl = k_ref[0, pl.ds(start, span), kh * d:(kh + 1) * d]
        vl = v_ref[0, pl.ds(start, span), kh * d:(kh + 1) * d]
        kc = kc_ref[0, :, kh * d:(kh + 1) * d]
        vc = vc_ref[0, :, kh * d:(kh + 1) * d]
        sl = jnp.where(valid, _nt(qg, kl) * scale, -jnp.inf)
        sc = _nt(qg, kc) * scale
        snk = _sink_column(sink_ref, kh, blk)
        m = jnp.maximum(jnp.maximum(jnp.max(sl, axis=-1, keepdims=True), jnp.max(sc, axis=-1, keepdims=True)), snk)
        el = jnp.exp(sl - m)
        ec = jnp.exp(sc - m)
        l = jnp.sum(el, axis=-1, keepdims=True) + jnp.sum(ec, axis=-1, keepdims=True) + jnp.exp(snk - m)
        o = (jnp.dot(el.astype(BF16), vl, preferred_element_type=F32)
             + jnp.dot(ec.astype(BF16), vc, preferred_element_type=F32)) / l
        outs.extend(o[j * blk:(j + 1) * blk] for j in range(g))
    o_ref[0] = jnp.concatenate(outs, axis=1).astype(BF16)


def _gqa_window(q, k, v, kc, vc, sink):
    b, s, _ = q.shape
    blk = WINDOW
    full = lambda a: pl.BlockSpec((1,) + a.shape[1:], lambda i, j: (i, 0, 0))
    return pl.pallas_call(
        _gqa_window_body,
        grid=(b, s // blk),
        in_specs=[pl.BlockSpec((1, blk, 1024), lambda i, j: (i, j, 0)), full(k), full(v), full(kc), full(vc),
                  pl.BlockSpec((8, LANES), lambda i, j: (0, 0))],
        out_specs=pl.BlockSpec((1, blk, 1024), lambda i, j: (i, j, 0)),
        out_shape=jax.ShapeDtypeStruct((b, s, 1024), BF16),
        compiler_params=_cparams(("parallel", "parallel")),
        name="gqa_window",
    )(q, k, v, kc, vc, sink)


def _ssd_body(nc, xs_ref, bc_ref, dt_ref, alog_ref, h0_ref, y_ref, st_ref):
    d = pl.program_id(1)
    c = pl.program_id(2)
    q = SSD_CHUNK
    hpg = SSD_HEADS // SSD_GROUPS

    @pl.when(c == 0)
    def _():
        st_ref[0, 0] = h0_ref[0, 0]

    row = lax.broadcasted_iota(jnp.int32, (q, LANES), 0)
    a = -jnp.exp(alog_ref[...])
    dta = dt_ref[...] * a

    def scan(x, rev):
        k = 1
        while k < q:
            if rev:
                x = x + jnp.where(row < q - k, pltpu.roll(x, q - k, 0), 0.0)
            else:
                x = x + jnp.where(row >= k, pltpu.roll(x, k, 0), 0.0)
            k *= 2
        return x

    li = lax.broadcasted_iota(jnp.int32, (q, q), 0)
    si = lax.broadcasted_iota(jnp.int32, (q, q), 1)

    def run(rev):
        cum = scan(dta, rev)
        cum_t = cum.T
        edge = cum[0:1, :] if rev else cum[q - 1:q, :]
        mask = (li <= si) if rev else (li >= si)
        xs = xs_ref[...]
        dt = dt_ref[...]
        ys = []
        for g in range(SSD_GROUPS):
            bm = bc_ref[:, g * SSD_STATE:(g + 1) * SSD_STATE]
            cm = bc_ref[:, (SSD_GROUPS + g) * SSD_STATE:(SSD_GROUPS + g + 1) * SSD_STATE]
            bm16 = bm.astype(BF16)
            cm16 = cm.astype(BF16)
            cb = _nt(cm16, bm16)
            for hh in range(hpg):
                h = g * hpg + hh
                j = (SSD_HEADS if rev else 0) + h
                cum_col = cum[:, j:j + 1]
                seg = cum_col - cum_t[j:j + 1, :]
                decay = jnp.exp(jnp.where(mask, seg, -jnp.inf))
                xdt = xs[:, h * SSD_HEAD_DIM:(h + 1) * SSD_HEAD_DIM] * dt[:, j:j + 1]
                y_diag = jnp.dot((cb * decay).astype(BF16), xdt.astype(BF16), preferred_element_type=F32)
                tail = jnp.exp(edge[:, j:j + 1] - cum_col)
                state = lax.dot_general((xdt * tail).astype(BF16), bm16, (((0,), (0,)), ((), ())),
                                        preferred_element_type=F32)
                h_prev = st_ref[0, 0, h]
                y_off = _nt(cm16, h_prev.astype(BF16)) * jnp.exp(cum_col)
                ys.append(y_diag + y_off)
                st_ref[0, 0, h] = h_prev * jnp.exp(edge[:, j:j + 1]) + state
        y_ref[0] = jnp.concatenate(ys, axis=1)

    @pl.when(d == 0)
    def _():
        run(False)

    @pl.when(d == 1)
    def _():
        run(True)


def _ssd(xs, bc, dtp, a_log, h0, seq):
    t = xs.shape[0]
    b = t // seq
    nc = seq // SSD_CHUNK

    def rmap(i, d, c):
        return i * nc + jnp.where(d == 0, c, nc - 1 - c)

    blk = lambda w: pl.BlockSpec((SSD_CHUNK, w), lambda i, d, c: (rmap(i, d, c), 0))
    st_spec = pl.BlockSpec((1, 1, SSD_HEADS, SSD_HEAD_DIM, SSD_STATE), lambda i, d, c: (d, i, 0, 0, 0))
    return pl.pallas_call(
        functools.partial(_ssd_body, nc),
        grid=(b, 2, nc),
        in_specs=[blk(1024), blk(512), blk(LANES), pl.BlockSpec((1, LANES), lambda i, d, c: (0, 0)), st_spec],
        out_specs=[pl.BlockSpec((1, SSD_CHUNK, 1024), lambda i, d, c: (d, rmap(i, d, c), 0)), st_spec],
        out_shape=[jax.ShapeDtypeStruct((2, t, 1024), F32),
                   jax.ShapeDtypeStruct((2, b, SSD_HEADS, SSD_HEAD_DIM, SSD_STATE), F32)],
        compiler_params=_cparams(("parallel", "arbitrary", "arbitrary")),
        name="ssd_scan",
    )(xs, bc, dtp, a_log, h0)


def _ssd_gate_body(y_ref, xs_ref, z_ref, dsk_ref, g_ref, o_ref):
    y = (y_ref[0] + y_ref[1] + xs_ref[...] * dsk_ref[...]) * _silu(z_ref[...])
    w = SSD_D_INNER // SSD_GROUPS
    outs = [_rms(y[:, g * w:(g + 1) * w], g_ref[:, g * w:(g + 1) * w]) for g in range(SSD_GROUPS)]
    o_ref[...] = jnp.concatenate(outs, axis=1).astype(BF16)


def _ssd_gate(y2, xs, proj, d_skip, norm_g):
    tm = 512
    return pl.pallas_call(
        _ssd_gate_body,
        grid=(N_TOK // tm,),
        in_specs=[pl.BlockSpec((2, tm, 1024), lambda i: (0, i, 0)),
                  pl.BlockSpec((tm, 1024), lambda i: (i, 0)),
                  pl.BlockSpec((tm, 1024), lambda i: (i, 2)),
                  pl.BlockSpec((1, 1024), lambda i: (0, 0)),
                  pl.BlockSpec((1, 1024), lambda i: (0, 0))],
        out_specs=pl.BlockSpec((tm, 1024), lambda i: (i, 0)),
        out_shape=jax.ShapeDtypeStruct((N_TOK, 1024), BF16),
        compiler_params=_cparams(("parallel",)),
        name="ssd_gate_norm",
    )(y2, xs, proj, d_skip, norm_g)


def _even_layer(x, mod, layer_idx, p, ctx, tables):
    w_in, q_norm, kv_norm, wq_b, wkv_b, lam, subln, w_out = p
    ctx_ckv, ctx_kpe, ctx_dk, ctx_dv = ctx
    cos, sin = tables
    lam_init = 0.8 - 0.6 * math.exp(-0.3 * layer_idx)

    w_r = jnp.concatenate([w_in[:, :1024], w_in[:, 1088:], w_in[:, 1024:1088],
                           jnp.zeros((D_MODEL, PROJ_N - w_in.shape[1]), F32)], axis=1).astype(BF16)
    proj = _proj(x, mod[0], mod[1], w_r)
    qn, ckv, dq, dk, kpe = _ev_post(proj, cos, sin, q_norm[None], kv_norm[None])
    dv = proj[:, 3072:4096]

    wq = wq_b.reshape(MLA_Q_RANK, MLA_HEADS, MLA_NOPE + MLA_ROPE)
    wq_r = jnp.pad(wq, ((0, 0), (0, 0), (0, MLA_ROPE))).reshape(MLA_Q_RANK, -1).astype(BF16)
    q = _mm(qn, wq_r, BF16)

    ckv_rows = jnp.concatenate([ckv.astype(BF16), ctx_ckv.reshape(-1, MLA_KV_RANK).astype(BF16)], axis=0)
    kv = _mm(ckv_rows, wkv_b.astype(BF16), BF16)

    def per_seq(a, b, s):
        return a.reshape(b, s, a.shape[-1])

    pr = slice(0, N_PROMPT)
    sm = slice(N_PROMPT, N_TOK)
    lam_args = (lam, subln[None])
    o1p, o2p = _even_attn(lam_init, per_seq(q[pr], BATCH, SEQ), cos, sin, per_seq(kv[pr], BATCH, SEQ),
                          per_seq(kpe[pr].astype(BF16), BATCH, SEQ), per_seq(dq[pr], BATCH, SEQ),
                          per_seq(dk[pr].astype(BF16), BATCH, SEQ), per_seq(dv[pr].astype(BF16), BATCH, SEQ),
                          *lam_args, row_off=0)

    def with_ctx(new, old):
        return jnp.concatenate([per_seq(new, DEC_BATCH, DEC_SEQ).astype(BF16), old.astype(BF16)], axis=1)

    kv_s = with_ctx(kv[sm], per_seq(kv[N_TOK:], DEC_BATCH, PAST_LEN))
    kpe_ctx = jnp.pad(ctx_kpe, ((0, 0), (0, 0), (0, LANES - MLA_ROPE)))
    kpe_s = with_ctx(kpe[sm], kpe_ctx)
    dk_s = with_ctx(dk[sm], ctx_dk.reshape(DEC_BATCH, PAST_LEN, -1))
    dv_s = with_ctx(dv[sm], ctx_dv.reshape(DEC_BATCH, PAST_LEN, -1))
    o1s, o2s = _even_attn(lam_init, per_seq(q[sm], DEC_BATCH, DEC_SEQ), cos, sin, kv_s, kpe_s,
                          per_seq(dq[sm], DEC_BATCH, DEC_SEQ), dk_s, dv_s, *lam_args, row_off=N_PROMPT)

    a1 = jnp.concatenate([o1p.reshape(N_PROMPT, -1), o1s.reshape(N_SAMPLE, -1)], axis=0)
    a2 = jnp.concatenate([o2p.reshape(N_PROMPT, -1), o2s.reshape(N_SAMPLE, -1)], axis=0)
    cache = (ckv[pr].reshape(BATCH, SEQ, MLA_KV_RANK),
             kpe[pr, :MLA_ROPE].reshape(BATCH, SEQ, MLA_ROPE),
             dk[pr].reshape(BATCH, SEQ, DIFF_HEADS, 2 * DIFF_QK),
             dv[pr].reshape(BATCH, SEQ, DIFF_HEADS, DIFF_V))
    return a1, a2, w_out.astype(BF16), cache


def _odd_layer(x, mod, p, ctx, tables):
    w_in, sink, conv_w, conv_b, dt_bias, a_log, d_skip, norm_g, w_out = p
    k_ctx, v_ctx, h0_f, h0_b = ctx
    cos, sin = tables

    w_r = jnp.concatenate([w_in[:, :1024], w_in[:, 2560:3584], w_in[:, 1536:2560], w_in[:, 1024:1536],
                           w_in[:, 3584:], jnp.zeros((D_MODEL, PROJ_N - w_in.shape[1]), F32)],
                          axis=1).astype(BF16)
    proj = _proj(x, mod[0], mod[1], w_r)
    cw = conv_w
    cb = conv_b[None]
    dtb = jnp.pad(dt_bias.reshape(1, -1), ((0, 0), (0, LANES - 2 * SSD_HEADS)))
    alog = jnp.pad(a_log.reshape(1, -1), ((0, 0), (0, LANES - 2 * SSD_HEADS)))
    post_p = _od_post(proj, cos, sin, cw, cb, dtb, SEQ, 0, N_PROMPT)
    post_s = _od_post(proj, cos, sin, cw, cb, dtb, DEC_SEQ, N_PROMPT, N_SAMPLE)
    qp, kp, xs_p, bc_p, dt_p = post_p
    qs, ks, xs_s, bc_s, dt_s = post_s
    k_raw = proj[:, 3072:3328]
    v_raw = proj[:, 3328:3584]
    sink_b = jnp.broadcast_to(sink[:, None], (GQA_HEADS, LANES))

    o_att_p = _gqa_prompt(qp.reshape(BATCH, SEQ, -1), kp.reshape(BATCH, SEQ, -1),
                          v_raw[:N_PROMPT].astype(BF16).reshape(BATCH, SEQ, -1), sink_b)
    o_att_s = _gqa_window(qs.reshape(DEC_BATCH, DEC_SEQ, -1), ks.reshape(DEC_BATCH, DEC_SEQ, -1),
                          v_raw[N_PROMPT:].astype(BF16).reshape(DEC_BATCH, DEC_SEQ, -1),
                          k_ctx.reshape(DEC_BATCH, PAST_LEN, -1).astype(BF16),
                          v_ctx.reshape(DEC_BATCH, PAST_LEN, -1).astype(BF16), sink_b)

    h0_p = jnp.zeros((2, BATCH, SSD_HEADS, SSD_HEAD_DIM, SSD_STATE), F32)
    y_p, st_p = _ssd(xs_p, bc_p, dt_p, alog, h0_p, SEQ)
    y_s, _ = _ssd(xs_s, bc_s, dt_s, alog, jnp.stack([h0_f, h0_b]), DEC_SEQ)
    y2 = jnp.concatenate([y_p, y_s], axis=1)
    xs_all = jnp.concatenate([xs_p, xs_s], axis=0)
    dsk = jnp.repeat(d_skip, SSD_HEAD_DIM)[None]
    y_n = _ssd_gate(y2, xs_all, proj, dsk, norm_g[None])

    a1 = jnp.concatenate([o_att_p.reshape(N_PROMPT, -1), o_att_s.reshape(N_SAMPLE, -1)], axis=0)
    cache = (k_raw[:N_PROMPT].reshape(BATCH, SEQ, GQA_KV_HEADS, GQA_HEAD_DIM),
             v_raw[:N_PROMPT].reshape(BATCH, SEQ, GQA_KV_HEADS, GQA_HEAD_DIM),
             st_p[0], st_p[1])
    return a1, y_n, w_out.astype(BF16), cache


def kernel(x_prompt, x_sample, cache_mla_ckv, cache_mla_kpe, cache_diff_k, cache_diff_v, cache_gqa_k, cache_gqa_v, state_ssd_fwd, state_ssd_bwd, c, c_ctx, w_mod, b_mod, ln1_g, ln1_b, ln2_g, ln2_b, ev_w_in, mla_q_norm, mla_kv_norm, mla_wq_b, mla_wkv_b, diff_lambda, diff_subln, ev_w_out, od_w_in, gqa_sink, ssd_conv_w, ssd_conv_b, ssd_dt_bias, ssd_a_log, ssd_d, ssd_norm, od_w_out, moe_router_group, moe_router_expert, moe_w_gate, moe_w_up, moe_w_down):
    x = jnp.concatenate([x_prompt.reshape(N_PROMPT, D_MODEL), x_sample.reshape(N_SAMPLE, D_MODEL)], axis=0)
    cs = jnp.concatenate([c_ctx[None], c, jnp.zeros((8 - 1 - DEC_BATCH, D_MODEL), F32)], axis=0)
    mods = _adaln(cs, w_mod, b_mod)[:, :1 + DEC_BATCH].reshape(DEPTH, 1 + DEC_BATCH, 6, 1, D_MODEL)
    tables_64 = _rope_tables(64)
    tables_128 = _rope_tables(GQA_HEAD_DIM)

    ev_caches, od_caches = [], []
    for l in range(DEPTH):
        i = l // 2
        mod = [mods[l, :, k] for k in range(6)]
        if l % 2 == 0:
            p = (ev_w_in[i], mla_q_norm[i], mla_kv_norm[i], mla_wq_b[i], mla_wkv_b[i], diff_lambda[i],
                 diff_subln[i], ev_w_out[i])
            ctx = (cache_mla_ckv[:, i], cache_mla_kpe[:, i], cache_diff_k[:, i], cache_diff_v[:, i])
            a1, a2, w_out, cache = _even_layer(x, mod, l, p, ctx, tables_64)
            ev_caches.append(cache)
        else:
            p = (od_w_in[i], gqa_sink[i], ssd_conv_w[i], ssd_conv_b[i], ssd_dt_bias[i], ssd_a_log[i], ssd_d[i],
                 ssd_norm[i], od_w_out[i])
            ctx = (cache_gqa_k[:, i], cache_gqa_v[:, i], state_ssd_fwd[:, i], state_ssd_bwd[:, i])
            a1, a2, w_out, cache = _odd_layer(x, mod, p, ctx, tables_128)
            od_caches.append(cache)
        w_router = jnp.concatenate([moe_router_group[l], moe_router_expert[l],
                                    jnp.zeros((D_MODEL, LANES - MOE_GROUPS - MOE_EXPERTS), F32)], axis=1)
        x, h2, comb = _out_proj(a1, a2, w_out, x, mod[2], ln1_g[l][None], ln1_b[l][None], mod[3], mod[4],
                                w_router)
        y = _moe(h2, comb, moe_w_gate[l], moe_w_up[l], moe_w_down[l])
        x = _res_ln(x, y, mod[5], ln2_g[l][None], ln2_b[l][None])

    y_prompt = x[:N_PROMPT].reshape(BATCH, SEQ, D_MODEL)
    y_sample = x[N_PROMPT:].reshape(DEC_BATCH, DEC_SEQ, D_MODEL)
    stack = lambda caches, k: jnp.stack([t[k] for t in caches], axis=1)
    return (y_prompt, y_sample, stack(ev_caches, 0), stack(ev_caches, 1), stack(ev_caches, 2), stack(ev_caches, 3),
            stack(od_caches, 0), stack(od_caches, 1), stack(od_caches, 2), stack(od_caches, 3))
```

```python
import functools
import math

import jax
import jax.numpy as jnp
from jax import lax
from jax.experimental import pallas as pl
from jax.experimental.pallas import tpu as pltpu

F32 = jnp.float32
BF16 = jnp.bfloat16

D_MODEL = 2048
BATCH = 16
SEQ = 256
DEPTH = 4
DEC_BATCH = 2
DEC_SEQ = 1024
PAST_LEN = 512
GRID_W = 64
ROPE_THETA = 10000.0
DEEPNORM_ALPHA = (2.0 * DEPTH) ** 0.25
LN_EPS = 1e-5
RMS_EPS = 1e-6

MLA_HEADS = 8
MLA_Q_RANK = 512
MLA_KV_RANK = 512
MLA_NOPE = 128
MLA_ROPE = 64
MLA_V = 128
DIFF_HEADS = 8
DIFF_QK = 64
DIFF_V = 128
GQA_HEADS = 8
GQA_KV_HEADS = 2
GQA_HEAD_DIM = 128
WINDOW = 128
SSD_HEADS = 16
SSD_HEAD_DIM = 64
SSD_D_INNER = SSD_HEADS * SSD_HEAD_DIM
SSD_GROUPS = 2
SSD_STATE = 128
SSD_CHUNK = 128
MOE_GROUPS = 4
MOE_EPG = 4
MOE_EXPERTS = 16
MOE_HIDDEN = 512

N_PROMPT = BATCH * SEQ
N_SAMPLE = DEC_BATCH * DEC_SEQ
N_TOK = N_PROMPT + N_SAMPLE
PROJ_N = 4224
LANES = 128
VMEM_LIMIT = 56 * 1024 * 1024


def _cparams(sem):
    return pltpu.CompilerParams(dimension_semantics=sem, vmem_limit_bytes=VMEM_LIMIT)


def _mod_group(row0):
    return jnp.maximum(row0 // DEC_SEQ - (N_PROMPT // DEC_SEQ - 1), 0)


def _silu(x):
    return x / (1.0 + jnp.exp(-x))


def _adaln_body(c_ref, w_ref, b_ref, o_ref):
    a = _silu(c_ref[...])
    o_ref[0] = jnp.dot(a, w_ref[0], preferred_element_type=F32) + b_ref[0]


def _adaln(cs, w_mod, b_mod):
    tn = 1024
    n = w_mod.shape[-1]
    return pl.pallas_call(
        _adaln_body,
        grid=(DEPTH, n // tn),
        in_specs=[
            pl.BlockSpec((8, D_MODEL), lambda l, j: (0, 0)),
            pl.BlockSpec((1, D_MODEL, tn), lambda l, j: (l, 0, j)),
            pl.BlockSpec((1, 1, tn), lambda l, j: (l, 0, j)),
        ],
        out_specs=pl.BlockSpec((1, 8, tn), lambda l, j: (l, 0, j)),
        out_shape=jax.ShapeDtypeStruct((DEPTH, 8, n), F32),
        compiler_params=_cparams(("parallel", "parallel")),
        name="adaln",
    )(cs, w_mod, b_mod.reshape(DEPTH, 1, n))


def _proj_body(x_ref, sh_ref, sc_ref, w_ref, o_ref, h_scr):
    @pl.when(pl.program_id(1) == 0)
    def _():
        h = x_ref[...] * (1.0 + sc_ref[0]) + sh_ref[0]
        h_scr[...] = h.astype(BF16)

    o_ref[...] = jnp.dot(h_scr[...], w_ref[...], preferred_element_type=F32)


def _proj(x, shift, scale, w):
    tm, tn = 1024, 384
    n = w.shape[1]
    mod_spec = pl.BlockSpec((1, 1, D_MODEL), lambda i, j: (_mod_group(i * tm), 0, 0))
    return pl.pallas_call(
        _proj_body,
        grid=(N_TOK // tm, n // tn),
        in_specs=[
            pl.BlockSpec((tm, D_MODEL), lambda i, j: (i, 0)),
            mod_spec,
            mod_spec,
            pl.BlockSpec((D_MODEL, tn), lambda i, j: (0, j)),
        ],
        out_specs=pl.BlockSpec((tm, tn), lambda i, j: (i, j)),
        out_shape=jax.ShapeDtypeStruct((N_TOK, n), F32),
        scratch_shapes=[pltpu.VMEM((tm, D_MODEL), BF16)],
        compiler_params=_cparams(("parallel", "arbitrary")),
        name="in_proj",
    )(x, shift, scale, w)


def _mm_body(a_ref, b_ref, o_ref):
    o_ref[...] = jnp.dot(a_ref[...], b_ref[...], preferred_element_type=F32).astype(o_ref.dtype)


def _mm(a, b, out_dtype, tm=512):
    m, k = a.shape
    n = b.shape[1]
    return pl.pallas_call(
        _mm_body,
        grid=(m // tm,),
        in_specs=[pl.BlockSpec((tm, k), lambda i: (i, 0)), pl.BlockSpec((k, n), lambda i: (0, 0))],
        out_specs=pl.BlockSpec((tm, n), lambda i: (i, 0)),
        out_shape=jax.ShapeDtypeStruct((m, n), out_dtype),
        compiler_params=_cparams(("parallel",)),
        name="mm_resident",
    )(a, b)


def _rope(x, cos, sin_signed, quarter):
    rows, w = x.shape
    reps = w // LANES
    if reps > 1:
        cos = jnp.concatenate([cos] * reps, axis=1)
        sin_signed = jnp.concatenate([sin_signed] * reps, axis=1)
    lane = lax.broadcasted_iota(jnp.int32, x.shape, 1)
    first = (lane % (2 * quarter)) < quarter
    up = pltpu.roll(x, w - quarter, 1)
    dn = pltpu.roll(x, quarter, 1)
    return x * cos + jnp.where(first, up, dn) * sin_signed


def _rope_tables(rdim):
    half = rdim // 2
    quarter = half // 2
    pos = jnp.arange(DEC_SEQ)
    row = (pos // GRID_W).astype(F32)
    col = (pos % GRID_W).astype(F32)
    inv = ROPE_THETA ** (-jnp.arange(quarter, dtype=F32) * 2.0 / half)
    lane = jnp.arange(LANES)
    r = lane % rdim
    use_col = (r // half) == 1
    j = r % quarter
    p = jnp.where(use_col[None, :], col[:, None], row[:, None])
    ang = p * inv[j][None, :]
    sign = jnp.where((r % half) < quarter, -1.0, 1.0).astype(F32)
    cos = jnp.cos(ang)
    sin = jnp.sin(ang) * sign[None, :]
    cos = jnp.concatenate([jnp.ones((N_PROMPT, LANES), F32), cos, cos], axis=0)
    sin = jnp.concatenate([jnp.zeros((N_PROMPT, LANES), F32), sin, sin], axis=0)
    return cos, sin


def _rms(x, g):
    return x * lax.rsqrt(jnp.mean(x * x, axis=-1, keepdims=True) + RMS_EPS) * g


def _ev_post_body(ql_ref, kvl_ref, dq_ref, dk_ref, kpe_ref, cos_ref, sin_ref, gq_ref, gkv_ref,
                  qn_ref, ckv_ref, dqr_ref, dkr_ref, kper_ref):
    cos = cos_ref[...]
    sin = sin_ref[...]
    qn_ref[...] = _rms(ql_ref[...], gq_ref[...]).astype(BF16)
    ckv_ref[...] = _rms(kvl_ref[...], gkv_ref[...])
    dqr_ref[...] = _rope(dq_ref[...], cos, sin, DIFF_QK // 4).astype(BF16)
    dkr_ref[...] = _rope(dk_ref[...], cos, sin, DIFF_QK // 4)
    kper_ref[...] = _rope(kpe_ref[...], cos, sin, MLA_ROPE // 4)


def _ev_post(proj, cos, sin, gq, gkv):
    tm = 256
    row = lambda c: (lambda i: (i, c))
    return pl.pallas_call(
        _ev_post_body,
        grid=(N_TOK // tm,),
        in_specs=[
            pl.BlockSpec((tm, 512), row(0)),
            pl.BlockSpec((tm, 512), row(1)),
            pl.BlockSpec((tm, 1024), row(1)),
            pl.BlockSpec((tm, 1024), row(2)),
            pl.BlockSpec((tm, LANES), row(32)),
            pl.BlockSpec((tm, LANES), row(0)),
            pl.BlockSpec((tm, LANES), row(0)),
            pl.BlockSpec((1, 512), lambda i: (0, 0)),
            pl.BlockSpec((1, 512), lambda i: (0, 0)),
        ],
        out_specs=[
            pl.BlockSpec((tm, 512), row(0)),
            pl.BlockSpec((tm, 512), row(0)),
            pl.BlockSpec((tm, 1024), row(0)),
            pl.BlockSpec((tm, 1024), row(0)),
            pl.BlockSpec((tm, LANES), row(0)),
        ],
        out_shape=[
            jax.ShapeDtypeStruct((N_TOK, 512), BF16),
            jax.ShapeDtypeStruct((N_TOK, 512), F32),
            jax.ShapeDtypeStruct((N_TOK, 1024), BF16),
            jax.ShapeDtypeStruct((N_TOK, 1024), F32),
            jax.ShapeDtypeStruct((N_TOK, LANES), F32),
        ],
        compiler_params=_cparams(("parallel",)),
        name="even_post",
    )(proj, proj, proj, proj, proj, cos, sin, gq, gkv)


def _softmax_av(s, v):
    m = jnp.max(s, axis=-1, keepdims=True)
    e = jnp.exp(s - m)
    l = jnp.sum(e, axis=-1, keepdims=True)
    return jnp.dot(e.astype(BF16), v, preferred_element_type=F32) / l


def _nt(a, b):
    return lax.dot_general(a, b, (((1,), (1,)), ((), ())), preferred_element_type=F32)


def _even_attn_body(lam_init, q_ref, cos_ref, sin_ref, kv_ref, kpe_ref, dq_ref, dk_ref, dv_ref,
                    lam_ref, sub_ref, omla_ref, odiff_ref):
    q = q_ref[0]
    qn = q[:, :MLA_NOPE]
    qr = _rope(q[:, MLA_NOPE:].astype(F32), cos_ref[...], sin_ref[...], MLA_ROPE // 4).astype(BF16)
    kn = kv_ref[0, :, :MLA_NOPE]
    v = kv_ref[0, :, MLA_NOPE:]
    s = (_nt(qn, kn) + _nt(qr, kpe_ref[0])) * (MLA_NOPE + MLA_ROPE) ** -0.5
    omla_ref[0] = _softmax_av(s, v).astype(BF16)

    lam = lam_ref[...]
    lam_full = (jnp.exp(jnp.sum(lam[0:1] * lam[1:2], axis=-1, keepdims=True))
                - jnp.exp(jnp.sum(lam[2:3] * lam[3:4], axis=-1, keepdims=True)) + lam_init)
    dq = dq_ref[0]
    dk = dk_ref[0]
    dv = dv_ref[0]
    lane = lax.broadcasted_iota(jnp.int32, dq.shape, 1)
    zero = jnp.zeros_like(dq)
    diff_scale = DIFF_QK ** -0.5
    a1 = _softmax_av(_nt(jnp.where(lane < DIFF_QK, dq, zero), dk) * diff_scale, dv)
    a2 = _softmax_av(_nt(jnp.where(lane >= DIFF_QK, dq, zero), dk) * diff_scale, dv)
    odiff_ref[0] = (_rms(a1 - lam_full * a2, sub_ref[...]) * (1.0 - lam_init)).astype(BF16)


def _even_attn(lam_init, q, cos, sin, kv, kpe, dq, dk, dv, lam, subln, row_off):
    b, sq, _ = q.shape
    sk = kv.shape[1]
    tq = 256
    nq = sq // tq
    off = row_off // tq
    qspec = lambda w: pl.BlockSpec((1, tq, w), lambda i, j, h: (i, j, h))
    kspec = lambda w: pl.BlockSpec((1, sk, w), lambda i, j, h: (i, 0, h))
    tspec = pl.BlockSpec((tq, LANES), lambda i, j, h: (off + i * nq + j, 0))
    return pl.pallas_call(
        functools.partial(_even_attn_body, lam_init),
        grid=(b, nq, MLA_HEADS),
        in_specs=[qspec(256), tspec, tspec, kspec(256),
                  pl.BlockSpec((1, sk, LANES), lambda i, j, h: (i, 0, 0)),
                  qspec(LANES), kspec(LANES), kspec(LANES),
                  pl.BlockSpec((4, DIFF_QK), lambda i, j, h: (0, 0)),
                  pl.BlockSpec((1, DIFF_V), lambda i, j, h: (0, 0))],
        out_specs=[qspec(LANES), qspec(LANES)],
        out_shape=[jax.ShapeDtypeStruct((b, sq, 1024), BF16), jax.ShapeDtypeStruct((b, sq, 1024), BF16)],
        compiler_params=_cparams(("parallel", "parallel", "parallel")),
        name="even_attn",
    )(q, cos, sin, kv, kpe, dq, dk, dv, lam, subln)


def _layer_norm(y, g, b):
    mu = jnp.mean(y, axis=-1, keepdims=True)
    yc = y - mu
    return yc * lax.rsqrt(jnp.mean(yc * yc, axis=-1, keepdims=True) + LN_EPS) * g + b


def _route(logits):
    lane = lax.broadcasted_iota(jnp.int32, logits.shape, 1).astype(F32)
    neg = -jnp.inf
    none = float(LANES)
    is_g = lane < MOE_GROUPS
    lg = jnp.where(is_g, logits, neg)
    mg = jnp.max(lg, axis=-1, keepdims=True)
    g_val = 1.0 / jnp.sum(jnp.exp(lg - mg), axis=-1, keepdims=True)
    g_idx = jnp.min(jnp.where(is_g & (lg == mg), lane, none), axis=-1, keepdims=True)
    e_lo = MOE_GROUPS + g_idx * MOE_EPG
    is_e = (lane >= e_lo) & (lane < e_lo + MOE_EPG)
    le = jnp.where(is_e, logits, neg)
    p = jnp.exp(le - jnp.max(le, axis=-1, keepdims=True))
    p1 = jnp.max(p, axis=-1, keepdims=True)
    i1 = jnp.min(jnp.where(is_e & (p == p1), lane, none), axis=-1, keepdims=True)
    rest = is_e & (lane != i1)
    p2 = jnp.max(jnp.where(rest, p, neg), axis=-1, keepdims=True)
    i2 = jnp.min(jnp.where(rest & (p == p2), lane, none), axis=-1, keepdims=True)
    s = g_val / (p1 + p2)
    return jnp.where(lane == 0.0, i1 - MOE_GROUPS,
                     jnp.where(lane == 1.0, i2 - MOE_GROUPS,
                               jnp.where(lane == 2.0, p1 * s, jnp.where(lane == 3.0, p2 * s, 0.0))))


def _out_proj_body(a1_ref, a2_ref, w_ref, x_ref, gate_ref, g_ref, b_ref, sh_ref, sc_ref, wr_ref,
                   xo_ref, h_ref, comb_ref):
    k1 = a1_ref.shape[1]
    acc = jnp.dot(a1_ref[...], w_ref[:k1, :], preferred_element_type=F32)
    acc = acc + jnp.dot(a2_ref[...], w_ref[k1:, :], preferred_element_type=F32)
    xn = _layer_norm(DEEPNORM_ALPHA * x_ref[...] + gate_ref[0] * acc, g_ref[...], b_ref[...])
    xo_ref[...] = xn
    h = xn * (1.0 + sc_ref[0]) + sh_ref[0]
    h_ref[...] = h
    h_hi = h.astype(BF16)
    h_lo = (h - h_hi.astype(F32)).astype(BF16)
    logits = (jnp.dot(h_hi, wr_ref[0], preferred_element_type=F32)
              + jnp.dot(h_lo, wr_ref[0], preferred_element_type=F32)
              + jnp.dot(h_hi, wr_ref[1], preferred_element_type=F32))
    comb_ref[...] = _route(logits)


def _out_proj(a1, a2, w, x, gate, ln_g, ln_b, shift, scale, w_router):
    tm = 512
    k1, k2 = a1.shape[1], a2.shape[1]
    mod_spec = pl.BlockSpec((1, 1, D_MODEL), lambda i: (_mod_group(i * tm), 0, 0))
    vec_spec = pl.BlockSpec((1, D_MODEL), lambda i: (0, 0))
    row_spec = pl.BlockSpec((tm, D_MODEL), lambda i: (i, 0))
    return pl.pallas_call(
        _out_proj_body,
        grid=(N_TOK // tm,),
        in_specs=[
            pl.BlockSpec((tm, k1), lambda i: (i, 0)),
            pl.BlockSpec((tm, k2), lambda i: (i, 0)),
            pl.BlockSpec((k1 + k2, D_MODEL), lambda i: (0, 0)),
            row_spec, mod_spec, vec_spec, vec_spec, mod_spec, mod_spec,
            pl.BlockSpec((2, D_MODEL, LANES), lambda i: (0, 0, 0)),
        ],
        out_specs=[row_spec, row_spec, pl.BlockSpec((tm, LANES), lambda i: (i, 0))],
        out_shape=[
            jax.ShapeDtypeStruct((N_TOK, D_MODEL), F32),
            jax.ShapeDtypeStruct((N_TOK, D_MODEL), F32),
            jax.ShapeDtypeStruct((N_TOK, LANES), F32),
        ],
        compiler_params=_cparams(("parallel",)),
        name="out_proj_ln_router",
    )(a1, a2, w, x, gate, ln_g, ln_b, shift, scale, w_router)


MOE_TM = 256
MOE_SLOTS = 2 * N_TOK
MOE_TILES = MOE_SLOTS // MOE_TM + MOE_EXPERTS
MOE_ROWS = MOE_TILES * MOE_TM


def _route_meta(route):
    e = route[:, :2].astype(jnp.int32).reshape(-1)
    onehot = (e[:, None] == jnp.arange(MOE_EXPERTS, dtype=jnp.int32)[None, :]).astype(jnp.int32)
    csum = jnp.cumsum(onehot, axis=0)
    rank = jnp.sum((csum - onehot) * onehot, axis=1)
    ntile = (csum[-1] + MOE_TM - 1) // MOE_TM
    tile_end = jnp.cumsum(ntile)
    pos = jnp.sum(onehot * (tile_end - ntile)[None, :], axis=1) * MOE_TM + rank
    n_used = tile_end[-1:]
    j = jnp.arange(MOE_TILES, dtype=jnp.int32)
    tile_expert = jnp.sum((tile_end[None, :] <= jnp.minimum(j, n_used - 1)[:, None]).astype(jnp.int32), axis=1)
    tail = jnp.where(ntile > 0, tile_end - 1, -1)
    return pos, tile_expert, n_used, tail


def _dispatch_body(pos_ref, tail_ref, nu_ref, h_ref, xs_ref, zero_ref, sem, zsem):
    i = pl.program_id(0)
    tm = h_ref.shape[0]

    def zero_tile(j):
        return pltpu.make_async_copy(zero_ref, xs_ref.at[pl.ds(j * MOE_TM, MOE_TM)], zsem)

    def start_unused(j, c):
        zero_tile(j).start()
        return c

    def wait_unused(j, c):
        zero_tile(j).wait()
        return c

    @pl.when(i == 0)
    def _():
        zero_ref[...] = jnp.zeros_like(zero_ref)
        for e in range(MOE_EXPERTS):
            @pl.when(tail_ref[e] >= 0)
            def _():
                zero_tile(tail_ref[e]).start()
        lax.fori_loop(nu_ref[0], MOE_TILES, start_unused, 0)
        for e in range(MOE_EXPERTS):
            @pl.when(tail_ref[e] >= 0)
            def _():
                zero_tile(tail_ref[e]).wait()
        lax.fori_loop(nu_ref[0], MOE_TILES, wait_unused, 0)

    def row_copy(r, k):
        p = pos_ref[2 * (i * tm + r) + k]
        return pltpu.make_async_copy(h_ref.at[pl.ds(r, 1)], xs_ref.at[pl.ds(p, 1)], sem)

    def issue(r, c):
        row_copy(r, 0).start()
        row_copy(r, 1).start()
        return c

    def drain(r, c):
        row_copy(r, 0).wait()
        row_copy(r, 1).wait()
        return c

    lax.fori_loop(0, tm, issue, 0)
    lax.fori_loop(0, tm, drain, 0)


def _dispatch(h, pos, tail, n_used):
    tm = 256
    return pl.pallas_call(
        _dispatch_body,
        grid_spec=pltpu.PrefetchScalarGridSpec(
            num_scalar_prefetch=3,
            grid=(N_TOK // tm,),
            in_specs=[pl.BlockSpec((tm, D_MODEL), lambda i, pos_ref, tail_ref, nu_ref: (i, 0))],
            out_specs=pl.BlockSpec(memory_space=pl.ANY),
            scratch_shapes=[pltpu.VMEM((MOE_TM, D_MODEL), F32), pltpu.SemaphoreType.DMA(()),
                            pltpu.SemaphoreType.DMA(())],
        ),
        out_shape=jax.ShapeDtypeStruct((MOE_ROWS, D_MODEL), F32),
        compiler_params=_cparams(("arbitrary",)),
        name="moe_dispatch",
    )(pos, tail, n_used, h)


def _moe_body(te_ref, nu_ref, x_ref, wg_ref, wu_ref, wd_ref, z_ref, wg_s, wu_s, wd_s):
    j = pl.program_id(0)
    changed = (j == 0) | (te_ref[j] != te_ref[jnp.maximum(j - 1, 0)])

    @pl.when(changed)
    def _():
        wg_s[...] = wg_ref[0, 0].astype(BF16)
        wu_s[...] = wu_ref[0, 0].astype(BF16)
        wd_s[...] = wd_ref[0, 0].astype(BF16)

    @pl.when(j < nu_ref[0])
    def _():
        x = x_ref[...].astype(BF16)
        hg = jnp.dot(x, wg_s[...], preferred_element_type=F32)
        hu = jnp.dot(x, wu_s[...], preferred_element_type=F32)
        act = (_silu(hg) * hu).astype(BF16)
        z_ref[...] = jnp.dot(act, wd_s[...], preferred_element_type=F32)

    @pl.when(j >= nu_ref[0])
    def _():
        z_ref[...] = jnp.zeros_like(z_ref)


def _moe_experts(layer, xs, tile_expert, n_used, wg, wu, wd):
    row_spec = pl.BlockSpec((MOE_TM, D_MODEL), lambda j, te, nu: (j, 0))
    return pl.pallas_call(
        _moe_body,
        grid_spec=pltpu.PrefetchScalarGridSpec(
            num_scalar_prefetch=2,
            grid=(MOE_TILES,),
            in_specs=[
                row_spec,
                pl.BlockSpec((1, 1, D_MODEL, MOE_HIDDEN), lambda j, te, nu: (layer, te[j], 0, 0)),
                pl.BlockSpec((1, 1, D_MODEL, MOE_HIDDEN), lambda j, te, nu: (layer, te[j], 0, 0)),
                pl.BlockSpec((1, 1, MOE_HIDDEN, D_MODEL), lambda j, te, nu: (layer, te[j], 0, 0)),
            ],
            out_specs=row_spec,
            scratch_shapes=[pltpu.VMEM((D_MODEL, MOE_HIDDEN), BF16), pltpu.VMEM((D_MODEL, MOE_HIDDEN), BF16),
                            pltpu.VMEM((MOE_HIDDEN, D_MODEL), BF16)],
        ),
        out_shape=jax.ShapeDtypeStruct((MOE_ROWS, D_MODEL), F32),
        compiler_params=_cparams(("arbitrary",)),
        name="moe_experts",
    )(tile_expert, n_used, xs, wg, wu, wd)


def _combine_body(pos_ref, x_ref, r_ref, gate_ref, g_ref, b_ref, z_ref, o_ref, zbuf, sem):
    i = pl.program_id(0)
    tm = x_ref.shape[0]
    slot = i % 2

    def row_copy(tile, s, r, k):
        p = pos_ref[2 * (tile * tm + r) + k]
        return pltpu.make_async_copy(z_ref.at[pl.ds(p, 1)], zbuf.at[s, k, pl.ds(r, 1)], sem.at[s])

    def start(tile, s):
        def issue(r, c):
            row_copy(tile, s, r, 0).start()
            row_copy(tile, s, r, 1).start()
            return c
        lax.fori_loop(0, tm, issue, 0)

    @pl.when(i == 0)
    def _():
        start(0, 0)

    @pl.when(i + 1 < pl.num_programs(0))
    def _():
        start(i + 1, 1 - slot)

    def drain(r, c):
        row_copy(i, slot, r, 0).wait()
        row_copy(i, slot, r, 1).wait()
        return c

    lax.fori_loop(0, tm, drain, 0)
    r = r_ref[...]
    y = r[:, 2:3] * zbuf[slot, 0] + r[:, 3:4] * zbuf[slot, 1]
    o_ref[...] = _layer_norm(DEEPNORM_ALPHA * x_ref[...] + gate_ref[0] * y, g_ref[...], b_ref[...])


def _combine(x, z, route, pos, gate, ln_g, ln_b):
    tm = 256
    row_spec = pl.BlockSpec((tm, D_MODEL), lambda i, pos_ref: (i, 0))
    vec_spec = pl.BlockSpec((1, D_MODEL), lambda i, pos_ref: (0, 0))
    return pl.pallas_call(
        _combine_body,
        grid_spec=pltpu.PrefetchScalarGridSpec(
            num_scalar_prefetch=1,
            grid=(N_TOK // tm,),
            in_specs=[row_spec, pl.BlockSpec((tm, LANES), lambda i, pos_ref: (i, 0)),
                      pl.BlockSpec((1, 1, D_MODEL), lambda i, pos_ref: (_mod_group(i * tm), 0, 0)),
                      vec_spec, vec_spec, pl.BlockSpec(memory_space=pl.ANY)],
            out_specs=row_spec,
            scratch_shapes=[pltpu.VMEM((2, 2, tm, D_MODEL), F32), pltpu.SemaphoreType.DMA((2,))],
        ),
        out_shape=jax.ShapeDtypeStruct((N_TOK, D_MODEL), F32),
        compiler_params=_cparams(("arbitrary",)),
        name="moe_combine_ln",
    )(pos, x, route, gate, ln_g, ln_b, z)


def _od_post_body(q_ref, k_ref, xs_ref, bc_ref, dt_ref, cos_ref, sin_ref, cw_ref, cb_ref, dtb_ref,
                  qr_ref, kr_ref, xc_ref, bcc_ref, dtp_ref):
    cos = cos_ref[...]
    sin = sin_ref[...]
    qr_ref[...] = _rope(q_ref[...], cos, sin, GQA_HEAD_DIM // 4).astype(BF16)
    kr_ref[...] = _rope(k_ref[...], cos, sin, GQA_HEAD_DIM // 4).astype(BF16)

    def conv_silu(x, w, b):
        rows = x.shape[0]
        r = lax.broadcasted_iota(jnp.int32, x.shape, 0)
        prev = jnp.where(r == 0, 0.0, pltpu.roll(x, 1, 0))
        nxt = jnp.where(r == rows - 1, 0.0, pltpu.roll(x, rows - 1, 0))
        return _silu(prev * w[0:1] + x * w[1:2] + nxt * w[2:3] + b)

    xc_ref[...] = conv_silu(xs_ref[...], cw_ref[:, :SSD_D_INNER], cb_ref[:, :SSD_D_INNER])
    bcc_ref[...] = conv_silu(bc_ref[...], cw_ref[:, SSD_D_INNER:], cb_ref[:, SSD_D_INNER:])
    t = dt_ref[...] + dtb_ref[...]
    dtp_ref[...] = jnp.maximum(t, 0.0) + jnp.log1p(jnp.exp(-jnp.abs(t)))


def _od_post(proj, cos, sin, conv_w, conv_b, dt_bias, rows, row_off, n_rows):
    off = row_off // rows
    blk = lambda w, c: pl.BlockSpec((rows, w), lambda i: (off + i, c))
    full = lambda a: pl.BlockSpec(a.shape, lambda i: (0, 0))
    oblk = lambda w: pl.BlockSpec((rows, w), lambda i: (i, 0))
    return pl.pallas_call(
        _od_post_body,
        grid=(n_rows // rows,),
        in_specs=[blk(1024, 0), blk(256, 12), blk(1024, 1), blk(512, 7), blk(LANES, 32),
                  blk(LANES, 0), blk(LANES, 0), full(conv_w), full(conv_b), full(dt_bias)],
        out_specs=[oblk(1024), oblk(256), oblk(1024), oblk(512), oblk(LANES)],
        out_shape=[
            jax.ShapeDtypeStruct((n_rows, 1024), BF16),
            jax.ShapeDtypeStruct((n_rows, 256), BF16),
            jax.ShapeDtypeStruct((n_rows, 1024), F32),
            jax.ShapeDtypeStruct((n_rows, 512), F32),
            jax.ShapeDtypeStruct((n_rows, LANES), F32),
        ],
        compiler_params=_cparams(("parallel",)),
        name="odd_post",
    )(proj, proj, proj, proj, proj, cos, sin, conv_w, conv_b, dt_bias)


def _sink_column(sink_ref, kh, rows):
    g = GQA_HEADS // GQA_KV_HEADS
    return jnp.concatenate(
        [jnp.broadcast_to(sink_ref[kh * g + j:kh * g + j + 1, 0:1], (rows, 1)) for j in range(g)], axis=0)


def _gqa_prompt_body(q_ref, k_ref, v_ref, sink_ref, o_ref):
    g = GQA_HEADS // GQA_KV_HEADS
    d = GQA_HEAD_DIM
    rows = q_ref.shape[1]
    scale = d ** -0.5
    outs = []
    for kh in range(GQA_KV_HEADS):
        qg = jnp.concatenate([q_ref[0, :, (kh * g + j) * d:(kh * g + j + 1) * d] for j in range(g)], axis=0)
        k = k_ref[0, :, kh * d:(kh + 1) * d]
        v = v_ref[0, :, kh * d:(kh + 1) * d]
        s = _nt(qg, k) * scale
        snk = _sink_column(sink_ref, kh, rows)
        m = jnp.maximum(jnp.max(s, axis=-1, keepdims=True), snk)
        e = jnp.exp(s - m)
        l = jnp.sum(e, axis=-1, keepdims=True) + jnp.exp(snk - m)
        o = jnp.dot(e.astype(BF16), v, preferred_element_type=F32) / l
        outs.extend(o[j * rows:(j + 1) * rows] for j in range(g))
    o_ref[0] = jnp.concatenate(outs, axis=1).astype(BF16)


def _gqa_prompt(q, k, v, sink):
    b, s, _ = q.shape
    spec = lambda w: pl.BlockSpec((1, s, w), lambda i: (i, 0, 0))
    return pl.pallas_call(
        _gqa_prompt_body,
        grid=(b,),
        in_specs=[spec(1024), spec(256), spec(256), pl.BlockSpec((8, LANES), lambda i: (0, 0))],
        out_specs=spec(1024),
        out_shape=jax.ShapeDtypeStruct((b, s, 1024), BF16),
        compiler_params=_cparams(("parallel",)),
        name="gqa_dense",
    )(q, k, v, sink)


def _gqa_window_body(q_ref, k_ref, v_ref, kc_ref, vc_ref, sink_ref, o_ref):
    g = GQA_HEADS // GQA_KV_HEADS
    d = GQA_HEAD_DIM
    blk = q_ref.shape[1]
    span = 3 * blk
    n = pl.program_id(1)
    start = pl.multiple_of(jnp.clip((n - 1) * blk, 0, DEC_SEQ - span), blk)
    scale = d ** -0.5
    qpos = n * blk + lax.broadcasted_iota(jnp.int32, (blk, span), 0)
    kpos = start + lax.broadcasted_iota(jnp.int32, (blk, span), 1)
    valid1 = jnp.abs(qpos - kpos) <= WINDOW
    valid = jnp.concatenate([valid1] * g, axis=0)
    outs = []
    for kh in range(GQA_KV_HEADS):
        qg = jnp.concatenate([q_ref[0, :, (kh * g + j) * d:(kh * g + j + 1) * d] for j in range(g)], axis=0)
        kl = k_ref[0, pl.ds(start, span), kh * d:(kh + 1) * d]
        vl = v_ref[0, pl.ds(start, span), kh * d:(kh + 1) * d]
        kc = kc_ref[0, :, kh * d:(kh + 1) * d]
        vc = vc_ref[0, :, kh * d:(kh + 1) * d]
        sl = jnp.where(valid, _nt(qg, kl) * scale, -jnp.inf)
        sc = _nt(qg, kc) * scale
        snk = _sink_column(sink_ref, kh, blk)
        m = jnp.maximum(jnp.maximum(jnp.max(sl, axis=-1, keepdims=True), jnp.max(sc, axis=-1, keepdims=True)), snk)
        el = jnp.exp(sl - m)
        ec = jnp.exp(sc - m)
        l = jnp.sum(el, axis=-1, keepdims=True) + jnp.sum(ec, axis=-1, keepdims=True) + jnp.exp(snk - m)
        o = (jnp.dot(el.astype(BF16), vl, preferred_element_type=F32)
             + jnp.dot(ec.astype(BF16), vc, preferred_element_type=F32)) / l
        outs.extend(o[j * blk:(j + 1) * blk] for j in range(g))
    o_ref[0] = jnp.concatenate(outs, axis=1).astype(BF16)


def _gqa_window(q, k, v, kc, vc, sink):
    b, s, _ = q.shape
    blk = WINDOW
    full = lambda a: pl.BlockSpec((1,) + a.shape[1:], lambda i, j: (i, 0, 0))
    return pl.pallas_call(
        _gqa_window_body,
        grid=(b, s // blk),
        in_specs=[pl.BlockSpec((1, blk, 1024), lambda i, j: (i, j, 0)), full(k), full(v), full(kc), full(vc),
                  pl.BlockSpec((8, LANES), lambda i, j: (0, 0))],
        out_specs=pl.BlockSpec((1, blk, 1024), lambda i, j: (i, j, 0)),
        out_shape=jax.ShapeDtypeStruct((b, s, 1024), BF16),
        compiler_params=_cparams(("parallel", "parallel")),
        name="gqa_window",
    )(q, k, v, kc, vc, sink)


def _ssd_body(nc, xs_ref, bc_ref, dt_ref, alog_ref, h0_ref, y_ref, st_ref):
    d = pl.program_id(1)
    c = pl.program_id(2)
    q = SSD_CHUNK
    hpg = SSD_HEADS // SSD_GROUPS

    @pl.when(c == 0)
    def _():
        st_ref[0, 0] = h0_ref[0, 0]

    row = lax.broadcasted_iota(jnp.int32, (q, LANES), 0)
    a = -jnp.exp(alog_ref[...])
    dta = dt_ref[...] * a

    def scan(x, rev):
        k = 1
        while k < q:
            if rev:
                x = x + jnp.where(row < q - k, pltpu.roll(x, q - k, 0), 0.0)
            else:
                x = x + jnp.where(row >= k, pltpu.roll(x, k, 0), 0.0)
            k *= 2
        return x

    li = lax.broadcasted_iota(jnp.int32, (q, q), 0)
    si = lax.broadcasted_iota(jnp.int32, (q, q), 1)

    def run(rev):
        cum = scan(dta, rev)
        cum_t = cum.T
        edge = cum[0:1, :] if rev else cum[q - 1:q, :]
        mask = (li <= si) if rev else (li >= si)
        xs = xs_ref[...]
        dt = dt_ref[...]
        ys = []
        for g in range(SSD_GROUPS):
            bm = bc_ref[:, g * SSD_STATE:(g + 1) * SSD_STATE]
            cm = bc_ref[:, (SSD_GROUPS + g) * SSD_STATE:(SSD_GROUPS + g + 1) * SSD_STATE]
            bm16 = bm.astype(BF16)
            cm16 = cm.astype(BF16)
            cb = _nt(cm16, bm16)
            for hh in range(hpg):
                h = g * hpg + hh
                j = (SSD_HEADS if rev else 0) + h
                cum_col = cum[:, j:j + 1]
                seg = cum_col - cum_t[j:j + 1, :]
                decay = jnp.exp(jnp.where(mask, seg, -jnp.inf))
                xdt = xs[:, h * SSD_HEAD_DIM:(h + 1) * SSD_HEAD_DIM] * dt[:, j:j + 1]
                y_diag = jnp.dot((cb * decay).astype(BF16), xdt.astype(BF16), preferred_element_type=F32)
                tail = jnp.exp(edge[:, j:j + 1] - cum_col)
                state = lax.dot_general((xdt * tail).astype(BF16), bm16, (((0,), (0,)), ((), ())),
                                        preferred_element_type=F32)
                h_prev = st_ref[0, 0, h]
                y_off = _nt(cm16, h_prev.astype(BF16)) * jnp.exp(cum_col)
                ys.append(y_diag + y_off)
                st_ref[0, 0, h] = h_prev * jnp.exp(edge[:, j:j + 1]) + state
        y_ref[0] = jnp.concatenate(ys, axis=1)

    @pl.when(d == 0)
    def _():
        run(False)

    @pl.when(d == 1)
    def _():
        run(True)


def _ssd(xs, bc, dtp, a_log, h0, seq):
    t = xs.shape[0]
    b = t // seq
    nc = seq // SSD_CHUNK

    def rmap(i, d, c):
        return i * nc + jnp.where(d == 0, c, nc - 1 - c)

    blk = lambda w: pl.BlockSpec((SSD_CHUNK, w), lambda i, d, c: (rmap(i, d, c), 0))
    st_spec = pl.BlockSpec((1, 1, SSD_HEADS, SSD_HEAD_DIM, SSD_STATE), lambda i, d, c: (d, i, 0, 0, 0))
    return pl.pallas_call(
        functools.partial(_ssd_body, nc),
        grid=(b, 2, nc),
        in_specs=[blk(1024), blk(512), blk(LANES), pl.BlockSpec((1, LANES), lambda i, d, c: (0, 0)), st_spec],
        out_specs=[pl.BlockSpec((1, SSD_CHUNK, 1024), lambda i, d, c: (d, rmap(i, d, c), 0)), st_spec],
        out_shape=[jax.ShapeDtypeStruct((2, t, 1024), F32),
                   jax.ShapeDtypeStruct((2, b, SSD_HEADS, SSD_HEAD_DIM, SSD_STATE), F32)],
        compiler_params=_cparams(("parallel", "arbitrary", "arbitrary")),
        name="ssd_scan",
    )(xs, bc, dtp, a_log, h0)


def _ssd_gate_body(y_ref, xs_ref, z_ref, dsk_ref, g_ref, o_ref):
    y = (y_ref[0] + y_ref[1] + xs_ref[...] * dsk_ref[...]) * _silu(z_ref[...])
    w = SSD_D_INNER // SSD_GROUPS
    outs = [_rms(y[:, g * w:(g + 1) * w], g_ref[:, g * w:(g + 1) * w]) for g in range(SSD_GROUPS)]
    o_ref[...] = jnp.concatenate(outs, axis=1).astype(BF16)


def _ssd_gate(y2, xs, proj, d_skip, norm_g):
    tm = 512
    return pl.pallas_call(
        _ssd_gate_body,
        grid=(N_TOK // tm,),
        in_specs=[pl.BlockSpec((2, tm, 1024), lambda i: (0, i, 0)),
                  pl.BlockSpec((tm, 1024), lambda i: (i, 0)),
                  pl.BlockSpec((tm, 1024), lambda i: (i, 2)),
                  pl.BlockSpec((1, 1024), lambda i: (0, 0)),
                  pl.BlockSpec((1, 1024), lambda i: (0, 0))],
        out_specs=pl.BlockSpec((tm, 1024), lambda i: (i, 0)),
        out_shape=jax.ShapeDtypeStruct((N_TOK, 1024), BF16),
        compiler_params=_cparams(("parallel",)),
        name="ssd_gate_norm",
    )(y2, xs, proj, d_skip, norm_g)


def _even_layer(x, mod, layer_idx, p, ctx, tables):
    w_in, q_norm, kv_norm, wq_b, wkv_b, lam, subln, w_out = p
    ctx_ckv, ctx_kpe, ctx_dk, ctx_dv = ctx
    cos, sin = tables
    lam_init = 0.8 - 0.6 * math.exp(-0.3 * layer_idx)

    w_r = jnp.concatenate([w_in[:, :1024], w_in[:, 1088:], w_in[:, 1024:1088],
                           jnp.zeros((D_MODEL, PROJ_N - w_in.shape[1]), F32)], axis=1).astype(BF16)
    proj = _proj(x, mod[0], mod[1], w_r)
    qn, ckv, dq, dk, kpe = _ev_post(proj, cos, sin, q_norm[None], kv_norm[None])
    dv = proj[:, 3072:4096]

    wq = wq_b.reshape(MLA_Q_RANK, MLA_HEADS, MLA_NOPE + MLA_ROPE)
    wq_r = jnp.pad(wq, ((0, 0), (0, 0), (0, MLA_ROPE))).reshape(MLA_Q_RANK, -1).astype(BF16)
    q = _mm(qn, wq_r, BF16)

    ckv_rows = jnp.concatenate([ckv.astype(BF16), ctx_ckv.reshape(-1, MLA_KV_RANK).astype(BF16)], axis=0)
    kv = _mm(ckv_rows, wkv_b.astype(BF16), BF16)

    def per_seq(a, b, s):
        return a.reshape(b, s, a.shape[-1])

    pr = slice(0, N_PROMPT)
    sm = slice(N_PROMPT, N_TOK)
    lam_args = (lam, subln[None])
    o1p, o2p = _even_attn(lam_init, per_seq(q[pr], BATCH, SEQ), cos, sin, per_seq(kv[pr], BATCH, SEQ),
                          per_seq(kpe[pr].astype(BF16), BATCH, SEQ), per_seq(dq[pr], BATCH, SEQ),
                          per_seq(dk[pr].astype(BF16), BATCH, SEQ), per_seq(dv[pr].astype(BF16), BATCH, SEQ),
                          *lam_args, row_off=0)

    def with_ctx(new, old):
        return jnp.concatenate([per_seq(new, DEC_BATCH, DEC_SEQ).astype(BF16), old.astype(BF16)], axis=1)

    kv_s = with_ctx(kv[sm], per_seq(kv[N_TOK:], DEC_BATCH, PAST_LEN))
    kpe_ctx = jnp.pad(ctx_kpe, ((0, 0), (0, 0), (0, LANES - MLA_ROPE)))
    kpe_s = with_ctx(kpe[sm], kpe_ctx)
    dk_s = with_ctx(dk[sm], ctx_dk.reshape(DEC_BATCH, PAST_LEN, -1))
    dv_s = with_ctx(dv[sm], ctx_dv.reshape(DEC_BATCH, PAST_LEN, -1))
    o1s, o2s = _even_attn(lam_init, per_seq(q[sm], DEC_BATCH, DEC_SEQ), cos, sin, kv_s, kpe_s,
                          per_seq(dq[sm], DEC_BATCH, DEC_SEQ), dk_s, dv_s, *lam_args, row_off=N_PROMPT)

    a1 = jnp.concatenate([o1p.reshape(N_PROMPT, -1), o1s.reshape(N_SAMPLE, -1)], axis=0)
    a2 = jnp.concatenate([o2p.reshape(N_PROMPT, -1), o2s.reshape(N_SAMPLE, -1)], axis=0)
    cache = (ckv[pr].reshape(BATCH, SEQ, MLA_KV_RANK),
             kpe[pr, :MLA_ROPE].reshape(BATCH, SEQ, MLA_ROPE),
             dk[pr].reshape(BATCH, SEQ, DIFF_HEADS, 2 * DIFF_QK),
             dv[pr].reshape(BATCH, SEQ, DIFF_HEADS, DIFF_V))
    return a1, a2, w_out.astype(BF16), cache


def _odd_layer(x, mod, p, ctx, tables):
    w_in, sink, conv_w, conv_b, dt_bias, a_log, d_skip, norm_g, w_out = p
    k_ctx, v_ctx, h0_f, h0_b = ctx
    cos, sin = tables

    w_r = jnp.concatenate([w_in[:, :1024], w_in[:, 2560:3584], w_in[:, 1536:2560], w_in[:, 1024:1536],
                           w_in[:, 3584:], jnp.zeros((D_MODEL, PROJ_N - w_in.shape[1]), F32)],
                          axis=1).astype(BF16)
    proj = _proj(x, mod[0], mod[1], w_r)
    cw = conv_w
    cb = conv_b[None]
    dtb = jnp.pad(dt_bias.reshape(1, -1), ((0, 0), (0, LANES - 2 * SSD_HEADS)))
    alog = jnp.pad(a_log.reshape(1, -1), ((0, 0), (0, LANES - 2 * SSD_HEADS)))
    post_p = _od_post(proj, cos, sin, cw, cb, dtb, SEQ, 0, N_PROMPT)
    post_s = _od_post(proj, cos, sin, cw, cb, dtb, DEC_SEQ, N_PROMPT, N_SAMPLE)
    qp, kp, xs_p, bc_p, dt_p = post_p
    qs, ks, xs_s, bc_s, dt_s = post_s
    k_raw = proj[:, 3072:3328]
    v_raw = proj[:, 3328:3584]
    sink_b = jnp.broadcast_to(sink[:, None], (GQA_HEADS, LANES))

    o_att_p = _gqa_prompt(qp.reshape(BATCH, SEQ, -1), kp.reshape(BATCH, SEQ, -1),
                          v_raw[:N_PROMPT].astype(BF16).reshape(BATCH, SEQ, -1), sink_b)
    o_att_s = _gqa_window(qs.reshape(DEC_BATCH, DEC_SEQ, -1), ks.reshape(DEC_BATCH, DEC_SEQ, -1),
                          v_raw[N_PROMPT:].astype(BF16).reshape(DEC_BATCH, DEC_SEQ, -1),
                          k_ctx.reshape(DEC_BATCH, PAST_LEN, -1).astype(BF16),
                          v_ctx.reshape(DEC_BATCH, PAST_LEN, -1).astype(BF16), sink_b)

    h0_p = jnp.zeros((2, BATCH, SSD_HEADS, SSD_HEAD_DIM, SSD_STATE), F32)
    y_p, st_p = _ssd(xs_p, bc_p, dt_p, alog, h0_p, SEQ)
    y_s, _ = _ssd(xs_s, bc_s, dt_s, alog, jnp.stack([h0_f, h0_b]), DEC_SEQ)
    y2 = jnp.concatenate([y_p, y_s], axis=1)
    xs_all = jnp.concatenate([xs_p, xs_s], axis=0)
    dsk = jnp.repeat(d_skip, SSD_HEAD_DIM)[None]
    y_n = _ssd_gate(y2, xs_all, proj, dsk, norm_g[None])

    a1 = jnp.concatenate([o_att_p.reshape(N_PROMPT, -1), o_att_s.reshape(N_SAMPLE, -1)], axis=0)
    cache = (k_raw[:N_PROMPT].reshape(BATCH, SEQ, GQA_KV_HEADS, GQA_HEAD_DIM),
             v_raw[:N_PROMPT].reshape(BATCH, SEQ, GQA_KV_HEADS, GQA_HEAD_DIM),
             st_p[0], st_p[1])
    return a1, y_n, w_out.astype(BF16), cache


def kernel(x_prompt, x_sample, cache_mla_ckv, cache_mla_kpe, cache_diff_k, cache_diff_v, cache_gqa_k, cache_gqa_v, state_ssd_fwd, state_ssd_bwd, c, c_ctx, w_mod, b_mod, ln1_g, ln1_b, ln2_g, ln2_b, ev_w_in, mla_q_norm, mla_kv_norm, mla_wq_b, mla_wkv_b, diff_lambda, diff_subln, ev_w_out, od_w_in, gqa_sink, ssd_conv_w, ssd_conv_b, ssd_dt_bias, ssd_a_log, ssd_d, ssd_norm, od_w_out, moe_router_group, moe_router_expert, moe_w_gate, moe_w_up, moe_w_down):
    x = jnp.concatenate([x_prompt.reshape(N_PROMPT, D_MODEL), x_sample.reshape(N_SAMPLE, D_MODEL)], axis=0)
    cs = jnp.concatenate([c_ctx[None], c, jnp.zeros((8 - 1 - DEC_BATCH, D_MODEL), F32)], axis=0)
    mods = _adaln(cs, w_mod, b_mod)[:, :1 + DEC_BATCH].reshape(DEPTH, 1 + DEC_BATCH, 6, 1, D_MODEL)
    tables_64 = _rope_tables(64)
    tables_128 = _rope_tables(GQA_HEAD_DIM)

    ev_caches, od_caches = [], []
    for l in range(DEPTH):
        i = l // 2
        mod = [mods[l, :, k] for k in range(6)]
        if l % 2 == 0:
            p = (ev_w_in[i], mla_q_norm[i], mla_kv_norm[i], mla_wq_b[i], mla_wkv_b[i], diff_lambda[i],
                 diff_subln[i], ev_w_out[i])
            ctx = (cache_mla_ckv[:, i], cache_mla_kpe[:, i], cache_diff_k[:, i], cache_diff_v[:, i])
            a1, a2, w_out, cache = _even_layer(x, mod, l, p, ctx, tables_64)
            ev_caches.append(cache)
        else:
            p = (od_w_in[i], gqa_sink[i], ssd_conv_w[i], ssd_conv_b[i], ssd_dt_bias[i], ssd_a_log[i], ssd_d[i],
                 ssd_norm[i], od_w_out[i])
            ctx = (cache_gqa_k[:, i], cache_gqa_v[:, i], state_ssd_fwd[:, i], state_ssd_bwd[:, i])
            a1, a2, w_out, cache = _odd_layer(x, mod, p, ctx, tables_128)
            od_caches.append(cache)
        w_router = jnp.concatenate([moe_router_group[l], moe_router_expert[l],
                                    jnp.zeros((D_MODEL, LANES - MOE_GROUPS - MOE_EXPERTS), F32)], axis=1)
        w_router_hi = w_router.astype(BF16)
        w_router = jnp.stack([w_router_hi, (w_router - w_router_hi.astype(F32)).astype(BF16)])
        x, h2, route = _out_proj(a1, a2, w_out, x, mod[2], ln1_g[l][None], ln1_b[l][None], mod[3], mod[4],
                                 w_router)
        pos, tile_expert, n_used, tail = _route_meta(route)
        xs = _dispatch(h2, pos, tail, n_used)
        z = _moe_experts(l, xs, tile_expert, n_used, moe_w_gate, moe_w_up, moe_w_down)
        x = _combine(x, z, route, pos, mod[5], ln2_g[l][None], ln2_b[l][None])

    y_prompt = x[:N_PROMPT].reshape(BATCH, SEQ, D_MODEL)
    y_sample = x[N_PROMPT:].reshape(DEC_BATCH, DEC_SEQ, D_MODEL)
    stack = lambda caches, k: jnp.stack([t[k] for t in caches], axis=1)
    return (y_prompt, y_sample, stack(ev_caches, 0), stack(ev_caches, 1), stack(ev_caches, 2), stack(ev_caches, 3),
            stack(od_caches, 0), stack(od_caches, 1), stack(od_caches, 2), stack(od_caches, 3))
```

```python
import functools
import math

import jax
import jax.numpy as jnp
from jax import lax
from jax.experimental import pallas as pl
from jax.experimental.pallas import tpu as pltpu

F32 = jnp.float32
BF16 = jnp.bfloat16

D_MODEL = 2048
BATCH = 16
SEQ = 256
DEPTH = 4
DEC_BATCH = 2
DEC_SEQ = 1024
PAST_LEN = 512
GRID_W = 64
ROPE_THETA = 10000.0
DEEPNORM_ALPHA = (2.0 * DEPTH) ** 0.25
LN_EPS = 1e-5
RMS_EPS = 1e-6

MLA_HEADS = 8
MLA_Q_RANK = 512
MLA_KV_RANK = 512
MLA_NOPE = 128
MLA_ROPE = 64
MLA_V = 128
DIFF_HEADS = 8
DIFF_QK = 64
DIFF_V = 128
GQA_HEADS = 8
GQA_KV_HEADS = 2
GQA_HEAD_DIM = 128
WINDOW = 128
SSD_HEADS = 16
SSD_HEAD_DIM = 64
SSD_D_INNER = SSD_HEADS * SSD_HEAD_DIM
SSD_GROUPS = 2
SSD_STATE = 128
SSD_CHUNK = 128
MOE_GROUPS = 4
MOE_EPG = 4
MOE_EXPERTS = 16
MOE_HIDDEN = 512

N_PROMPT = BATCH * SEQ
N_SAMPLE = DEC_BATCH * DEC_SEQ
N_TOK = N_PROMPT + N_SAMPLE
PROJ_N = 4224
LANES = 128
VMEM_LIMIT = 56 * 1024 * 1024

PROJ_TM, PROJ_TN = 1024, 1408
OUT_TM = 512
ATT_TQ = 256
POST_TM = 256
MOE_TM = 256
MOE_SLOTS = 2 * N_TOK
MOE_TILES = MOE_SLOTS // MOE_TM + MOE_EXPERTS
MOE_ROWS = MOE_TILES * MOE_TM


def _cparams(sem):
    return pltpu.CompilerParams(dimension_semantics=sem, vmem_limit_bytes=VMEM_LIMIT)


def _mod_group(row0):
    return jnp.maximum(row0 // DEC_SEQ - (N_PROMPT // DEC_SEQ - 1), 0)


def _silu(x):
    return x / (1.0 + jnp.exp(-x))


def _nt(a, b):
    return lax.dot_general(a, b, (((1,), (1,)), ((), ())), preferred_element_type=F32)


def _rms(x, g):
    return x * lax.rsqrt(jnp.mean(x * x, axis=-1, keepdims=True) + RMS_EPS) * g


def _layer_norm(y, g, b):
    mu = jnp.mean(y, axis=-1, keepdims=True)
    yc = y - mu
    return yc * lax.rsqrt(jnp.mean(yc * yc, axis=-1, keepdims=True) + LN_EPS) * g + b


def _adaln_body(c_ref, w_ref, b_ref, o_ref):
    a = _silu(c_ref[...])
    o_ref[0] = jnp.dot(a, w_ref[0], preferred_element_type=F32) + b_ref[0]


def _adaln(cs, w_mod, b_mod):
    tn = 1024
    n = w_mod.shape[-1]
    return pl.pallas_call(
        _adaln_body,
        grid=(DEPTH, n // tn),
        in_specs=[
            pl.BlockSpec((8, D_MODEL), lambda l, j: (0, 0)),
            pl.BlockSpec((1, D_MODEL, tn), lambda l, j: (l, 0, j)),
            pl.BlockSpec((1, 1, tn), lambda l, j: (l, 0, j)),
        ],
        out_specs=pl.BlockSpec((1, 8, tn), lambda l, j: (l, 0, j)),
        out_shape=jax.ShapeDtypeStruct((DEPTH, 8, n), F32),
        compiler_params=_cparams(("parallel", "parallel")),
        name="adaln",
    )(cs, w_mod, b_mod.reshape(DEPTH, 1, n))


def _proj_body(x_ref, sh_ref, sc_ref, w_ref, o_ref, h_scr):
    @pl.when(pl.program_id(1) == 0)
    def _():
        h = x_ref[...] * (1.0 + sc_ref[0]) + sh_ref[0]
        h_scr[...] = h.astype(BF16)

    o_ref[...] = jnp.dot(h_scr[...], w_ref[...], preferred_element_type=F32)


def _proj(x, shift, scale, w):
    tm, tn = PROJ_TM, PROJ_TN
    n = w.shape[1]
    mod_spec = pl.BlockSpec((1, 1, D_MODEL), lambda i, j: (_mod_group(i * tm), 0, 0))
    return pl.pallas_call(
        _proj_body,
        grid=(N_TOK // tm, n // tn),
        in_specs=[
            pl.BlockSpec((tm, D_MODEL), lambda i, j: (i, 0)),
            mod_spec,
            mod_spec,
            pl.BlockSpec((D_MODEL, tn), lambda i, j: (0, j)),
        ],
        out_specs=pl.BlockSpec((tm, tn), lambda i, j: (i, j)),
        out_shape=jax.ShapeDtypeStruct((N_TOK, n), F32),
        scratch_shapes=[pltpu.VMEM((tm, D_MODEL), BF16)],
        compiler_params=_cparams(("parallel", "arbitrary")),
        name="in_proj",
    )(x, shift, scale, w)


def _mm_body(a_ref, b_ref, o_ref):
    o_ref[...] = jnp.dot(a_ref[...].astype(BF16), b_ref[...], preferred_element_type=F32).astype(o_ref.dtype)


def _mm(a, b, out_dtype, tm, n_tiles, row_block=lambda i: i):
    k = a.shape[1]
    n = b.shape[1]
    return pl.pallas_call(
        _mm_body,
        grid=(n_tiles,),
        in_specs=[pl.BlockSpec((tm, k), lambda i: (row_block(i), 0)), pl.BlockSpec((k, n), lambda i: (0, 0))],
        out_specs=pl.BlockSpec((tm, n), lambda i: (i, 0)),
        out_shape=jax.ShapeDtypeStruct((n_tiles * tm, n), out_dtype),
        compiler_params=_cparams(("parallel",)),
        name="mm_resident",
    )(a, b)


def _rope(x, cos, sin_signed, quarter):
    rows, w = x.shape
    reps = w // LANES
    if reps > 1:
        cos = jnp.concatenate([cos] * reps, axis=1)
        sin_signed = jnp.concatenate([sin_signed] * reps, axis=1)
    lane = lax.broadcasted_iota(jnp.int32, x.shape, 1)
    first = (lane % (2 * quarter)) < quarter
    up = pltpu.roll(x, w - quarter, 1)
    dn = pltpu.roll(x, quarter, 1)
    return x * cos + jnp.where(first, up, dn) * sin_signed


def _rope_tables(rdim):
    half = rdim // 2
    quarter = half // 2
    pos = jnp.arange(DEC_SEQ)
    row = (pos // GRID_W).astype(F32)
    col = (pos % GRID_W).astype(F32)
    inv = ROPE_THETA ** (-jnp.arange(quarter, dtype=F32) * 2.0 / half)
    lane = jnp.arange(LANES)
    r = lane % rdim
    use_col = (r // half) == 1
    j = r % quarter
    p = jnp.where(use_col[None, :], col[:, None], row[:, None])
    ang = p * inv[j][None, :]
    sign = jnp.where((r % half) < quarter, -1.0, 1.0).astype(F32)
    cos = jnp.cos(ang)
    sin = jnp.sin(ang) * sign[None, :]
    cos = jnp.concatenate([jnp.ones((N_PROMPT, LANES), F32), cos, cos], axis=0)
    sin = jnp.concatenate([jnp.zeros((N_PROMPT, LANES), F32), sin, sin], axis=0)
    return cos, sin


def _ev_post_body(n_prev, ql_ref, kvl_ref, dq_ref, dk_ref, dv_ref, kpe_ref, cos_ref, sin_ref, gq_ref, gkv_ref,
                  *refs):
    prev = refs[:4 * n_prev]
    qn_ref, ckv_ref, dqr_ref, dkr_ref, dvb_ref, kper_ref, c_ckv, c_kpe, c_dk, c_dv = refs[4 * n_prev:]
    cos = cos_ref[...]
    sin = sin_ref[...]
    qn_ref[...] = _rms(ql_ref[...], gq_ref[...]).astype(BF16)
    ckv = _rms(kvl_ref[...], gkv_ref[...])
    ckv_ref[...] = ckv.astype(BF16)
    dqr_ref[...] = _rope(dq_ref[...], cos, sin, DIFF_QK // 4).astype(BF16)
    dk = _rope(dk_ref[...], cos, sin, DIFF_QK // 4)
    dkr_ref[...] = dk.astype(BF16)
    dv = dv_ref[...]
    dvb_ref[...] = dv.astype(BF16)
    kpe = _rope(kpe_ref[...], cos, sin, MLA_ROPE // 4)
    kper_ref[...] = kpe.astype(BF16)

    @pl.when(pl.program_id(0) < BATCH)
    def _():
        if n_prev:
            c_ckv[0, 0] = prev[0][0, 0]
            c_kpe[0, 0] = prev[1][0, 0]
            c_dk[0, 0] = prev[2][0, 0]
            c_dv[0, 0] = prev[3][0, 0]
        c_ckv[0, n_prev] = ckv
        c_kpe[0, n_prev] = kpe[:, :MLA_ROPE]
        c_dk[0, n_prev] = dk.reshape(SEQ, DIFF_HEADS, 2 * DIFF_QK)
        c_dv[0, n_prev] = dv.reshape(SEQ, DIFF_HEADS, DIFF_V)


def _ev_post(proj, cos, sin, gq, gkv, prev):
    tm = POST_TM
    n_prev = 1 if prev else 0
    nl = n_prev + 1
    row = lambda c: (lambda i: (i, c))
    seq4 = lambda i: (jnp.minimum(i, BATCH - 1), 0, 0, 0)
    seq5 = lambda i: (jnp.minimum(i, BATCH - 1), 0, 0, 0, 0)
    cache_shapes = [(MLA_KV_RANK,), (MLA_ROPE,), (DIFF_HEADS, 2 * DIFF_QK), (DIFF_HEADS, DIFF_V)]

    def cache_spec(layers, tail):
        return pl.BlockSpec((1, layers, SEQ) + tail, seq4 if len(tail) == 1 else seq5)

    return pl.pallas_call(
        functools.partial(_ev_post_body, n_prev),
        grid=(N_TOK // tm,),
        in_specs=[
            pl.BlockSpec((tm, 512), row(0)),
            pl.BlockSpec((tm, 512), row(1)),
            pl.BlockSpec((tm, 1024), row(1)),
            pl.BlockSpec((tm, 1024), row(2)),
            pl.BlockSpec((tm, 1024), row(3)),
            pl.BlockSpec((tm, LANES), row(32)),
            pl.BlockSpec((tm, LANES), row(0)),
            pl.BlockSpec((tm, LANES), row(0)),
            pl.BlockSpec((1, 512), lambda i: (0, 0)),
            pl.BlockSpec((1, 512), lambda i: (0, 0)),
        ] + [cache_spec(1, t) for t in cache_shapes] * n_prev,
        out_specs=[
            pl.BlockSpec((tm, 512), row(0)),
            pl.BlockSpec((tm, 512), row(0)),
            pl.BlockSpec((tm, 1024), row(0)),
            pl.BlockSpec((tm, 1024), row(0)),
            pl.BlockSpec((tm, 1024), row(0)),
            pl.BlockSpec((tm, LANES), row(0)),
        ] + [cache_spec(nl, t) for t in cache_shapes],
        out_shape=[
            jax.ShapeDtypeStruct((N_TOK, 512), BF16),
            jax.ShapeDtypeStruct((N_TOK, 512), BF16),
            jax.ShapeDtypeStruct((N_TOK, 1024), BF16),
            jax.ShapeDtypeStruct((N_TOK, 1024), BF16),
            jax.ShapeDtypeStruct((N_TOK, 1024), BF16),
            jax.ShapeDtypeStruct((N_TOK, LANES), BF16),
        ] + [jax.ShapeDtypeStruct((BATCH, nl, SEQ) + t, F32) for t in cache_shapes],
        compiler_params=_cparams(("arbitrary",)),
        name="even_post",
    )(proj, proj, proj, proj, proj, proj, cos, sin, gq, gkv, *prev)


def _softmax_av(scores, values):
    m = functools.reduce(jnp.maximum, [jnp.max(s, axis=-1, keepdims=True) for s in scores])
    es = [jnp.exp(s - m) for s in scores]
    l = sum(jnp.sum(e, axis=-1, keepdims=True) for e in es)
    o = sum(jnp.dot(e.astype(BF16), v, preferred_element_type=F32) for e, v in zip(es, values))
    return o / l


def _even_attn_body(lam_init, has_ctx, q_ref, cos_ref, sin_ref, kv_ref, kpe_ref, dq_ref, dk_ref, dv_ref,
                    lam_ref, sub_ref, *refs):
    if has_ctx:
        kvc_ref, kpec_ref, dkc_ref, dvc_ref, omla_ref, odiff_ref = refs
    else:
        omla_ref, odiff_ref = refs
    h = pl.program_id(2)
    q = q_ref[...]
    qn = q[:, :MLA_NOPE]
    qr = _rope(q[:, MLA_NOPE:].astype(F32), cos_ref[...], sin_ref[...], MLA_ROPE // 4).astype(BF16)
    mla_scale = (MLA_NOPE + MLA_ROPE) ** -0.5
    scores = [(_nt(qn, kv_ref[:, :MLA_NOPE]) + _nt(qr, kpe_ref[...])) * mla_scale]
    values = [kv_ref[:, MLA_NOPE:]]
    if has_ctx:
        kpec = kpec_ref[0, 0].astype(BF16)
        scores.append((_nt(qn, kvc_ref[:, :MLA_NOPE]) + _nt(qr[:, :MLA_ROPE], kpec)) * mla_scale)
        values.append(kvc_ref[:, MLA_NOPE:])
    omla_ref[...] = _softmax_av(scores, values).astype(BF16)

    lam = lam_ref[...]
    lam_full = (jnp.exp(jnp.sum(lam[0:1] * lam[1:2], axis=-1, keepdims=True))
                - jnp.exp(jnp.sum(lam[2:3] * lam[3:4], axis=-1, keepdims=True)) + lam_init)
    dq = dq_ref[...]
    lane = lax.broadcasted_iota(jnp.int32, dq.shape, 1)
    zero = jnp.zeros_like(dq)
    dq1 = jnp.where(lane < DIFF_QK, dq, zero)
    dq2 = jnp.where(lane >= DIFF_QK, dq, zero)
    keys = [dk_ref[...]]
    values = [dv_ref[...]]
    if has_ctx:
        keys.append(dkc_ref[0, 0, :, h, :].astype(BF16))
        values.append(dvc_ref[0, 0, :, h, :].astype(BF16))
    diff_scale = DIFF_QK ** -0.5
    a1 = _softmax_av([_nt(dq1, k) * diff_scale for k in keys], values)
    a2 = _softmax_av([_nt(dq2, k) * diff_scale for k in keys], values)
    odiff_ref[...] = (_rms(a1 - lam_full * a2, sub_ref[...]) * (1.0 - lam_init)).astype(BF16)


def _even_attn(lam_init, q, cos, sin, kv, kpe, dq, dk, dv, lam, subln, n_seq, seq, row_off, ctx=None):
    tq = ATT_TQ
    nq = seq // tq
    q0 = row_off // tq
    s0 = row_off // seq
    qspec = lambda w: pl.BlockSpec((tq, w), lambda b, j, h: (q0 + b * nq + j, h))
    kspec = lambda w: pl.BlockSpec((seq, w), lambda b, j, h: (s0 + b, h))
    tspec = pl.BlockSpec((tq, LANES), lambda b, j, h: (q0 + b * nq + j, 0))
    ospec = pl.BlockSpec((tq, LANES), lambda b, j, h: (b * nq + j, h))
    in_specs = [qspec(256), tspec, tspec, kspec(256),
                pl.BlockSpec((seq, LANES), lambda b, j, h: (s0 + b, 0)),
                qspec(LANES), kspec(LANES), kspec(LANES),
                pl.BlockSpec((4, DIFF_QK), lambda b, j, h: (0, 0)),
                pl.BlockSpec((1, DIFF_V), lambda b, j, h: (0, 0))]
    args = [q, cos, sin, kv, kpe, dq, dk, dv, lam, subln]
    if ctx is not None:
        layer, kv_ctx, c_kpe, c_dk, c_dv = ctx
        in_specs += [
            pl.BlockSpec((PAST_LEN, 256), lambda b, j, h: (b, h)),
            pl.BlockSpec((1, 1, PAST_LEN, MLA_ROPE), lambda b, j, h: (b, layer, 0, 0)),
            pl.BlockSpec((1, 1, PAST_LEN, DIFF_HEADS, 2 * DIFF_QK), lambda b, j, h: (b, layer, 0, 0, 0)),
            pl.BlockSpec((1, 1, PAST_LEN, DIFF_HEADS, DIFF_V), lambda b, j, h: (b, layer, 0, 0, 0)),
        ]
        args += [kv_ctx, c_kpe, c_dk, c_dv]
    return pl.pallas_call(
        functools.partial(_even_attn_body, lam_init, ctx is not None),
        grid=(n_seq, nq, MLA_HEADS),
        in_specs=in_specs,
        out_specs=[ospec, ospec],
        out_shape=[jax.ShapeDtypeStruct((n_seq * seq, 1024), BF16)] * 2,
        compiler_params=_cparams(("parallel", "parallel", "parallel")),
        name="even_attn",
    )(*args)


def _route(logits):
    lane = lax.broadcasted_iota(jnp.int32, logits.shape, 1).astype(F32)
    neg = -jnp.inf
    none = float(LANES)
    is_g = lane < MOE_GROUPS
    lg = jnp.where(is_g, logits, neg)
    mg = jnp.max(lg, axis=-1, keepdims=True)
    g_val = 1.0 / jnp.sum(jnp.exp(lg - mg), axis=-1, keepdims=True)
    g_idx = jnp.min(jnp.where(is_g & (lg == mg), lane, none), axis=-1, keepdims=True)
    e_lo = MOE_GROUPS + g_idx * MOE_EPG
    is_e = (lane >= e_lo) & (lane < e_lo + MOE_EPG)
    le = jnp.where(is_e, logits, neg)
    p = jnp.exp(le - jnp.max(le, axis=-1, keepdims=True))
    p1 = jnp.max(p, axis=-1, keepdims=True)
    i1 = jnp.min(jnp.where(is_e & (p == p1), lane, none), axis=-1, keepdims=True)
    rest = is_e & (lane != i1)
    p2 = jnp.max(jnp.where(rest, p, neg), axis=-1, keepdims=True)
    i2 = jnp.min(jnp.where(rest & (p == p2), lane, none), axis=-1, keepdims=True)
    s = g_val / (p1 + p2)
    return jnp.where(lane == 0.0, i1 - MOE_GROUPS,
                     jnp.where(lane == 1.0, i2 - MOE_GROUPS,
                               jnp.where(lane == 2.0, p1 * s, jnp.where(lane == 3.0, p2 * s, 0.0))))


def _out_proj_body(a1p_ref, a1s_ref, a2p_ref, a2s_ref, w_ref, x_ref, gate_ref, g_ref, b_ref, sh_ref, sc_ref,
                   wr_ref, xo_ref, h_ref, comb_ref):
    is_prompt = pl.program_id(0) < N_PROMPT // OUT_TM
    a1 = jnp.where(is_prompt, a1p_ref[...], a1s_ref[...])
    a2 = jnp.where(is_prompt, a2p_ref[...], a2s_ref[...])
    k1 = a1.shape[1]
    acc = jnp.dot(a1, w_ref[:k1, :], preferred_element_type=F32)
    acc = acc + jnp.dot(a2, w_ref[k1:, :], preferred_element_type=F32)
    xn = _layer_norm(DEEPNORM_ALPHA * x_ref[...] + gate_ref[0] * acc, g_ref[...], b_ref[...])
    xo_ref[...] = xn
    h = xn * (1.0 + sc_ref[0]) + sh_ref[0]
    h_ref[...] = h
    h_hi = h.astype(BF16)
    h_lo = (h - h_hi.astype(F32)).astype(BF16)
    logits = (jnp.dot(h_hi, wr_ref[0], preferred_element_type=F32)
              + jnp.dot(h_lo, wr_ref[0], preferred_element_type=F32)
              + jnp.dot(h_hi, wr_ref[1], preferred_element_type=F32))
    comb_ref[...] = _route(logits)


def _out_proj(a1p, a1s, a2p, a2s, w, x, gate, ln_g, ln_b, shift, scale, w_router):
    tm = OUT_TM
    np_tiles = N_PROMPT // tm
    k1, k2 = a1p.shape[1], a2p.shape[1]
    mod_spec = pl.BlockSpec((1, 1, D_MODEL), lambda i: (_mod_group(i * tm), 0, 0))
    vec_spec = pl.BlockSpec((1, D_MODEL), lambda i: (0, 0))
    row_spec = pl.BlockSpec((tm, D_MODEL), lambda i: (i, 0))
    pspec = lambda k: pl.BlockSpec((tm, k), lambda i: (jnp.minimum(i, np_tiles - 1), 0))
    sspec = lambda k: pl.BlockSpec((tm, k), lambda i: (jnp.maximum(i - np_tiles, 0), 0))
    return pl.pallas_call(
        _out_proj_body,
        grid=(N_TOK // tm,),
        in_specs=[
            pspec(k1), sspec(k1), pspec(k2), sspec(k2),
            pl.BlockSpec((k1 + k2, D_MODEL), lambda i: (0, 0)),
            row_spec, mod_spec, vec_spec, vec_spec, mod_spec, mod_spec,
            pl.BlockSpec((2, D_MODEL, LANES), lambda i: (0, 0, 0)),
        ],
        out_specs=[row_spec, row_spec, pl.BlockSpec((tm, LANES), lambda i: (i, 0))],
        out_shape=[
            jax.ShapeDtypeStruct((N_TOK, D_MODEL), F32),
            jax.ShapeDtypeStruct((N_TOK, D_MODEL), F32),
            jax.ShapeDtypeStruct((N_TOK, LANES), F32),
        ],
        compiler_params=_cparams(("parallel",)),
        name="out_proj_ln_router",
    )(a1p, a1s, a2p, a2s, w, x, gate, ln_g, ln_b, shift, scale, w_router)


def _route_meta(route):
    e = route[:, :2].astype(jnp.int32).reshape(-1)
    onehot = (e[:, None] == jnp.arange(MOE_EXPERTS, dtype=jnp.int32)[None, :]).astype(jnp.int32)
    csum = jnp.cumsum(onehot, axis=0)
    rank = jnp.sum((csum - onehot) * onehot, axis=1)
    ntile = (csum[-1] + MOE_TM - 1) // MOE_TM
    tile_end = jnp.cumsum(ntile)
    pos = jnp.sum(onehot * (tile_end - ntile)[None, :], axis=1) * MOE_TM + rank
    n_used = tile_end[-1:]
    j = jnp.arange(MOE_TILES, dtype=jnp.int32)
    tile_expert = jnp.sum((tile_end[None, :] <= jnp.minimum(j, n_used - 1)[:, None]).astype(jnp.int32), axis=1)
    tail = jnp.where(ntile > 0, tile_end - 1, -1)
    return pos, tile_expert, n_used, tail


def _dispatch_body(pos_ref, tail_ref, nu_ref, h_ref, xs_ref, zero_ref, sem, zsem):
    i = pl.program_id(0)
    tm = h_ref.shape[0]

    def zero_tile(j):
        return pltpu.make_async_copy(zero_ref, xs_ref.at[pl.ds(j * MOE_TM, MOE_TM)], zsem)

    def start_unused(j, c):
        zero_tile(j).start()
        return c

    def wait_unused(j, c):
        zero_tile(j).wait()
        return c

    @pl.when(i == 0)
    def _():
        zero_ref[...] = jnp.zeros_like(zero_ref)
        for e in range(MOE_EXPERTS):
            @pl.when(tail_ref[e] >= 0)
            def _():
                zero_tile(tail_ref[e]).start()
        lax.fori_loop(nu_ref[0], MOE_TILES, start_unused, 0)
        for e in range(MOE_EXPERTS):
            @pl.when(tail_ref[e] >= 0)
            def _():
                zero_tile(tail_ref[e]).wait()
        lax.fori_loop(nu_ref[0], MOE_TILES, wait_unused, 0)

    def row_copy(r, k):
        p = pos_ref[2 * (i * tm + r) + k]
        return pltpu.make_async_copy(h_ref.at[pl.ds(r, 1)], xs_ref.at[pl.ds(p, 1)], sem)

    def issue(r, c):
        row_copy(r, 0).start()
        row_copy(r, 1).start()
        return c

    def drain(r, c):
        row_copy(r, 0).wait()
        row_copy(r, 1).wait()
        return c

    lax.fori_loop(0, tm, issue, 0, unroll=8)
    lax.fori_loop(0, tm, drain, 0, unroll=8)


def _dispatch(h, pos, tail, n_used):
    tm = 256
    return pl.pallas_call(
        _dispatch_body,
        grid_spec=pltpu.PrefetchScalarGridSpec(
            num_scalar_prefetch=3,
            grid=(N_TOK // tm,),
            in_specs=[pl.BlockSpec((tm, D_MODEL), lambda i, pos_ref, tail_ref, nu_ref: (i, 0))],
            out_specs=pl.BlockSpec(memory_space=pl.ANY),
            scratch_shapes=[pltpu.VMEM((MOE_TM, D_MODEL), F32), pltpu.SemaphoreType.DMA(()),
                            pltpu.SemaphoreType.DMA(())],
        ),
        out_shape=jax.ShapeDtypeStruct((MOE_ROWS, D_MODEL), F32),
        compiler_params=_cparams(("arbitrary",)),
        name="moe_dispatch",
    )(pos, tail, n_used, h)


def _moe_body(te_ref, nu_ref, x_ref, wg_ref, wu_ref, wd_ref, z_ref, wg_s, wu_s, wd_s):
    j = pl.program_id(0)
    changed = (j == 0) | (te_ref[j] != te_ref[jnp.maximum(j - 1, 0)])

    @pl.when(changed)
    def _():
        wg_s[...] = wg_ref[0, 0].astype(BF16)
        wu_s[...] = wu_ref[0, 0].astype(BF16)
        wd_s[...] = wd_ref[0, 0].astype(BF16)

    @pl.when(j < nu_ref[0])
    def _():
        x = x_ref[...].astype(BF16)
        hg = jnp.dot(x, wg_s[...], preferred_element_type=F32)
        hu = jnp.dot(x, wu_s[...], preferred_element_type=F32)
        act = (_silu(hg) * hu).astype(BF16)
        z_ref[...] = jnp.dot(act, wd_s[...], preferred_element_type=F32)

    @pl.when(j >= nu_ref[0])
    def _():
        z_ref[...] = jnp.zeros_like(z_ref)


def _moe_experts(layer, xs, tile_expert, n_used, wg, wu, wd):
    row_spec = pl.BlockSpec((MOE_TM, D_MODEL), lambda j, te, nu: (j, 0))
    return pl.pallas_call(
        _moe_body,
        grid_spec=pltpu.PrefetchScalarGridSpec(
            num_scalar_prefetch=2,
            grid=(MOE_TILES,),
            in_specs=[
                row_spec,
                pl.BlockSpec((1, 1, D_MODEL, MOE_HIDDEN), lambda j, te, nu: (layer, te[j], 0, 0)),
                pl.BlockSpec((1, 1, D_MODEL, MOE_HIDDEN), lambda j, te, nu: (layer, te[j], 0, 0)),
                pl.BlockSpec((1, 1, MOE_HIDDEN, D_MODEL), lambda j, te, nu: (layer, te[j], 0, 0)),
            ],
            out_specs=row_spec,
            scratch_shapes=[pltpu.VMEM((D_MODEL, MOE_HIDDEN), BF16), pltpu.VMEM((D_MODEL, MOE_HIDDEN), BF16),
                            pltpu.VMEM((MOE_HIDDEN, D_MODEL), BF16)],
        ),
        out_shape=jax.ShapeDtypeStruct((MOE_ROWS, D_MODEL), F32),
        compiler_params=_cparams(("arbitrary",)),
        name="moe_experts",
    )(tile_expert, n_used, xs, wg, wu, wd)


def _combine_body(pos_ref, x_ref, r_ref, gate_ref, g_ref, b_ref, z_ref, o_ref, zbuf, sem):
    i = pl.program_id(0)
    tm = x_ref.shape[0]
    slot = i % 2

    def row_copy(tile, s, r, k):
        p = pos_ref[2 * (tile * tm + r) + k]
        return pltpu.make_async_copy(z_ref.at[pl.ds(p, 1)], zbuf.at[s, k, pl.ds(r, 1)], sem.at[s])

    def start(tile, s):
        def issue(r, c):
            row_copy(tile, s, r, 0).start()
            row_copy(tile, s, r, 1).start()
            return c
        lax.fori_loop(0, tm, issue, 0, unroll=8)

    @pl.when(i == 0)
    def _():
        start(0, 0)

    @pl.when(i + 1 < pl.num_programs(0))
    def _():
        start(i + 1, 1 - slot)

    def drain(r, c):
        row_copy(i, slot, r, 0).wait()
        row_copy(i, slot, r, 1).wait()
        return c

    lax.fori_loop(0, tm, drain, 0, unroll=8)
    r = r_ref[...]
    y = r[:, 2:3] * zbuf[slot, 0] + r[:, 3:4] * zbuf[slot, 1]
    o_ref[...] = _layer_norm(DEEPNORM_ALPHA * x_ref[...] + gate_ref[0] * y, g_ref[...], b_ref[...])


def _combine(x, z, route, pos, gate, ln_g, ln_b):
    tm = 256
    row_spec = pl.BlockSpec((tm, D_MODEL), lambda i, pos_ref: (i, 0))
    vec_spec = pl.BlockSpec((1, D_MODEL), lambda i, pos_ref: (0, 0))
    return pl.pallas_call(
        _combine_body,
        grid_spec=pltpu.PrefetchScalarGridSpec(
            num_scalar_prefetch=1,
            grid=(N_TOK // tm,),
            in_specs=[row_spec, pl.BlockSpec((tm, LANES), lambda i, pos_ref: (i, 0)),
                      pl.BlockSpec((1, 1, D_MODEL), lambda i, pos_ref: (_mod_group(i * tm), 0, 0)),
                      vec_spec, vec_spec, pl.BlockSpec(memory_space=pl.ANY)],
            out_specs=row_spec,
            scratch_shapes=[pltpu.VMEM((2, 2, tm, D_MODEL), F32), pltpu.SemaphoreType.DMA((2,))],
        ),
        out_shape=jax.ShapeDtypeStruct((N_TOK, D_MODEL), F32),
        compiler_params=_cparams(("arbitrary",)),
        name="moe_combine_ln",
    )(pos, x, route, gate, ln_g, ln_b, z)


def _od_post_body(n_prev, q_ref, k_ref, v_ref, xs_ref, xs_lo_ref, xs_hi_ref, bc_ref, bc_lo_ref, bc_hi_ref, dt_ref,
                  cos_ref, sin_ref, cw_ref, cb_ref, dtb_ref, *refs):
    prev = refs[:2 * n_prev]
    qr_ref, kr_ref, vb_ref, xc_ref, bcc_ref, dtp_ref, c_k, c_v = refs[2 * n_prev:]
    i = pl.program_id(0)
    cos = cos_ref[...]
    sin = sin_ref[...]
    k = k_ref[...]
    v = v_ref[...]
    qr_ref[...] = _rope(q_ref[...], cos, sin, GQA_HEAD_DIM // 4).astype(BF16)
    kr_ref[...] = _rope(k, cos, sin, GQA_HEAD_DIM // 4).astype(BF16)
    vb_ref[...] = v.astype(BF16)

    is_prompt = i < BATCH
    tile_in_seq = (i - BATCH) % (DEC_SEQ // POST_TM)
    seq_start = is_prompt | (tile_in_seq == 0)
    seq_end = is_prompt | (tile_in_seq == DEC_SEQ // POST_TM - 1)

    def conv_silu(x, lo_ref, hi_ref, w, b):
        rows = x.shape[0]
        r = lax.broadcasted_iota(jnp.int32, x.shape, 0)
        lo = jnp.where(seq_start, 0.0, lo_ref[7:8, :])
        hi = jnp.where(seq_end, 0.0, hi_ref[0:1, :])
        before = jnp.where(r == 0, lo, pltpu.roll(x, 1, 0))
        after = jnp.where(r == rows - 1, hi, pltpu.roll(x, rows - 1, 0))
        return _silu(before * w[0:1] + x * w[1:2] + after * w[2:3] + b)

    xc_ref[...] = conv_silu(xs_ref[...], xs_lo_ref, xs_hi_ref, cw_ref[:, :SSD_D_INNER], cb_ref[:, :SSD_D_INNER])
    bcc_ref[...] = conv_silu(bc_ref[...], bc_lo_ref, bc_hi_ref, cw_ref[:, SSD_D_INNER:], cb_ref[:, SSD_D_INNER:])
    t = dt_ref[...] + dtb_ref[...]
    dtp_ref[...] = jnp.maximum(t, 0.0) + jnp.log1p(jnp.exp(-jnp.abs(t)))

    @pl.when(is_prompt)
    def _():
        if n_prev:
            c_k[0, 0] = prev[0][0, 0]
            c_v[0, 0] = prev[1][0, 0]
        c_k[0, n_prev] = k.reshape(SEQ, GQA_KV_HEADS, GQA_HEAD_DIM)
        c_v[0, n_prev] = v.reshape(SEQ, GQA_KV_HEADS, GQA_HEAD_DIM)


def _od_post(proj, cos, sin, conv_w, conv_b, dt_bias, prev):
    tm = POST_TM
    halo = 8
    per = tm // halo
    n_halo = N_TOK // halo
    n_prev = 1 if prev else 0
    blk = lambda w, c: pl.BlockSpec((tm, w), lambda i: (i, c))
    lo = lambda w, c: pl.BlockSpec((halo, w), lambda i: (jnp.maximum(i * per - 1, 0), c))
    hi = lambda w, c: pl.BlockSpec((halo, w), lambda i: (jnp.minimum((i + 1) * per, n_halo - 1), c))
    full = lambda a: pl.BlockSpec(a.shape, lambda i: (0, 0))
    oblk = lambda w: pl.BlockSpec((tm, w), lambda i: (i, 0))
    cache_spec = lambda layers: pl.BlockSpec((1, layers, SEQ, GQA_KV_HEADS, GQA_HEAD_DIM),
                                             lambda i: (jnp.minimum(i, BATCH - 1), 0, 0, 0, 0))
    cache_shape = jax.ShapeDtypeStruct((BATCH, n_prev + 1, SEQ, GQA_KV_HEADS, GQA_HEAD_DIM), F32)
    return pl.pallas_call(
        functools.partial(_od_post_body, n_prev),
        grid=(N_TOK // tm,),
        in_specs=[blk(1024, 0), blk(256, 12), blk(256, 13),
                  blk(1024, 1), lo(1024, 1), hi(1024, 1), blk(512, 7), lo(512, 7), hi(512, 7), blk(LANES, 32),
                  blk(LANES, 0), blk(LANES, 0), full(conv_w), full(conv_b), full(dt_bias)]
        + [cache_spec(1)] * (2 * n_prev),
        out_specs=[oblk(1024), oblk(256), oblk(256), oblk(1024), oblk(512), oblk(LANES),
                   cache_spec(n_prev + 1), cache_spec(n_prev + 1)],
        out_shape=[
            jax.ShapeDtypeStruct((N_TOK, 1024), BF16),
            jax.ShapeDtypeStruct((N_TOK, 256), BF16),
            jax.ShapeDtypeStruct((N_TOK, 256), BF16),
            jax.ShapeDtypeStruct((N_TOK, 1024), F32),
            jax.ShapeDtypeStruct((N_TOK, 512), F32),
            jax.ShapeDtypeStruct((N_TOK, LANES), F32),
            cache_shape, cache_shape,
        ],
        compiler_params=_cparams(("arbitrary",)),
        name="odd_post",
    )(proj, proj, proj, proj, proj, proj, proj, proj, proj, proj, cos, sin, conv_w, conv_b, dt_bias, *prev)


def _sink_column(sink_ref, kh, rows):
    g = GQA_HEADS // GQA_KV_HEADS
    return jnp.concatenate(
        [jnp.broadcast_to(sink_ref[kh * g + j:kh * g + j + 1, 0:1], (rows, 1)) for j in range(g)], axis=0)


def _gqa_prompt_body(q_ref, k_ref, v_ref, sink_ref, o_ref):
    g = GQA_HEADS // GQA_KV_HEADS
    d = GQA_HEAD_DIM
    rows = q_ref.shape[0]
    scale = d ** -0.5
    outs = []
    for kh in range(GQA_KV_HEADS):
        qg = jnp.concatenate([q_ref[:, (kh * g + j) * d:(kh * g + j + 1) * d] for j in range(g)], axis=0)
        k = k_ref[:, kh * d:(kh + 1) * d]
        v = v_ref[:, kh * d:(kh + 1) * d]
        s = _nt(qg, k) * scale
        snk = _sink_column(sink_ref, kh, rows)
        m = jnp.maximum(jnp.max(s, axis=-1, keepdims=True), snk)
        e = jnp.exp(s - m)
        l = jnp.sum(e, axis=-1, keepdims=True) + jnp.exp(snk - m)
        o = jnp.dot(e.astype(BF16), v, preferred_element_type=F32) / l
        outs.extend(o[j * rows:(j + 1) * rows] for j in range(g))
    o_ref[...] = jnp.concatenate(outs, axis=1).astype(BF16)


def _gqa_prompt(q, k, v, sink):
    spec = lambda w: pl.BlockSpec((SEQ, w), lambda i: (i, 0))
    return pl.pallas_call(
        _gqa_prompt_body,
        grid=(BATCH,),
        in_specs=[spec(1024), spec(256), spec(256), pl.BlockSpec((8, LANES), lambda i: (0, 0))],
        out_specs=spec(1024),
        out_shape=jax.ShapeDtypeStruct((N_PROMPT, 1024), BF16),
        compiler_params=_cparams(("parallel",)),
        name="gqa_dense",
    )(q, k, v, sink)


def _gqa_window_body(q_ref, k_ref, v_ref, kc_ref, vc_ref, sink_ref, o_ref):
    g = GQA_HEADS // GQA_KV_HEADS
    d = GQA_HEAD_DIM
    blk = q_ref.shape[0]
    span = 3 * blk
    n = pl.program_id(1)
    start = pl.multiple_of(jnp.clip((n - 1) * blk, 0, DEC_SEQ - span), blk)
    scale = d ** -0.5
    qpos = n * blk + lax.broadcasted_iota(jnp.int32, (blk, span), 0)
    kpos = start + lax.broadcasted_iota(jnp.int32, (blk, span), 1)
    valid1 = jnp.abs(qpos - kpos) <= WINDOW
    valid = jnp.concatenate([valid1] * g, axis=0)
    outs = []
    for kh in range(GQA_KV_HEADS):
        qg = jnp.concatenate([q_ref[:, (kh * g + j) * d:(kh * g + j + 1) * d] for j in range(g)], axis=0)
        kl = k_ref[pl.ds(start, span), kh * d:(kh + 1) * d]
        vl = v_ref[pl.ds(start, span), kh * d:(kh + 1) * d]
        kc = kc_ref[0, 0, :, kh, :].astype(BF16)
        vc = vc_ref[0, 0, :, kh, :].astype(BF16)
        sl = jnp.where(valid, _nt(qg, kl) * scale, -jnp.inf)
        sc = _nt(qg, kc) * scale
        snk = _sink_column(sink_ref, kh, blk)
        m = jnp.maximum(jnp.maximum(jnp.max(sl, axis=-1, keepdims=True), jnp.max(sc, axis=-1, keepdims=True)), snk)
        el = jnp.exp(sl - m)
        ec = jnp.exp(sc - m)
        l = jnp.sum(el, axis=-1, keepdims=True) + jnp.sum(ec, axis=-1, keepdims=True) + jnp.exp(snk - m)
        o = (jnp.dot(el.astype(BF16), vl, preferred_element_type=F32)
             + jnp.dot(ec.astype(BF16), vc, preferred_element_type=F32)) / l
        outs.extend(o[j * blk:(j + 1) * blk] for j in range(g))
    o_ref[...] = jnp.concatenate(outs, axis=1).astype(BF16)


def _gqa_window(q, k, v, layer, cache_k, cache_v, sink):
    blk = WINDOW
    nq = DEC_SEQ // blk
    q0 = N_PROMPT // blk
    s0 = N_PROMPT // DEC_SEQ
    seq_spec = pl.BlockSpec((DEC_SEQ, 256), lambda b, j: (s0 + b, 0))
    ctx_spec = pl.BlockSpec((1, 1, PAST_LEN, GQA_KV_HEADS, GQA_HEAD_DIM), lambda b, j: (b, layer, 0, 0, 0))
    return pl.pallas_call(
        _gqa_window_body,
        grid=(DEC_BATCH, nq),
        in_specs=[pl.BlockSpec((blk, 1024), lambda b, j: (q0 + b * nq + j, 0)), seq_spec, seq_spec, ctx_spec,
                  ctx_spec,
                  pl.BlockSpec((8, LANES), lambda b, j: (0, 0))],
        out_specs=pl.BlockSpec((blk, 1024), lambda b, j: (b * nq + j, 0)),
        out_shape=jax.ShapeDtypeStruct((N_SAMPLE, 1024), BF16),
        compiler_params=_cparams(("parallel", "parallel")),
        name="gqa_window",
    )(q, k, v, cache_k, cache_v, sink)


def _ssd_body(has_h0, xs_ref, bc_ref, dt_ref, alog_ref, *refs):
    if has_h0:
        h0f_ref, h0b_ref, y_ref, stf_ref, stb_ref = refs
    else:
        y_ref, stf_ref, stb_ref = refs
    d = pl.program_id(1)
    c = pl.program_id(2)
    q = SSD_CHUNK
    hpg = SSD_HEADS // SSD_GROUPS

    row = lax.broadcasted_iota(jnp.int32, (q, LANES), 0)
    a = -jnp.exp(alog_ref[...])
    dta = dt_ref[...] * a

    def scan(x, rev):
        k = 1
        while k < q:
            if rev:
                x = x + jnp.where(row < q - k, pltpu.roll(x, q - k, 0), 0.0)
            else:
                x = x + jnp.where(row >= k, pltpu.roll(x, k, 0), 0.0)
            k *= 2
        return x

    li = lax.broadcasted_iota(jnp.int32, (q, q), 0)
    si = lax.broadcasted_iota(jnp.int32, (q, q), 1)

    def run(rev):
        st_ref = stb_ref if rev else stf_ref

        @pl.when(c == 0)
        def _():
            if has_h0:
                st_ref[0] = (h0b_ref if rev else h0f_ref)[0, 0]
            else:
                st_ref[0] = jnp.zeros(st_ref.shape[1:], F32)

        cum = scan(dta, rev)
        cum_t = cum.T
        edge = cum[0:1, :] if rev else cum[q - 1:q, :]
        mask = (li <= si) if rev else (li >= si)
        xs = xs_ref[...]
        dt = dt_ref[...]
        ys = []
        for g in range(SSD_GROUPS):
            bm16 = bc_ref[:, g * SSD_STATE:(g + 1) * SSD_STATE].astype(BF16)
            cm16 = bc_ref[:, (SSD_GROUPS + g) * SSD_STATE:(SSD_GROUPS + g + 1) * SSD_STATE].astype(BF16)
            cb = _nt(cm16, bm16)
            for hh in range(hpg):
                h = g * hpg + hh
                j = (SSD_HEADS if rev else 0) + h
                cum_col = cum[:, j:j + 1]
                seg = cum_col - cum_t[j:j + 1, :]
                decay = jnp.exp(jnp.where(mask, seg, -jnp.inf))
                xdt = xs[:, h * SSD_HEAD_DIM:(h + 1) * SSD_HEAD_DIM] * dt[:, j:j + 1]
                y_diag = jnp.dot((cb * decay).astype(BF16), xdt.astype(BF16), preferred_element_type=F32)
                tail = jnp.exp(edge[:, j:j + 1] - cum_col)
                state = lax.dot_general((xdt * tail).astype(BF16), bm16, (((0,), (0,)), ((), ())),
                                        preferred_element_type=F32)
                h_prev = st_ref[0, h]
                y_off = _nt(cm16, h_prev.astype(BF16)) * jnp.exp(cum_col)
                ys.append(y_diag + y_off)
                st_ref[0, h] = h_prev * jnp.exp(edge[:, j:j + 1]) + state
        y_ref[0] = jnp.concatenate(ys, axis=1)

    @pl.when(d == 0)
    def _():
        run(False)

    @pl.when(d == 1)
    def _():
        run(True)


def _ssd(xs, bc, dtp, a_log, n_seq, seq, row_off, h0=None):
    b = n_seq
    t = n_seq * seq
    nc = seq // SSD_CHUNK
    c0 = row_off // SSD_CHUNK

    def rmap(i, d, c):
        return i * nc + jnp.where(d == 0, c, nc - 1 - c)

    blk = lambda w: pl.BlockSpec((SSD_CHUNK, w), lambda i, d, c: (c0 + rmap(i, d, c), 0))
    st_spec = pl.BlockSpec((1, SSD_HEADS, SSD_HEAD_DIM, SSD_STATE), lambda i, d, c: (i, 0, 0, 0))
    st_shape = jax.ShapeDtypeStruct((b, SSD_HEADS, SSD_HEAD_DIM, SSD_STATE), F32)
    in_specs = [blk(1024), blk(512), blk(LANES), pl.BlockSpec((1, LANES), lambda i, d, c: (0, 0))]
    args = [xs, bc, dtp, a_log]
    if h0 is not None:
        layer, h0_f, h0_b = h0
        h0_spec = pl.BlockSpec((1, 1, SSD_HEADS, SSD_HEAD_DIM, SSD_STATE), lambda i, d, c: (i, layer, 0, 0, 0))
        in_specs += [h0_spec, h0_spec]
        args += [h0_f, h0_b]
    return pl.pallas_call(
        functools.partial(_ssd_body, h0 is not None),
        grid=(b, 2, nc),
        in_specs=in_specs,
        out_specs=[pl.BlockSpec((1, SSD_CHUNK, 1024), lambda i, d, c: (d, rmap(i, d, c), 0)), st_spec, st_spec],
        out_shape=[jax.ShapeDtypeStruct((2, t, 1024), F32), st_shape, st_shape],
        compiler_params=_cparams(("parallel", "arbitrary", "arbitrary")),
        name="ssd_scan",
    )(*args)


def _ssd_gate_body(y_ref, xs_ref, z_ref, dsk_ref, g_ref, o_ref):
    y = (y_ref[0] + y_ref[1] + xs_ref[...] * dsk_ref[...]) * _silu(z_ref[...])
    w = SSD_D_INNER // SSD_GROUPS
    outs = [_rms(y[:, g * w:(g + 1) * w], g_ref[:, g * w:(g + 1) * w]) for g in range(SSD_GROUPS)]
    o_ref[...] = jnp.concatenate(outs, axis=1).astype(BF16)


def _ssd_gate(y2, xs, proj, d_skip, norm_g, row_off):
    tm = 512
    t = y2.shape[1]
    off = row_off // tm
    return pl.pallas_call(
        _ssd_gate_body,
        grid=(t // tm,),
        in_specs=[pl.BlockSpec((2, tm, 1024), lambda i: (0, i, 0)),
                  pl.BlockSpec((tm, 1024), lambda i: (off + i, 0)),
                  pl.BlockSpec((tm, 1024), lambda i: (off + i, 2)),
                  pl.BlockSpec((1, 1024), lambda i: (0, 0)),
                  pl.BlockSpec((1, 1024), lambda i: (0, 0))],
        out_specs=pl.BlockSpec((tm, 1024), lambda i: (i, 0)),
        out_shape=jax.ShapeDtypeStruct((t, 1024), BF16),
        compiler_params=_cparams(("parallel",)),
        name="ssd_gate_norm",
    )(y2, xs, proj, d_skip, norm_g)


def _even_layer(x, mod, layer_idx, i, p, ctx, tables, prev_caches):
    w_in, q_norm, kv_norm, wq_b, wkv_b, lam, subln, w_out = p
    ctx_ckv, ctx_kpe, ctx_dk, ctx_dv = ctx
    cos, sin = tables
    lam_init = 0.8 - 0.6 * math.exp(-0.3 * layer_idx)

    w_r = jnp.concatenate([w_in[:, :1024], w_in[:, 1088:], w_in[:, 1024:1088],
                           jnp.zeros((D_MODEL, PROJ_N - w_in.shape[1]), F32)], axis=1).astype(BF16)
    proj = _proj(x, mod[0], mod[1], w_r)
    qn, ckv, dq, dk, dv, kpe, *caches = _ev_post(proj, cos, sin, q_norm[None], kv_norm[None], prev_caches)

    wq = wq_b.reshape(MLA_Q_RANK, MLA_HEADS, MLA_NOPE + MLA_ROPE)
    wq_r = jnp.pad(wq, ((0, 0), (0, 0), (0, MLA_ROPE))).reshape(MLA_Q_RANK, -1).astype(BF16)
    wkv = wkv_b.astype(BF16)
    q = _mm(qn, wq_r, BF16, 512, N_TOK // 512)
    kv = _mm(ckv, wkv, BF16, 512, N_TOK // 512)
    n_even = ctx_ckv.shape[1]
    kv_ctx = _mm(ctx_ckv.reshape(-1, MLA_KV_RANK), wkv, BF16, PAST_LEN, DEC_BATCH,
                 row_block=lambda b: b * n_even + i)

    args = (lam_init, q, cos, sin, kv, kpe, dq, dk, dv, lam, subln[None])
    o1p, o2p = _even_attn(*args, n_seq=BATCH, seq=SEQ, row_off=0)
    o1s, o2s = _even_attn(*args, n_seq=DEC_BATCH, seq=DEC_SEQ, row_off=N_PROMPT,
                          ctx=(i, kv_ctx, ctx_kpe, ctx_dk, ctx_dv))
    return (o1p, o1s, o2p, o2s), w_out.astype(BF16), caches


def _odd_layer(x, mod, i, p, ctx, tables, prev_caches):
    w_in, sink, conv_w, conv_b, dt_bias, a_log, d_skip, norm_g, w_out = p
    k_ctx, v_ctx, h0_f, h0_b = ctx
    cos, sin = tables

    w_r = jnp.concatenate([w_in[:, :1024], w_in[:, 2560:3584], w_in[:, 1536:2560], w_in[:, 1024:1536],
                           w_in[:, 3584:], jnp.zeros((D_MODEL, PROJ_N - w_in.shape[1]), F32)],
                          axis=1).astype(BF16)
    proj = _proj(x, mod[0], mod[1], w_r)
    cb = conv_b[None]
    dtb = jnp.pad(dt_bias.reshape(1, -1), ((0, 0), (0, LANES - 2 * SSD_HEADS)))
    alog = jnp.pad(a_log.reshape(1, -1), ((0, 0), (0, LANES - 2 * SSD_HEADS)))
    q, k, v, xs, bc, dtp, *caches = _od_post(proj, cos, sin, conv_w, cb, dtb, prev_caches)
    sink_b = jnp.broadcast_to(sink[:, None], (GQA_HEADS, LANES))

    o_att_p = _gqa_prompt(q, k, v, sink_b)
    o_att_s = _gqa_window(q, k, v, i, k_ctx, v_ctx, sink_b)

    y_p, st_f, st_b = _ssd(xs, bc, dtp, alog, BATCH, SEQ, 0)
    y_s, _, _ = _ssd(xs, bc, dtp, alog, DEC_BATCH, DEC_SEQ, N_PROMPT, h0=(i, h0_f, h0_b))
    dsk = jnp.repeat(d_skip, SSD_HEAD_DIM)[None]
    y_n_p = _ssd_gate(y_p, xs, proj, dsk, norm_g[None], 0)
    y_n_s = _ssd_gate(y_s, xs, proj, dsk, norm_g[None], N_PROMPT)
    return (o_att_p, o_att_s, y_n_p, y_n_s), w_out.astype(BF16), caches, (st_f, st_b)


def kernel(x_prompt, x_sample, cache_mla_ckv, cache_mla_kpe, cache_diff_k, cache_diff_v, cache_gqa_k, cache_gqa_v, state_ssd_fwd, state_ssd_bwd, c, c_ctx, w_mod, b_mod, ln1_g, ln1_b, ln2_g, ln2_b, ev_w_in, mla_q_norm, mla_kv_norm, mla_wq_b, mla_wkv_b, diff_lambda, diff_subln, ev_w_out, od_w_in, gqa_sink, ssd_conv_w, ssd_conv_b, ssd_dt_bias, ssd_a_log, ssd_d, ssd_norm, od_w_out, moe_router_group, moe_router_expert, moe_w_gate, moe_w_up, moe_w_down):
    x = jnp.concatenate([x_prompt.reshape(N_PROMPT, D_MODEL), x_sample.reshape(N_SAMPLE, D_MODEL)], axis=0)
    cs = jnp.concatenate([c_ctx[None], c, jnp.zeros((8 - 1 - DEC_BATCH, D_MODEL), F32)], axis=0)
    mods = _adaln(cs, w_mod, b_mod)[:, :1 + DEC_BATCH].reshape(DEPTH, 1 + DEC_BATCH, 6, 1, D_MODEL)
    tables_64 = _rope_tables(64)
    tables_128 = _rope_tables(GQA_HEAD_DIM)

    ev_caches, od_caches, ssd_states = (), (), []
    for l in range(DEPTH):
        i = l // 2
        mod = [mods[l, :, k] for k in range(6)]
        if l % 2 == 0:
            p = (ev_w_in[i], mla_q_norm[i], mla_kv_norm[i], mla_wq_b[i], mla_wkv_b[i], diff_lambda[i],
                 diff_subln[i], ev_w_out[i])
            ctx = (cache_mla_ckv, cache_mla_kpe, cache_diff_k, cache_diff_v)
            mix, w_out, ev_caches = _even_layer(x, mod, l, i, p, ctx, tables_64, ev_caches)
        else:
            p = (od_w_in[i], gqa_sink[i], ssd_conv_w[i], ssd_conv_b[i], ssd_dt_bias[i], ssd_a_log[i], ssd_d[i],
                 ssd_norm[i], od_w_out[i])
            ctx = (cache_gqa_k, cache_gqa_v, state_ssd_fwd, state_ssd_bwd)
            mix, w_out, od_caches, states = _odd_layer(x, mod, i, p, ctx, tables_128, od_caches)
            ssd_states.append(states)
        w_router = jnp.concatenate([moe_router_group[l], moe_router_expert[l],
                                    jnp.zeros((D_MODEL, LANES - MOE_GROUPS - MOE_EXPERTS), F32)], axis=1)
        w_router_hi = w_router.astype(BF16)
        w_router = jnp.stack([w_router_hi, (w_router - w_router_hi.astype(F32)).astype(BF16)])
        x, h2, route = _out_proj(*mix, w_out, x, mod[2], ln1_g[l][None], ln1_b[l][None], mod[3], mod[4], w_router)
        pos, tile_expert, n_used, tail = _route_meta(route)
        xs = _dispatch(h2, pos, tail, n_used)
        z = _moe_experts(l, xs, tile_expert, n_used, moe_w_gate, moe_w_up, moe_w_down)
        x = _combine(x, z, route, pos, mod[5], ln2_g[l][None], ln2_b[l][None])

    y_prompt = x[:N_PROMPT].reshape(BATCH, SEQ, D_MODEL)
    y_sample = x[N_PROMPT:].reshape(DEC_BATCH, DEC_SEQ, D_MODEL)
    new_ssd_fwd = jnp.stack([s[0] for s in ssd_states], axis=1)
    new_ssd_bwd = jnp.stack([s[1] for s in ssd_states], axis=1)
    return (y_prompt, y_sample, *ev_caches, *od_caches, new_ssd_fwd, new_ssd_bwd)
```

```python
import functools
import math

import jax
import jax.numpy as jnp
from jax import lax
from jax.experimental import pallas as pl
from jax.experimental.pallas import tpu as pltpu

F32 = jnp.float32
BF16 = jnp.bfloat16

D_MODEL = 2048
BATCH = 16
SEQ = 256
DEPTH = 4
DEC_BATCH = 2
DEC_SEQ = 1024
PAST_LEN = 512
GRID_W = 64
ROPE_THETA = 10000.0
DEEPNORM_ALPHA = (2.0 * DEPTH) ** 0.25
LN_EPS = 1e-5
RMS_EPS = 1e-6

MLA_HEADS = 8
MLA_Q_RANK = 512
MLA_KV_RANK = 512
MLA_NOPE = 128
MLA_ROPE = 64
MLA_V = 128
DIFF_HEADS = 8
DIFF_QK = 64
DIFF_V = 128
GQA_HEADS = 8
GQA_KV_HEADS = 2
GQA_HEAD_DIM = 128
WINDOW = 128
SSD_HEADS = 16
SSD_HEAD_DIM = 64
SSD_D_INNER = SSD_HEADS * SSD_HEAD_DIM
SSD_GROUPS = 2
SSD_STATE = 128
SSD_CHUNK = 128
MOE_GROUPS = 4
MOE_EPG = 4
MOE_EXPERTS = 16
MOE_HIDDEN = 512

N_PROMPT = BATCH * SEQ
N_SAMPLE = DEC_BATCH * DEC_SEQ
N_TOK = N_PROMPT + N_SAMPLE
PROJ_N = 4224
LANES = 128
VMEM_LIMIT = 56 * 1024 * 1024

PROJ_TM, PROJ_TN = 1024, 1408
OUT_TM = 512
ATT_TQ = 256
POST_TM = 256
MOE_TM = 256
MOE_SLOTS = 2 * N_TOK
MOE_TILES = MOE_SLOTS // MOE_TM + MOE_EXPERTS
MOE_ROWS = MOE_TILES * MOE_TM


def _cparams(sem):
    return pltpu.CompilerParams(dimension_semantics=sem, vmem_limit_bytes=VMEM_LIMIT)


def _mod_group(row0):
    return jnp.maximum(row0 // DEC_SEQ - (N_PROMPT // DEC_SEQ - 1), 0)


def _silu(x):
    return x / (1.0 + jnp.exp(-x))


def _nt(a, b):
    return lax.dot_general(a, b, (((1,), (1,)), ((), ())), preferred_element_type=F32)


def _rms(x, g):
    return x * lax.rsqrt(jnp.mean(x * x, axis=-1, keepdims=True) + RMS_EPS) * g


def _layer_norm(y, g, b):
    mu = jnp.mean(y, axis=-1, keepdims=True)
    yc = y - mu
    return yc * lax.rsqrt(jnp.mean(yc * yc, axis=-1, keepdims=True) + LN_EPS) * g + b


def _adaln_body(c_ref, w_ref, b_ref, o_ref):
    a = _silu(c_ref[...])
    o_ref[0] = jnp.dot(a, w_ref[0], preferred_element_type=F32) + b_ref[0]


def _adaln(cs, w_mod, b_mod):
    tn = 1024
    n = w_mod.shape[-1]
    return pl.pallas_call(
        _adaln_body,
        grid=(DEPTH, n // tn),
        in_specs=[
            pl.BlockSpec((8, D_MODEL), lambda l, j: (0, 0)),
            pl.BlockSpec((1, D_MODEL, tn), lambda l, j: (l, 0, j)),
            pl.BlockSpec((1, 1, tn), lambda l, j: (l, 0, j)),
        ],
        out_specs=pl.BlockSpec((1, 8, tn), lambda l, j: (l, 0, j)),
        out_shape=jax.ShapeDtypeStruct((DEPTH, 8, n), F32),
        compiler_params=_cparams(("parallel", "parallel")),
        name="adaln",
    )(cs, w_mod, b_mod.reshape(DEPTH, 1, n))


def _proj_body(x_ref, sh_ref, sc_ref, w_ref, o_ref, h_scr):
    @pl.when(pl.program_id(1) == 0)
    def _():
        h = x_ref[...] * (1.0 + sc_ref[0]) + sh_ref[0]
        h_scr[...] = h.astype(BF16)

    o_ref[...] = jnp.dot(h_scr[...], w_ref[...], preferred_element_type=F32)


def _proj(x, shift, scale, w):
    tm, tn = PROJ_TM, PROJ_TN
    n = w.shape[1]
    mod_spec = pl.BlockSpec((1, 1, D_MODEL), lambda i, j: (_mod_group(i * tm), 0, 0))
    return pl.pallas_call(
        _proj_body,
        grid=(N_TOK // tm, n // tn),
        in_specs=[
            pl.BlockSpec((tm, D_MODEL), lambda i, j: (i, 0)),
            mod_spec,
            mod_spec,
            pl.BlockSpec((D_MODEL, tn), lambda i, j: (0, j)),
        ],
        out_specs=pl.BlockSpec((tm, tn), lambda i, j: (i, j)),
        out_shape=jax.ShapeDtypeStruct((N_TOK, n), F32),
        scratch_shapes=[pltpu.VMEM((tm, D_MODEL), BF16)],
        compiler_params=_cparams(("parallel", "arbitrary")),
        name="in_proj",
    )(x, shift, scale, w)


def _mm_body(a_ref, b_ref, o_ref):
    o_ref[...] = jnp.dot(a_ref[...].astype(BF16), b_ref[...], preferred_element_type=F32).astype(o_ref.dtype)


def _mm(a, b, out_dtype, tm, n_tiles, row_block=lambda i: i):
    k = a.shape[1]
    n = b.shape[1]
    return pl.pallas_call(
        _mm_body,
        grid=(n_tiles,),
        in_specs=[pl.BlockSpec((tm, k), lambda i: (row_block(i), 0)), pl.BlockSpec((k, n), lambda i: (0, 0))],
        out_specs=pl.BlockSpec((tm, n), lambda i: (i, 0)),
        out_shape=jax.ShapeDtypeStruct((n_tiles * tm, n), out_dtype),
        compiler_params=_cparams(("parallel",)),
        name="mm_resident",
    )(a, b)


def _rope(x, cos, sin_signed, quarter):
    rows, w = x.shape
    reps = w // LANES
    if reps > 1:
        cos = jnp.concatenate([cos] * reps, axis=1)
        sin_signed = jnp.concatenate([sin_signed] * reps, axis=1)
    lane = lax.broadcasted_iota(jnp.int32, x.shape, 1)
    first = (lane % (2 * quarter)) < quarter
    up = pltpu.roll(x, w - quarter, 1)
    dn = pltpu.roll(x, quarter, 1)
    return x * cos + jnp.where(first, up, dn) * sin_signed


def _rope_tables(rdim):
    half = rdim // 2
    quarter = half // 2
    pos = jnp.arange(DEC_SEQ)
    row = (pos // GRID_W).astype(F32)
    col = (pos % GRID_W).astype(F32)
    inv = ROPE_THETA ** (-jnp.arange(quarter, dtype=F32) * 2.0 / half)
    lane = jnp.arange(LANES)
    r = lane % rdim
    use_col = (r // half) == 1
    j = r % quarter
    p = jnp.where(use_col[None, :], col[:, None], row[:, None])
    ang = p * inv[j][None, :]
    sign = jnp.where((r % half) < quarter, -1.0, 1.0).astype(F32)
    cos = jnp.cos(ang)
    sin = jnp.sin(ang) * sign[None, :]
    cos = jnp.concatenate([jnp.ones((N_PROMPT, LANES), F32), cos, cos], axis=0)
    sin = jnp.concatenate([jnp.zeros((N_PROMPT, LANES), F32), sin, sin], axis=0)
    return cos, sin


def _ev_post_body(n_prev, ql_ref, kvl_ref, dq_ref, dk_ref, dv_ref, kpe_ref, cos_ref, sin_ref, gq_ref, gkv_ref,
                  *refs):
    prev = refs[:4 * n_prev]
    qn_ref, ckv_ref, dqr_ref, dkr_ref, dvb_ref, kper_ref, c_ckv, c_kpe, c_dk, c_dv = refs[4 * n_prev:]
    cos = cos_ref[...]
    sin = sin_ref[...]
    qn_ref[...] = _rms(ql_ref[...], gq_ref[...]).astype(BF16)
    ckv = _rms(kvl_ref[...], gkv_ref[...])
    ckv_ref[...] = ckv.astype(BF16)
    dqr_ref[...] = _rope(dq_ref[...], cos, sin, DIFF_QK // 4).astype(BF16)
    dk = _rope(dk_ref[...], cos, sin, DIFF_QK // 4)
    dkr_ref[...] = dk.astype(BF16)
    dv = dv_ref[...]
    dvb_ref[...] = dv.astype(BF16)
    kpe = _rope(kpe_ref[...], cos, sin, MLA_ROPE // 4)
    kper_ref[...] = kpe.astype(BF16)

    @pl.when(pl.program_id(0) < BATCH)
    def _():
        if n_prev:
            c_ckv[0, 0] = prev[0][0, 0]
            c_kpe[0, 0] = prev[1][0, 0]
            c_dk[0, 0] = prev[2][0, 0]
            c_dv[0, 0] = prev[3][0, 0]
        c_ckv[0, n_prev] = ckv
        c_kpe[0, n_prev] = kpe[:, :MLA_ROPE]
        c_dk[0, n_prev] = dk.reshape(SEQ, DIFF_HEADS, 2 * DIFF_QK)
        c_dv[0, n_prev] = dv.reshape(SEQ, DIFF_HEADS, DIFF_V)


def _ev_post(proj, cos, sin, gq, gkv, prev):
    tm = POST_TM
    n_prev = 1 if prev else 0
    nl = n_prev + 1
    row = lambda c: (lambda i: (i, c))
    seq4 = lambda i: (jnp.minimum(i, BATCH - 1), 0, 0, 0)
    seq5 = lambda i: (jnp.minimum(i, BATCH - 1), 0, 0, 0, 0)
    cache_shapes = [(MLA_KV_RANK,), (MLA_ROPE,), (DIFF_HEADS, 2 * DIFF_QK), (DIFF_HEADS, DIFF_V)]

    def cache_spec(layers, tail):
        return pl.BlockSpec((1, layers, SEQ) + tail, seq4 if len(tail) == 1 else seq5)

    return pl.pallas_call(
        functools.partial(_ev_post_body, n_prev),
        grid=(N_TOK // tm,),
        in_specs=[
            pl.BlockSpec((tm, 512), row(0)),
            pl.BlockSpec((tm, 512), row(1)),
            pl.BlockSpec((tm, 1024), row(1)),
            pl.BlockSpec((tm, 1024), row(2)),
            pl.BlockSpec((tm, 1024), row(3)),
            pl.BlockSpec((tm, LANES), row(32)),
            pl.BlockSpec((tm, LANES), row(0)),
            pl.BlockSpec((tm, LANES), row(0)),
            pl.BlockSpec((1, 512), lambda i: (0, 0)),
            pl.BlockSpec((1, 512), lambda i: (0, 0)),
        ] + [cache_spec(1, t) for t in cache_shapes] * n_prev,
        out_specs=[
            pl.BlockSpec((tm, 512), row(0)),
            pl.BlockSpec((tm, 512), row(0)),
            pl.BlockSpec((tm, 1024), row(0)),
            pl.BlockSpec((tm, 1024), row(0)),
            pl.BlockSpec((tm, 1024), row(0)),
            pl.BlockSpec((tm, LANES), row(0)),
        ] + [cache_spec(nl, t) for t in cache_shapes],
        out_shape=[
            jax.ShapeDtypeStruct((N_TOK, 512), BF16),
            jax.ShapeDtypeStruct((N_TOK, 512), BF16),
            jax.ShapeDtypeStruct((N_TOK, 1024), BF16),
            jax.ShapeDtypeStruct((N_TOK, 1024), BF16),
            jax.ShapeDtypeStruct((N_TOK, 1024), BF16),
            jax.ShapeDtypeStruct((N_TOK, LANES), BF16),
        ] + [jax.ShapeDtypeStruct((BATCH, nl, SEQ) + t, F32) for t in cache_shapes],
        compiler_params=_cparams(("arbitrary",)),
        name="even_post",
    )(proj, proj, proj, proj, proj, proj, cos, sin, gq, gkv, *prev)


def _softmax_av(scores, values):
    m = functools.reduce(jnp.maximum, [jnp.max(s, axis=-1, keepdims=True) for s in scores])
    es = [jnp.exp(s - m) for s in scores]
    l = sum(jnp.sum(e, axis=-1, keepdims=True) for e in es)
    o = sum(jnp.dot(e.astype(BF16), v, preferred_element_type=F32) for e, v in zip(es, values))
    return o / l


def _even_attn_body(lam_init, has_ctx, q_ref, cos_ref, sin_ref, kv_ref, kpe_ref, dq_ref, dk_ref, dv_ref,
                    lam_ref, sub_ref, *refs):
    if has_ctx:
        kvc_ref, kpec_ref, dkc_ref, dvc_ref, omla_ref, odiff_ref = refs
    else:
        omla_ref, odiff_ref = refs
    q = q_ref[...]
    qr = _rope(q[:, MLA_NOPE:].astype(F32), cos_ref[...], sin_ref[...], MLA_ROPE // 4).astype(BF16)
    qcat = jnp.concatenate([q[:, :MLA_NOPE], qr], axis=1)
    mla_scale = (MLA_NOPE + MLA_ROPE) ** -0.5
    scores = [_nt(qcat, jnp.concatenate([kv_ref[:, :MLA_NOPE], kpe_ref[...]], axis=1)) * mla_scale]
    values = [kv_ref[:, MLA_NOPE:]]
    if has_ctx:
        scores.append(_nt(qcat, jnp.concatenate([kvc_ref[:, :MLA_NOPE], kpec_ref[...]], axis=1)) * mla_scale)
        values.append(kvc_ref[:, MLA_NOPE:])
    omla_ref[...] = _softmax_av(scores, values).astype(BF16)

    lam = lam_ref[...]
    lam_full = (jnp.exp(jnp.sum(lam[0:1] * lam[1:2], axis=-1, keepdims=True))
                - jnp.exp(jnp.sum(lam[2:3] * lam[3:4], axis=-1, keepdims=True)) + lam_init)
    dq = dq_ref[...] * jnp.asarray(DIFF_QK ** -0.5, BF16)
    lane = lax.broadcasted_iota(jnp.int32, dq.shape, 1)
    zero = jnp.zeros_like(dq)
    keys = [dk_ref[...]]
    values = [dv_ref[...]]
    if has_ctx:
        keys.append(dkc_ref[...])
        values.append(dvc_ref[...])
    a1 = _softmax_av([_nt(jnp.where(lane < DIFF_QK, dq, zero), k) for k in keys], values)
    a2 = _softmax_av([_nt(jnp.where(lane >= DIFF_QK, dq, zero), k) for k in keys], values)
    odiff_ref[...] = (_rms(a1 - lam_full * a2, sub_ref[...]) * (1.0 - lam_init)).astype(BF16)


def _even_attn(lam_init, q, cos, sin, kv, kpe, dq, dk, dv, lam, subln, n_seq, seq, row_off, ctx=None):
    tq = ATT_TQ
    nq = seq // tq
    q0 = row_off // tq
    s0 = row_off // seq
    qspec = lambda w: pl.BlockSpec((tq, w), lambda b, j, h: (q0 + b * nq + j, h))
    kspec = lambda w: pl.BlockSpec((seq, w), lambda b, j, h: (s0 + b, h))
    tspec = pl.BlockSpec((tq, LANES), lambda b, j, h: (q0 + b * nq + j, 0))
    ospec = pl.BlockSpec((tq, LANES), lambda b, j, h: (b * nq + j, h))
    in_specs = [qspec(256), tspec, tspec, kspec(256),
                pl.BlockSpec((seq, LANES), lambda b, j, h: (s0 + b, 0)),
                qspec(LANES), kspec(LANES), kspec(LANES),
                pl.BlockSpec((4, DIFF_QK), lambda b, j, h: (0, 0)),
                pl.BlockSpec((1, DIFF_V), lambda b, j, h: (0, 0))]
    args = [q, cos, sin, kv, kpe, dq, dk, dv, lam, subln]
    if ctx is not None:
        cspec = lambda w: pl.BlockSpec((PAST_LEN, w), lambda b, j, h: (b, h))
        in_specs += [cspec(256), pl.BlockSpec((PAST_LEN, LANES), lambda b, j, h: (b, 0)), cspec(LANES), cspec(LANES)]
        args += list(ctx)
    return pl.pallas_call(
        functools.partial(_even_attn_body, lam_init, ctx is not None),
        grid=(n_seq, nq, MLA_HEADS),
        in_specs=in_specs,
        out_specs=[ospec, ospec],
        out_shape=[jax.ShapeDtypeStruct((n_seq * seq, 1024), BF16)] * 2,
        compiler_params=_cparams(("parallel", "parallel", "parallel")),
        name="even_attn",
    )(*args)


def _route(logits):
    lane = lax.broadcasted_iota(jnp.int32, logits.shape, 1).astype(F32)
    neg = -jnp.inf
    none = float(LANES)
    is_g = lane < MOE_GROUPS
    lg = jnp.where(is_g, logits, neg)
    mg = jnp.max(lg, axis=-1, keepdims=True)
    g_val = 1.0 / jnp.sum(jnp.exp(lg - mg), axis=-1, keepdims=True)
    g_idx = jnp.min(jnp.where(is_g & (lg == mg), lane, none), axis=-1, keepdims=True)
    e_lo = MOE_GROUPS + g_idx * MOE_EPG
    is_e = (lane >= e_lo) & (lane < e_lo + MOE_EPG)
    le = jnp.where(is_e, logits, neg)
    p = jnp.exp(le - jnp.max(le, axis=-1, keepdims=True))
    p1 = jnp.max(p, axis=-1, keepdims=True)
    i1 = jnp.min(jnp.where(is_e & (p == p1), lane, none), axis=-1, keepdims=True)
    rest = is_e & (lane != i1)
    p2 = jnp.max(jnp.where(rest, p, neg), axis=-1, keepdims=True)
    i2 = jnp.min(jnp.where(rest & (p == p2), lane, none), axis=-1, keepdims=True)
    s = g_val / (p1 + p2)
    return jnp.where(lane == 0.0, i1 - MOE_GROUPS,
                     jnp.where(lane == 1.0, i2 - MOE_GROUPS,
                               jnp.where(lane == 2.0, p1 * s, jnp.where(lane == 3.0, p2 * s, 0.0))))


def _out_proj_body(a1p_ref, a1s_ref, a2p_ref, a2s_ref, w_ref, x_ref, gate_ref, g_ref, b_ref, sh_ref, sc_ref,
                   wr_ref, xo_ref, h_ref, comb_ref):
    is_prompt = pl.program_id(0) < N_PROMPT // OUT_TM
    a1 = jnp.where(is_prompt, a1p_ref[...], a1s_ref[...])
    a2 = jnp.where(is_prompt, a2p_ref[...], a2s_ref[...])
    k1 = a1.shape[1]
    acc = jnp.dot(a1, w_ref[:k1, :], preferred_element_type=F32)
    acc = acc + jnp.dot(a2, w_ref[k1:, :], preferred_element_type=F32)
    xn = _layer_norm(DEEPNORM_ALPHA * x_ref[...] + gate_ref[0] * acc, g_ref[...], b_ref[...])
    xo_ref[...] = xn
    h = xn * (1.0 + sc_ref[0]) + sh_ref[0]
    h_ref[...] = h
    h_hi = h.astype(BF16)
    h_lo = (h - h_hi.astype(F32)).astype(BF16)
    logits = (jnp.dot(h_hi, wr_ref[0], preferred_element_type=F32)
              + jnp.dot(h_lo, wr_ref[0], preferred_element_type=F32)
              + jnp.dot(h_hi, wr_ref[1], preferred_element_type=F32))
    comb_ref[...] = _route(logits)


def _out_proj(a1p, a1s, a2p, a2s, w, x, gate, ln_g, ln_b, shift, scale, w_router):
    tm = OUT_TM
    np_tiles = N_PROMPT // tm
    k1, k2 = a1p.shape[1], a2p.shape[1]
    mod_spec = pl.BlockSpec((1, 1, D_MODEL), lambda i: (_mod_group(i * tm), 0, 0))
    vec_spec = pl.BlockSpec((1, D_MODEL), lambda i: (0, 0))
    row_spec = pl.BlockSpec((tm, D_MODEL), lambda i: (i, 0))
    pspec = lambda k: pl.BlockSpec((tm, k), lambda i: (jnp.minimum(i, np_tiles - 1), 0))
    sspec = lambda k: pl.BlockSpec((tm, k), lambda i: (jnp.maximum(i - np_tiles, 0), 0))
    return pl.pallas_call(
        _out_proj_body,
        grid=(N_TOK // tm,),
        in_specs=[
            pspec(k1), sspec(k1), pspec(k2), sspec(k2),
            pl.BlockSpec((k1 + k2, D_MODEL), lambda i: (0, 0)),
            row_spec, mod_spec, vec_spec, vec_spec, mod_spec, mod_spec,
            pl.BlockSpec((2, D_MODEL, LANES), lambda i: (0, 0, 0)),
        ],
        out_specs=[row_spec, row_spec, pl.BlockSpec((tm, LANES), lambda i: (i, 0))],
        out_shape=[
            jax.ShapeDtypeStruct((N_TOK, D_MODEL), F32),
            jax.ShapeDtypeStruct((N_TOK, D_MODEL), F32),
            jax.ShapeDtypeStruct((N_TOK, LANES), F32),
        ],
        compiler_params=_cparams(("parallel",)),
        name="out_proj_ln_router",
    )(a1p, a1s, a2p, a2s, w, x, gate, ln_g, ln_b, shift, scale, w_router)


def _route_meta(route):
    e = route[:, :2].astype(jnp.int32).reshape(-1)
    onehot = (e[:, None] == jnp.arange(MOE_EXPERTS, dtype=jnp.int32)[None, :]).astype(jnp.int32)
    csum = jnp.cumsum(onehot, axis=0)
    rank = jnp.sum((csum - onehot) * onehot, axis=1)
    count = csum[-1]
    ntile = (count + MOE_TM - 1) // MOE_TM
    tile_end = jnp.cumsum(ntile)
    tile_off = tile_end - ntile
    pos = jnp.sum(onehot * tile_off[None, :], axis=1) * MOE_TM + rank
    n_used = tile_end[-1:]
    j = jnp.arange(MOE_TILES, dtype=jnp.int32)
    tile_expert = jnp.sum((tile_end[None, :] <= jnp.minimum(j, n_used - 1)[:, None]).astype(jnp.int32), axis=1)
    is_e = tile_expert[:, None] == jnp.arange(MOE_EXPERTS, dtype=jnp.int32)[None, :]
    left = jnp.sum(jnp.where(is_e, (count - (j[:, None] - tile_off[None, :]) * MOE_TM)[...], 0), axis=1)
    n_valid = jnp.where(j < n_used, jnp.clip(left, 0, MOE_TM), 0)
    return pos, tile_expert, n_valid, n_used


def _moe_body(pos_ref, te_ref, nv_ref, nu_ref, h_ref, wg_ref, wu_ref, wd_ref, z_ref,
              inv, xbuf, zbuf, wg_s, wu_s, wd_s, gsem, ssem):
    j = pl.program_id(0)
    slot = j % 2
    n_used = nu_ref[0]
    group = 8

    def gather_row(tile, s, r):
        tok = lax.shift_right_logical(inv[tile * MOE_TM + r], 1)
        return pltpu.make_async_copy(h_ref.at[pl.ds(tok, 1)], xbuf.at[s, pl.ds(r, 1)], gsem.at[s])

    def scatter_row(tile, s, r):
        sl = inv[tile * MOE_TM + r]
        dst = (sl & 1) * N_TOK + lax.shift_right_logical(sl, 1)
        return pltpu.make_async_copy(zbuf.at[s, pl.ds(r, 1)], z_ref.at[pl.ds(dst, 1)], ssem.at[s])

    def for_valid_rows(tile, fn):
        n = nv_ref[tile]

        def block(g, c):
            for u in range(group):
                fn(g * group + u)
            return c

        def single(r, c):
            fn(r)
            return c

        full = lax.shift_right_logical(n, 3)
        lax.fori_loop(0, full, block, 0)
        lax.fori_loop(full * group, n, single, 0)

    @pl.when(j == 0)
    def _():
        def fill(g, c):
            for u in range(group):
                sl = g * group + u
                inv[pos_ref[sl]] = sl
            return c
        lax.fori_loop(0, MOE_SLOTS // group, fill, 0)
        xbuf[...] = jnp.zeros_like(xbuf)
        for_valid_rows(0, lambda r: gather_row(0, 0, r).start())

    changed = (j == 0) | (te_ref[j] != te_ref[jnp.maximum(j - 1, 0)])

    @pl.when(changed)
    def _():
        wg_s[...] = wg_ref[0, 0].astype(BF16)
        wu_s[...] = wu_ref[0, 0].astype(BF16)
        wd_s[...] = wd_ref[0, 0].astype(BF16)

    @pl.when(j < n_used)
    def _():
        @pl.when(j + 1 < n_used)
        def _():
            for_valid_rows(j + 1, lambda r: gather_row(j + 1, 1 - slot, r).start())

        for_valid_rows(j, lambda r: gather_row(j, slot, r).wait())

        @pl.when(j >= 2)
        def _():
            for_valid_rows(j - 2, lambda r: scatter_row(j - 2, slot, r).wait())

        x = xbuf[slot].astype(BF16)
        hg = jnp.dot(x, wg_s[...], preferred_element_type=F32)
        hu = jnp.dot(x, wu_s[...], preferred_element_type=F32)
        act = (_silu(hg) * hu).astype(BF16)
        zbuf[slot] = jnp.dot(act, wd_s[...], preferred_element_type=F32)
        for_valid_rows(j, lambda r: scatter_row(j, slot, r).start())

        @pl.when(j == n_used - 1)
        def _():
            @pl.when(j >= 1)
            def _():
                for_valid_rows(j - 1, lambda r: scatter_row(j - 1, 1 - slot, r).wait())
            for_valid_rows(j, lambda r: scatter_row(j, slot, r).wait())


def _moe_experts(layer, h, pos, tile_expert, n_valid, n_used, wg, wu, wd):
    wspec = lambda r, c: pl.BlockSpec((1, 1, r, c), lambda j, pos_ref, te, nv, nu: (layer, te[j], 0, 0))
    return pl.pallas_call(
        _moe_body,
        grid_spec=pltpu.PrefetchScalarGridSpec(
            num_scalar_prefetch=4,
            grid=(MOE_TILES,),
            in_specs=[pl.BlockSpec(memory_space=pl.ANY), wspec(D_MODEL, MOE_HIDDEN), wspec(D_MODEL, MOE_HIDDEN),
                      wspec(MOE_HIDDEN, D_MODEL)],
            out_specs=pl.BlockSpec(memory_space=pl.ANY),
            scratch_shapes=[pltpu.SMEM((MOE_ROWS,), jnp.int32),
                            pltpu.VMEM((2, MOE_TM, D_MODEL), F32), pltpu.VMEM((2, MOE_TM, D_MODEL), F32),
                            pltpu.VMEM((D_MODEL, MOE_HIDDEN), BF16), pltpu.VMEM((D_MODEL, MOE_HIDDEN), BF16),
                            pltpu.VMEM((MOE_HIDDEN, D_MODEL), BF16),
                            pltpu.SemaphoreType.DMA((2,)), pltpu.SemaphoreType.DMA((2,))],
        ),
        out_shape=jax.ShapeDtypeStruct((2 * N_TOK, D_MODEL), F32),
        compiler_params=_cparams(("arbitrary",)),
        name="moe_experts",
    )(pos, tile_expert, n_valid, n_used, h, wg, wu, wd)


def _combine_body(x_ref, r_ref, gate_ref, g_ref, b_ref, z1_ref, z2_ref, o_ref):
    r = r_ref[...]
    y = r[:, 2:3] * z1_ref[...] + r[:, 3:4] * z2_ref[...]
    o_ref[...] = _layer_norm(DEEPNORM_ALPHA * x_ref[...] + gate_ref[0] * y, g_ref[...], b_ref[...])


def _combine(x, z, route, gate, ln_g, ln_b):
    tm = 512
    nt = N_TOK // tm
    row_spec = pl.BlockSpec((tm, D_MODEL), lambda i: (i, 0))
    vec_spec = pl.BlockSpec((1, D_MODEL), lambda i: (0, 0))
    return pl.pallas_call(
        _combine_body,
        grid=(nt,),
        in_specs=[row_spec, pl.BlockSpec((tm, LANES), lambda i: (i, 0)),
                  pl.BlockSpec((1, 1, D_MODEL), lambda i: (_mod_group(i * tm), 0, 0)),
                  vec_spec, vec_spec, row_spec, pl.BlockSpec((tm, D_MODEL), lambda i: (nt + i, 0))],
        out_specs=row_spec,
        out_shape=jax.ShapeDtypeStruct((N_TOK, D_MODEL), F32),
        compiler_params=_cparams(("parallel",)),
        name="moe_combine_ln",
    )(x, route, gate, ln_g, ln_b, z, z)


def _od_post_body(n_prev, q_ref, k_ref, v_ref, xs_ref, xs_lo_ref, xs_hi_ref, bc_ref, bc_lo_ref, bc_hi_ref, dt_ref,
                  cos_ref, sin_ref, cw_ref, cb_ref, dtb_ref, *refs):
    prev = refs[:2 * n_prev]
    qr_ref, kr_ref, vb_ref, xc_ref, bcc_ref, dtp_ref, c_k, c_v = refs[2 * n_prev:]
    i = pl.program_id(0)
    cos = cos_ref[...]
    sin = sin_ref[...]
    k = k_ref[...]
    v = v_ref[...]
    qr_ref[...] = _rope(q_ref[...], cos, sin, GQA_HEAD_DIM // 4).astype(BF16)
    kr_ref[...] = _rope(k, cos, sin, GQA_HEAD_DIM // 4).astype(BF16)
    vb_ref[...] = v.astype(BF16)

    is_prompt = i < BATCH
    tile_in_seq = (i - BATCH) % (DEC_SEQ // POST_TM)
    seq_start = is_prompt | (tile_in_seq == 0)
    seq_end = is_prompt | (tile_in_seq == DEC_SEQ // POST_TM - 1)

    def conv_silu(x, lo_ref, hi_ref, w, b):
        rows = x.shape[0]
        r = lax.broadcasted_iota(jnp.int32, x.shape, 0)
        lo = jnp.where(seq_start, 0.0, lo_ref[7:8, :])
        hi = jnp.where(seq_end, 0.0, hi_ref[0:1, :])
        before = jnp.where(r == 0, lo, pltpu.roll(x, 1, 0))
        after = jnp.where(r == rows - 1, hi, pltpu.roll(x, rows - 1, 0))
        return _silu(before * w[0:1] + x * w[1:2] + after * w[2:3] + b)

    xc_ref[...] = conv_silu(xs_ref[...], xs_lo_ref, xs_hi_ref, cw_ref[:, :SSD_D_INNER], cb_ref[:, :SSD_D_INNER])
    bcc_ref[...] = conv_silu(bc_ref[...], bc_lo_ref, bc_hi_ref, cw_ref[:, SSD_D_INNER:], cb_ref[:, SSD_D_INNER:])
    t = dt_ref[...] + dtb_ref[...]
    dtp_ref[...] = jnp.maximum(t, 0.0) + jnp.log1p(jnp.exp(-jnp.abs(t)))

    @pl.when(is_prompt)
    def _():
        if n_prev:
            c_k[0, 0] = prev[0][0, 0]
            c_v[0, 0] = prev[1][0, 0]
        c_k[0, n_prev] = k.reshape(SEQ, GQA_KV_HEADS, GQA_HEAD_DIM)
        c_v[0, n_prev] = v.reshape(SEQ, GQA_KV_HEADS, GQA_HEAD_DIM)


def _od_post(proj, cos, sin, conv_w, conv_b, dt_bias, prev):
    tm = POST_TM
    halo = 8
    per = tm // halo
    n_halo = N_TOK // halo
    n_prev = 1 if prev else 0
    blk = lambda w, c: pl.BlockSpec((tm, w), lambda i: (i, c))
    lo = lambda w, c: pl.BlockSpec((halo, w), lambda i: (jnp.maximum(i * per - 1, 0), c))
    hi = lambda w, c: pl.BlockSpec((halo, w), lambda i: (jnp.minimum((i + 1) * per, n_halo - 1), c))
    full = lambda a: pl.BlockSpec(a.shape, lambda i: (0, 0))
    oblk = lambda w: pl.BlockSpec((tm, w), lambda i: (i, 0))
    cache_spec = lambda layers: pl.BlockSpec((1, layers, SEQ, GQA_KV_HEADS, GQA_HEAD_DIM),
                                             lambda i: (jnp.minimum(i, BATCH - 1), 0, 0, 0, 0))
    cache_shape = jax.ShapeDtypeStruct((BATCH, n_prev + 1, SEQ, GQA_KV_HEADS, GQA_HEAD_DIM), F32)
    return pl.pallas_call(
        functools.partial(_od_post_body, n_prev),
        grid=(N_TOK // tm,),
        in_specs=[blk(1024, 0), blk(256, 12), blk(256, 13),
                  blk(1024, 1), lo(1024, 1), hi(1024, 1), blk(512, 7), lo(512, 7), hi(512, 7), blk(LANES, 32),
                  blk(LANES, 0), blk(LANES, 0), full(conv_w), full(conv_b), full(dt_bias)]
        + [cache_spec(1)] * (2 * n_prev),
        out_specs=[oblk(1024), oblk(256), oblk(256), oblk(1024), oblk(512), oblk(LANES),
                   cache_spec(n_prev + 1), cache_spec(n_prev + 1)],
        out_shape=[
            jax.ShapeDtypeStruct((N_TOK, 1024), BF16),
            jax.ShapeDtypeStruct((N_TOK, 256), BF16),
            jax.ShapeDtypeStruct((N_TOK, 256), BF16),
            jax.ShapeDtypeStruct((N_TOK, 1024), F32),
            jax.ShapeDtypeStruct((N_TOK, 512), F32),
            jax.ShapeDtypeStruct((N_TOK, LANES), F32),
            cache_shape, cache_shape,
        ],
        compiler_params=_cparams(("arbitrary",)),
        name="odd_post",
    )(proj, proj, proj, proj, proj, proj, proj, proj, proj, proj, cos, sin, conv_w, conv_b, dt_bias, *prev)


def _sink_column(sink_ref, kh, rows):
    g = GQA_HEADS // GQA_KV_HEADS
    return jnp.concatenate(
        [jnp.broadcast_to(sink_ref[kh * g + j:kh * g + j + 1, 0:1], (rows, 1)) for j in range(g)], axis=0)


def _gqa_prompt_body(q_ref, k_ref, v_ref, sink_ref, o_ref):
    g = GQA_HEADS // GQA_KV_HEADS
    d = GQA_HEAD_DIM
    rows = q_ref.shape[0]
    scale = d ** -0.5
    outs = []
    for kh in range(GQA_KV_HEADS):
        qg = jnp.concatenate([q_ref[:, (kh * g + j) * d:(kh * g + j + 1) * d] for j in range(g)], axis=0)
        k = k_ref[:, kh * d:(kh + 1) * d]
        v = v_ref[:, kh * d:(kh + 1) * d]
        s = _nt(qg, k) * scale
        snk = _sink_column(sink_ref, kh, rows)
        m = jnp.maximum(jnp.max(s, axis=-1, keepdims=True), snk)
        e = jnp.exp(s - m)
        l = jnp.sum(e, axis=-1, keepdims=True) + jnp.exp(snk - m)
        o = jnp.dot(e.astype(BF16), v, preferred_element_type=F32) / l
        outs.extend(o[j * rows:(j + 1) * rows] for j in range(g))
    o_ref[...] = jnp.concatenate(outs, axis=1).astype(BF16)


def _gqa_prompt(q, k, v, sink):
    spec = lambda w: pl.BlockSpec((SEQ, w), lambda i: (i, 0))
    return pl.pallas_call(
        _gqa_prompt_body,
        grid=(BATCH,),
        in_specs=[spec(1024), spec(256), spec(256), pl.BlockSpec((8, LANES), lambda i: (0, 0))],
        out_specs=spec(1024),
        out_shape=jax.ShapeDtypeStruct((N_PROMPT, 1024), BF16),
        compiler_params=_cparams(("parallel",)),
        name="gqa_dense",
    )(q, k, v, sink)


def _gqa_window_body(q_ref, k_ref, v_ref, kc_ref, vc_ref, sink_ref, o_ref):
    g = GQA_HEADS // GQA_KV_HEADS
    d = GQA_HEAD_DIM
    blk = q_ref.shape[0]
    span = 3 * blk
    n = pl.program_id(1)
    start = pl.multiple_of(jnp.clip((n - 1) * blk, 0, DEC_SEQ - span), blk)
    scale = d ** -0.5
    qpos = n * blk + lax.broadcasted_iota(jnp.int32, (blk, span), 0)
    kpos = start + lax.broadcasted_iota(jnp.int32, (blk, span), 1)
    valid1 = jnp.abs(qpos - kpos) <= WINDOW
    valid = jnp.concatenate([valid1] * g, axis=0)
    outs = []
    for kh in range(GQA_KV_HEADS):
        qg = jnp.concatenate([q_ref[:, (kh * g + j) * d:(kh * g + j + 1) * d] for j in range(g)], axis=0)
        kl = k_ref[pl.ds(start, span), kh * d:(kh + 1) * d]
        vl = v_ref[pl.ds(start, span), kh * d:(kh + 1) * d]
        kc = kc_ref[:, kh * d:(kh + 1) * d]
        vc = vc_ref[:, kh * d:(kh + 1) * d]
        sl = jnp.where(valid, _nt(qg, kl) * scale, -jnp.inf)
        sc = _nt(qg, kc) * scale
        snk = _sink_column(sink_ref, kh, blk)
        m = jnp.maximum(jnp.maximum(jnp.max(sl, axis=-1, keepdims=True), jnp.max(sc, axis=-1, keepdims=True)), snk)
        el = jnp.exp(sl - m)
        ec = jnp.exp(sc - m)
        l = jnp.sum(el, axis=-1, keepdims=True) + jnp.sum(ec, axis=-1, keepdims=True) + jnp.exp(snk - m)
        o = (jnp.dot(el.astype(BF16), vl, preferred_element_type=F32)
             + jnp.dot(ec.astype(BF16), vc, preferred_element_type=F32)) / l
        outs.extend(o[j * blk:(j + 1) * blk] for j in range(g))
    o_ref[...] = jnp.concatenate(outs, axis=1).astype(BF16)


def _gqa_window(q, k, v, cache_k, cache_v, sink):
    blk = WINDOW
    nq = DEC_SEQ // blk
    q0 = N_PROMPT // blk
    s0 = N_PROMPT // DEC_SEQ
    seq_spec = pl.BlockSpec((DEC_SEQ, 256), lambda b, j: (s0 + b, 0))
    ctx_spec = pl.BlockSpec((PAST_LEN, 256), lambda b, j: (b, 0))
    return pl.pallas_call(
        _gqa_window_body,
        grid=(DEC_BATCH, nq),
        in_specs=[pl.BlockSpec((blk, 1024), lambda b, j: (q0 + b * nq + j, 0)), seq_spec, seq_spec, ctx_spec,
                  ctx_spec,
                  pl.BlockSpec((8, LANES), lambda b, j: (0, 0))],
        out_specs=pl.BlockSpec((blk, 1024), lambda b, j: (b * nq + j, 0)),
        out_shape=jax.ShapeDtypeStruct((N_SAMPLE, 1024), BF16),
        compiler_params=_cparams(("parallel", "parallel")),
        name="gqa_window",
    )(q, k, v, cache_k, cache_v, sink)


def _ssd_body(has_h0, xs_ref, bc_ref, dt_ref, alog_ref, *refs):
    if has_h0:
        h0f_ref, h0b_ref, y_ref, stf_ref, stb_ref = refs
    else:
        y_ref, stf_ref, stb_ref = refs
    d = pl.program_id(1)
    c = pl.program_id(2)
    q = SSD_CHUNK
    hpg = SSD_HEADS // SSD_GROUPS

    row = lax.broadcasted_iota(jnp.int32, (q, LANES), 0)
    a = -jnp.exp(alog_ref[...])
    dta = dt_ref[...] * a

    def scan(x, rev):
        k = 1
        while k < q:
            if rev:
                x = x + jnp.where(row < q - k, pltpu.roll(x, q - k, 0), 0.0)
            else:
                x = x + jnp.where(row >= k, pltpu.roll(x, k, 0), 0.0)
            k *= 2
        return x

    li = lax.broadcasted_iota(jnp.int32, (q, q), 0)
    si = lax.broadcasted_iota(jnp.int32, (q, q), 1)

    def run(rev):
        st_ref = stb_ref if rev else stf_ref

        @pl.when(c == 0)
        def _():
            if has_h0:
                st_ref[0] = (h0b_ref if rev else h0f_ref)[0, 0]
            else:
                st_ref[0] = jnp.zeros(st_ref.shape[1:], F32)

        cum = scan(dta, rev)
        cum_t = cum.T
        edge = cum[0:1, :] if rev else cum[q - 1:q, :]
        mask = (li <= si) if rev else (li >= si)
        xs = xs_ref[...]
        dt = dt_ref[...]
        ys = []
        for g in range(SSD_GROUPS):
            bm16 = bc_ref[:, g * SSD_STATE:(g + 1) * SSD_STATE].astype(BF16)
            cm16 = bc_ref[:, (SSD_GROUPS + g) * SSD_STATE:(SSD_GROUPS + g + 1) * SSD_STATE].astype(BF16)
            cb = _nt(cm16, bm16)
            for hh in range(hpg):
                h = g * hpg + hh
                j = (SSD_HEADS if rev else 0) + h
                cum_col = cum[:, j:j + 1]
                seg = cum_col - cum_t[j:j + 1, :]
                decay = jnp.exp(jnp.where(mask, seg, -jnp.inf))
                xdt = xs[:, h * SSD_HEAD_DIM:(h + 1) * SSD_HEAD_DIM] * dt[:, j:j + 1]
                y_diag = jnp.dot((cb * decay).astype(BF16), xdt.astype(BF16), preferred_element_type=F32)
                tail = jnp.exp(edge[:, j:j + 1] - cum_col)
                state = lax.dot_general((xdt * tail).astype(BF16), bm16, (((0,), (0,)), ((), ())),
                                        preferred_element_type=F32)
                h_prev = st_ref[0, h]
                y_off = _nt(cm16, h_prev.astype(BF16)) * jnp.exp(cum_col)
                ys.append(y_diag + y_off)
                st_ref[0, h] = h_prev * jnp.exp(edge[:, j:j + 1]) + state
        y_ref[0] = jnp.concatenate(ys, axis=1)

    @pl.when(d == 0)
    def _():
        run(False)

    @pl.when(d == 1)
    def _():
        run(True)


def _ssd(xs, bc, dtp, a_log, n_seq, seq, row_off, h0=None):
    b = n_seq
    t = n_seq * seq
    nc = seq // SSD_CHUNK
    c0 = row_off // SSD_CHUNK

    def rmap(i, d, c):
        return i * nc + jnp.where(d == 0, c, nc - 1 - c)

    blk = lambda w: pl.BlockSpec((SSD_CHUNK, w), lambda i, d, c: (c0 + rmap(i, d, c), 0))
    st_spec = pl.BlockSpec((1, SSD_HEADS, SSD_HEAD_DIM, SSD_STATE), lambda i, d, c: (i, 0, 0, 0))
    st_shape = jax.ShapeDtypeStruct((b, SSD_HEADS, SSD_HEAD_DIM, SSD_STATE), F32)
    in_specs = [blk(1024), blk(512), blk(LANES), pl.BlockSpec((1, LANES), lambda i, d, c: (0, 0))]
    args = [xs, bc, dtp, a_log]
    if h0 is not None:
        layer, h0_f, h0_b = h0
        h0_spec = pl.BlockSpec((1, 1, SSD_HEADS, SSD_HEAD_DIM, SSD_STATE), lambda i, d, c: (i, layer, 0, 0, 0))
        in_specs += [h0_spec, h0_spec]
        args += [h0_f, h0_b]
    return pl.pallas_call(
        functools.partial(_ssd_body, h0 is not None),
        grid=(b, 2, nc),
        in_specs=in_specs,
        out_specs=[pl.BlockSpec((1, SSD_CHUNK, 1024), lambda i, d, c: (d, rmap(i, d, c), 0)), st_spec, st_spec],
        out_shape=[jax.ShapeDtypeStruct((2, t, 1024), F32), st_shape, st_shape],
        compiler_params=_cparams(("parallel", "arbitrary", "arbitrary")),
        name="ssd_scan",
    )(*args)


def _ssd_gate_body(y_ref, xs_ref, z_ref, dsk_ref, g_ref, o_ref):
    y = (y_ref[0] + y_ref[1] + xs_ref[...] * dsk_ref[...]) * _silu(z_ref[...])
    w = SSD_D_INNER // SSD_GROUPS
    outs = [_rms(y[:, g * w:(g + 1) * w], g_ref[:, g * w:(g + 1) * w]) for g in range(SSD_GROUPS)]
    o_ref[...] = jnp.concatenate(outs, axis=1).astype(BF16)


def _ssd_gate(y2, xs, proj, d_skip, norm_g, row_off):
    tm = 512
    t = y2.shape[1]
    off = row_off // tm
    return pl.pallas_call(
        _ssd_gate_body,
        grid=(t // tm,),
        in_specs=[pl.BlockSpec((2, tm, 1024), lambda i: (0, i, 0)),
                  pl.BlockSpec((tm, 1024), lambda i: (off + i, 0)),
                  pl.BlockSpec((tm, 1024), lambda i: (off + i, 2)),
                  pl.BlockSpec((1, 1024), lambda i: (0, 0)),
                  pl.BlockSpec((1, 1024), lambda i: (0, 0))],
        out_specs=pl.BlockSpec((tm, 1024), lambda i: (i, 0)),
        out_shape=jax.ShapeDtypeStruct((t, 1024), BF16),
        compiler_params=_cparams(("parallel",)),
        name="ssd_gate_norm",
    )(y2, xs, proj, d_skip, norm_g)


def _even_layer(x, mod, layer_idx, i, p, ctx, tables, prev_caches):
    w_in, q_norm, kv_norm, wq_b, wkv_b, lam, subln, w_out = p
    ctx_ckv, ctx_kpe, ctx_dk, ctx_dv = ctx
    cos, sin = tables
    lam_init = 0.8 - 0.6 * math.exp(-0.3 * layer_idx)

    w_r = jnp.concatenate([w_in[:, :1024], w_in[:, 1088:], w_in[:, 1024:1088],
                           jnp.zeros((D_MODEL, PROJ_N - w_in.shape[1]), F32)], axis=1).astype(BF16)
    proj = _proj(x, mod[0], mod[1], w_r)
    qn, ckv, dq, dk, dv, kpe, *caches = _ev_post(proj, cos, sin, q_norm[None], kv_norm[None], prev_caches)

    wq = wq_b.reshape(MLA_Q_RANK, MLA_HEADS, MLA_NOPE + MLA_ROPE)
    wq_r = jnp.pad(wq, ((0, 0), (0, 0), (0, MLA_ROPE))).reshape(MLA_Q_RANK, -1).astype(BF16)
    wkv = wkv_b.astype(BF16)
    q = _mm(qn, wq_r, BF16, 512, N_TOK // 512)
    kv = _mm(ckv, wkv, BF16, 512, N_TOK // 512)
    n_even = ctx_ckv.shape[1]
    kv_ctx = _mm(ctx_ckv.reshape(-1, MLA_KV_RANK), wkv, BF16, PAST_LEN, DEC_BATCH,
                 row_block=lambda b: b * n_even + i)

    n_ctx = DEC_BATCH * PAST_LEN
    kpe_ctx = jnp.pad(ctx_kpe[:, i].reshape(n_ctx, MLA_ROPE), ((0, 0), (0, LANES - MLA_ROPE))).astype(BF16)
    dk_ctx = ctx_dk[:, i].reshape(n_ctx, -1).astype(BF16)
    dv_ctx = ctx_dv[:, i].reshape(n_ctx, -1).astype(BF16)

    args = (lam_init, q, cos, sin, kv, kpe, dq, dk, dv, lam, subln[None])
    o1p, o2p = _even_attn(*args, n_seq=BATCH, seq=SEQ, row_off=0)
    o1s, o2s = _even_attn(*args, n_seq=DEC_BATCH, seq=DEC_SEQ, row_off=N_PROMPT,
                          ctx=(kv_ctx, kpe_ctx, dk_ctx, dv_ctx))
    return (o1p, o1s, o2p, o2s), w_out.astype(BF16), caches


def _odd_layer(x, mod, i, p, ctx, tables, prev_caches):
    w_in, sink, conv_w, conv_b, dt_bias, a_log, d_skip, norm_g, w_out = p
    k_ctx, v_ctx, h0_f, h0_b = ctx
    cos, sin = tables

    w_r = jnp.concatenate([w_in[:, :1024], w_in[:, 2560:3584], w_in[:, 1536:2560], w_in[:, 1024:1536],
                           w_in[:, 3584:], jnp.zeros((D_MODEL, PROJ_N - w_in.shape[1]), F32)],
                          axis=1).astype(BF16)
    proj = _proj(x, mod[0], mod[1], w_r)
    cb = conv_b[None]
    dtb = jnp.pad(dt_bias.reshape(1, -1), ((0, 0), (0, LANES - 2 * SSD_HEADS)))
    alog = jnp.pad(a_log.reshape(1, -1), ((0, 0), (0, LANES - 2 * SSD_HEADS)))
    q, k, v, xs, bc, dtp, *caches = _od_post(proj, cos, sin, conv_w, cb, dtb, prev_caches)
    sink_b = jnp.broadcast_to(sink[:, None], (GQA_HEADS, LANES))

    o_att_p = _gqa_prompt(q, k, v, sink_b)
    n_ctx = DEC_BATCH * PAST_LEN
    o_att_s = _gqa_window(q, k, v, k_ctx[:, i].reshape(n_ctx, -1).astype(BF16),
                          v_ctx[:, i].reshape(n_ctx, -1).astype(BF16), sink_b)

    y_p, st_f, st_b = _ssd(xs, bc, dtp, alog, BATCH, SEQ, 0)
    y_s, _, _ = _ssd(xs, bc, dtp, alog, DEC_BATCH, DEC_SEQ, N_PROMPT, h0=(i, h0_f, h0_b))
    dsk = jnp.repeat(d_skip, SSD_HEAD_DIM)[None]
    y_n_p = _ssd_gate(y_p, xs, proj, dsk, norm_g[None], 0)
    y_n_s = _ssd_gate(y_s, xs, proj, dsk, norm_g[None], N_PROMPT)
    return (o_att_p, o_att_s, y_n_p, y_n_s), w_out.astype(BF16), caches, (st_f, st_b)


def kernel(x_prompt, x_sample, cache_mla_ckv, cache_mla_kpe, cache_diff_k, cache_diff_v, cache_gqa_k, cache_gqa_v, state_ssd_fwd, state_ssd_bwd, c, c_ctx, w_mod, b_mod, ln1_g, ln1_b, ln2_g, ln2_b, ev_w_in, mla_q_norm, mla_kv_norm, mla_wq_b, mla_wkv_b, diff_lambda, diff_subln, ev_w_out, od_w_in, gqa_sink, ssd_conv_w, ssd_conv_b, ssd_dt_bias, ssd_a_log, ssd_d, ssd_norm, od_w_out, moe_router_group, moe_router_expert, moe_w_gate, moe_w_up, moe_w_down):
    x = jnp.concatenate([x_prompt.reshape(N_PROMPT, D_MODEL), x_sample.reshape(N_SAMPLE, D_MODEL)], axis=0)
    cs = jnp.concatenate([c_ctx[None], c, jnp.zeros((8 - 1 - DEC_BATCH, D_MODEL), F32)], axis=0)
    mods = _adaln(cs, w_mod, b_mod)[:, :1 + DEC_BATCH].reshape(DEPTH, 1 + DEC_BATCH, 6, 1, D_MODEL)
    tables_64 = _rope_tables(64)
    tables_128 = _rope_tables(GQA_HEAD_DIM)

    ev_caches, od_caches, ssd_states = (), (), []
    for l in range(DEPTH):
        i = l // 2
        mod = [mods[l, :, k] for k in range(6)]
        if l % 2 == 0:
            p = (ev_w_in[i], mla_q_norm[i], mla_kv_norm[i], mla_wq_b[i], mla_wkv_b[i], diff_lambda[i],
                 diff_subln[i], ev_w_out[i])
            ctx = (cache_mla_ckv, cache_mla_kpe, cache_diff_k, cache_diff_v)
            mix, w_out, ev_caches = _even_layer(x, mod, l, i, p, ctx, tables_64, ev_caches)
        else:
            p = (od_w_in[i], gqa_sink[i], ssd_conv_w[i], ssd_conv_b[i], ssd_dt_bias[i], ssd_a_log[i], ssd_d[i],
                 ssd_norm[i], od_w_out[i])
            ctx = (cache_gqa_k, cache_gqa_v, state_ssd_fwd, state_ssd_bwd)
            mix, w_out, od_caches, states = _odd_layer(x, mod, i, p, ctx, tables_128, od_caches)
            ssd_states.append(states)
        w_router = jnp.concatenate([moe_router_group[l], moe_router_expert[l],
                                    jnp.zeros((D_MODEL, LANES - MOE_GROUPS - MOE_EXPERTS), F32)], axis=1)
        w_router_hi = w_router.astype(BF16)
        w_router = jnp.stack([w_router_hi, (w_router - w_router_hi.astype(F32)).astype(BF16)])
        x, h2, route = _out_proj(*mix, w_out, x, mod[2], ln1_g[l][None], ln1_b[l][None], mod[3], mod[4], w_router)
        pos, tile_expert, n_valid, n_used = _route_meta(route)
        z = _moe_experts(l, h2, pos, tile_expert, n_valid, n_used, moe_w_gate, moe_w_up, moe_w_down)
        x = _combine(x, z, route, mod[5], ln2_g[l][None], ln2_b[l][None])

    y_prompt = x[:N_PROMPT].reshape(BATCH, SEQ, D_MODEL)
    y_sample = x[N_PROMPT:].reshape(DEC_BATCH, DEC_SEQ, D_MODEL)
    new_ssd_fwd = jnp.stack([s[0] for s in ssd_states], axis=1)
    new_ssd_bwd = jnp.stack([s[1] for s in ssd_states], axis=1)
    return (y_prompt, y_sample, *ev_caches, *od_caches, new_ssd_fwd, new_ssd_bwd)
```

```python
import functools
import math

import jax
import jax.numpy as jnp
from jax import lax
from jax.experimental import pallas as pl
from jax.experimental.pallas import tpu as pltpu

F32 = jnp.float32
BF16 = jnp.bfloat16

D_MODEL = 2048
BATCH = 16
SEQ = 256
DEPTH = 4
DEC_BATCH = 2
DEC_SEQ = 1024
PAST_LEN = 512
GRID_W = 64
ROPE_THETA = 10000.0
DEEPNORM_ALPHA = (2.0 * DEPTH) ** 0.25
LN_EPS = 1e-5
RMS_EPS = 1e-6

MLA_HEADS = 8
MLA_Q_RANK = 512
MLA_KV_RANK = 512
MLA_NOPE = 128
MLA_ROPE = 64
MLA_V = 128
DIFF_HEADS = 8
DIFF_QK = 64
DIFF_V = 128
GQA_HEADS = 8
GQA_KV_HEADS = 2
GQA_HEAD_DIM = 128
WINDOW = 128
SSD_HEADS = 16
SSD_HEAD_DIM = 64
SSD_D_INNER = SSD_HEADS * SSD_HEAD_DIM
SSD_GROUPS = 2
SSD_STATE = 128
SSD_CHUNK = 128
MOE_GROUPS = 4
MOE_EPG = 4
MOE_EXPERTS = 16
MOE_HIDDEN = 512

N_PROMPT = BATCH * SEQ
N_SAMPLE = DEC_BATCH * DEC_SEQ
N_TOK = N_PROMPT + N_SAMPLE
PROJ_N = 4224
LANES = 128
VMEM_LIMIT = 56 * 1024 * 1024

PROJ_TM, PROJ_TN = 1024, 1408
OUT_TM = 512
ATT_TQ = 256
POST_TM = 256
MOE_TM = 256
MOE_SLOTS = 2 * N_TOK
MOE_TILES = MOE_SLOTS // MOE_TM + MOE_EXPERTS
MOE_ROWS = MOE_TILES * MOE_TM
SLAB = (D_MODEL // LANES, LANES)


def _cparams(sem):
    return pltpu.CompilerParams(dimension_semantics=sem, vmem_limit_bytes=VMEM_LIMIT)


def _mod_group(row0):
    return jnp.maximum(row0 // DEC_SEQ - (N_PROMPT // DEC_SEQ - 1), 0)


def _silu(x):
    return x / (1.0 + jnp.exp(-x))


def _nt(a, b):
    return lax.dot_general(a, b, (((1,), (1,)), ((), ())), preferred_element_type=F32)


def _rms(x, g):
    return x * lax.rsqrt(jnp.mean(x * x, axis=-1, keepdims=True) + RMS_EPS) * g


def _layer_norm(y, g, b):
    mu = jnp.mean(y, axis=-1, keepdims=True)
    yc = y - mu
    return yc * lax.rsqrt(jnp.mean(yc * yc, axis=-1, keepdims=True) + LN_EPS) * g + b


def _adaln_body(c_ref, w_ref, b_ref, o_ref):
    a = _silu(c_ref[...])
    o_ref[0] = jnp.dot(a, w_ref[0], preferred_element_type=F32) + b_ref[0]


def _adaln(cs, w_mod, b_mod):
    tn = 1024
    n = w_mod.shape[-1]
    return pl.pallas_call(
        _adaln_body,
        grid=(DEPTH, n // tn),
        in_specs=[
            pl.BlockSpec((8, D_MODEL), lambda l, j: (0, 0)),
            pl.BlockSpec((1, D_MODEL, tn), lambda l, j: (l, 0, j)),
            pl.BlockSpec((1, 1, tn), lambda l, j: (l, 0, j)),
        ],
        out_specs=pl.BlockSpec((1, 8, tn), lambda l, j: (l, 0, j)),
        out_shape=jax.ShapeDtypeStruct((DEPTH, 8, n), F32),
        compiler_params=_cparams(("parallel", "parallel")),
        name="adaln",
    )(cs, w_mod, b_mod.reshape(DEPTH, 1, n))


def _proj_body(x_ref, sh_ref, sc_ref, w_ref, o_ref, h_scr):
    @pl.when(pl.program_id(1) == 0)
    def _():
        h = x_ref[...] * (1.0 + sc_ref[0]) + sh_ref[0]
        h_scr[...] = h.astype(BF16)

    o_ref[...] = jnp.dot(h_scr[...], w_ref[...], preferred_element_type=F32)


def _proj(x, shift, scale, w):
    tm, tn = PROJ_TM, PROJ_TN
    n = w.shape[1]
    mod_spec = pl.BlockSpec((1, 1, D_MODEL), lambda i, j: (_mod_group(i * tm), 0, 0))
    return pl.pallas_call(
        _proj_body,
        grid=(N_TOK // tm, n // tn),
        in_specs=[
            pl.BlockSpec((tm, D_MODEL), lambda i, j: (i, 0)),
            mod_spec,
            mod_spec,
            pl.BlockSpec((D_MODEL, tn), lambda i, j: (0, j)),
        ],
        out_specs=pl.BlockSpec((tm, tn), lambda i, j: (i, j)),
        out_shape=jax.ShapeDtypeStruct((N_TOK, n), F32),
        scratch_shapes=[pltpu.VMEM((tm, D_MODEL), BF16)],
        compiler_params=_cparams(("parallel", "arbitrary")),
        name="in_proj",
    )(x, shift, scale, w)


def _mm_body(a_ref, b_ref, o_ref):
    o_ref[...] = jnp.dot(a_ref[...].astype(BF16), b_ref[...], preferred_element_type=F32).astype(o_ref.dtype)


def _mm(a, b, out_dtype, tm, n_tiles, row_block=lambda i: i):
    k = a.shape[1]
    n = b.shape[1]
    return pl.pallas_call(
        _mm_body,
        grid=(n_tiles,),
        in_specs=[pl.BlockSpec((tm, k), lambda i: (row_block(i), 0)), pl.BlockSpec((k, n), lambda i: (0, 0))],
        out_specs=pl.BlockSpec((tm, n), lambda i: (i, 0)),
        out_shape=jax.ShapeDtypeStruct((n_tiles * tm, n), out_dtype),
        compiler_params=_cparams(("parallel",)),
        name="mm_resident",
    )(a, b)


def _rope(x, cos, sin_signed, quarter):
    rows, w = x.shape
    reps = w // LANES
    if reps > 1:
        cos = jnp.concatenate([cos] * reps, axis=1)
        sin_signed = jnp.concatenate([sin_signed] * reps, axis=1)
    lane = lax.broadcasted_iota(jnp.int32, x.shape, 1)
    first = (lane % (2 * quarter)) < quarter
    up = pltpu.roll(x, w - quarter, 1)
    dn = pltpu.roll(x, quarter, 1)
    return x * cos + jnp.where(first, up, dn) * sin_signed


def _rope_tables(rdim):
    half = rdim // 2
    quarter = half // 2
    pos = jnp.arange(DEC_SEQ)
    row = (pos // GRID_W).astype(F32)
    col = (pos % GRID_W).astype(F32)
    inv = ROPE_THETA ** (-jnp.arange(quarter, dtype=F32) * 2.0 / half)
    lane = jnp.arange(LANES)
    r = lane % rdim
    use_col = (r // half) == 1
    j = r % quarter
    p = jnp.where(use_col[None, :], col[:, None], row[:, None])
    ang = p * inv[j][None, :]
    sign = jnp.where((r % half) < quarter, -1.0, 1.0).astype(F32)
    cos = jnp.cos(ang)
    sin = jnp.sin(ang) * sign[None, :]
    cos = jnp.concatenate([jnp.ones((N_PROMPT, LANES), F32), cos, cos], axis=0)
    sin = jnp.concatenate([jnp.zeros((N_PROMPT, LANES), F32), sin, sin], axis=0)
    return cos, sin


def _ev_post_body(n_prev, ql_ref, kvl_ref, dq_ref, dk_ref, dv_ref, kpe_ref, cos_ref, sin_ref, gq_ref, gkv_ref,
                  *refs):
    prev = refs[:4 * n_prev]
    qn_ref, ckv_ref, dqr_ref, dkr_ref, dvb_ref, kper_ref, c_ckv, c_kpe, c_dk, c_dv = refs[4 * n_prev:]
    cos = cos_ref[...]
    sin = sin_ref[...]
    qn_ref[...] = _rms(ql_ref[...], gq_ref[...]).astype(BF16)
    ckv = _rms(kvl_ref[...], gkv_ref[...])
    ckv_ref[...] = ckv.astype(BF16)
    dqr_ref[...] = _rope(dq_ref[...], cos, sin, DIFF_QK // 4).astype(BF16)
    dk = _rope(dk_ref[...], cos, sin, DIFF_QK // 4)
    dkr_ref[...] = dk.astype(BF16)
    dv = dv_ref[...]
    dvb_ref[...] = dv.astype(BF16)
    kpe = _rope(kpe_ref[...], cos, sin, MLA_ROPE // 4)
    kper_ref[...] = kpe.astype(BF16)

    @pl.when(pl.program_id(0) < BATCH)
    def _():
        if n_prev:
            c_ckv[0, 0] = prev[0][0, 0]
            c_kpe[0, 0] = prev[1][0, 0]
            c_dk[0, 0] = prev[2][0, 0]
            c_dv[0, 0] = prev[3][0, 0]
        c_ckv[0, n_prev] = ckv
        c_kpe[0, n_prev] = kpe[:, :MLA_ROPE]
        c_dk[0, n_prev] = dk.reshape(SEQ, DIFF_HEADS, 2 * DIFF_QK)
        c_dv[0, n_prev] = dv.reshape(SEQ, DIFF_HEADS, DIFF_V)


def _ev_post(proj, cos, sin, gq, gkv, prev):
    tm = POST_TM
    n_prev = 1 if prev else 0
    nl = n_prev + 1
    row = lambda c: (lambda i: (i, c))
    seq4 = lambda i: (jnp.minimum(i, BATCH - 1), 0, 0, 0)
    seq5 = lambda i: (jnp.minimum(i, BATCH - 1), 0, 0, 0, 0)
    cache_shapes = [(MLA_KV_RANK,), (MLA_ROPE,), (DIFF_HEADS, 2 * DIFF_QK), (DIFF_HEADS, DIFF_V)]

    def cache_spec(layers, tail):
        return pl.BlockSpec((1, layers, SEQ) + tail, seq4 if len(tail) == 1 else seq5)

    return pl.pallas_call(
        functools.partial(_ev_post_body, n_prev),
        grid=(N_TOK // tm,),
        in_specs=[
            pl.BlockSpec((tm, 512), row(0)),
            pl.BlockSpec((tm, 512), row(1)),
            pl.BlockSpec((tm, 1024), row(1)),
            pl.BlockSpec((tm, 1024), row(2)),
            pl.BlockSpec((tm, 1024), row(3)),
            pl.BlockSpec((tm, LANES), row(32)),
            pl.BlockSpec((tm, LANES), row(0)),
            pl.BlockSpec((tm, LANES), row(0)),
            pl.BlockSpec((1, 512), lambda i: (0, 0)),
            pl.BlockSpec((1, 512), lambda i: (0, 0)),
        ] + [cache_spec(1, t) for t in cache_shapes] * n_prev,
        out_specs=[
            pl.BlockSpec((tm, 512), row(0)),
            pl.BlockSpec((tm, 512), row(0)),
            pl.BlockSpec((tm, 1024), row(0)),
            pl.BlockSpec((tm, 1024), row(0)),
            pl.BlockSpec((tm, 1024), row(0)),
            pl.BlockSpec((tm, LANES), row(0)),
        ] + [cache_spec(nl, t) for t in cache_shapes],
        out_shape=[
            jax.ShapeDtypeStruct((N_TOK, 512), BF16),
            jax.ShapeDtypeStruct((N_TOK, 512), BF16),
            jax.ShapeDtypeStruct((N_TOK, 1024), BF16),
            jax.ShapeDtypeStruct((N_TOK, 1024), BF16),
            jax.ShapeDtypeStruct((N_TOK, 1024), BF16),
            jax.ShapeDtypeStruct((N_TOK, LANES), BF16),
        ] + [jax.ShapeDtypeStruct((BATCH, nl, SEQ) + t, F32) for t in cache_shapes],
        compiler_params=_cparams(("arbitrary",)),
        name="even_post",
    )(proj, proj, proj, proj, proj, proj, cos, sin, gq, gkv, *prev)


def _softmax_av(scores, values):
    m = functools.reduce(jnp.maximum, [jnp.max(s, axis=-1, keepdims=True) for s in scores])
    es = [jnp.exp(s - m) for s in scores]
    l = sum(jnp.sum(e, axis=-1, keepdims=True) for e in es)
    o = sum(jnp.dot(e.astype(BF16), v, preferred_element_type=F32) for e, v in zip(es, values))
    return o / l


def _even_attn_body(lam_init, has_ctx, q_ref, cos_ref, sin_ref, kv_ref, kpe_ref, dq_ref, dk_ref, dv_ref,
                    lam_ref, sub_ref, *refs):
    if has_ctx:
        kvc_ref, kpec_ref, dkc_ref, dvc_ref, omla_ref, odiff_ref = refs
    else:
        omla_ref, odiff_ref = refs
    q = q_ref[...]
    qr = _rope(q[:, MLA_NOPE:].astype(F32), cos_ref[...], sin_ref[...], MLA_ROPE // 4).astype(BF16)
    qcat = jnp.concatenate([q[:, :MLA_NOPE], qr], axis=1)
    mla_scale = (MLA_NOPE + MLA_ROPE) ** -0.5
    scores = [_nt(qcat, jnp.concatenate([kv_ref[:, :MLA_NOPE], kpe_ref[...]], axis=1)) * mla_scale]
    values = [kv_ref[:, MLA_NOPE:]]
    if has_ctx:
        scores.append(_nt(qcat, jnp.concatenate([kvc_ref[:, :MLA_NOPE], kpec_ref[...]], axis=1)) * mla_scale)
        values.append(kvc_ref[:, MLA_NOPE:])
    omla_ref[...] = _softmax_av(scores, values).astype(BF16)

    lam = lam_ref[...]
    lam_full = (jnp.exp(jnp.sum(lam[0:1] * lam[1:2], axis=-1, keepdims=True))
                - jnp.exp(jnp.sum(lam[2:3] * lam[3:4], axis=-1, keepdims=True)) + lam_init)
    dq = dq_ref[...] * jnp.asarray(DIFF_QK ** -0.5, BF16)
    lane = lax.broadcasted_iota(jnp.int32, dq.shape, 1)
    zero = jnp.zeros_like(dq)
    keys = [dk_ref[...]]
    values = [dv_ref[...]]
    if has_ctx:
        keys.append(dkc_ref[...])
        values.append(dvc_ref[...])
    a1 = _softmax_av([_nt(jnp.where(lane < DIFF_QK, dq, zero), k) for k in keys], values)
    a2 = _softmax_av([_nt(jnp.where(lane >= DIFF_QK, dq, zero), k) for k in keys], values)
    odiff_ref[...] = (_rms(a1 - lam_full * a2, sub_ref[...]) * (1.0 - lam_init)).astype(BF16)


def _even_attn(lam_init, q, cos, sin, kv, kpe, dq, dk, dv, lam, subln, n_seq, seq, row_off, ctx=None):
    tq = ATT_TQ
    nq = seq // tq
    q0 = row_off // tq
    s0 = row_off // seq
    qspec = lambda w: pl.BlockSpec((tq, w), lambda b, j, h: (q0 + b * nq + j, h))
    kspec = lambda w: pl.BlockSpec((seq, w), lambda b, j, h: (s0 + b, h))
    tspec = pl.BlockSpec((tq, LANES), lambda b, j, h: (q0 + b * nq + j, 0))
    ospec = pl.BlockSpec((tq, LANES), lambda b, j, h: (b * nq + j, h))
    in_specs = [qspec(256), tspec, tspec, kspec(256),
                pl.BlockSpec((seq, LANES), lambda b, j, h: (s0 + b, 0)),
                qspec(LANES), kspec(LANES), kspec(LANES),
                pl.BlockSpec((4, DIFF_QK), lambda b, j, h: (0, 0)),
                pl.BlockSpec((1, DIFF_V), lambda b, j, h: (0, 0))]
    args = [q, cos, sin, kv, kpe, dq, dk, dv, lam, subln]
    if ctx is not None:
        cspec = lambda w: pl.BlockSpec((PAST_LEN, w), lambda b, j, h: (b, h))
        in_specs += [cspec(256), pl.BlockSpec((PAST_LEN, LANES), lambda b, j, h: (b, 0)), cspec(LANES), cspec(LANES)]
        args += list(ctx)
    return pl.pallas_call(
        functools.partial(_even_attn_body, lam_init, ctx is not None),
        grid=(n_seq, nq, MLA_HEADS),
        in_specs=in_specs,
        out_specs=[ospec, ospec],
        out_shape=[jax.ShapeDtypeStruct((n_seq * seq, 1024), BF16)] * 2,
        compiler_params=_cparams(("parallel", "parallel", "parallel")),
        name="even_attn",
    )(*args)


def _route(logits):
    lane = lax.broadcasted_iota(jnp.int32, logits.shape, 1).astype(F32)
    neg = -jnp.inf
    none = float(LANES)
    is_g = lane < MOE_GROUPS
    lg = jnp.where(is_g, logits, neg)
    mg = jnp.max(lg, axis=-1, keepdims=True)
    g_val = 1.0 / jnp.sum(jnp.exp(lg - mg), axis=-1, keepdims=True)
    g_idx = jnp.min(jnp.where(is_g & (lg == mg), lane, none), axis=-1, keepdims=True)
    e_lo = MOE_GROUPS + g_idx * MOE_EPG
    is_e = (lane >= e_lo) & (lane < e_lo + MOE_EPG)
    le = jnp.where(is_e, logits, neg)
    p = jnp.exp(le - jnp.max(le, axis=-1, keepdims=True))
    p1 = jnp.max(p, axis=-1, keepdims=True)
    i1 = jnp.min(jnp.where(is_e & (p == p1), lane, none), axis=-1, keepdims=True)
    rest = is_e & (lane != i1)
    p2 = jnp.max(jnp.where(rest, p, neg), axis=-1, keepdims=True)
    i2 = jnp.min(jnp.where(rest & (p == p2), lane, none), axis=-1, keepdims=True)
    s = g_val / (p1 + p2)
    return jnp.where(lane == 0.0, i1 - MOE_GROUPS,
                     jnp.where(lane == 1.0, i2 - MOE_GROUPS,
                               jnp.where(lane == 2.0, p1 * s, jnp.where(lane == 3.0, p2 * s, 0.0))))


def _out_proj_body(a1p_ref, a1s_ref, a2p_ref, a2s_ref, w_ref, x_ref, gate_ref, g_ref, b_ref, sh_ref, sc_ref,
                   wr_ref, xo_ref, h_ref, comb_ref):
    is_prompt = pl.program_id(0) < N_PROMPT // OUT_TM
    a1 = jnp.where(is_prompt, a1p_ref[...], a1s_ref[...])
    a2 = jnp.where(is_prompt, a2p_ref[...], a2s_ref[...])
    k1 = a1.shape[1]
    acc = jnp.dot(a1, w_ref[:k1, :], preferred_element_type=F32)
    acc = acc + jnp.dot(a2, w_ref[k1:, :], preferred_element_type=F32)
    xn = _layer_norm(DEEPNORM_ALPHA * x_ref[...] + gate_ref[0] * acc, g_ref[...], b_ref[...])
    xo_ref[...] = xn
    h = xn * (1.0 + sc_ref[0]) + sh_ref[0]
    h_ref[...] = h.reshape(h_ref.shape)
    h_hi = h.astype(BF16)
    h_lo = (h - h_hi.astype(F32)).astype(BF16)
    logits = (jnp.dot(h_hi, wr_ref[0], preferred_element_type=F32)
              + jnp.dot(h_lo, wr_ref[0], preferred_element_type=F32)
              + jnp.dot(h_hi, wr_ref[1], preferred_element_type=F32))
    comb_ref[...] = _route(logits)


def _out_proj(a1p, a1s, a2p, a2s, w, x, gate, ln_g, ln_b, shift, scale, w_router):
    tm = OUT_TM
    np_tiles = N_PROMPT // tm
    k1, k2 = a1p.shape[1], a2p.shape[1]
    mod_spec = pl.BlockSpec((1, 1, D_MODEL), lambda i: (_mod_group(i * tm), 0, 0))
    vec_spec = pl.BlockSpec((1, D_MODEL), lambda i: (0, 0))
    row_spec = pl.BlockSpec((tm, D_MODEL), lambda i: (i, 0))
    pspec = lambda k: pl.BlockSpec((tm, k), lambda i: (jnp.minimum(i, np_tiles - 1), 0))
    sspec = lambda k: pl.BlockSpec((tm, k), lambda i: (jnp.maximum(i - np_tiles, 0), 0))
    return pl.pallas_call(
        _out_proj_body,
        grid=(N_TOK // tm,),
        in_specs=[
            pspec(k1), sspec(k1), pspec(k2), sspec(k2),
            pl.BlockSpec((k1 + k2, D_MODEL), lambda i: (0, 0)),
            row_spec, mod_spec, vec_spec, vec_spec, mod_spec, mod_spec,
            pl.BlockSpec((2, D_MODEL, LANES), lambda i: (0, 0, 0)),
        ],
        out_specs=[row_spec, pl.BlockSpec((tm,) + SLAB, lambda i: (i, 0, 0)),
                   pl.BlockSpec((tm, LANES), lambda i: (i, 0))],
        out_shape=[
            jax.ShapeDtypeStruct((N_TOK, D_MODEL), F32),
            jax.ShapeDtypeStruct((N_TOK,) + SLAB, F32),
            jax.ShapeDtypeStruct((N_TOK, LANES), F32),
        ],
        compiler_params=_cparams(("parallel",)),
        name="out_proj_ln_router",
    )(a1p, a1s, a2p, a2s, w, x, gate, ln_g, ln_b, shift, scale, w_router)


def _route_meta(route):
    e = route[:, :2].astype(jnp.int32).reshape(-1)
    onehot = (e[:, None] == jnp.arange(MOE_EXPERTS, dtype=jnp.int32)[None, :]).astype(jnp.int32)
    csum = jnp.cumsum(onehot, axis=0)
    rank = jnp.sum((csum - onehot) * onehot, axis=1)
    count = csum[-1]
    ntile = (count + MOE_TM - 1) // MOE_TM
    tile_end = jnp.cumsum(ntile)
    tile_off = tile_end - ntile
    pos = jnp.sum(onehot * tile_off[None, :], axis=1) * MOE_TM + rank
    n_used = tile_end[-1:]
    j = jnp.arange(MOE_TILES, dtype=jnp.int32)
    tile_expert = jnp.sum((tile_end[None, :] <= jnp.minimum(j, n_used - 1)[:, None]).astype(jnp.int32), axis=1)
    is_e = tile_expert[:, None] == jnp.arange(MOE_EXPERTS, dtype=jnp.int32)[None, :]
    left = jnp.sum(jnp.where(is_e, (count - (j[:, None] - tile_off[None, :]) * MOE_TM)[...], 0), axis=1)
    n_valid = jnp.where(j < n_used, jnp.clip(left, 0, MOE_TM), 0)
    return pos, tile_expert, n_valid, n_used


def _moe_body(pos_ref, te_ref, nv_ref, nu_ref, h_ref, wg_ref, wu_ref, wd_ref, z_ref,
              inv, xbuf, zbuf, wg_s, wu_s, wd_s, gsem, ssem):
    j = pl.program_id(0)
    slot = j % 2
    n_used = nu_ref[0]
    group = 8

    def gather_row(tile, s, r):
        tok = lax.shift_right_logical(inv[tile * MOE_TM + r], 1)
        return pltpu.make_async_copy(h_ref.at[tok], xbuf.at[s, r], gsem.at[s])

    def scatter_row(tile, s, r):
        sl = inv[tile * MOE_TM + r]
        dst = (sl & 1) * N_TOK + lax.shift_right_logical(sl, 1)
        return pltpu.make_async_copy(zbuf.at[s, r], z_ref.at[dst], ssem.at[s])

    def for_valid_rows(tile, fn):
        n = nv_ref[tile]

        def block(g, c):
            for u in range(group):
                fn(g * group + u)
            return c

        def single(r, c):
            fn(r)
            return c

        full = lax.shift_right_logical(n, 3)
        lax.fori_loop(0, full, block, 0)
        lax.fori_loop(full * group, n, single, 0)

    @pl.when(j == 0)
    def _():
        def fill(g, c):
            for u in range(group):
                sl = g * group + u
                inv[pos_ref[sl]] = sl
            return c
        lax.fori_loop(0, MOE_SLOTS // group, fill, 0)
        xbuf[...] = jnp.zeros_like(xbuf)
        for_valid_rows(0, lambda r: gather_row(0, 0, r).start())

    changed = (j == 0) | (te_ref[j] != te_ref[jnp.maximum(j - 1, 0)])

    @pl.when(changed)
    def _():
        wg_s[...] = wg_ref[0, 0].astype(BF16)
        wu_s[...] = wu_ref[0, 0].astype(BF16)
        wd_s[...] = wd_ref[0, 0].astype(BF16)

    @pl.when(j < n_used)
    def _():
        @pl.when(j + 1 < n_used)
        def _():
            for_valid_rows(j + 1, lambda r: gather_row(j + 1, 1 - slot, r).start())

        for_valid_rows(j, lambda r: gather_row(j, slot, r).wait())

        @pl.when(j >= 2)
        def _():
            for_valid_rows(j - 2, lambda r: scatter_row(j - 2, slot, r).wait())

        x = xbuf[slot].reshape(MOE_TM, D_MODEL).astype(BF16)
        hg = jnp.dot(x, wg_s[...], preferred_element_type=F32)
        hu = jnp.dot(x, wu_s[...], preferred_element_type=F32)
        act = (_silu(hg) * hu).astype(BF16)
        zbuf[slot] = jnp.dot(act, wd_s[...], preferred_element_type=F32).reshape((MOE_TM,) + SLAB)
        for_valid_rows(j, lambda r: scatter_row(j, slot, r).start())

        @pl.when(j == n_used - 1)
        def _():
            @pl.when(j >= 1)
            def _():
                for_valid_rows(j - 1, lambda r: scatter_row(j - 1, 1 - slot, r).wait())
            for_valid_rows(j, lambda r: scatter_row(j, slot, r).wait())


def _moe_experts(layer, h, pos, tile_expert, n_valid, n_used, wg, wu, wd):
    wspec = lambda r, c: pl.BlockSpec((1, 1, r, c), lambda j, pos_ref, te, nv, nu: (layer, te[j], 0, 0))
    return pl.pallas_call(
        _moe_body,
        grid_spec=pltpu.PrefetchScalarGridSpec(
            num_scalar_prefetch=4,
            grid=(MOE_TILES,),
            in_specs=[pl.BlockSpec(memory_space=pl.ANY), wspec(D_MODEL, MOE_HIDDEN), wspec(D_MODEL, MOE_HIDDEN),
                      wspec(MOE_HIDDEN, D_MODEL)],
            out_specs=pl.BlockSpec(memory_space=pl.ANY),
            scratch_shapes=[pltpu.SMEM((MOE_ROWS,), jnp.int32),
                            pltpu.VMEM((2, MOE_TM) + SLAB, F32), pltpu.VMEM((2, MOE_TM) + SLAB, F32),
                            pltpu.VMEM((D_MODEL, MOE_HIDDEN), BF16), pltpu.VMEM((D_MODEL, MOE_HIDDEN), BF16),
                            pltpu.VMEM((MOE_HIDDEN, D_MODEL), BF16),
                            pltpu.SemaphoreType.DMA((2,)), pltpu.SemaphoreType.DMA((2,))],
        ),
        out_shape=jax.ShapeDtypeStruct((2 * N_TOK,) + SLAB, F32),
        compiler_params=_cparams(("arbitrary",)),
        name="moe_experts",
    )(pos, tile_expert, n_valid, n_used, h, wg, wu, wd)


def _combine_body(x_ref, r_ref, gate_ref, g_ref, b_ref, z1_ref, z2_ref, o_ref):
    r = r_ref[...]
    z1 = z1_ref[...].reshape(x_ref.shape)
    z2 = z2_ref[...].reshape(x_ref.shape)
    y = r[:, 2:3] * z1 + r[:, 3:4] * z2
    o_ref[...] = _layer_norm(DEEPNORM_ALPHA * x_ref[...] + gate_ref[0] * y, g_ref[...], b_ref[...])


def _combine(x, z, route, gate, ln_g, ln_b):
    tm = 512
    nt = N_TOK // tm
    row_spec = pl.BlockSpec((tm, D_MODEL), lambda i: (i, 0))
    vec_spec = pl.BlockSpec((1, D_MODEL), lambda i: (0, 0))
    return pl.pallas_call(
        _combine_body,
        grid=(nt,),
        in_specs=[row_spec, pl.BlockSpec((tm, LANES), lambda i: (i, 0)),
                  pl.BlockSpec((1, 1, D_MODEL), lambda i: (_mod_group(i * tm), 0, 0)),
                  vec_spec, vec_spec, pl.BlockSpec((tm,) + SLAB, lambda i: (i, 0, 0)),
                  pl.BlockSpec((tm,) + SLAB, lambda i: (nt + i, 0, 0))],
        out_specs=row_spec,
        out_shape=jax.ShapeDtypeStruct((N_TOK, D_MODEL), F32),
        compiler_params=_cparams(("parallel",)),
        name="moe_combine_ln",
    )(x, route, gate, ln_g, ln_b, z, z)


def _od_post_body(n_prev, q_ref, k_ref, v_ref, xs_ref, xs_lo_ref, xs_hi_ref, bc_ref, bc_lo_ref, bc_hi_ref, dt_ref,
                  cos_ref, sin_ref, cw_ref, cb_ref, dtb_ref, *refs):
    prev = refs[:2 * n_prev]
    qr_ref, kr_ref, vb_ref, xc_ref, bcc_ref, dtp_ref, c_k, c_v = refs[2 * n_prev:]
    i = pl.program_id(0)
    cos = cos_ref[...]
    sin = sin_ref[...]
    k = k_ref[...]
    v = v_ref[...]
    qr_ref[...] = _rope(q_ref[...], cos, sin, GQA_HEAD_DIM // 4).astype(BF16)
    kr_ref[...] = _rope(k, cos, sin, GQA_HEAD_DIM // 4).astype(BF16)
    vb_ref[...] = v.astype(BF16)

    is_prompt = i < BATCH
    tile_in_seq = (i - BATCH) % (DEC_SEQ // POST_TM)
    seq_start = is_prompt | (tile_in_seq == 0)
    seq_end = is_prompt | (tile_in_seq == DEC_SEQ // POST_TM - 1)

    def conv_silu(x, lo_ref, hi_ref, w, b):
        rows = x.shape[0]
        r = lax.broadcasted_iota(jnp.int32, x.shape, 0)
        lo = jnp.where(seq_start, 0.0, lo_ref[7:8, :])
        hi = jnp.where(seq_end, 0.0, hi_ref[0:1, :])
        before = jnp.where(r == 0, lo, pltpu.roll(x, 1, 0))
        after = jnp.where(r == rows - 1, hi, pltpu.roll(x, rows - 1, 0))
        return _silu(before * w[0:1] + x * w[1:2] + after * w[2:3] + b)

    xc_ref[...] = conv_silu(xs_ref[...], xs_lo_ref, xs_hi_ref, cw_ref[:, :SSD_D_INNER], cb_ref[:, :SSD_D_INNER])
    bcc_ref[...] = conv_silu(bc_ref[...], bc_lo_ref, bc_hi_ref, cw_ref[:, SSD_D_INNER:], cb_ref[:, SSD_D_INNER:])
    t = dt_ref[...] + dtb_ref[...]
    dtp_ref[...] = jnp.maximum(t, 0.0) + jnp.log1p(jnp.exp(-jnp.abs(t)))

    @pl.when(is_prompt)
    def _():
        if n_prev:
            c_k[0, 0] = prev[0][0, 0]
            c_v[0, 0] = prev[1][0, 0]
        c_k[0, n_prev] = k.reshape(SEQ, GQA_KV_HEADS, GQA_HEAD_DIM)
        c_v[0, n_prev] = v.reshape(SEQ, GQA_KV_HEADS, GQA_HEAD_DIM)


def _od_post(proj, cos, sin, conv_w, conv_b, dt_bias, prev):
    tm = POST_TM
    halo = 8
    per = tm // halo
    n_halo = N_TOK // halo
    n_prev = 1 if prev else 0
    blk = lambda w, c: pl.BlockSpec((tm, w), lambda i: (i, c))
    lo = lambda w, c: pl.BlockSpec((halo, w), lambda i: (jnp.maximum(i * per - 1, 0), c))
    hi = lambda w, c: pl.BlockSpec((halo, w), lambda i: (jnp.minimum((i + 1) * per, n_halo - 1), c))
    full = lambda a: pl.BlockSpec(a.shape, lambda i: (0, 0))
    oblk = lambda w: pl.BlockSpec((tm, w), lambda i: (i, 0))
    cache_spec = lambda layers: pl.BlockSpec((1, layers, SEQ, GQA_KV_HEADS, GQA_HEAD_DIM),
                                             lambda i: (jnp.minimum(i, BATCH - 1), 0, 0, 0, 0))
    cache_shape = jax.ShapeDtypeStruct((BATCH, n_prev + 1, SEQ, GQA_KV_HEADS, GQA_HEAD_DIM), F32)
    return pl.pallas_call(
        functools.partial(_od_post_body, n_prev),
        grid=(N_TOK // tm,),
        in_specs=[blk(1024, 0), blk(256, 12), blk(256, 13),
                  blk(1024, 1), lo(1024, 1), hi(1024, 1), blk(512, 7), lo(512, 7), hi(512, 7), blk(LANES, 32),
                  blk(LANES, 0), blk(LANES, 0), full(conv_w), full(conv_b), full(dt_bias)]
        + [cache_spec(1)] * (2 * n_prev),
        out_specs=[oblk(1024), oblk(256), oblk(256), oblk(1024), oblk(512), oblk(LANES),
                   cache_spec(n_prev + 1), cache_spec(n_prev + 1)],
        out_shape=[
            jax.ShapeDtypeStruct((N_TOK, 1024), BF16),
            jax.ShapeDtypeStruct((N_TOK, 256), BF16),
            jax.ShapeDtypeStruct((N_TOK, 256), BF16),
            jax.ShapeDtypeStruct((N_TOK, 1024), F32),
            jax.ShapeDtypeStruct((N_TOK, 512), F32),
            jax.ShapeDtypeStruct((N_TOK, LANES), F32),
            cache_shape, cache_shape,
        ],
        compiler_params=_cparams(("arbitrary",)),
        name="odd_post",
    )(proj, proj, proj, proj, proj, proj, proj, proj, proj, proj, cos, sin, conv_w, conv_b, dt_bias, *prev)


def _sink_column(sink_ref, kh, rows):
    g = GQA_HEADS // GQA_KV_HEADS
    return jnp.concatenate(
        [jnp.broadcast_to(sink_ref[kh * g + j:kh * g + j + 1, 0:1], (rows, 1)) for j in range(g)], axis=0)


def _gqa_prompt_body(q_ref, k_ref, v_ref, sink_ref, o_ref):
    g = GQA_HEADS // GQA_KV_HEADS
    d = GQA_HEAD_DIM
    rows = q_ref.shape[0]
    scale = d ** -0.5
    outs = []
    for kh in range(GQA_KV_HEADS):
        qg = jnp.concatenate([q_ref[:, (kh * g + j) * d:(kh * g + j + 1) * d] for j in range(g)], axis=0)
        k = k_ref[:, kh * d:(kh + 1) * d]
        v = v_ref[:, kh * d:(kh + 1) * d]
        s = _nt(qg, k) * scale
        snk = _sink_column(sink_ref, kh, rows)
        m = jnp.maximum(jnp.max(s, axis=-1, keepdims=True), snk)
        e = jnp.exp(s - m)
        l = jnp.sum(e, axis=-1, keepdims=True) + jnp.exp(snk - m)
        o = jnp.dot(e.astype(BF16), v, preferred_element_type=F32) / l
        outs.extend(o[j * rows:(j + 1) * rows] for j in range(g))
    o_ref[...] = jnp.concatenate(outs, axis=1).astype(BF16)


def _gqa_prompt(q, k, v, sink):
    spec = lambda w: pl.BlockSpec((SEQ, w), lambda i: (i, 0))
    return pl.pallas_call(
        _gqa_prompt_body,
        grid=(BATCH,),
        in_specs=[spec(1024), spec(256), spec(256), pl.BlockSpec((8, LANES), lambda i: (0, 0))],
        out_specs=spec(1024),
        out_shape=jax.ShapeDtypeStruct((N_PROMPT, 1024), BF16),
        compiler_params=_cparams(("parallel",)),
        name="gqa_dense",
    )(q, k, v, sink)


def _gqa_window_body(q_ref, k_ref, v_ref, kc_ref, vc_ref, sink_ref, o_ref):
    g = GQA_HEADS // GQA_KV_HEADS
    d = GQA_HEAD_DIM
    blk = q_ref.shape[0]
    span = 3 * blk
    n = pl.program_id(1)
    start = pl.multiple_of(jnp.clip((n - 1) * blk, 0, DEC_SEQ - span), blk)
    scale = d ** -0.5
    qpos = n * blk + lax.broadcasted_iota(jnp.int32, (blk, span), 0)
    kpos = start + lax.broadcasted_iota(jnp.int32, (blk, span), 1)
    valid1 = jnp.abs(qpos - kpos) <= WINDOW
    valid = jnp.concatenate([valid1] * g, axis=0)
    outs = []
    for kh in range(GQA_KV_HEADS):
        qg = jnp.concatenate([q_ref[:, (kh * g + j) * d:(kh * g + j + 1) * d] for j in range(g)], axis=0)
        kl = k_ref[pl.ds(start, span), kh * d:(kh + 1) * d]
        vl = v_ref[pl.ds(start, span), kh * d:(kh + 1) * d]
        kc = kc_ref[:, kh * d:(kh + 1) * d]
        vc = vc_ref[:, kh * d:(kh + 1) * d]
        sl = jnp.where(valid, _nt(qg, kl) * scale, -jnp.inf)
        sc = _nt(qg, kc) * scale
        snk = _sink_column(sink_ref, kh, blk)
        m = jnp.maximum(jnp.maximum(jnp.max(sl, axis=-1, keepdims=True), jnp.max(sc, axis=-1, keepdims=True)), snk)
        el = jnp.exp(sl - m)
        ec = jnp.exp(sc - m)
        l = jnp.sum(el, axis=-1, keepdims=True) + jnp.sum(ec, axis=-1, keepdims=True) + jnp.exp(snk - m)
        o = (jnp.dot(el.astype(BF16), vl, preferred_element_type=F32)
             + jnp.dot(ec.astype(BF16), vc, preferred_element_type=F32)) / l
        outs.extend(o[j * blk:(j + 1) * blk] for j in range(g))
    o_ref[...] = jnp.concatenate(outs, axis=1).astype(BF16)


def _gqa_window(q, k, v, cache_k, cache_v, sink):
    blk = WINDOW
    nq = DEC_SEQ // blk
    q0 = N_PROMPT // blk
    s0 = N_PROMPT // DEC_SEQ
    seq_spec = pl.BlockSpec((DEC_SEQ, 256), lambda b, j: (s0 + b, 0))
    ctx_spec = pl.BlockSpec((PAST_LEN, 256), lambda b, j: (b, 0))
    return pl.pallas_call(
        _gqa_window_body,
        grid=(DEC_BATCH, nq),
        in_specs=[pl.BlockSpec((blk, 1024), lambda b, j: (q0 + b * nq + j, 0)), seq_spec, seq_spec, ctx_spec,
                  ctx_spec,
                  pl.BlockSpec((8, LANES), lambda b, j: (0, 0))],
        out_specs=pl.BlockSpec((blk, 1024), lambda b, j: (b * nq + j, 0)),
        out_shape=jax.ShapeDtypeStruct((N_SAMPLE, 1024), BF16),
        compiler_params=_cparams(("parallel", "parallel")),
        name="gqa_window",
    )(q, k, v, cache_k, cache_v, sink)


def _expand(x, sel):
    hi = x.astype(BF16)
    r1 = x - hi.astype(F32)
    mid = r1.astype(BF16)
    lo = (r1 - mid.astype(F32)).astype(BF16)
    return (jnp.dot(hi, sel, preferred_element_type=F32) + jnp.dot(mid, sel, preferred_element_type=F32)
            + jnp.dot(lo, sel, preferred_element_type=F32))


def _ssd_body(has_h0, xs_ref, bc_ref, dt_ref, alog_ref, e1_ref, e2_ref, *refs):
    if has_h0:
        h0f_ref, h0b_ref, y_ref, stf_ref, stb_ref = refs
    else:
        y_ref, stf_ref, stb_ref = refs
    d = pl.program_id(1)
    c = pl.program_id(2)
    q = SSD_CHUNK
    ppg = SSD_HEADS // SSD_GROUPS // 2
    hd = SSD_HEAD_DIM

    row = lax.broadcasted_iota(jnp.int32, (q, LANES), 0)
    a = -jnp.exp(alog_ref[...])
    dta = dt_ref[...] * a

    def scan(x, rev):
        k = 1
        while k < q:
            if rev:
                x = x + jnp.where(row < q - k, pltpu.roll(x, q - k, 0), 0.0)
            else:
                x = x + jnp.where(row >= k, pltpu.roll(x, k, 0), 0.0)
            k *= 2
        return x

    li = lax.broadcasted_iota(jnp.int32, (q, q), 0)
    si = lax.broadcasted_iota(jnp.int32, (q, q), 1)

    def run(rev):
        st_ref = stb_ref if rev else stf_ref

        @pl.when(c == 0)
        def _():
            if has_h0:
                st_ref[0] = (h0b_ref if rev else h0f_ref)[0, 0]
            else:
                st_ref[0] = jnp.zeros(st_ref.shape[1:], F32)

        cum = scan(dta, rev)
        cum_t = cum.T
        e_row = 0 if rev else q - 1
        mask = (li <= si) if rev else (li >= si)
        dt_x = _expand(dt_ref[...], e1_ref[0])
        cum_x = _expand(cum, e1_ref[0])
        cum_b = _expand(cum, e2_ref[0])
        edge_x = cum_x[e_row:e_row + 1, :]
        xdt = xs_ref[...] * dt_x
        xdt16 = xdt.astype(BF16)
        xw16 = (xdt * jnp.exp(edge_x - cum_x)).astype(BF16)
        ecum = jnp.exp(cum_x)
        lower = si < hd
        ys = []
        for g in range(SSD_GROUPS):
            bm16 = bc_ref[:, g * SSD_STATE:(g + 1) * SSD_STATE].astype(BF16)
            cm16 = bc_ref[:, (SSD_GROUPS + g) * SSD_STATE:(SSD_GROUPS + g + 1) * SSD_STATE].astype(BF16)
            cb = _nt(cm16, bm16)
            for pp in range(ppg):
                p = g * ppg + pp
                cols = slice(p * LANES, (p + 1) * LANES)
                y_pair, fac = [], []
                for u in range(2):
                    h = 2 * p + u
                    j = (SSD_HEADS if rev else 0) + h
                    seg = cum_b[:, h * LANES:(h + 1) * LANES] - cum_t[j:j + 1, :]
                    decay = jnp.exp(jnp.where(mask, seg, -jnp.inf))
                    y_pair.append(jnp.dot((cb * decay).astype(BF16), xdt16[:, cols], preferred_element_type=F32))
                    fac.append(jnp.broadcast_to(jnp.exp(cum_t[j:j + 1, e_row:e_row + 1]), (hd, SSD_STATE)))
                y_diag = jnp.where(lower, y_pair[0], y_pair[1])
                state = lax.dot_general(xw16[:, cols], bm16, (((0,), (0,)), ((), ())),
                                        preferred_element_type=F32)
                h_prev = st_ref[0, 2 * p:2 * p + 2].reshape(2 * hd, SSD_STATE)
                y_off = _nt(cm16, h_prev.astype(BF16)) * ecum[:, cols]
                ys.append(y_diag + y_off)
                h_new = h_prev * jnp.concatenate(fac, axis=0) + state
                st_ref[0, 2 * p:2 * p + 2] = h_new.reshape(2, hd, SSD_STATE)
        y_ref[0] = jnp.concatenate(ys, axis=1)

    @pl.when(d == 0)
    def _():
        run(False)

    @pl.when(d == 1)
    def _():
        run(True)


def _ssd(xs, bc, dtp, a_log, n_seq, seq, row_off, h0=None):
    b = n_seq
    t = n_seq * seq
    nc = seq // SSD_CHUNK
    c0 = row_off // SSD_CHUNK

    def rmap(i, d, c):
        return i * nc + jnp.where(d == 0, c, nc - 1 - c)

    blk = lambda w: pl.BlockSpec((SSD_CHUNK, w), lambda i, d, c: (c0 + rmap(i, d, c), 0))
    st_spec = pl.BlockSpec((1, SSD_HEADS, SSD_HEAD_DIM, SSD_STATE), lambda i, d, c: (i, 0, 0, 0))
    st_shape = jax.ShapeDtypeStruct((b, SSD_HEADS, SSD_HEAD_DIM, SSD_STATE), F32)
    def sel(w):
        row = jnp.arange(LANES)[None, :, None]
        head = SSD_HEADS * jnp.arange(2)[:, None, None] + (jnp.arange(SSD_HEADS * w) // w)[None, None, :]
        return (row == head).astype(BF16)

    sel_spec = lambda w: pl.BlockSpec((1, LANES, SSD_HEADS * w), lambda i, d, c: (d, 0, 0))
    in_specs = [blk(1024), blk(512), blk(LANES), pl.BlockSpec((1, LANES), lambda i, d, c: (0, 0)),
                sel_spec(SSD_HEAD_DIM), sel_spec(LANES)]
    args = [xs, bc, dtp, a_log, sel(SSD_HEAD_DIM), sel(LANES)]
    if h0 is not None:
        layer, h0_f, h0_b = h0
        h0_spec = pl.BlockSpec((1, 1, SSD_HEADS, SSD_HEAD_DIM, SSD_STATE), lambda i, d, c: (i, layer, 0, 0, 0))
        in_specs += [h0_spec, h0_spec]
        args += [h0_f, h0_b]
    return pl.pallas_call(
        functools.partial(_ssd_body, h0 is not None),
        grid=(b, 2, nc),
        in_specs=in_specs,
        out_specs=[pl.BlockSpec((1, SSD_CHUNK, 1024), lambda i, d, c: (d, rmap(i, d, c), 0)), st_spec, st_spec],
        out_shape=[jax.ShapeDtypeStruct((2, t, 1024), F32), st_shape, st_shape],
        compiler_params=_cparams(("parallel", "arbitrary", "arbitrary")),
        name="ssd_scan",
    )(*args)


def _ssd_gate_body(y_ref, xs_ref, z_ref, dsk_ref, g_ref, o_ref):
    y = (y_ref[0] + y_ref[1] + xs_ref[...] * dsk_ref[...]) * _silu(z_ref[...])
    w = SSD_D_INNER // SSD_GROUPS
    outs = [_rms(y[:, g * w:(g + 1) * w], g_ref[:, g * w:(g + 1) * w]) for g in range(SSD_GROUPS)]
    o_ref[...] = jnp.concatenate(outs, axis=1).astype(BF16)


def _ssd_gate(y2, xs, proj, d_skip, norm_g, row_off):
    tm = 512
    t = y2.shape[1]
    off = row_off // tm
    return pl.pallas_call(
        _ssd_gate_body,
        grid=(t // tm,),
        in_specs=[pl.BlockSpec((2, tm, 1024), lambda i: (0, i, 0)),
                  pl.BlockSpec((tm, 1024), lambda i: (off + i, 0)),
                  pl.BlockSpec((tm, 1024), lambda i: (off + i, 2)),
                  pl.BlockSpec((1, 1024), lambda i: (0, 0)),
                  pl.BlockSpec((1, 1024), lambda i: (0, 0))],
        out_specs=pl.BlockSpec((tm, 1024), lambda i: (i, 0)),
        out_shape=jax.ShapeDtypeStruct((t, 1024), BF16),
        compiler_params=_cparams(("parallel",)),
        name="ssd_gate_norm",
    )(y2, xs, proj, d_skip, norm_g)


def _even_layer(x, mod, layer_idx, i, p, ctx, tables, prev_caches):
    w_in, q_norm, kv_norm, wq_b, wkv_b, lam, subln, w_out = p
    ctx_ckv, ctx_kpe, ctx_dk, ctx_dv = ctx
    cos, sin = tables
    lam_init = 0.8 - 0.6 * math.exp(-0.3 * layer_idx)

    w_r = jnp.concatenate([w_in[:, :1024], w_in[:, 1088:], w_in[:, 1024:1088],
                           jnp.zeros((D_MODEL, PROJ_N - w_in.shape[1]), F32)], axis=1).astype(BF16)
    proj = _proj(x, mod[0], mod[1], w_r)
    qn, ckv, dq, dk, dv, kpe, *caches = _ev_post(proj, cos, sin, q_norm[None], kv_norm[None], prev_caches)

    wq = wq_b.reshape(MLA_Q_RANK, MLA_HEADS, MLA_NOPE + MLA_ROPE)
    wq_r = jnp.pad(wq, ((0, 0), (0, 0), (0, MLA_ROPE))).reshape(MLA_Q_RANK, -1).astype(BF16)
    wkv = wkv_b.astype(BF16)
    q = _mm(qn, wq_r, BF16, 512, N_TOK // 512)
    kv = _mm(ckv, wkv, BF16, 512, N_TOK // 512)
    n_even = ctx_ckv.shape[1]
    kv_ctx = _mm(ctx_ckv.reshape(-1, MLA_KV_RANK), wkv, BF16, PAST_LEN, DEC_BATCH,
                 row_block=lambda b: b * n_even + i)

    n_ctx = DEC_BATCH * PAST_LEN
    kpe_ctx = jnp.pad(ctx_kpe[:, i].reshape(n_ctx, MLA_ROPE), ((0, 0), (0, LANES - MLA_ROPE))).astype(BF16)
    dk_ctx = ctx_dk[:, i].reshape(n_ctx, -1).astype(BF16)
    dv_ctx = ctx_dv[:, i].reshape(n_ctx, -1).astype(BF16)

    args = (lam_init, q, cos, sin, kv, kpe, dq, dk, dv, lam, subln[None])
    o1p, o2p = _even_attn(*args, n_seq=BATCH, seq=SEQ, row_off=0)
    o1s, o2s = _even_attn(*args, n_seq=DEC_BATCH, seq=DEC_SEQ, row_off=N_PROMPT,
                          ctx=(kv_ctx, kpe_ctx, dk_ctx, dv_ctx))
    return (o1p, o1s, o2p, o2s), w_out.astype(BF16), caches


def _odd_layer(x, mod, i, p, ctx, tables, prev_caches):
    w_in, sink, conv_w, conv_b, dt_bias, a_log, d_skip, norm_g, w_out = p
    k_ctx, v_ctx, h0_f, h0_b = ctx
    cos, sin = tables

    w_r = jnp.concatenate([w_in[:, :1024], w_in[:, 2560:3584], w_in[:, 1536:2560], w_in[:, 1024:1536],
                           w_in[:, 3584:], jnp.zeros((D_MODEL, PROJ_N - w_in.shape[1]), F32)],
                          axis=1).astype(BF16)
    proj = _proj(x, mod[0], mod[1], w_r)
    cb = conv_b[None]
    dtb = jnp.pad(dt_bias.reshape(1, -1), ((0, 0), (0, LANES - 2 * SSD_HEADS)))
    alog = jnp.pad(a_log.reshape(1, -1), ((0, 0), (0, LANES - 2 * SSD_HEADS)))
    q, k, v, xs, bc, dtp, *caches = _od_post(proj, cos, sin, conv_w, cb, dtb, prev_caches)
    sink_b = jnp.broadcast_to(sink[:, None], (GQA_HEADS, LANES))

    o_att_p = _gqa_prompt(q, k, v, sink_b)
    n_ctx = DEC_BATCH * PAST_LEN
    o_att_s = _gqa_window(q, k, v, k_ctx[:, i].reshape(n_ctx, -1).astype(BF16),
                          v_ctx[:, i].reshape(n_ctx, -1).astype(BF16), sink_b)

    y_p, st_f, st_b = _ssd(xs, bc, dtp, alog, BATCH, SEQ, 0)
    y_s, _, _ = _ssd(xs, bc, dtp, alog, DEC_BATCH, DEC_SEQ, N_PROMPT, h0=(i, h0_f, h0_b))
    dsk = jnp.repeat(d_skip, SSD_HEAD_DIM)[None]
    y_n_p = _ssd_gate(y_p, xs, proj, dsk, norm_g[None], 0)
    y_n_s = _ssd_gate(y_s, xs, proj, dsk, norm_g[None], N_PROMPT)
    return (o_att_p, o_att_s, y_n_p, y_n_s), w_out.astype(BF16), caches, (st_f, st_b)


def kernel(x_prompt, x_sample, cache_mla_ckv, cache_mla_kpe, cache_diff_k, cache_diff_v, cache_gqa_k, cache_gqa_v, state_ssd_fwd, state_ssd_bwd, c, c_ctx, w_mod, b_mod, ln1_g, ln1_b, ln2_g, ln2_b, ev_w_in, mla_q_norm, mla_kv_norm, mla_wq_b, mla_wkv_b, diff_lambda, diff_subln, ev_w_out, od_w_in, gqa_sink, ssd_conv_w, ssd_conv_b, ssd_dt_bias, ssd_a_log, ssd_d, ssd_norm, od_w_out, moe_router_group, moe_router_expert, moe_w_gate, moe_w_up, moe_w_down):
    x = jnp.concatenate([x_prompt.reshape(N_PROMPT, D_MODEL), x_sample.reshape(N_SAMPLE, D_MODEL)], axis=0)
    cs = jnp.concatenate([c_ctx[None], c, jnp.zeros((8 - 1 - DEC_BATCH, D_MODEL), F32)], axis=0)
    mods = _adaln(cs, w_mod, b_mod)[:, :1 + DEC_BATCH].reshape(DEPTH, 1 + DEC_BATCH, 6, 1, D_MODEL)
    tables_64 = _rope_tables(64)
    tables_128 = _rope_tables(GQA_HEAD_DIM)

    ev_caches, od_caches, ssd_states = (), (), []
    for l in range(DEPTH):
        i = l // 2
        mod = [mods[l, :, k] for k in range(6)]
        if l % 2 == 0:
            p = (ev_w_in[i], mla_q_norm[i], mla_kv_norm[i], mla_wq_b[i], mla_wkv_b[i], diff_lambda[i],
                 diff_subln[i], ev_w_out[i])
            ctx = (cache_mla_ckv, cache_mla_kpe, cache_diff_k, cache_diff_v)
            mix, w_out, ev_caches = _even_layer(x, mod, l, i, p, ctx, tables_64, ev_caches)
        else:
            p = (od_w_in[i], gqa_sink[i], ssd_conv_w[i], ssd_conv_b[i], ssd_dt_bias[i], ssd_a_log[i], ssd_d[i],
                 ssd_norm[i], od_w_out[i])
            ctx = (cache_gqa_k, cache_gqa_v, state_ssd_fwd, state_ssd_bwd)
            mix, w_out, od_caches, states = _odd_layer(x, mod, i, p, ctx, tables_128, od_caches)
            ssd_states.append(states)
        w_router = jnp.concatenate([moe_router_group[l], moe_router_expert[l],
                                    jnp.zeros((D_MODEL, LANES - MOE_GROUPS - MOE_EXPERTS), F32)], axis=1)
        w_router_hi = w_router.astype(BF16)
        w_router = jnp.stack([w_router_hi, (w_router - w_router_hi.astype(F32)).astype(BF16)])
        x, h2, route = _out_proj(*mix, w_out, x, mod[2], ln1_g[l][None], ln1_b[l][None], mod[3], mod[4], w_router)
        pos, tile_expert, n_valid, n_used = _route_meta(route)
        z = _moe_experts(l, h2, pos, tile_expert, n_valid, n_used, moe_w_gate, moe_w_up, moe_w_down)
        x = _combine(x, z, route, mod[5], ln2_g[l][None], ln2_b[l][None])

    y_prompt = x[:N_PROMPT].reshape(BATCH, SEQ, D_MODEL)
    y_sample = x[N_PROMPT:].reshape(DEC_BATCH, DEC_SEQ, D_MODEL)
    new_ssd_fwd = jnp.stack([s[0] for s in ssd_states], axis=1)
    new_ssd_bwd = jnp.stack([s[1] for s in ssd_states], axis=1)
    return (y_prompt, y_sample, *ev_caches, *od_caches, new_ssd_fwd, new_ssd_bwd)
```

```python
import functools
import math

import jax
import jax.numpy as jnp
from jax import lax
from jax.experimental import pallas as pl
from jax.experimental.pallas import tpu as pltpu

F32 = jnp.float32
BF16 = jnp.bfloat16

D_MODEL = 2048
BATCH = 16
SEQ = 256
DEPTH = 4
DEC_BATCH = 2
DEC_SEQ = 1024
PAST_LEN = 512
GRID_W = 64
ROPE_THETA = 10000.0
DEEPNORM_ALPHA = (2.0 * DEPTH) ** 0.25
LN_EPS = 1e-5
RMS_EPS = 1e-6

MLA_HEADS = 8
MLA_Q_RANK = 512
MLA_KV_RANK = 512
MLA_NOPE = 128
MLA_ROPE = 64
MLA_V = 128
DIFF_HEADS = 8
DIFF_QK = 64
DIFF_V = 128
GQA_HEADS = 8
GQA_KV_HEADS = 2
GQA_HEAD_DIM = 128
WINDOW = 128
SSD_HEADS = 16
SSD_HEAD_DIM = 64
SSD_D_INNER = SSD_HEADS * SSD_HEAD_DIM
SSD_GROUPS = 2
SSD_STATE = 128
SSD_CHUNK = 128
MOE_GROUPS = 4
MOE_EPG = 4
MOE_EXPERTS = 16
MOE_HIDDEN = 512

N_PROMPT = BATCH * SEQ
N_SAMPLE = DEC_BATCH * DEC_SEQ
N_TOK = N_PROMPT + N_SAMPLE
PROJ_N = 4224
LANES = 128
VMEM_LIMIT = 56 * 1024 * 1024

PROJ_TM, PROJ_TN = 1024, 1408
OUT_TM = 512
ATT_TQ = 256
ATT_HEADS_PER_STEP = 2
ATT_HEADS_PER_STEP_CTX = 4
POST_TM = 256
MOE_TM = 256
MOE_SLOTS = 2 * N_TOK
MOE_TILES = MOE_SLOTS // MOE_TM + MOE_EXPERTS
MOE_ROWS = MOE_TILES * MOE_TM
SLAB = (D_MODEL // LANES, LANES)


def _cparams(sem):
    return pltpu.CompilerParams(dimension_semantics=sem, vmem_limit_bytes=VMEM_LIMIT)


def _mod_group(row0):
    return jnp.maximum(row0 // DEC_SEQ - (N_PROMPT // DEC_SEQ - 1), 0)


def _silu(x):
    return x / (1.0 + jnp.exp(-x))


def _nt(a, b):
    return lax.dot_general(a, b, (((1,), (1,)), ((), ())), preferred_element_type=F32)


def _rms(x, g):
    return x * lax.rsqrt(jnp.mean(x * x, axis=-1, keepdims=True) + RMS_EPS) * g


def _layer_norm(y, g, b):
    mu = jnp.mean(y, axis=-1, keepdims=True)
    yc = y - mu
    return yc * lax.rsqrt(jnp.mean(yc * yc, axis=-1, keepdims=True) + LN_EPS) * g + b


def _adaln_body(c_ref, w_ref, b_ref, o_ref):
    a = _silu(c_ref[...])
    o_ref[0] = jnp.dot(a, w_ref[0], preferred_element_type=F32) + b_ref[0]


def _adaln(cs, w_mod, b_mod):
    tn = 1024
    n = w_mod.shape[-1]
    return pl.pallas_call(
        _adaln_body,
        grid=(DEPTH, n // tn),
        in_specs=[
            pl.BlockSpec((8, D_MODEL), lambda l, j: (0, 0)),
            pl.BlockSpec((1, D_MODEL, tn), lambda l, j: (l, 0, j)),
            pl.BlockSpec((1, 1, tn), lambda l, j: (l, 0, j)),
        ],
        out_specs=pl.BlockSpec((1, 8, tn), lambda l, j: (l, 0, j)),
        out_shape=jax.ShapeDtypeStruct((DEPTH, 8, n), F32),
        compiler_params=_cparams(("parallel", "parallel")),
        name="adaln",
    )(cs, w_mod, b_mod.reshape(DEPTH, 1, n))


def _proj_body(x_ref, sh_ref, sc_ref, w_ref, o_ref, h_scr):
    @pl.when(pl.program_id(1) == 0)
    def _():
        h = x_ref[...] * (1.0 + sc_ref[0]) + sh_ref[0]
        h_scr[...] = h.astype(BF16)

    o_ref[...] = jnp.dot(h_scr[...], w_ref[...], preferred_element_type=F32)


def _proj(x, shift, scale, w):
    tm, tn = PROJ_TM, PROJ_TN
    n = w.shape[1]
    mod_spec = pl.BlockSpec((1, 1, D_MODEL), lambda i, j: (_mod_group(i * tm), 0, 0))
    return pl.pallas_call(
        _proj_body,
        grid=(N_TOK // tm, n // tn),
        in_specs=[
            pl.BlockSpec((tm, D_MODEL), lambda i, j: (i, 0)),
            mod_spec,
            mod_spec,
            pl.BlockSpec((D_MODEL, tn), lambda i, j: (0, j)),
        ],
        out_specs=pl.BlockSpec((tm, tn), lambda i, j: (i, j)),
        out_shape=jax.ShapeDtypeStruct((N_TOK, n), F32),
        scratch_shapes=[pltpu.VMEM((tm, D_MODEL), BF16)],
        compiler_params=_cparams(("parallel", "arbitrary")),
        name="in_proj",
    )(x, shift, scale, w)


def _mm_body(a_ref, b_ref, o_ref):
    o_ref[...] = jnp.dot(a_ref[...].astype(BF16), b_ref[...], preferred_element_type=F32).astype(o_ref.dtype)


def _mm(a, b, out_dtype, tm, n_tiles, row_block=lambda i: i):
    k = a.shape[1]
    n = b.shape[1]
    return pl.pallas_call(
        _mm_body,
        grid=(n_tiles,),
        in_specs=[pl.BlockSpec((tm, k), lambda i: (row_block(i), 0)), pl.BlockSpec((k, n), lambda i: (0, 0))],
        out_specs=pl.BlockSpec((tm, n), lambda i: (i, 0)),
        out_shape=jax.ShapeDtypeStruct((n_tiles * tm, n), out_dtype),
        compiler_params=_cparams(("parallel",)),
        name="mm_resident",
    )(a, b)


def _rope(x, cos, sin_signed, quarter):
    rows, w = x.shape
    reps = w // LANES
    if reps > 1:
        cos = jnp.concatenate([cos] * reps, axis=1)
        sin_signed = jnp.concatenate([sin_signed] * reps, axis=1)
    lane = lax.broadcasted_iota(jnp.int32, x.shape, 1)
    first = (lane % (2 * quarter)) < quarter
    up = pltpu.roll(x, w - quarter, 1)
    dn = pltpu.roll(x, quarter, 1)
    return x * cos + jnp.where(first, up, dn) * sin_signed


def _rope_tables(rdim):
    half = rdim // 2
    quarter = half // 2
    pos = jnp.arange(DEC_SEQ)
    row = (pos // GRID_W).astype(F32)
    col = (pos % GRID_W).astype(F32)
    inv = ROPE_THETA ** (-jnp.arange(quarter, dtype=F32) * 2.0 / half)
    lane = jnp.arange(LANES)
    r = lane % rdim
    use_col = (r // half) == 1
    j = r % quarter
    p = jnp.where(use_col[None, :], col[:, None], row[:, None])
    ang = p * inv[j][None, :]
    sign = jnp.where((r % half) < quarter, -1.0, 1.0).astype(F32)
    cos = jnp.cos(ang)
    sin = jnp.sin(ang) * sign[None, :]
    cos = jnp.concatenate([jnp.ones((N_PROMPT, LANES), F32), cos, cos], axis=0)
    sin = jnp.concatenate([jnp.zeros((N_PROMPT, LANES), F32), sin, sin], axis=0)
    return cos, sin


def _ev_post_body(n_prev, ql_ref, kvl_ref, dq_ref, dk_ref, dv_ref, kpe_ref, cos_ref, sin_ref, gq_ref, gkv_ref,
                  *refs):
    prev = refs[:4 * n_prev]
    qn_ref, ckv_ref, dqr_ref, dkr_ref, dvb_ref, kper_ref, c_ckv, c_kpe, c_dk, c_dv = refs[4 * n_prev:]
    cos = cos_ref[...]
    sin = sin_ref[...]
    qn_ref[...] = _rms(ql_ref[...], gq_ref[...]).astype(BF16)
    ckv = _rms(kvl_ref[...], gkv_ref[...])
    ckv_ref[...] = ckv.astype(BF16)
    dqr_ref[...] = _rope(dq_ref[...], cos, sin, DIFF_QK // 4).astype(BF16)
    dk = _rope(dk_ref[...], cos, sin, DIFF_QK // 4)
    dkr_ref[...] = dk.astype(BF16)
    dv = dv_ref[...]
    dvb_ref[...] = dv.astype(BF16)
    kpe = _rope(kpe_ref[...], cos, sin, MLA_ROPE // 4)
    kper_ref[...] = kpe.astype(BF16)

    @pl.when(pl.program_id(0) < BATCH)
    def _():
        if n_prev:
            c_ckv[0, 0] = prev[0][0, 0]
            c_kpe[0, 0] = prev[1][0, 0]
            c_dk[0, 0] = prev[2][0, 0]
            c_dv[0, 0] = prev[3][0, 0]
        c_ckv[0, n_prev] = ckv
        c_kpe[0, n_prev] = kpe[:, :MLA_ROPE]
        c_dk[0, n_prev] = dk.reshape(SEQ, DIFF_HEADS, 2 * DIFF_QK)
        c_dv[0, n_prev] = dv.reshape(SEQ, DIFF_HEADS, DIFF_V)


def _ev_post(proj, cos, sin, gq, gkv, prev):
    tm = POST_TM
    n_prev = 1 if prev else 0
    nl = n_prev + 1
    row = lambda c: (lambda i: (i, c))
    seq4 = lambda i: (jnp.minimum(i, BATCH - 1), 0, 0, 0)
    seq5 = lambda i: (jnp.minimum(i, BATCH - 1), 0, 0, 0, 0)
    cache_shapes = [(MLA_KV_RANK,), (MLA_ROPE,), (DIFF_HEADS, 2 * DIFF_QK), (DIFF_HEADS, DIFF_V)]

    def cache_spec(layers, tail):
        return pl.BlockSpec((1, layers, SEQ) + tail, seq4 if len(tail) == 1 else seq5)

    return pl.pallas_call(
        functools.partial(_ev_post_body, n_prev),
        grid=(N_TOK // tm,),
        in_specs=[
            pl.BlockSpec((tm, 512), row(0)),
            pl.BlockSpec((tm, 512), row(1)),
            pl.BlockSpec((tm, 1024), row(1)),
            pl.BlockSpec((tm, 1024), row(2)),
            pl.BlockSpec((tm, 1024), row(3)),
            pl.BlockSpec((tm, LANES), row(32)),
            pl.BlockSpec((tm, LANES), row(0)),
            pl.BlockSpec((tm, LANES), row(0)),
            pl.BlockSpec((1, 512), lambda i: (0, 0)),
            pl.BlockSpec((1, 512), lambda i: (0, 0)),
        ] + [cache_spec(1, t) for t in cache_shapes] * n_prev,
        out_specs=[
            pl.BlockSpec((tm, 512), row(0)),
            pl.BlockSpec((tm, 512), row(0)),
            pl.BlockSpec((tm, 1024), row(0)),
            pl.BlockSpec((tm, 1024), row(0)),
            pl.BlockSpec((tm, 1024), row(0)),
            pl.BlockSpec((tm, LANES), row(0)),
        ] + [cache_spec(nl, t) for t in cache_shapes],
        out_shape=[
            jax.ShapeDtypeStruct((N_TOK, 512), BF16),
            jax.ShapeDtypeStruct((N_TOK, 512), BF16),
            jax.ShapeDtypeStruct((N_TOK, 1024), BF16),
            jax.ShapeDtypeStruct((N_TOK, 1024), BF16),
            jax.ShapeDtypeStruct((N_TOK, 1024), BF16),
            jax.ShapeDtypeStruct((N_TOK, LANES), BF16),
        ] + [jax.ShapeDtypeStruct((BATCH, nl, SEQ) + t, F32) for t in cache_shapes],
        compiler_params=_cparams(("arbitrary",)),
        name="even_post",
    )(proj, proj, proj, proj, proj, proj, cos, sin, gq, gkv, *prev)


def _softmax_av(scores, values):
    m = functools.reduce(jnp.maximum, [jnp.max(s, axis=-1, keepdims=True) for s in scores])
    es = [jnp.exp(s - m) for s in scores]
    l = sum(jnp.sum(e, axis=-1, keepdims=True) for e in es)
    o = sum(jnp.dot(e.astype(BF16), v, preferred_element_type=F32) for e, v in zip(es, values))
    return o / l


def _even_attn_body(lam_init, has_ctx, hps, q_ref, cos_ref, sin_ref, kv_ref, kpe_ref, dq_ref, dk_ref, dv_ref,
                    lam_ref, sub_ref, *refs):
    if has_ctx:
        kvc_ref, kpec_ref, dkc_ref, dvc_ref, omla_ref, odiff_ref = refs
    else:
        omla_ref, odiff_ref = refs
    cos = cos_ref[...]
    sin = sin_ref[...]
    lam = lam_ref[...]
    lam_full = (jnp.exp(jnp.sum(lam[0:1] * lam[1:2], axis=-1, keepdims=True))
                - jnp.exp(jnp.sum(lam[2:3] * lam[3:4], axis=-1, keepdims=True)) + lam_init)
    mla_scale = (MLA_NOPE + MLA_ROPE) ** -0.5
    lane = lax.broadcasted_iota(jnp.int32, (q_ref.shape[0], 2 * DIFF_QK), 1)
    for u in range(hps):
        qc = slice(u * 256, (u + 1) * 256)
        dc = slice(u * LANES, (u + 1) * LANES)
        q = q_ref[:, qc]
        qr = _rope(q[:, MLA_NOPE:].astype(F32), cos, sin, MLA_ROPE // 4).astype(BF16)
        qcat = jnp.concatenate([q[:, :MLA_NOPE], qr], axis=1)
        kv = kv_ref[:, qc]
        scores = [_nt(qcat, jnp.concatenate([kv[:, :MLA_NOPE], kpe_ref[...]], axis=1)) * mla_scale]
        values = [kv[:, MLA_NOPE:]]
        if has_ctx:
            kvc = kvc_ref[:, qc]
            scores.append(_nt(qcat, jnp.concatenate([kvc[:, :MLA_NOPE], kpec_ref[...]], axis=1)) * mla_scale)
            values.append(kvc[:, MLA_NOPE:])
        omla_ref[:, dc] = _softmax_av(scores, values).astype(BF16)

        dq = dq_ref[:, dc] * jnp.asarray(DIFF_QK ** -0.5, BF16)
        zero = jnp.zeros_like(dq)
        keys = [dk_ref[:, dc]]
        values = [dv_ref[:, dc]]
        if has_ctx:
            keys.append(dkc_ref[:, dc])
            values.append(dvc_ref[:, dc])
        a1 = _softmax_av([_nt(jnp.where(lane < DIFF_QK, dq, zero), k) for k in keys], values)
        a2 = _softmax_av([_nt(jnp.where(lane >= DIFF_QK, dq, zero), k) for k in keys], values)
        odiff_ref[:, dc] = (_rms(a1 - lam_full * a2, sub_ref[...]) * (1.0 - lam_init)).astype(BF16)


def _even_attn(lam_init, q, cos, sin, kv, kpe, dq, dk, dv, lam, subln, n_seq, seq, row_off, ctx=None):
    tq = ATT_TQ
    hps = ATT_HEADS_PER_STEP if ctx is not None else ATT_HEADS_PER_STEP_CTX
    nq = seq // tq
    q0 = row_off // tq
    s0 = row_off // seq
    qspec = lambda w: pl.BlockSpec((tq, hps * w), lambda b, j, h: (q0 + b * nq + j, h))
    kspec = lambda w: pl.BlockSpec((seq, hps * w), lambda b, j, h: (s0 + b, h))
    tspec = pl.BlockSpec((tq, LANES), lambda b, j, h: (q0 + b * nq + j, 0))
    ospec = pl.BlockSpec((tq, hps * LANES), lambda b, j, h: (b * nq + j, h))
    in_specs = [qspec(256), tspec, tspec, kspec(256),
                pl.BlockSpec((seq, LANES), lambda b, j, h: (s0 + b, 0)),
                qspec(LANES), kspec(LANES), kspec(LANES),
                pl.BlockSpec((4, DIFF_QK), lambda b, j, h: (0, 0)),
                pl.BlockSpec((1, DIFF_V), lambda b, j, h: (0, 0))]
    args = [q, cos, sin, kv, kpe, dq, dk, dv, lam, subln]
    if ctx is not None:
        cspec = lambda w: pl.BlockSpec((PAST_LEN, hps * w), lambda b, j, h: (b, h))
        in_specs += [cspec(256), pl.BlockSpec((PAST_LEN, LANES), lambda b, j, h: (b, 0)), cspec(LANES), cspec(LANES)]
        args += list(ctx)
    return pl.pallas_call(
        functools.partial(_even_attn_body, lam_init, ctx is not None, hps),
        grid=(n_seq, nq, MLA_HEADS // hps),
        in_specs=in_specs,
        out_specs=[ospec, ospec],
        out_shape=[jax.ShapeDtypeStruct((n_seq * seq, 1024), BF16)] * 2,
        compiler_params=_cparams(("parallel", "parallel", "parallel")),
        name="even_attn",
    )(*args)


def _route(logits):
    lane = lax.broadcasted_iota(jnp.int32, logits.shape, 1).astype(F32)
    neg = -jnp.inf
    none = float(LANES)
    is_g = lane < MOE_GROUPS
    lg = jnp.where(is_g, logits, neg)
    mg = jnp.max(lg, axis=-1, keepdims=True)
    g_val = 1.0 / jnp.sum(jnp.exp(lg - mg), axis=-1, keepdims=True)
    g_idx = jnp.min(jnp.where(is_g & (lg == mg), lane, none), axis=-1, keepdims=True)
    e_lo = MOE_GROUPS + g_idx * MOE_EPG
    is_e = (lane >= e_lo) & (lane < e_lo + MOE_EPG)
    le = jnp.where(is_e, logits, neg)
    p = jnp.exp(le - jnp.max(le, axis=-1, keepdims=True))
    p1 = jnp.max(p, axis=-1, keepdims=True)
    i1 = jnp.min(jnp.where(is_e & (p == p1), lane, none), axis=-1, keepdims=True)
    rest = is_e & (lane != i1)
    p2 = jnp.max(jnp.where(rest, p, neg), axis=-1, keepdims=True)
    i2 = jnp.min(jnp.where(rest & (p == p2), lane, none), axis=-1, keepdims=True)
    s = g_val / (p1 + p2)
    return jnp.where(lane == 0.0, i1 - MOE_GROUPS,
                     jnp.where(lane == 1.0, i2 - MOE_GROUPS,
                               jnp.where(lane == 2.0, p1 * s, jnp.where(lane == 3.0, p2 * s, 0.0))))


def _out_proj_body(a1p_ref, a1s_ref, a2p_ref, a2s_ref, w_ref, x_ref, gate_ref, g_ref, b_ref, sh_ref, sc_ref,
                   wr_ref, xo_ref, h_ref, comb_ref):
    is_prompt = pl.program_id(0) < N_PROMPT // OUT_TM
    a1 = jnp.where(is_prompt, a1p_ref[...], a1s_ref[...])
    a2 = jnp.where(is_prompt, a2p_ref[...], a2s_ref[...])
    k1 = a1.shape[1]
    acc = jnp.dot(a1, w_ref[:k1, :], preferred_element_type=F32)
    acc = acc + jnp.dot(a2, w_ref[k1:, :], preferred_element_type=F32)
    xn = _layer_norm(DEEPNORM_ALPHA * x_ref[...] + gate_ref[0] * acc, g_ref[...], b_ref[...])
    xo_ref[...] = xn
    h = xn * (1.0 + sc_ref[0]) + sh_ref[0]
    h_ref[...] = h.reshape(h_ref.shape)
    h_hi = h.astype(BF16)
    h_lo = (h - h_hi.astype(F32)).astype(BF16)
    logits = (jnp.dot(h_hi, wr_ref[0], preferred_element_type=F32)
              + jnp.dot(h_lo, wr_ref[0], preferred_element_type=F32)
              + jnp.dot(h_hi, wr_ref[1], preferred_element_type=F32))
    comb_ref[...] = _route(logits)


def _out_proj(a1p, a1s, a2p, a2s, w, x, gate, ln_g, ln_b, shift, scale, w_router):
    tm = OUT_TM
    np_tiles = N_PROMPT // tm
    k1, k2 = a1p.shape[1], a2p.shape[1]
    mod_spec = pl.BlockSpec((1, 1, D_MODEL), lambda i: (_mod_group(i * tm), 0, 0))
    vec_spec = pl.BlockSpec((1, D_MODEL), lambda i: (0, 0))
    row_spec = pl.BlockSpec((tm, D_MODEL), lambda i: (i, 0))
    pspec = lambda k: pl.BlockSpec((tm, k), lambda i: (jnp.minimum(i, np_tiles - 1), 0))
    sspec = lambda k: pl.BlockSpec((tm, k), lambda i: (jnp.maximum(i - np_tiles, 0), 0))
    return pl.pallas_call(
        _out_proj_body,
        grid=(N_TOK // tm,),
        in_specs=[
            pspec(k1), sspec(k1), pspec(k2), sspec(k2),
            pl.BlockSpec((k1 + k2, D_MODEL), lambda i: (0, 0)),
            row_spec, mod_spec, vec_spec, vec_spec, mod_spec, mod_spec,
            pl.BlockSpec((2, D_MODEL, LANES), lambda i: (0, 0, 0)),
        ],
        out_specs=[row_spec, pl.BlockSpec((tm,) + SLAB, lambda i: (i, 0, 0)),
                   pl.BlockSpec((tm, LANES), lambda i: (i, 0))],
        out_shape=[
            jax.ShapeDtypeStruct((N_TOK, D_MODEL), F32),
            jax.ShapeDtypeStruct((N_TOK,) + SLAB, F32),
            jax.ShapeDtypeStruct((N_TOK, LANES), F32),
        ],
        compiler_params=_cparams(("parallel",)),
        name="out_proj_ln_router",
    )(a1p, a1s, a2p, a2s, w, x, gate, ln_g, ln_b, shift, scale, w_router)


def _route_meta(route):
    e = route[:, :2].astype(jnp.int32).reshape(-1)
    onehot = (e[:, None] == jnp.arange(MOE_EXPERTS, dtype=jnp.int32)[None, :]).astype(jnp.int32)
    csum = jnp.cumsum(onehot, axis=0)
    rank = jnp.sum((csum - onehot) * onehot, axis=1)
    count = csum[-1]
    ntile = (count + MOE_TM - 1) // MOE_TM
    tile_end = jnp.cumsum(ntile)
    tile_off = tile_end - ntile
    pos = jnp.sum(onehot * tile_off[None, :], axis=1) * MOE_TM + rank
    n_used = tile_end[-1:]
    j = jnp.arange(MOE_TILES, dtype=jnp.int32)
    tile_expert = jnp.sum((tile_end[None, :] <= jnp.minimum(j, n_used - 1)[:, None]).astype(jnp.int32), axis=1)
    is_e = tile_expert[:, None] == jnp.arange(MOE_EXPERTS, dtype=jnp.int32)[None, :]
    left = jnp.sum(jnp.where(is_e, (count - (j[:, None] - tile_off[None, :]) * MOE_TM)[...], 0), axis=1)
    n_valid = jnp.where(j < n_used, jnp.clip(left, 0, MOE_TM), 0)
    ids = jnp.arange(MOE_EXPERTS, dtype=jnp.int32)
    later = (ids[None, :] > ids[:, None]) & (ntile[None, :] > 0)
    next_of = jnp.min(jnp.where(later, ids[None, :], MOE_EXPERTS), axis=1)
    next_of = jnp.where(next_of == MOE_EXPERTS, -1, next_of)
    next_expert = jnp.sum(jnp.where(is_e, next_of[None, :], 0), axis=1)
    ordinal = jnp.cumsum((ntile > 0).astype(jnp.int32)) - 1
    w_slot = jnp.sum(jnp.where(is_e, ordinal[None, :], 0), axis=1) % 2
    return pos, tile_expert, n_valid, n_used, next_expert, w_slot


def _moe_body(layer, pos_ref, te_ref, nv_ref, nu_ref, ne_ref, ws_ref, h_ref, wg_ref, wu_ref, wd_ref, z_ref,
              inv, xbuf, zbuf, wgbuf, wubuf, wdbuf, wg_s, wu_s, wd_s, gsem, ssem, wsem):
    j = pl.program_id(0)
    slot = j % 2
    n_used = nu_ref[0]
    group = 8

    def weight_copies(e, s):
        return [pltpu.make_async_copy(wg_ref.at[layer, e], wgbuf.at[s], wsem.at[s]),
                pltpu.make_async_copy(wu_ref.at[layer, e], wubuf.at[s], wsem.at[s]),
                pltpu.make_async_copy(wd_ref.at[layer, e], wdbuf.at[s], wsem.at[s])]

    def gather_row(tile, s, r):
        tok = lax.shift_right_logical(inv[tile * MOE_TM + r], 1)
        return pltpu.make_async_copy(h_ref.at[tok], xbuf.at[s, r], gsem.at[s])

    def scatter_row(tile, s, r):
        sl = inv[tile * MOE_TM + r]
        dst = (sl & 1) * N_TOK + lax.shift_right_logical(sl, 1)
        return pltpu.make_async_copy(zbuf.at[s, r], z_ref.at[dst], ssem.at[s])

    def for_valid_rows(tile, fn):
        n = nv_ref[tile]

        def block(g, c):
            for u in range(group):
                fn(g * group + u)
            return c

        def single(r, c):
            fn(r)
            return c

        full = lax.shift_right_logical(n, 3)
        lax.fori_loop(0, full, block, 0)
        lax.fori_loop(full * group, n, single, 0)

    @pl.when(j == 0)
    def _():
        def fill(g, c):
            for u in range(group):
                sl = g * group + u
                inv[pos_ref[sl]] = sl
            return c
        lax.fori_loop(0, MOE_SLOTS // group, fill, 0)
        xbuf[...] = jnp.zeros_like(xbuf)
        for_valid_rows(0, lambda r: gather_row(0, 0, r).start())
        for cp in weight_copies(te_ref[0], ws_ref[0]):
            cp.start()

    changed = (j == 0) | (te_ref[j] != te_ref[jnp.maximum(j - 1, 0)])

    @pl.when(changed)
    def _():
        ws = ws_ref[j]
        for cp in weight_copies(te_ref[j], ws):
            cp.wait()
        wg_s[...] = wgbuf[ws].astype(BF16)
        wu_s[...] = wubuf[ws].astype(BF16)
        wd_s[...] = wdbuf[ws].astype(BF16)

        @pl.when(ne_ref[j] >= 0)
        def _():
            for cp in weight_copies(ne_ref[j], 1 - ws):
                cp.start()

    @pl.when(j < n_used)
    def _():
        @pl.when(j + 1 < n_used)
        def _():
            for_valid_rows(j + 1, lambda r: gather_row(j + 1, 1 - slot, r).start())

        for_valid_rows(j, lambda r: gather_row(j, slot, r).wait())

        @pl.when(j >= 2)
        def _():
            for_valid_rows(j - 2, lambda r: scatter_row(j - 2, slot, r).wait())

        x = xbuf[slot].reshape(MOE_TM, D_MODEL).astype(BF16)
        hg = jnp.dot(x, wg_s[...], preferred_element_type=F32)
        hu = jnp.dot(x, wu_s[...], preferred_element_type=F32)
        act = (_silu(hg) * hu).astype(BF16)
        zbuf[slot] = jnp.dot(act, wd_s[...], preferred_element_type=F32).reshape((MOE_TM,) + SLAB)
        for_valid_rows(j, lambda r: scatter_row(j, slot, r).start())

        @pl.when(j == n_used - 1)
        def _():
            @pl.when(j >= 1)
            def _():
                for_valid_rows(j - 1, lambda r: scatter_row(j - 1, 1 - slot, r).wait())
            for_valid_rows(j, lambda r: scatter_row(j, slot, r).wait())


def _moe_experts(layer, h, meta, wg, wu, wd):
    any_spec = pl.BlockSpec(memory_space=pl.ANY)
    up_shape, down_shape = (D_MODEL, MOE_HIDDEN), (MOE_HIDDEN, D_MODEL)
    return pl.pallas_call(
        functools.partial(_moe_body, layer),
        grid_spec=pltpu.PrefetchScalarGridSpec(
            num_scalar_prefetch=len(meta),
            grid=(MOE_TILES,),
            in_specs=[any_spec, any_spec, any_spec, any_spec],
            out_specs=any_spec,
            scratch_shapes=[pltpu.SMEM((MOE_ROWS,), jnp.int32),
                            pltpu.VMEM((2, MOE_TM) + SLAB, F32), pltpu.VMEM((2, MOE_TM) + SLAB, F32),
                            pltpu.VMEM((2,) + up_shape, F32), pltpu.VMEM((2,) + up_shape, F32),
                            pltpu.VMEM((2,) + down_shape, F32),
                            pltpu.VMEM(up_shape, BF16), pltpu.VMEM(up_shape, BF16), pltpu.VMEM(down_shape, BF16),
                            pltpu.SemaphoreType.DMA((2,)), pltpu.SemaphoreType.DMA((2,)),
                            pltpu.SemaphoreType.DMA((2,))],
        ),
        out_shape=jax.ShapeDtypeStruct((2 * N_TOK,) + SLAB, F32),
        compiler_params=_cparams(("arbitrary",)),
        name="moe_experts",
    )(*meta, h, wg, wu, wd)


def _combine_body(split, x_ref, r_ref, gate_ref, g_ref, b_ref, z1_ref, z2_ref, *o_refs):
    r = r_ref[...]
    z1 = z1_ref[...].reshape(x_ref.shape)
    z2 = z2_ref[...].reshape(x_ref.shape)
    y = r[:, 2:3] * z1 + r[:, 3:4] * z2
    out = _layer_norm(DEEPNORM_ALPHA * x_ref[...] + gate_ref[0] * y, g_ref[...], b_ref[...])
    if split:
        is_prompt = pl.program_id(0) < N_PROMPT // x_ref.shape[0]

        @pl.when(is_prompt)
        def _():
            o_refs[0][...] = out

        @pl.when(jnp.logical_not(is_prompt))
        def _():
            o_refs[1][...] = out
    else:
        o_refs[0][...] = out


def _combine(x, z, route, gate, ln_g, ln_b, split=False):
    tm = 512
    nt = N_TOK // tm
    npt = N_PROMPT // tm
    row_spec = pl.BlockSpec((tm, D_MODEL), lambda i: (i, 0))
    vec_spec = pl.BlockSpec((1, D_MODEL), lambda i: (0, 0))
    if split:
        out_specs = [pl.BlockSpec((tm, D_MODEL), lambda i: (jnp.minimum(i, npt - 1), 0)),
                     pl.BlockSpec((tm, D_MODEL), lambda i: (jnp.maximum(i - npt, 0), 0))]
        out_shape = [jax.ShapeDtypeStruct((N_PROMPT, D_MODEL), F32), jax.ShapeDtypeStruct((N_SAMPLE, D_MODEL), F32)]
    else:
        out_specs = row_spec
        out_shape = jax.ShapeDtypeStruct((N_TOK, D_MODEL), F32)
    return pl.pallas_call(
        functools.partial(_combine_body, split),
        grid=(nt,),
        in_specs=[row_spec, pl.BlockSpec((tm, LANES), lambda i: (i, 0)),
                  pl.BlockSpec((1, 1, D_MODEL), lambda i: (_mod_group(i * tm), 0, 0)),
                  vec_spec, vec_spec, pl.BlockSpec((tm,) + SLAB, lambda i: (i, 0, 0)),
                  pl.BlockSpec((tm,) + SLAB, lambda i: (nt + i, 0, 0))],
        out_specs=out_specs,
        out_shape=out_shape,
        compiler_params=_cparams(("arbitrary",)),
        name="moe_combine_ln",
    )(x, route, gate, ln_g, ln_b, z, z)


def _od_post_body(n_prev, q_ref, k_ref, v_ref, xs_ref, xs_lo_ref, xs_hi_ref, bc_ref, bc_lo_ref, bc_hi_ref, dt_ref,
                  cos_ref, sin_ref, cw_ref, cb_ref, dtb_ref, *refs):
    prev = refs[:2 * n_prev]
    qr_ref, kr_ref, vb_ref, xc_ref, bcc_ref, dtp_ref, c_k, c_v = refs[2 * n_prev:]
    i = pl.program_id(0)
    cos = cos_ref[...]
    sin = sin_ref[...]
    k = k_ref[...]
    v = v_ref[...]
    qr_ref[...] = _rope(q_ref[...], cos, sin, GQA_HEAD_DIM // 4).astype(BF16)
    kr_ref[...] = _rope(k, cos, sin, GQA_HEAD_DIM // 4).astype(BF16)
    vb_ref[...] = v.astype(BF16)

    is_prompt = i < BATCH
    tile_in_seq = (i - BATCH) % (DEC_SEQ // POST_TM)
    seq_start = is_prompt | (tile_in_seq == 0)
    seq_end = is_prompt | (tile_in_seq == DEC_SEQ // POST_TM - 1)

    def conv_silu(x, lo_ref, hi_ref, w, b):
        rows = x.shape[0]
        r = lax.broadcasted_iota(jnp.int32, x.shape, 0)
        lo = jnp.where(seq_start, 0.0, lo_ref[7:8, :])
        hi = jnp.where(seq_end, 0.0, hi_ref[0:1, :])
        before = jnp.where(r == 0, lo, pltpu.roll(x, 1, 0))
        after = jnp.where(r == rows - 1, hi, pltpu.roll(x, rows - 1, 0))
        return _silu(before * w[0:1] + x * w[1:2] + after * w[2:3] + b)

    xc_ref[...] = conv_silu(xs_ref[...], xs_lo_ref, xs_hi_ref, cw_ref[:, :SSD_D_INNER], cb_ref[:, :SSD_D_INNER])
    bcc_ref[...] = conv_silu(bc_ref[...], bc_lo_ref, bc_hi_ref, cw_ref[:, SSD_D_INNER:], cb_ref[:, SSD_D_INNER:])
    t = dt_ref[...] + dtb_ref[...]
    dtp_ref[...] = jnp.maximum(t, 0.0) + jnp.log1p(jnp.exp(-jnp.abs(t)))

    @pl.when(is_prompt)
    def _():
        if n_prev:
            c_k[0, 0] = prev[0][0, 0]
            c_v[0, 0] = prev[1][0, 0]
        c_k[0, n_prev] = k.reshape(SEQ, GQA_KV_HEADS, GQA_HEAD_DIM)
        c_v[0, n_prev] = v.reshape(SEQ, GQA_KV_HEADS, GQA_HEAD_DIM)


def _od_post(proj, cos, sin, conv_w, conv_b, dt_bias, prev):
    tm = POST_TM
    halo = 8
    per = tm // halo
    n_halo = N_TOK // halo
    n_prev = 1 if prev else 0
    blk = lambda w, c: pl.BlockSpec((tm, w), lambda i: (i, c))
    lo = lambda w, c: pl.BlockSpec((halo, w), lambda i: (jnp.maximum(i * per - 1, 0), c))
    hi = lambda w, c: pl.BlockSpec((halo, w), lambda i: (jnp.minimum((i + 1) * per, n_halo - 1), c))
    full = lambda a: pl.BlockSpec(a.shape, lambda i: (0, 0))
    oblk = lambda w: pl.BlockSpec((tm, w), lambda i: (i, 0))
    cache_spec = lambda layers: pl.BlockSpec((1, layers, SEQ, GQA_KV_HEADS, GQA_HEAD_DIM),
                                             lambda i: (jnp.minimum(i, BATCH - 1), 0, 0, 0, 0))
    cache_shape = jax.ShapeDtypeStruct((BATCH, n_prev + 1, SEQ, GQA_KV_HEADS, GQA_HEAD_DIM), F32)
    return pl.pallas_call(
        functools.partial(_od_post_body, n_prev),
        grid=(N_TOK // tm,),
        in_specs=[blk(1024, 0), blk(256, 12), blk(256, 13),
                  blk(1024, 1), lo(1024, 1), hi(1024, 1), blk(512, 7), lo(512, 7), hi(512, 7), blk(LANES, 32),
                  blk(LANES, 0), blk(LANES, 0), full(conv_w), full(conv_b), full(dt_bias)]
        + [cache_spec(1)] * (2 * n_prev),
        out_specs=[oblk(1024), oblk(256), oblk(256), oblk(1024), oblk(512), oblk(LANES),
                   cache_spec(n_prev + 1), cache_spec(n_prev + 1)],
        out_shape=[
            jax.ShapeDtypeStruct((N_TOK, 1024), BF16),
            jax.ShapeDtypeStruct((N_TOK, 256), BF16),
            jax.ShapeDtypeStruct((N_TOK, 256), BF16),
            jax.ShapeDtypeStruct((N_TOK, 1024), F32),
            jax.ShapeDtypeStruct((N_TOK, 512), F32),
            jax.ShapeDtypeStruct((N_TOK, LANES), F32),
            cache_shape, cache_shape,
        ],
        compiler_params=_cparams(("arbitrary",)),
        name="odd_post",
    )(proj, proj, proj, proj, proj, proj, proj, proj, proj, proj, cos, sin, conv_w, conv_b, dt_bias, *prev)


def _sink_column(sink_ref, kh, rows):
    g = GQA_HEADS // GQA_KV_HEADS
    return jnp.concatenate(
        [jnp.broadcast_to(sink_ref[kh * g + j:kh * g + j + 1, 0:1], (rows, 1)) for j in range(g)], axis=0)


def _gqa_prompt_body(q_ref, k_ref, v_ref, sink_ref, o_ref):
    g = GQA_HEADS // GQA_KV_HEADS
    d = GQA_HEAD_DIM
    rows = q_ref.shape[0]
    scale = d ** -0.5
    outs = []
    for kh in range(GQA_KV_HEADS):
        qg = jnp.concatenate([q_ref[:, (kh * g + j) * d:(kh * g + j + 1) * d] for j in range(g)], axis=0)
        k = k_ref[:, kh * d:(kh + 1) * d]
        v = v_ref[:, kh * d:(kh + 1) * d]
        s = _nt(qg, k) * scale
        snk = _sink_column(sink_ref, kh, rows)
        m = jnp.maximum(jnp.max(s, axis=-1, keepdims=True), snk)
        e = jnp.exp(s - m)
        l = jnp.sum(e, axis=-1, keepdims=True) + jnp.exp(snk - m)
        o = jnp.dot(e.astype(BF16), v, preferred_element_type=F32) / l
        outs.extend(o[j * rows:(j + 1) * rows] for j in range(g))
    o_ref[...] = jnp.concatenate(outs, axis=1).astype(BF16)


def _gqa_prompt(q, k, v, sink):
    spec = lambda w: pl.BlockSpec((SEQ, w), lambda i: (i, 0))
    return pl.pallas_call(
        _gqa_prompt_body,
        grid=(BATCH,),
        in_specs=[spec(1024), spec(256), spec(256), pl.BlockSpec((8, LANES), lambda i: (0, 0))],
        out_specs=spec(1024),
        out_shape=jax.ShapeDtypeStruct((N_PROMPT, 1024), BF16),
        compiler_params=_cparams(("parallel",)),
        name="gqa_dense",
    )(q, k, v, sink)


def _gqa_window_body(q_ref, k_ref, v_ref, kc_ref, vc_ref, sink_ref, o_ref):
    g = GQA_HEADS // GQA_KV_HEADS
    d = GQA_HEAD_DIM
    blk = q_ref.shape[0]
    span = 3 * blk
    n = pl.program_id(1)
    start = pl.multiple_of(jnp.clip((n - 1) * blk, 0, DEC_SEQ - span), blk)
    scale = d ** -0.5
    qpos = n * blk + lax.broadcasted_iota(jnp.int32, (blk, span), 0)
    kpos = start + lax.broadcasted_iota(jnp.int32, (blk, span), 1)
    valid1 = jnp.abs(qpos - kpos) <= WINDOW
    valid = jnp.concatenate([valid1] * g, axis=0)
    outs = []
    for kh in range(GQA_KV_HEADS):
        qg = jnp.concatenate([q_ref[:, (kh * g + j) * d:(kh * g + j + 1) * d] for j in range(g)], axis=0)
        kl = k_ref[pl.ds(start, span), kh * d:(kh + 1) * d]
        vl = v_ref[pl.ds(start, span), kh * d:(kh + 1) * d]
        kc = kc_ref[:, kh * d:(kh + 1) * d]
        vc = vc_ref[:, kh * d:(kh + 1) * d]
        sl = jnp.where(valid, _nt(qg, kl) * scale, -jnp.inf)
        sc = _nt(qg, kc) * scale
        snk = _sink_column(sink_ref, kh, blk)
        m = jnp.maximum(jnp.maximum(jnp.max(sl, axis=-1, keepdims=True), jnp.max(sc, axis=-1, keepdims=True)), snk)
        el = jnp.exp(sl - m)
        ec = jnp.exp(sc - m)
        l = jnp.sum(el, axis=-1, keepdims=True) + jnp.sum(ec, axis=-1, keepdims=True) + jnp.exp(snk - m)
        o = (jnp.dot(el.astype(BF16), vl, preferred_element_type=F32)
             + jnp.dot(ec.astype(BF16), vc, preferred_element_type=F32)) / l
        outs.extend(o[j * blk:(j + 1) * blk] for j in range(g))
    o_ref[...] = jnp.concatenate(outs, axis=1).astype(BF16)


def _gqa_window(q, k, v, cache_k, cache_v, sink):
    blk = WINDOW
    nq = DEC_SEQ // blk
    q0 = N_PROMPT // blk
    s0 = N_PROMPT // DEC_SEQ
    seq_spec = pl.BlockSpec((DEC_SEQ, 256), lambda b, j: (s0 + b, 0))
    ctx_spec = pl.BlockSpec((PAST_LEN, 256), lambda b, j: (b, 0))
    return pl.pallas_call(
        _gqa_window_body,
        grid=(DEC_BATCH, nq),
        in_specs=[pl.BlockSpec((blk, 1024), lambda b, j: (q0 + b * nq + j, 0)), seq_spec, seq_spec, ctx_spec,
                  ctx_spec,
                  pl.BlockSpec((8, LANES), lambda b, j: (0, 0))],
        out_specs=pl.BlockSpec((blk, 1024), lambda b, j: (b * nq + j, 0)),
        out_shape=jax.ShapeDtypeStruct((N_SAMPLE, 1024), BF16),
        compiler_params=_cparams(("parallel", "parallel")),
        name="gqa_window",
    )(q, k, v, cache_k, cache_v, sink)


def _expand(x, sel):
    hi = x.astype(BF16)
    r1 = x - hi.astype(F32)
    mid = r1.astype(BF16)
    lo = (r1 - mid.astype(F32)).astype(BF16)
    return (jnp.dot(hi, sel, preferred_element_type=F32) + jnp.dot(mid, sel, preferred_element_type=F32)
            + jnp.dot(lo, sel, preferred_element_type=F32))


def _ssd_body(has_h0, xs_ref, bc_ref, dt_ref, alog_ref, e1_ref, e2_ref, *refs):
    if has_h0:
        h0f_ref, h0b_ref, y_ref, stf_ref, stb_ref = refs
    else:
        y_ref, stf_ref, stb_ref = refs
    d = pl.program_id(1)
    c = pl.program_id(2)
    q = SSD_CHUNK
    ppg = SSD_HEADS // SSD_GROUPS // 2
    hd = SSD_HEAD_DIM

    row = lax.broadcasted_iota(jnp.int32, (q, LANES), 0)
    a = -jnp.exp(alog_ref[...])
    dta = dt_ref[...] * a

    def scan(x, rev):
        k = 1
        while k < q:
            if rev:
                x = x + jnp.where(row < q - k, pltpu.roll(x, q - k, 0), 0.0)
            else:
                x = x + jnp.where(row >= k, pltpu.roll(x, k, 0), 0.0)
            k *= 2
        return x

    li = lax.broadcasted_iota(jnp.int32, (q, q), 0)
    si = lax.broadcasted_iota(jnp.int32, (q, q), 1)

    def run(rev):
        st_ref = stb_ref if rev else stf_ref

        @pl.when(c == 0)
        def _():
            if has_h0:
                st_ref[0] = (h0b_ref if rev else h0f_ref)[0, 0]
            else:
                st_ref[0] = jnp.zeros(st_ref.shape[1:], F32)

        cum = scan(dta, rev)
        cum_t = cum.T
        e_row = 0 if rev else q - 1
        mask = (li <= si) if rev else (li >= si)
        dt_x = _expand(dt_ref[...], e1_ref[0])
        cum_x = _expand(cum, e1_ref[0])
        cum_b = _expand(cum, e2_ref[0])
        edge_x = cum_x[e_row:e_row + 1, :]
        xdt = xs_ref[...] * dt_x
        xdt16 = xdt.astype(BF16)
        xw16 = (xdt * jnp.exp(edge_x - cum_x)).astype(BF16)
        ecum = jnp.exp(cum_x)
        lower = si < hd
        ys = []
        for g in range(SSD_GROUPS):
            bm16 = bc_ref[:, g * SSD_STATE:(g + 1) * SSD_STATE].astype(BF16)
            cm16 = bc_ref[:, (SSD_GROUPS + g) * SSD_STATE:(SSD_GROUPS + g + 1) * SSD_STATE].astype(BF16)
            cb = _nt(cm16, bm16)
            for pp in range(ppg):
                p = g * ppg + pp
                cols = slice(p * LANES, (p + 1) * LANES)
                y_pair, fac = [], []
                for u in range(2):
                    h = 2 * p + u
                    j = (SSD_HEADS if rev else 0) + h
                    seg = cum_b[:, h * LANES:(h + 1) * LANES] - cum_t[j:j + 1, :]
                    decay = jnp.exp(jnp.where(mask, seg, -jnp.inf))
                    y_pair.append(jnp.dot((cb * decay).astype(BF16), xdt16[:, cols], preferred_element_type=F32))
                    fac.append(jnp.broadcast_to(jnp.exp(cum_t[j:j + 1, e_row:e_row + 1]), (hd, SSD_STATE)))
                y_diag = jnp.where(lower, y_pair[0], y_pair[1])
                state = lax.dot_general(xw16[:, cols], bm16, (((0,), (0,)), ((), ())),
                                        preferred_element_type=F32)
                h_prev = st_ref[0, 2 * p:2 * p + 2].reshape(2 * hd, SSD_STATE)
                y_off = _nt(cm16, h_prev.astype(BF16)) * ecum[:, cols]
                ys.append(y_diag + y_off)
                h_new = h_prev * jnp.concatenate(fac, axis=0) + state
                st_ref[0, 2 * p:2 * p + 2] = h_new.reshape(2, hd, SSD_STATE)
        y_ref[0] = jnp.concatenate(ys, axis=1)

    @pl.when(d == 0)
    def _():
        run(False)

    @pl.when(d == 1)
    def _():
        run(True)


def _ssd(xs, bc, dtp, a_log, n_seq, seq, row_off, h0=None):
    b = n_seq
    t = n_seq * seq
    nc = seq // SSD_CHUNK
    c0 = row_off // SSD_CHUNK

    def rmap(i, d, c):
        return i * nc + jnp.where(d == 0, c, nc - 1 - c)

    blk = lambda w: pl.BlockSpec((SSD_CHUNK, w), lambda i, d, c: (c0 + rmap(i, d, c), 0))
    st_spec = pl.BlockSpec((1, SSD_HEADS, SSD_HEAD_DIM, SSD_STATE), lambda i, d, c: (i, 0, 0, 0))
    st_shape = jax.ShapeDtypeStruct((b, SSD_HEADS, SSD_HEAD_DIM, SSD_STATE), F32)
    def sel(w):
        row = jnp.arange(LANES)[None, :, None]
        head = SSD_HEADS * jnp.arange(2)[:, None, None] + (jnp.arange(SSD_HEADS * w) // w)[None, None, :]
        return (row == head).astype(BF16)

    sel_spec = lambda w: pl.BlockSpec((1, LANES, SSD_HEADS * w), lambda i, d, c: (d, 0, 0))
    in_specs = [blk(1024), blk(512), blk(LANES), pl.BlockSpec((1, LANES), lambda i, d, c: (0, 0)),
                sel_spec(SSD_HEAD_DIM), sel_spec(LANES)]
    args = [xs, bc, dtp, a_log, sel(SSD_HEAD_DIM), sel(LANES)]
    if h0 is not None:
        layer, h0_f, h0_b = h0
        h0_spec = pl.BlockSpec((1, 1, SSD_HEADS, SSD_HEAD_DIM, SSD_STATE), lambda i, d, c: (i, layer, 0, 0, 0))
        in_specs += [h0_spec, h0_spec]
        args += [h0_f, h0_b]
    return pl.pallas_call(
        functools.partial(_ssd_body, h0 is not None),
        grid=(b, 2, nc),
        in_specs=in_specs,
        out_specs=[pl.BlockSpec((1, SSD_CHUNK, 1024), lambda i, d, c: (d, rmap(i, d, c), 0)), st_spec, st_spec],
        out_shape=[jax.ShapeDtypeStruct((2, t, 1024), F32), st_shape, st_shape],
        compiler_params=_cparams(("parallel", "arbitrary", "arbitrary")),
        name="ssd_scan",
    )(*args)


def _ssd_gate_body(y_ref, xs_ref, z_ref, dsk_ref, g_ref, o_ref):
    y = (y_ref[0] + y_ref[1] + xs_ref[...] * dsk_ref[...]) * _silu(z_ref[...])
    w = SSD_D_INNER // SSD_GROUPS
    outs = [_rms(y[:, g * w:(g + 1) * w], g_ref[:, g * w:(g + 1) * w]) for g in range(SSD_GROUPS)]
    o_ref[...] = jnp.concatenate(outs, axis=1).astype(BF16)


def _ssd_gate(y2, xs, proj, d_skip, norm_g, row_off):
    tm = 512
    t = y2.shape[1]
    off = row_off // tm
    return pl.pallas_call(
        _ssd_gate_body,
        grid=(t // tm,),
        in_specs=[pl.BlockSpec((2, tm, 1024), lambda i: (0, i, 0)),
                  pl.BlockSpec((tm, 1024), lambda i: (off + i, 0)),
                  pl.BlockSpec((tm, 1024), lambda i: (off + i, 2)),
                  pl.BlockSpec((1, 1024), lambda i: (0, 0)),
                  pl.BlockSpec((1, 1024), lambda i: (0, 0))],
        out_specs=pl.BlockSpec((tm, 1024), lambda i: (i, 0)),
        out_shape=jax.ShapeDtypeStruct((t, 1024), BF16),
        compiler_params=_cparams(("parallel",)),
        name="ssd_gate_norm",
    )(y2, xs, proj, d_skip, norm_g)


def _even_layer(x, mod, layer_idx, i, p, ctx, tables, prev_caches):
    w_in, q_norm, kv_norm, wq_b, wkv_b, lam, subln, w_out = p
    ctx_ckv, ctx_kpe, ctx_dk, ctx_dv = ctx
    cos, sin = tables
    lam_init = 0.8 - 0.6 * math.exp(-0.3 * layer_idx)

    w_r = jnp.concatenate([w_in[:, :1024], w_in[:, 1088:], w_in[:, 1024:1088],
                           jnp.zeros((D_MODEL, PROJ_N - w_in.shape[1]), F32)], axis=1).astype(BF16)
    proj = _proj(x, mod[0], mod[1], w_r)
    qn, ckv, dq, dk, dv, kpe, *caches = _ev_post(proj, cos, sin, q_norm[None], kv_norm[None], prev_caches)

    wq = wq_b.reshape(MLA_Q_RANK, MLA_HEADS, MLA_NOPE + MLA_ROPE)
    wq_r = jnp.pad(wq, ((0, 0), (0, 0), (0, MLA_ROPE))).reshape(MLA_Q_RANK, -1).astype(BF16)
    wkv = wkv_b.astype(BF16)
    q = _mm(qn, wq_r, BF16, 512, N_TOK // 512)
    kv = _mm(ckv, wkv, BF16, 512, N_TOK // 512)
    n_even = ctx_ckv.shape[1]
    kv_ctx = _mm(ctx_ckv.reshape(-1, MLA_KV_RANK), wkv, BF16, PAST_LEN, DEC_BATCH,
                 row_block=lambda b: b * n_even + i)

    n_ctx = DEC_BATCH * PAST_LEN
    kpe_ctx = jnp.pad(ctx_kpe[:, i].reshape(n_ctx, MLA_ROPE), ((0, 0), (0, LANES - MLA_ROPE))).astype(BF16)
    dk_ctx = ctx_dk[:, i].reshape(n_ctx, -1).astype(BF16)
    dv_ctx = ctx_dv[:, i].reshape(n_ctx, -1).astype(BF16)

    args = (lam_init, q, cos, sin, kv, kpe, dq, dk, dv, lam, subln[None])
    o1p, o2p = _even_attn(*args, n_seq=BATCH, seq=SEQ, row_off=0)
    o1s, o2s = _even_attn(*args, n_seq=DEC_BATCH, seq=DEC_SEQ, row_off=N_PROMPT,
                          ctx=(kv_ctx, kpe_ctx, dk_ctx, dv_ctx))
    return (o1p, o1s, o2p, o2s), w_out.astype(BF16), caches


def _odd_layer(x, mod, i, p, ctx, tables, prev_caches):
    w_in, sink, conv_w, conv_b, dt_bias, a_log, d_skip, norm_g, w_out = p
    k_ctx, v_ctx, h0_f, h0_b = ctx
    cos, sin = tables

    w_r = jnp.concatenate([w_in[:, :1024], w_in[:, 2560:3584], w_in[:, 1536:2560], w_in[:, 1024:1536],
                           w_in[:, 3584:], jnp.zeros((D_MODEL, PROJ_N - w_in.shape[1]), F32)],
                          axis=1).astype(BF16)
    proj = _proj(x, mod[0], mod[1], w_r)
    cb = conv_b[None]
    dtb = jnp.pad(dt_bias.reshape(1, -1), ((0, 0), (0, LANES - 2 * SSD_HEADS)))
    alog = jnp.pad(a_log.reshape(1, -1), ((0, 0), (0, LANES - 2 * SSD_HEADS)))
    q, k, v, xs, bc, dtp, *caches = _od_post(proj, cos, sin, conv_w, cb, dtb, prev_caches)
    sink_b = jnp.broadcast_to(sink[:, None], (GQA_HEADS, LANES))

    o_att_p = _gqa_prompt(q, k, v, sink_b)
    n_ctx = DEC_BATCH * PAST_LEN
    o_att_s = _gqa_window(q, k, v, k_ctx[:, i].reshape(n_ctx, -1).astype(BF16),
                          v_ctx[:, i].reshape(n_ctx, -1).astype(BF16), sink_b)

    y_p, st_f, st_b = _ssd(xs, bc, dtp, alog, BATCH, SEQ, 0)
    y_s, _, _ = _ssd(xs, bc, dtp, alog, DEC_BATCH, DEC_SEQ, N_PROMPT, h0=(i, h0_f, h0_b))
    dsk = jnp.repeat(d_skip, SSD_HEAD_DIM)[None]
    y_n_p = _ssd_gate(y_p, xs, proj, dsk, norm_g[None], 0)
    y_n_s = _ssd_gate(y_s, xs, proj, dsk, norm_g[None], N_PROMPT)
    return (o_att_p, o_att_s, y_n_p, y_n_s), w_out.astype(BF16), caches, (st_f, st_b)


def kernel(x_prompt, x_sample, cache_mla_ckv, cache_mla_kpe, cache_diff_k, cache_diff_v, cache_gqa_k, cache_gqa_v, state_ssd_fwd, state_ssd_bwd, c, c_ctx, w_mod, b_mod, ln1_g, ln1_b, ln2_g, ln2_b, ev_w_in, mla_q_norm, mla_kv_norm, mla_wq_b, mla_wkv_b, diff_lambda, diff_subln, ev_w_out, od_w_in, gqa_sink, ssd_conv_w, ssd_conv_b, ssd_dt_bias, ssd_a_log, ssd_d, ssd_norm, od_w_out, moe_router_group, moe_router_expert, moe_w_gate, moe_w_up, moe_w_down):
    x = jnp.concatenate([x_prompt.reshape(N_PROMPT, D_MODEL), x_sample.reshape(N_SAMPLE, D_MODEL)], axis=0)
    cs = jnp.concatenate([c_ctx[None], c, jnp.zeros((8 - 1 - DEC_BATCH, D_MODEL), F32)], axis=0)
    mods = _adaln(cs, w_mod, b_mod)[:, :1 + DEC_BATCH].reshape(DEPTH, 1 + DEC_BATCH, 6, 1, D_MODEL)
    tables_64 = _rope_tables(64)
    tables_128 = _rope_tables(GQA_HEAD_DIM)

    ev_caches, od_caches, ssd_states = (), (), []
    for l in range(DEPTH):
        i = l // 2
        mod = [mods[l, :, k] for k in range(6)]
        if l % 2 == 0:
            p = (ev_w_in[i], mla_q_norm[i], mla_kv_norm[i], mla_wq_b[i], mla_wkv_b[i], diff_lambda[i],
                 diff_subln[i], ev_w_out[i])
            ctx = (cache_mla_ckv, cache_mla_kpe, cache_diff_k, cache_diff_v)
            mix, w_out, ev_caches = _even_layer(x, mod, l, i, p, ctx, tables_64, ev_caches)
        else:
            p = (od_w_in[i], gqa_sink[i], ssd_conv_w[i], ssd_conv_b[i], ssd_dt_bias[i], ssd_a_log[i], ssd_d[i],
                 ssd_norm[i], od_w_out[i])
            ctx = (cache_gqa_k, cache_gqa_v, state_ssd_fwd, state_ssd_bwd)
            mix, w_out, od_caches, states = _odd_layer(x, mod, i, p, ctx, tables_128, od_caches)
            ssd_states.append(states)
        w_router = jnp.concatenate([moe_router_group[l], moe_router_expert[l],
                                    jnp.zeros((D_MODEL, LANES - MOE_GROUPS - MOE_EXPERTS), F32)], axis=1)
        w_router_hi = w_router.astype(BF16)
        w_router = jnp.stack([w_router_hi, (w_router - w_router_hi.astype(F32)).astype(BF16)])
        x, h2, route = _out_proj(*mix, w_out, x, mod[2], ln1_g[l][None], ln1_b[l][None], mod[3], mod[4], w_router)
        z = _moe_experts(l, h2, _route_meta(route), moe_w_gate, moe_w_up, moe_w_down)
        x = _combine(x, z, route, mod[5], ln2_g[l][None], ln2_b[l][None], split=(l == DEPTH - 1))

    y_prompt = x[0].reshape(BATCH, SEQ, D_MODEL)
    y_sample = x[1].reshape(DEC_BATCH, DEC_SEQ, D_MODEL)
    new_ssd_fwd = jnp.stack([s[0] for s in ssd_states], axis=1)
    new_ssd_bwd = jnp.stack([s[1] for s in ssd_states], axis=1)
    return (y_prompt, y_sample, *ev_caches, *od_caches, new_ssd_fwd, new_ssd_bwd)
```

```python
import functools
import math

import jax
import jax.numpy as jnp
from jax import lax
from jax.experimental import pallas as pl
from jax.experimental.pallas import tpu as pltpu

F32 = jnp.float32
BF16 = jnp.bfloat16

D_MODEL = 2048
BATCH = 16
SEQ = 256
DEPTH = 4
DEC_BATCH = 2
DEC_SEQ = 1024
PAST_LEN = 512
GRID_W = 64
ROPE_THETA = 10000.0
DEEPNORM_ALPHA = (2.0 * DEPTH) ** 0.25
LN_EPS = 1e-5
RMS_EPS = 1e-6

MLA_HEADS = 8
MLA_Q_RANK = 512
MLA_KV_RANK = 512
MLA_NOPE = 128
MLA_ROPE = 64
MLA_V = 128
DIFF_HEADS = 8
DIFF_QK = 64
DIFF_V = 128
GQA_HEADS = 8
GQA_KV_HEADS = 2
GQA_HEAD_DIM = 128
WINDOW = 128
SSD_HEADS = 16
SSD_HEAD_DIM = 64
SSD_D_INNER = SSD_HEADS * SSD_HEAD_DIM
SSD_GROUPS = 2
SSD_STATE = 128
SSD_CHUNK = 128
MOE_GROUPS = 4
MOE_EPG = 4
MOE_EXPERTS = 16
MOE_HIDDEN = 512

N_PROMPT = BATCH * SEQ
N_SAMPLE = DEC_BATCH * DEC_SEQ
N_TOK = N_PROMPT + N_SAMPLE
PROJ_N = 4224
LANES = 128
VMEM_LIMIT = 56 * 1024 * 1024

PROJ_TM, PROJ_TN = 1024, 1408
OUT_TM = 512
ATT_TQ = 256
ATT_HEADS_PER_STEP = 2
ATT_HEADS_PER_STEP_CTX = 4
POST_TM = 256
MOE_TM = 256
MOE_SLOTS = 2 * N_TOK
MOE_TILES = MOE_SLOTS // MOE_TM + MOE_EXPERTS
MOE_ROWS = MOE_TILES * MOE_TM
SLAB = (D_MODEL // LANES, LANES)


def _cparams(sem):
    return pltpu.CompilerParams(dimension_semantics=sem, vmem_limit_bytes=VMEM_LIMIT)


def _mod_group(row0):
    return jnp.maximum(row0 // DEC_SEQ - (N_PROMPT // DEC_SEQ - 1), 0)


def _silu(x):
    return x / (1.0 + jnp.exp(-x))


def _nt(a, b):
    return lax.dot_general(a, b, (((1,), (1,)), ((), ())), preferred_element_type=F32)


def _rms(x, g):
    return x * lax.rsqrt(jnp.mean(x * x, axis=-1, keepdims=True) + RMS_EPS) * g


def _layer_norm(y, g, b):
    mu = jnp.mean(y, axis=-1, keepdims=True)
    yc = y - mu
    return yc * lax.rsqrt(jnp.mean(yc * yc, axis=-1, keepdims=True) + LN_EPS) * g + b


def _adaln_body(c_ref, w_ref, b_ref, o_ref):
    a = _silu(c_ref[...])
    o_ref[0] = jnp.dot(a, w_ref[0], preferred_element_type=F32) + b_ref[0]


def _adaln(cs, w_mod, b_mod):
    tn = 1024
    n = w_mod.shape[-1]
    return pl.pallas_call(
        _adaln_body,
        grid=(DEPTH, n // tn),
        in_specs=[
            pl.BlockSpec((8, D_MODEL), lambda l, j: (0, 0)),
            pl.BlockSpec((1, D_MODEL, tn), lambda l, j: (l, 0, j)),
            pl.BlockSpec((1, 1, tn), lambda l, j: (l, 0, j)),
        ],
        out_specs=pl.BlockSpec((1, 8, tn), lambda l, j: (l, 0, j)),
        out_shape=jax.ShapeDtypeStruct((DEPTH, 8, n), F32),
        compiler_params=_cparams(("parallel", "parallel")),
        name="adaln",
    )(cs, w_mod, b_mod.reshape(DEPTH, 1, n))


def _proj_body(x_ref, sh_ref, sc_ref, w_ref, o_ref, h_scr):
    @pl.when(pl.program_id(1) == 0)
    def _():
        h = x_ref[...] * (1.0 + sc_ref[0]) + sh_ref[0]
        h_scr[...] = h.astype(BF16)

    o_ref[...] = jnp.dot(h_scr[...], w_ref[...], preferred_element_type=F32)


def _proj(x, shift, scale, w):
    tm, tn = PROJ_TM, PROJ_TN
    n = w.shape[1]
    mod_spec = pl.BlockSpec((1, 1, D_MODEL), lambda i, j: (_mod_group(i * tm), 0, 0))
    return pl.pallas_call(
        _proj_body,
        grid=(N_TOK // tm, n // tn),
        in_specs=[
            pl.BlockSpec((tm, D_MODEL), lambda i, j: (i, 0)),
            mod_spec,
            mod_spec,
            pl.BlockSpec((D_MODEL, tn), lambda i, j: (0, j)),
        ],
        out_specs=pl.BlockSpec((tm, tn), lambda i, j: (i, j)),
        out_shape=jax.ShapeDtypeStruct((N_TOK, n), F32),
        scratch_shapes=[pltpu.VMEM((tm, D_MODEL), BF16)],
        compiler_params=_cparams(("parallel", "arbitrary")),
        name="in_proj",
    )(x, shift, scale, w)


PROJ_BLOCKS = PROJ_N // LANES


def _prep_even_body(a_ref, b_ref, o_ref):
    c = pl.program_id(0)
    half = LANES // 2
    aligned = (MLA_Q_RANK + MLA_KV_RANK) // LANES

    @pl.when(c < aligned)
    def _():
        o_ref[...] = a_ref[0].astype(BF16)

    @pl.when((c >= aligned) & (c < PROJ_BLOCKS - 1))
    def _():
        o_ref[...] = jnp.concatenate([a_ref[0][:, half:], b_ref[0][:, :half]], axis=1).astype(BF16)

    @pl.when(c == PROJ_BLOCKS - 1)
    def _():
        k_pe = a_ref[0][:, :half]
        o_ref[...] = jnp.concatenate([k_pe, jnp.zeros_like(k_pe)], axis=1).astype(BF16)


def _prep_even(w_in, layer):
    aligned = (MLA_Q_RANK + MLA_KV_RANK) // LANES
    last = PROJ_BLOCKS - 1
    return pl.pallas_call(
        _prep_even_body,
        grid=(PROJ_BLOCKS,),
        in_specs=[pl.BlockSpec((1, D_MODEL, LANES), lambda c: (layer, 0, jnp.where(c == last, aligned, c))),
                  pl.BlockSpec((1, D_MODEL, LANES), lambda c: (layer, 0, jnp.minimum(c + 1, last)))],
        out_specs=pl.BlockSpec((D_MODEL, LANES), lambda c: (0, c)),
        out_shape=jax.ShapeDtypeStruct((D_MODEL, PROJ_N), BF16),
        compiler_params=_cparams(("parallel",)),
        name="prep_w_even",
    )(w_in, w_in)


def _prep_odd_body(perm_ref, a_ref, dt_ref, o_ref):
    c = pl.program_id(0)

    @pl.when(c < PROJ_BLOCKS - 1)
    def _():
        o_ref[...] = a_ref[0].astype(BF16)

    @pl.when(c == PROJ_BLOCKS - 1)
    def _():
        o_ref[...] = dt_ref[...]


def _prep_odd(w_in, layer):
    order = list(range(0, 8)) + list(range(20, 28)) + list(range(12, 20)) + [8, 9, 10, 11, 28, 29, 30, 31, 31]
    perm = jnp.asarray(order, jnp.int32)
    n_dt = 2 * SSD_HEADS
    w_dt = jnp.pad(w_in[layer, :, w_in.shape[2] - n_dt:], ((0, 0), (0, LANES - n_dt))).astype(BF16)
    return pl.pallas_call(
        _prep_odd_body,
        grid_spec=pltpu.PrefetchScalarGridSpec(
            num_scalar_prefetch=1,
            grid=(PROJ_BLOCKS,),
            in_specs=[pl.BlockSpec((1, D_MODEL, LANES), lambda c, perm_ref: (layer, 0, perm_ref[c])),
                      pl.BlockSpec((D_MODEL, LANES), lambda c, perm_ref: (0, 0))],
            out_specs=pl.BlockSpec((D_MODEL, LANES), lambda c, perm_ref: (0, c)),
        ),
        out_shape=jax.ShapeDtypeStruct((D_MODEL, PROJ_N), BF16),
        compiler_params=_cparams(("parallel",)),
        name="prep_w_odd",
    )(perm, w_in, w_dt)


def _mm_body(a_ref, b_ref, o_ref):
    o_ref[...] = jnp.dot(a_ref[...].astype(BF16), b_ref[...], preferred_element_type=F32).astype(o_ref.dtype)


def _mm(a, b, out_dtype, tm, n_tiles, row_block=lambda i: i):
    k = a.shape[1]
    n = b.shape[1]
    return pl.pallas_call(
        _mm_body,
        grid=(n_tiles,),
        in_specs=[pl.BlockSpec((tm, k), lambda i: (row_block(i), 0)), pl.BlockSpec((k, n), lambda i: (0, 0))],
        out_specs=pl.BlockSpec((tm, n), lambda i: (i, 0)),
        out_shape=jax.ShapeDtypeStruct((n_tiles * tm, n), out_dtype),
        compiler_params=_cparams(("parallel",)),
        name="mm_resident",
    )(a, b)


def _rope(x, cos, sin_signed, quarter):
    rows, w = x.shape
    reps = w // LANES
    if reps > 1:
        cos = jnp.concatenate([cos] * reps, axis=1)
        sin_signed = jnp.concatenate([sin_signed] * reps, axis=1)
    lane = lax.broadcasted_iota(jnp.int32, x.shape, 1)
    first = (lane % (2 * quarter)) < quarter
    up = pltpu.roll(x, w - quarter, 1)
    dn = pltpu.roll(x, quarter, 1)
    return x * cos + jnp.where(first, up, dn) * sin_signed


def _rope_tables(rdim):
    half = rdim // 2
    quarter = half // 2
    pos = jnp.arange(DEC_SEQ)
    row = (pos // GRID_W).astype(F32)
    col = (pos % GRID_W).astype(F32)
    inv = ROPE_THETA ** (-jnp.arange(quarter, dtype=F32) * 2.0 / half)
    lane = jnp.arange(LANES)
    r = lane % rdim
    use_col = (r // half) == 1
    j = r % quarter
    p = jnp.where(use_col[None, :], col[:, None], row[:, None])
    ang = p * inv[j][None, :]
    sign = jnp.where((r % half) < quarter, -1.0, 1.0).astype(F32)
    cos = jnp.cos(ang)
    sin = jnp.sin(ang) * sign[None, :]
    cos = jnp.concatenate([jnp.ones((N_PROMPT, LANES), F32), cos, cos], axis=0)
    sin = jnp.concatenate([jnp.zeros((N_PROMPT, LANES), F32), sin, sin], axis=0)
    return cos, sin


def _ev_post_body(n_prev, ql_ref, kvl_ref, dq_ref, dk_ref, dv_ref, kpe_ref, cos_ref, sin_ref, gq_ref, gkv_ref,
                  *refs):
    prev = refs[:4 * n_prev]
    qn_ref, ckv_ref, dqr_ref, dkr_ref, dvb_ref, kper_ref, c_ckv, c_kpe, c_dk, c_dv = refs[4 * n_prev:]
    cos = cos_ref[...]
    sin = sin_ref[...]
    qn_ref[...] = _rms(ql_ref[...], gq_ref[...]).astype(BF16)
    ckv = _rms(kvl_ref[...], gkv_ref[...])
    ckv_ref[...] = ckv.astype(BF16)
    dqr_ref[...] = _rope(dq_ref[...], cos, sin, DIFF_QK // 4).astype(BF16)
    dk = _rope(dk_ref[...], cos, sin, DIFF_QK // 4)
    dkr_ref[...] = dk.astype(BF16)
    dv = dv_ref[...]
    dvb_ref[...] = dv.astype(BF16)
    kpe = _rope(kpe_ref[...], cos, sin, MLA_ROPE // 4)
    kper_ref[...] = kpe.astype(BF16)

    @pl.when(pl.program_id(0) < BATCH)
    def _():
        if n_prev:
            c_ckv[0, 0] = prev[0][0, 0]
            c_kpe[0, 0] = prev[1][0, 0]
            c_dk[0, 0] = prev[2][0, 0]
            c_dv[0, 0] = prev[3][0, 0]
        c_ckv[0, n_prev] = ckv
        c_kpe[0, n_prev] = kpe[:, :MLA_ROPE]
        c_dk[0, n_prev] = dk.reshape(SEQ, DIFF_HEADS, 2 * DIFF_QK)
        c_dv[0, n_prev] = dv.reshape(SEQ, DIFF_HEADS, DIFF_V)


def _ev_post(proj, cos, sin, gq, gkv, prev):
    tm = POST_TM
    n_prev = 1 if prev else 0
    nl = n_prev + 1
    row = lambda c: (lambda i: (i, c))
    seq4 = lambda i: (jnp.minimum(i, BATCH - 1), 0, 0, 0)
    seq5 = lambda i: (jnp.minimum(i, BATCH - 1), 0, 0, 0, 0)
    cache_shapes = [(MLA_KV_RANK,), (MLA_ROPE,), (DIFF_HEADS, 2 * DIFF_QK), (DIFF_HEADS, DIFF_V)]

    def cache_spec(layers, tail):
        return pl.BlockSpec((1, layers, SEQ) + tail, seq4 if len(tail) == 1 else seq5)

    return pl.pallas_call(
        functools.partial(_ev_post_body, n_prev),
        grid=(N_TOK // tm,),
        in_specs=[
            pl.BlockSpec((tm, 512), row(0)),
            pl.BlockSpec((tm, 512), row(1)),
            pl.BlockSpec((tm, 1024), row(1)),
            pl.BlockSpec((tm, 1024), row(2)),
            pl.BlockSpec((tm, 1024), row(3)),
            pl.BlockSpec((tm, LANES), row(32)),
            pl.BlockSpec((tm, LANES), row(0)),
            pl.BlockSpec((tm, LANES), row(0)),
            pl.BlockSpec((1, 512), lambda i: (0, 0)),
            pl.BlockSpec((1, 512), lambda i: (0, 0)),
        ] + [cache_spec(1, t) for t in cache_shapes] * n_prev,
        out_specs=[
            pl.BlockSpec((tm, 512), row(0)),
            pl.BlockSpec((tm, 512), row(0)),
            pl.BlockSpec((tm, 1024), row(0)),
            pl.BlockSpec((tm, 1024), row(0)),
            pl.BlockSpec((tm, 1024), row(0)),
            pl.BlockSpec((tm, LANES), row(0)),
        ] + [cache_spec(nl, t) for t in cache_shapes],
        out_shape=[
            jax.ShapeDtypeStruct((N_TOK, 512), BF16),
            jax.ShapeDtypeStruct((N_TOK, 512), BF16),
            jax.ShapeDtypeStruct((N_TOK, 1024), BF16),
            jax.ShapeDtypeStruct((N_TOK, 1024), BF16),
            jax.ShapeDtypeStruct((N_TOK, 1024), BF16),
            jax.ShapeDtypeStruct((N_TOK, LANES), BF16),
        ] + [jax.ShapeDtypeStruct((BATCH, nl, SEQ) + t, F32) for t in cache_shapes],
        compiler_params=_cparams(("arbitrary",)),
        name="even_post",
    )(proj, proj, proj, proj, proj, proj, cos, sin, gq, gkv, *prev)


def _softmax_av(scores, values):
    m = functools.reduce(jnp.maximum, [jnp.max(s, axis=-1, keepdims=True) for s in scores])
    es = [jnp.exp(s - m) for s in scores]
    l = sum(jnp.sum(e, axis=-1, keepdims=True) for e in es)
    o = sum(jnp.dot(e.astype(BF16), v, preferred_element_type=F32) for e, v in zip(es, values))
    return o / l


def _even_attn_body(lam_init, has_ctx, hps, q_ref, cos_ref, sin_ref, kv_ref, kpe_ref, dq_ref, dk_ref, dv_ref,
                    lam_ref, sub_ref, *refs):
    if has_ctx:
        kvc_ref, kpec_ref, dkc_ref, dvc_ref, omla_ref, odiff_ref = refs
    else:
        omla_ref, odiff_ref = refs
    cos = cos_ref[...]
    sin = sin_ref[...]
    lam = lam_ref[...]
    lam_full = (jnp.exp(jnp.sum(lam[0:1] * lam[1:2], axis=-1, keepdims=True))
                - jnp.exp(jnp.sum(lam[2:3] * lam[3:4], axis=-1, keepdims=True)) + lam_init)
    mla_scale = (MLA_NOPE + MLA_ROPE) ** -0.5
    lane = lax.broadcasted_iota(jnp.int32, (q_ref.shape[0], 2 * DIFF_QK), 1)
    for u in range(hps):
        qc = slice(u * 256, (u + 1) * 256)
        dc = slice(u * LANES, (u + 1) * LANES)
        q = q_ref[:, qc]
        qr = _rope(q[:, MLA_NOPE:].astype(F32), cos, sin, MLA_ROPE // 4).astype(BF16)
        qcat = jnp.concatenate([q[:, :MLA_NOPE], qr], axis=1)
        kv = kv_ref[:, qc]
        scores = [_nt(qcat, jnp.concatenate([kv[:, :MLA_NOPE], kpe_ref[...]], axis=1)) * mla_scale]
        values = [kv[:, MLA_NOPE:]]
        if has_ctx:
            kvc = kvc_ref[:, qc]
            scores.append(_nt(qcat, jnp.concatenate([kvc[:, :MLA_NOPE], kpec_ref[...]], axis=1)) * mla_scale)
            values.append(kvc[:, MLA_NOPE:])
        omla_ref[:, dc] = _softmax_av(scores, values).astype(BF16)

        dq = dq_ref[:, dc] * jnp.asarray(DIFF_QK ** -0.5, BF16)
        zero = jnp.zeros_like(dq)
        keys = [dk_ref[:, dc]]
        values = [dv_ref[:, dc]]
        if has_ctx:
            keys.append(dkc_ref[:, dc])
            values.append(dvc_ref[:, dc])
        a1 = _softmax_av([_nt(jnp.where(lane < DIFF_QK, dq, zero), k) for k in keys], values)
        a2 = _softmax_av([_nt(jnp.where(lane >= DIFF_QK, dq, zero), k) for k in keys], values)
        odiff_ref[:, dc] = (_rms(a1 - lam_full * a2, sub_ref[...]) * (1.0 - lam_init)).astype(BF16)


def _even_attn(lam_init, q, cos, sin, kv, kpe, dq, dk, dv, lam, subln, n_seq, seq, row_off, ctx=None):
    tq = ATT_TQ
    hps = ATT_HEADS_PER_STEP if ctx is not None else ATT_HEADS_PER_STEP_CTX
    nq = seq // tq
    q0 = row_off // tq
    s0 = row_off // seq
    qspec = lambda w: pl.BlockSpec((tq, hps * w), lambda b, j, h: (q0 + b * nq + j, h))
    kspec = lambda w: pl.BlockSpec((seq, hps * w), lambda b, j, h: (s0 + b, h))
    tspec = pl.BlockSpec((tq, LANES), lambda b, j, h: (q0 + b * nq + j, 0))
    ospec = pl.BlockSpec((tq, hps * LANES), lambda b, j, h: (b * nq + j, h))
    in_specs = [qspec(256), tspec, tspec, kspec(256),
                pl.BlockSpec((seq, LANES), lambda b, j, h: (s0 + b, 0)),
                qspec(LANES), kspec(LANES), kspec(LANES),
                pl.BlockSpec((4, DIFF_QK), lambda b, j, h: (0, 0)),
                pl.BlockSpec((1, DIFF_V), lambda b, j, h: (0, 0))]
    args = [q, cos, sin, kv, kpe, dq, dk, dv, lam, subln]
    if ctx is not None:
        cspec = lambda w: pl.BlockSpec((PAST_LEN, hps * w), lambda b, j, h: (b, h))
        in_specs += [cspec(256), pl.BlockSpec((PAST_LEN, LANES), lambda b, j, h: (b, 0)), cspec(LANES), cspec(LANES)]
        args += list(ctx)
    return pl.pallas_call(
        functools.partial(_even_attn_body, lam_init, ctx is not None, hps),
        grid=(n_seq, nq, MLA_HEADS // hps),
        in_specs=in_specs,
        out_specs=[ospec, ospec],
        out_shape=[jax.ShapeDtypeStruct((n_seq * seq, 1024), BF16)] * 2,
        compiler_params=_cparams(("parallel", "parallel", "parallel")),
        name="even_attn",
    )(*args)


def _route(logits):
    lane = lax.broadcasted_iota(jnp.int32, logits.shape, 1).astype(F32)
    neg = -jnp.inf
    none = float(LANES)
    is_g = lane < MOE_GROUPS
    lg = jnp.where(is_g, logits, neg)
    mg = jnp.max(lg, axis=-1, keepdims=True)
    g_val = 1.0 / jnp.sum(jnp.exp(lg - mg), axis=-1, keepdims=True)
    g_idx = jnp.min(jnp.where(is_g & (lg == mg), lane, none), axis=-1, keepdims=True)
    e_lo = MOE_GROUPS + g_idx * MOE_EPG
    is_e = (lane >= e_lo) & (lane < e_lo + MOE_EPG)
    le = jnp.where(is_e, logits, neg)
    p = jnp.exp(le - jnp.max(le, axis=-1, keepdims=True))
    p1 = jnp.max(p, axis=-1, keepdims=True)
    i1 = jnp.min(jnp.where(is_e & (p == p1), lane, none), axis=-1, keepdims=True)
    rest = is_e & (lane != i1)
    p2 = jnp.max(jnp.where(rest, p, neg), axis=-1, keepdims=True)
    i2 = jnp.min(jnp.where(rest & (p == p2), lane, none), axis=-1, keepdims=True)
    s = g_val / (p1 + p2)
    return jnp.where(lane == 0.0, i1 - MOE_GROUPS,
                     jnp.where(lane == 1.0, i2 - MOE_GROUPS,
                               jnp.where(lane == 2.0, p1 * s, jnp.where(lane == 3.0, p2 * s, 0.0))))


def _out_proj_body(a1p_ref, a1s_ref, a2p_ref, a2s_ref, w_ref, x_ref, gate_ref, g_ref, b_ref, sh_ref, sc_ref,
                   wr_ref, xo_ref, h_ref, comb_ref):
    is_prompt = pl.program_id(0) < N_PROMPT // OUT_TM
    a1 = jnp.where(is_prompt, a1p_ref[...], a1s_ref[...])
    a2 = jnp.where(is_prompt, a2p_ref[...], a2s_ref[...])
    k1 = a1.shape[1]
    acc = jnp.dot(a1, w_ref[:k1, :], preferred_element_type=F32)
    acc = acc + jnp.dot(a2, w_ref[k1:, :], preferred_element_type=F32)
    xn = _layer_norm(DEEPNORM_ALPHA * x_ref[...] + gate_ref[0] * acc, g_ref[...], b_ref[...])
    xo_ref[...] = xn
    h = xn * (1.0 + sc_ref[0]) + sh_ref[0]
    h_ref[...] = h.reshape(h_ref.shape).astype(BF16)
    h_hi = h.astype(BF16)
    h_lo = (h - h_hi.astype(F32)).astype(BF16)
    logits = (jnp.dot(h_hi, wr_ref[0], preferred_element_type=F32)
              + jnp.dot(h_lo, wr_ref[0], preferred_element_type=F32)
              + jnp.dot(h_hi, wr_ref[1], preferred_element_type=F32))
    comb_ref[...] = _route(logits)


def _out_proj(a1p, a1s, a2p, a2s, w, x, gate, ln_g, ln_b, shift, scale, w_router):
    tm = OUT_TM
    np_tiles = N_PROMPT // tm
    k1, k2 = a1p.shape[1], a2p.shape[1]
    mod_spec = pl.BlockSpec((1, 1, D_MODEL), lambda i: (_mod_group(i * tm), 0, 0))
    vec_spec = pl.BlockSpec((1, D_MODEL), lambda i: (0, 0))
    row_spec = pl.BlockSpec((tm, D_MODEL), lambda i: (i, 0))
    pspec = lambda k: pl.BlockSpec((tm, k), lambda i: (jnp.minimum(i, np_tiles - 1), 0))
    sspec = lambda k: pl.BlockSpec((tm, k), lambda i: (jnp.maximum(i - np_tiles, 0), 0))
    return pl.pallas_call(
        _out_proj_body,
        grid=(N_TOK // tm,),
        in_specs=[
            pspec(k1), sspec(k1), pspec(k2), sspec(k2),
            pl.BlockSpec((k1 + k2, D_MODEL), lambda i: (0, 0)),
            row_spec, mod_spec, vec_spec, vec_spec, mod_spec, mod_spec,
            pl.BlockSpec((2, D_MODEL, LANES), lambda i: (0, 0, 0)),
        ],
        out_specs=[row_spec, pl.BlockSpec((tm,) + SLAB, lambda i: (i, 0, 0)),
                   pl.BlockSpec((tm, LANES), lambda i: (i, 0))],
        out_shape=[
            jax.ShapeDtypeStruct((N_TOK, D_MODEL), F32),
            jax.ShapeDtypeStruct((N_TOK,) + SLAB, BF16),
            jax.ShapeDtypeStruct((N_TOK, LANES), F32),
        ],
        compiler_params=_cparams(("parallel",)),
        name="out_proj_ln_router",
    )(a1p, a1s, a2p, a2s, w, x, gate, ln_g, ln_b, shift, scale, w_router)


def _route_meta(route):
    e = route[:, :2].astype(jnp.int32).reshape(-1)
    onehot = (e[:, None] == jnp.arange(MOE_EXPERTS, dtype=jnp.int32)[None, :]).astype(jnp.int32)
    csum = jnp.cumsum(onehot, axis=0)
    rank = jnp.sum((csum - onehot) * onehot, axis=1)
    count = csum[-1]
    ntile = (count + MOE_TM - 1) // MOE_TM
    tile_end = jnp.cumsum(ntile)
    tile_off = tile_end - ntile
    pos = jnp.sum(onehot * tile_off[None, :], axis=1) * MOE_TM + rank
    n_used = tile_end[-1:]
    j = jnp.arange(MOE_TILES, dtype=jnp.int32)
    tile_expert = jnp.sum((tile_end[None, :] <= jnp.minimum(j, n_used - 1)[:, None]).astype(jnp.int32), axis=1)
    is_e = tile_expert[:, None] == jnp.arange(MOE_EXPERTS, dtype=jnp.int32)[None, :]
    left = jnp.sum(jnp.where(is_e, (count - (j[:, None] - tile_off[None, :]) * MOE_TM)[...], 0), axis=1)
    n_valid = jnp.where(j < n_used, jnp.clip(left, 0, MOE_TM), 0)
    ids = jnp.arange(MOE_EXPERTS, dtype=jnp.int32)
    later = (ids[None, :] > ids[:, None]) & (ntile[None, :] > 0)
    next_of = jnp.min(jnp.where(later, ids[None, :], MOE_EXPERTS), axis=1)
    next_of = jnp.where(next_of == MOE_EXPERTS, -1, next_of)
    next_expert = jnp.sum(jnp.where(is_e, next_of[None, :], 0), axis=1)
    ordinal = jnp.cumsum((ntile > 0).astype(jnp.int32)) - 1
    w_slot = jnp.sum(jnp.where(is_e, ordinal[None, :], 0), axis=1) % 2
    return pos, tile_expert, n_valid, n_used, next_expert, w_slot


def _moe_body(layer, pos_ref, te_ref, nv_ref, nu_ref, ne_ref, ws_ref, h_ref, wg_ref, wu_ref, wd_ref, z_ref,
              inv, xbuf, zbuf, wgbuf, wubuf, wdbuf, wg_s, wu_s, wd_s, gsem, ssem, wsem):
    j = pl.program_id(0)
    slot = j % 2
    n_used = nu_ref[0]
    group = 8

    def weight_copies(e, s):
        return [pltpu.make_async_copy(wg_ref.at[layer, e], wgbuf.at[s], wsem.at[s]),
                pltpu.make_async_copy(wu_ref.at[layer, e], wubuf.at[s], wsem.at[s]),
                pltpu.make_async_copy(wd_ref.at[layer, e], wdbuf.at[s], wsem.at[s])]

    def gather_row(tile, s, r):
        tok = lax.shift_right_logical(inv[tile * MOE_TM + r], 1)
        return pltpu.make_async_copy(h_ref.at[tok], xbuf.at[s, r], gsem.at[s])

    def scatter_row(tile, s, r):
        sl = inv[tile * MOE_TM + r]
        dst = (sl & 1) * N_TOK + lax.shift_right_logical(sl, 1)
        return pltpu.make_async_copy(zbuf.at[s, r], z_ref.at[dst], ssem.at[s])

    def for_valid_rows(tile, fn):
        n = nv_ref[tile]

        def block(g, c):
            for u in range(group):
                fn(g * group + u)
            return c

        def single(r, c):
            fn(r)
            return c

        full = lax.shift_right_logical(n, 3)
        lax.fori_loop(0, full, block, 0)
        lax.fori_loop(full * group, n, single, 0)

    @pl.when(j == 0)
    def _():
        def fill(g, c):
            for u in range(group):
                sl = g * group + u
                inv[pos_ref[sl]] = sl
            return c
        lax.fori_loop(0, MOE_SLOTS // group, fill, 0)
        xbuf[...] = jnp.zeros_like(xbuf)
        for_valid_rows(0, lambda r: gather_row(0, 0, r).start())
        for cp in weight_copies(te_ref[0], ws_ref[0]):
            cp.start()

    changed = (j == 0) | (te_ref[j] != te_ref[jnp.maximum(j - 1, 0)])

    @pl.when(changed)
    def _():
        ws = ws_ref[j]
        for cp in weight_copies(te_ref[j], ws):
            cp.wait()
        wg_s[...] = wgbuf[ws].astype(BF16)
        wu_s[...] = wubuf[ws].astype(BF16)
        wd_s[...] = wdbuf[ws].astype(BF16)

        @pl.when(ne_ref[j] >= 0)
        def _():
            for cp in weight_copies(ne_ref[j], 1 - ws):
                cp.start()

    @pl.when(j < n_used)
    def _():
        @pl.when(j + 1 < n_used)
        def _():
            for_valid_rows(j + 1, lambda r: gather_row(j + 1, 1 - slot, r).start())

        for_valid_rows(j, lambda r: gather_row(j, slot, r).wait())

        @pl.when(j >= 2)
        def _():
            for_valid_rows(j - 2, lambda r: scatter_row(j - 2, slot, r).wait())

        x = xbuf[slot].reshape(MOE_TM, D_MODEL)
        hg = jnp.dot(x, wg_s[...], preferred_element_type=F32)
        hu = jnp.dot(x, wu_s[...], preferred_element_type=F32)
        act = (_silu(hg) * hu).astype(BF16)
        z = jnp.dot(act, wd_s[...], preferred_element_type=F32)
        zbuf[slot] = z.reshape((MOE_TM,) + SLAB).astype(BF16)
        for_valid_rows(j, lambda r: scatter_row(j, slot, r).start())

        @pl.when(j == n_used - 1)
        def _():
            @pl.when(j >= 1)
            def _():
                for_valid_rows(j - 1, lambda r: scatter_row(j - 1, 1 - slot, r).wait())
            for_valid_rows(j, lambda r: scatter_row(j, slot, r).wait())


def _moe_experts(layer, h, meta, wg, wu, wd):
    any_spec = pl.BlockSpec(memory_space=pl.ANY)
    up_shape, down_shape = (D_MODEL, MOE_HIDDEN), (MOE_HIDDEN, D_MODEL)
    return pl.pallas_call(
        functools.partial(_moe_body, layer),
        grid_spec=pltpu.PrefetchScalarGridSpec(
            num_scalar_prefetch=len(meta),
            grid=(MOE_TILES,),
            in_specs=[any_spec, any_spec, any_spec, any_spec],
            out_specs=any_spec,
            scratch_shapes=[pltpu.SMEM((MOE_ROWS,), jnp.int32),
                            pltpu.VMEM((2, MOE_TM) + SLAB, BF16), pltpu.VMEM((2, MOE_TM) + SLAB, BF16),
                            pltpu.VMEM((2,) + up_shape, F32), pltpu.VMEM((2,) + up_shape, F32),
                            pltpu.VMEM((2,) + down_shape, F32),
                            pltpu.VMEM(up_shape, BF16), pltpu.VMEM(up_shape, BF16), pltpu.VMEM(down_shape, BF16),
                            pltpu.SemaphoreType.DMA((2,)), pltpu.SemaphoreType.DMA((2,)),
                            pltpu.SemaphoreType.DMA((2,))],
        ),
        out_shape=jax.ShapeDtypeStruct((2 * N_TOK,) + SLAB, BF16),
        compiler_params=_cparams(("arbitrary",)),
        name="moe_experts",
    )(*meta, h, wg, wu, wd)


def _combine_body(split, x_ref, r_ref, gate_ref, g_ref, b_ref, z1_ref, z2_ref, *o_refs):
    r = r_ref[...]
    z1 = z1_ref[...].reshape(x_ref.shape).astype(F32)
    z2 = z2_ref[...].reshape(x_ref.shape).astype(F32)
    y = r[:, 2:3] * z1 + r[:, 3:4] * z2
    out = _layer_norm(DEEPNORM_ALPHA * x_ref[...] + gate_ref[0] * y, g_ref[...], b_ref[...])
    if split:
        is_prompt = pl.program_id(0) < N_PROMPT // x_ref.shape[0]

        @pl.when(is_prompt)
        def _():
            o_refs[0][...] = out

        @pl.when(jnp.logical_not(is_prompt))
        def _():
            o_refs[1][...] = out
    else:
        o_refs[0][...] = out


def _combine(x, z, route, gate, ln_g, ln_b, split=False):
    tm = 512
    nt = N_TOK // tm
    npt = N_PROMPT // tm
    row_spec = pl.BlockSpec((tm, D_MODEL), lambda i: (i, 0))
    vec_spec = pl.BlockSpec((1, D_MODEL), lambda i: (0, 0))
    if split:
        out_specs = [pl.BlockSpec((tm, D_MODEL), lambda i: (jnp.minimum(i, npt - 1), 0)),
                     pl.BlockSpec((tm, D_MODEL), lambda i: (jnp.maximum(i - npt, 0), 0))]
        out_shape = [jax.ShapeDtypeStruct((N_PROMPT, D_MODEL), F32), jax.ShapeDtypeStruct((N_SAMPLE, D_MODEL), F32)]
    else:
        out_specs = row_spec
        out_shape = jax.ShapeDtypeStruct((N_TOK, D_MODEL), F32)
    return pl.pallas_call(
        functools.partial(_combine_body, split),
        grid=(nt,),
        in_specs=[row_spec, pl.BlockSpec((tm, LANES), lambda i: (i, 0)),
                  pl.BlockSpec((1, 1, D_MODEL), lambda i: (_mod_group(i * tm), 0, 0)),
                  vec_spec, vec_spec, pl.BlockSpec((tm,) + SLAB, lambda i: (i, 0, 0)),
                  pl.BlockSpec((tm,) + SLAB, lambda i: (nt + i, 0, 0))],
        out_specs=out_specs,
        out_shape=out_shape,
        compiler_params=_cparams(("arbitrary",)),
        name="moe_combine_ln",
    )(x, route, gate, ln_g, ln_b, z, z)


def _od_post_body(n_prev, q_ref, k_ref, v_ref, xs_ref, xs_lo_ref, xs_hi_ref, bc_ref, bc_lo_ref, bc_hi_ref, dt_ref,
                  cos_ref, sin_ref, cw_ref, cb_ref, dtb_ref, *refs):
    prev = refs[:2 * n_prev]
    qr_ref, kr_ref, vb_ref, xc_ref, bcc_ref, dtp_ref, c_k, c_v = refs[2 * n_prev:]
    i = pl.program_id(0)
    cos = cos_ref[...]
    sin = sin_ref[...]
    k = k_ref[...]
    v = v_ref[...]
    qr_ref[...] = _rope(q_ref[...], cos, sin, GQA_HEAD_DIM // 4).astype(BF16)
    kr_ref[...] = _rope(k, cos, sin, GQA_HEAD_DIM // 4).astype(BF16)
    vb_ref[...] = v.astype(BF16)

    is_prompt = i < BATCH
    tile_in_seq = (i - BATCH) % (DEC_SEQ // POST_TM)
    seq_start = is_prompt | (tile_in_seq == 0)
    seq_end = is_prompt | (tile_in_seq == DEC_SEQ // POST_TM - 1)

    def conv_silu(x, lo_ref, hi_ref, w, b):
        rows = x.shape[0]
        r = lax.broadcasted_iota(jnp.int32, x.shape, 0)
        lo = jnp.where(seq_start, 0.0, lo_ref[7:8, :])
        hi = jnp.where(seq_end, 0.0, hi_ref[0:1, :])
        before = jnp.where(r == 0, lo, pltpu.roll(x, 1, 0))
        after = jnp.where(r == rows - 1, hi, pltpu.roll(x, rows - 1, 0))
        return _silu(before * w[0:1] + x * w[1:2] + after * w[2:3] + b)

    xc_ref[...] = conv_silu(xs_ref[...], xs_lo_ref, xs_hi_ref, cw_ref[:, :SSD_D_INNER], cb_ref[:, :SSD_D_INNER])
    bcc_ref[...] = conv_silu(bc_ref[...], bc_lo_ref, bc_hi_ref, cw_ref[:, SSD_D_INNER:], cb_ref[:, SSD_D_INNER:])
    t = dt_ref[...] + dtb_ref[...]
    dtp_ref[...] = jnp.maximum(t, 0.0) + jnp.log1p(jnp.exp(-jnp.abs(t)))

    @pl.when(is_prompt)
    def _():
        if n_prev:
            c_k[0, 0] = prev[0][0, 0]
            c_v[0, 0] = prev[1][0, 0]
        c_k[0, n_prev] = k.reshape(SEQ, GQA_KV_HEADS, GQA_HEAD_DIM)
        c_v[0, n_prev] = v.reshape(SEQ, GQA_KV_HEADS, GQA_HEAD_DIM)


def _od_post(proj, cos, sin, conv_w, conv_b, dt_bias, prev):
    tm = POST_TM
    halo = 8
    per = tm // halo
    n_halo = N_TOK // halo
    n_prev = 1 if prev else 0
    blk = lambda w, c: pl.BlockSpec((tm, w), lambda i: (i, c))
    lo = lambda w, c: pl.BlockSpec((halo, w), lambda i: (jnp.maximum(i * per - 1, 0), c))
    hi = lambda w, c: pl.BlockSpec((halo, w), lambda i: (jnp.minimum((i + 1) * per, n_halo - 1), c))
    full = lambda a: pl.BlockSpec(a.shape, lambda i: (0, 0))
    oblk = lambda w: pl.BlockSpec((tm, w), lambda i: (i, 0))
    cache_spec = lambda layers: pl.BlockSpec((1, layers, SEQ, GQA_KV_HEADS, GQA_HEAD_DIM),
                                             lambda i: (jnp.minimum(i, BATCH - 1), 0, 0, 0, 0))
    cache_shape = jax.ShapeDtypeStruct((BATCH, n_prev + 1, SEQ, GQA_KV_HEADS, GQA_HEAD_DIM), F32)
    return pl.pallas_call(
        functools.partial(_od_post_body, n_prev),
        grid=(N_TOK // tm,),
        in_specs=[blk(1024, 0), blk(256, 12), blk(256, 13),
                  blk(1024, 1), lo(1024, 1), hi(1024, 1), blk(512, 7), lo(512, 7), hi(512, 7), blk(LANES, 32),
                  blk(LANES, 0), blk(LANES, 0), full(conv_w), full(conv_b), full(dt_bias)]
        + [cache_spec(1)] * (2 * n_prev),
        out_specs=[oblk(1024), oblk(256), oblk(256), oblk(1024), oblk(512), oblk(LANES),
                   cache_spec(n_prev + 1), cache_spec(n_prev + 1)],
        out_shape=[
            jax.ShapeDtypeStruct((N_TOK, 1024), BF16),
            jax.ShapeDtypeStruct((N_TOK, 256), BF16),
            jax.ShapeDtypeStruct((N_TOK, 256), BF16),
            jax.ShapeDtypeStruct((N_TOK, 1024), F32),
            jax.ShapeDtypeStruct((N_TOK, 512), F32),
            jax.ShapeDtypeStruct((N_TOK, LANES), F32),
            cache_shape, cache_shape,
        ],
        compiler_params=_cparams(("arbitrary",)),
        name="odd_post",
    )(proj, proj, proj, proj, proj, proj, proj, proj, proj, proj, cos, sin, conv_w, conv_b, dt_bias, *prev)


def _sink_column(sink_ref, kh, rows):
    g = GQA_HEADS // GQA_KV_HEADS
    return jnp.concatenate(
        [jnp.broadcast_to(sink_ref[kh * g + j:kh * g + j + 1, 0:1], (rows, 1)) for j in range(g)], axis=0)


def _gqa_prompt_body(q_ref, k_ref, v_ref, sink_ref, o_ref):
    g = GQA_HEADS // GQA_KV_HEADS
    d = GQA_HEAD_DIM
    rows = q_ref.shape[0]
    scale = d ** -0.5
    outs = []
    for kh in range(GQA_KV_HEADS):
        qg = jnp.concatenate([q_ref[:, (kh * g + j) * d:(kh * g + j + 1) * d] for j in range(g)], axis=0)
        k = k_ref[:, kh * d:(kh + 1) * d]
        v = v_ref[:, kh * d:(kh + 1) * d]
        s = _nt(qg, k) * scale
        snk = _sink_column(sink_ref, kh, rows)
        m = jnp.maximum(jnp.max(s, axis=-1, keepdims=True), snk)
        e = jnp.exp(s - m)
        l = jnp.sum(e, axis=-1, keepdims=True) + jnp.exp(snk - m)
        o = jnp.dot(e.astype(BF16), v, preferred_element_type=F32) / l
        outs.extend(o[j * rows:(j + 1) * rows] for j in range(g))
    o_ref[...] = jnp.concatenate(outs, axis=1).astype(BF16)


def _gqa_prompt(q, k, v, sink):
    spec = lambda w: pl.BlockSpec((SEQ, w), lambda i: (i, 0))
    return pl.pallas_call(
        _gqa_prompt_body,
        grid=(BATCH,),
        in_specs=[spec(1024), spec(256), spec(256), pl.BlockSpec((8, LANES), lambda i: (0, 0))],
        out_specs=spec(1024),
        out_shape=jax.ShapeDtypeStruct((N_PROMPT, 1024), BF16),
        compiler_params=_cparams(("parallel",)),
        name="gqa_dense",
    )(q, k, v, sink)


def _gqa_window_body(q_ref, k_ref, v_ref, kc_ref, vc_ref, sink_ref, o_ref):
    g = GQA_HEADS // GQA_KV_HEADS
    d = GQA_HEAD_DIM
    blk = q_ref.shape[0]
    span = 3 * blk
    n = pl.program_id(1)
    start = pl.multiple_of(jnp.clip((n - 1) * blk, 0, DEC_SEQ - span), blk)
    scale = d ** -0.5
    qpos = n * blk + lax.broadcasted_iota(jnp.int32, (blk, span), 0)
    kpos = start + lax.broadcasted_iota(jnp.int32, (blk, span), 1)
    valid1 = jnp.abs(qpos - kpos) <= WINDOW
    valid = jnp.concatenate([valid1] * g, axis=0)
    outs = []
    for kh in range(GQA_KV_HEADS):
        qg = jnp.concatenate([q_ref[:, (kh * g + j) * d:(kh * g + j + 1) * d] for j in range(g)], axis=0)
        kl = k_ref[pl.ds(start, span), kh * d:(kh + 1) * d]
        vl = v_ref[pl.ds(start, span), kh * d:(kh + 1) * d]
        kc = kc_ref[:, kh * d:(kh + 1) * d]
        vc = vc_ref[:, kh * d:(kh + 1) * d]
        sl = jnp.where(valid, _nt(qg, kl) * scale, -jnp.inf)
        sc = _nt(qg, kc) * scale
        snk = _sink_column(sink_ref, kh, blk)
        m = jnp.maximum(jnp.maximum(jnp.max(sl, axis=-1, keepdims=True), jnp.max(sc, axis=-1, keepdims=True)), snk)
        el = jnp.exp(sl - m)
        ec = jnp.exp(sc - m)
        l = jnp.sum(el, axis=-1, keepdims=True) + jnp.sum(ec, axis=-1, keepdims=True) + jnp.exp(snk - m)
        o = (jnp.dot(el.astype(BF16), vl, preferred_element_type=F32)
             + jnp.dot(ec.astype(BF16), vc, preferred_element_type=F32)) / l
        outs.extend(o[j * blk:(j + 1) * blk] for j in range(g))
    o_ref[...] = jnp.concatenate(outs, axis=1).astype(BF16)


def _gqa_window(q, k, v, cache_k, cache_v, sink):
    blk = WINDOW
    nq = DEC_SEQ // blk
    q0 = N_PROMPT // blk
    s0 = N_PROMPT // DEC_SEQ
    seq_spec = pl.BlockSpec((DEC_SEQ, 256), lambda b, j: (s0 + b, 0))
    ctx_spec = pl.BlockSpec((PAST_LEN, 256), lambda b, j: (b, 0))
    return pl.pallas_call(
        _gqa_window_body,
        grid=(DEC_BATCH, nq),
        in_specs=[pl.BlockSpec((blk, 1024), lambda b, j: (q0 + b * nq + j, 0)), seq_spec, seq_spec, ctx_spec,
                  ctx_spec,
                  pl.BlockSpec((8, LANES), lambda b, j: (0, 0))],
        out_specs=pl.BlockSpec((blk, 1024), lambda b, j: (b * nq + j, 0)),
        out_shape=jax.ShapeDtypeStruct((N_SAMPLE, 1024), BF16),
        compiler_params=_cparams(("parallel", "parallel")),
        name="gqa_window",
    )(q, k, v, cache_k, cache_v, sink)


def _expand(x, sel):
    hi = x.astype(BF16)
    r1 = x - hi.astype(F32)
    mid = r1.astype(BF16)
    lo = (r1 - mid.astype(F32)).astype(BF16)
    return (jnp.dot(hi, sel, preferred_element_type=F32) + jnp.dot(mid, sel, preferred_element_type=F32)
            + jnp.dot(lo, sel, preferred_element_type=F32))


def _ssd_body(has_h0, xs_ref, bc_ref, dt_ref, alog_ref, e1_ref, e2_ref, *refs):
    if has_h0:
        h0f_ref, h0b_ref, y_ref, stf_ref, stb_ref = refs
    else:
        y_ref, stf_ref, stb_ref = refs
    d = pl.program_id(1)
    c = pl.program_id(2)
    q = SSD_CHUNK
    ppg = SSD_HEADS // SSD_GROUPS // 2
    hd = SSD_HEAD_DIM

    row = lax.broadcasted_iota(jnp.int32, (q, LANES), 0)
    a = -jnp.exp(alog_ref[...])
    dta = dt_ref[...] * a

    def scan(x, rev):
        k = 1
        while k < q:
            if rev:
                x = x + jnp.where(row < q - k, pltpu.roll(x, q - k, 0), 0.0)
            else:
                x = x + jnp.where(row >= k, pltpu.roll(x, k, 0), 0.0)
            k *= 2
        return x

    li = lax.broadcasted_iota(jnp.int32, (q, q), 0)
    si = lax.broadcasted_iota(jnp.int32, (q, q), 1)

    def run(rev):
        st_ref = stb_ref if rev else stf_ref

        @pl.when(c == 0)
        def _():
            if has_h0:
                st_ref[0] = (h0b_ref if rev else h0f_ref)[0, 0]
            else:
                st_ref[0] = jnp.zeros(st_ref.shape[1:], F32)

        cum = scan(dta, rev)
        cum_t = cum.T
        e_row = 0 if rev else q - 1
        mask = (li <= si) if rev else (li >= si)
        dt_x = _expand(dt_ref[...], e1_ref[0])
        cum_x = _expand(cum, e1_ref[0])
        cum_b = _expand(cum, e2_ref[0])
        edge_x = cum_x[e_row:e_row + 1, :]
        xdt = xs_ref[...] * dt_x
        xdt16 = xdt.astype(BF16)
        xw16 = (xdt * jnp.exp(edge_x - cum_x)).astype(BF16)
        ecum = jnp.exp(cum_x)
        lower = si < hd
        ys = []
        for g in range(SSD_GROUPS):
            bm16 = bc_ref[:, g * SSD_STATE:(g + 1) * SSD_STATE].astype(BF16)
            cm16 = bc_ref[:, (SSD_GROUPS + g) * SSD_STATE:(SSD_GROUPS + g + 1) * SSD_STATE].astype(BF16)
            cb = _nt(cm16, bm16)
            for pp in range(ppg):
                p = g * ppg + pp
                cols = slice(p * LANES, (p + 1) * LANES)
                y_pair, fac = [], []
                for u in range(2):
                    h = 2 * p + u
                    j = (SSD_HEADS if rev else 0) + h
                    seg = cum_b[:, h * LANES:(h + 1) * LANES] - cum_t[j:j + 1, :]
                    decay = jnp.exp(jnp.where(mask, seg, -jnp.inf))
                    y_pair.append(jnp.dot((cb * decay).astype(BF16), xdt16[:, cols], preferred_element_type=F32))
                    fac.append(jnp.broadcast_to(jnp.exp(cum_t[j:j + 1, e_row:e_row + 1]), (hd, SSD_STATE)))
                y_diag = jnp.where(lower, y_pair[0], y_pair[1])
                state = lax.dot_general(xw16[:, cols], bm16, (((0,), (0,)), ((), ())),
                                        preferred_element_type=F32)
                h_prev = st_ref[0, 2 * p:2 * p + 2].reshape(2 * hd, SSD_STATE)
                y_off = _nt(cm16, h_prev.astype(BF16)) * ecum[:, cols]
                ys.append(y_diag + y_off)
                h_new = h_prev * jnp.concatenate(fac, axis=0) + state
                st_ref[0, 2 * p:2 * p + 2] = h_new.reshape(2, hd, SSD_STATE)
        y_ref[0] = jnp.concatenate(ys, axis=1)

    @pl.when(d == 0)
    def _():
        run(False)

    @pl.when(d == 1)
    def _():
        run(True)


def _ssd(xs, bc, dtp, a_log, n_seq, seq, row_off, h0=None):
    b = n_seq
    t = n_seq * seq
    nc = seq // SSD_CHUNK
    c0 = row_off // SSD_CHUNK

    def rmap(i, d, c):
        return i * nc + jnp.where(d == 0, c, nc - 1 - c)

    blk = lambda w: pl.BlockSpec((SSD_CHUNK, w), lambda i, d, c: (c0 + rmap(i, d, c), 0))
    st_spec = pl.BlockSpec((1, SSD_HEADS, SSD_HEAD_DIM, SSD_STATE), lambda i, d, c: (i, 0, 0, 0))
    st_shape = jax.ShapeDtypeStruct((b, SSD_HEADS, SSD_HEAD_DIM, SSD_STATE), F32)
    def sel(w):
        row = jnp.arange(LANES)[None, :, None]
        head = SSD_HEADS * jnp.arange(2)[:, None, None] + (jnp.arange(SSD_HEADS * w) // w)[None, None, :]
        return (row == head).astype(BF16)

    sel_spec = lambda w: pl.BlockSpec((1, LANES, SSD_HEADS * w), lambda i, d, c: (d, 0, 0))
    in_specs = [blk(1024), blk(512), blk(LANES), pl.BlockSpec((1, LANES), lambda i, d, c: (0, 0)),
                sel_spec(SSD_HEAD_DIM), sel_spec(LANES)]
    args = [xs, bc, dtp, a_log, sel(SSD_HEAD_DIM), sel(LANES)]
    if h0 is not None:
        layer, h0_f, h0_b = h0
        h0_spec = pl.BlockSpec((1, 1, SSD_HEADS, SSD_HEAD_DIM, SSD_STATE), lambda i, d, c: (i, layer, 0, 0, 0))
        in_specs += [h0_spec, h0_spec]
        args += [h0_f, h0_b]
    return pl.pallas_call(
        functools.partial(_ssd_body, h0 is not None),
        grid=(b, 2, nc),
        in_specs=in_specs,
        out_specs=[pl.BlockSpec((1, SSD_CHUNK, 1024), lambda i, d, c: (d, rmap(i, d, c), 0)), st_spec, st_spec],
        out_shape=[jax.ShapeDtypeStruct((2, t, 1024), F32), st_shape, st_shape],
        compiler_params=_cparams(("parallel", "arbitrary", "arbitrary")),
        name="ssd_scan",
    )(*args)


def _ssd_gate_body(y_ref, xs_ref, z_ref, dsk_ref, g_ref, o_ref):
    y = (y_ref[0] + y_ref[1] + xs_ref[...] * dsk_ref[...]) * _silu(z_ref[...])
    w = SSD_D_INNER // SSD_GROUPS
    outs = [_rms(y[:, g * w:(g + 1) * w], g_ref[:, g * w:(g + 1) * w]) for g in range(SSD_GROUPS)]
    o_ref[...] = jnp.concatenate(outs, axis=1).astype(BF16)


def _ssd_gate(y2, xs, proj, d_skip, norm_g, row_off):
    tm = 512
    t = y2.shape[1]
    off = row_off // tm
    return pl.pallas_call(
        _ssd_gate_body,
        grid=(t // tm,),
        in_specs=[pl.BlockSpec((2, tm, 1024), lambda i: (0, i, 0)),
                  pl.BlockSpec((tm, 1024), lambda i: (off + i, 0)),
                  pl.BlockSpec((tm, 1024), lambda i: (off + i, 2)),
                  pl.BlockSpec((1, 1024), lambda i: (0, 0)),
                  pl.BlockSpec((1, 1024), lambda i: (0, 0))],
        out_specs=pl.BlockSpec((tm, 1024), lambda i: (i, 0)),
        out_shape=jax.ShapeDtypeStruct((t, 1024), BF16),
        compiler_params=_cparams(("parallel",)),
        name="ssd_gate_norm",
    )(y2, xs, proj, d_skip, norm_g)


def _even_layer(x, mod, layer_idx, i, p, ctx, tables, prev_caches):
    w_in, q_norm, kv_norm, wq_b, wkv_b, lam, subln, w_out = p
    ctx_ckv, ctx_kpe, ctx_dk, ctx_dv = ctx
    cos, sin = tables
    lam_init = 0.8 - 0.6 * math.exp(-0.3 * layer_idx)

    proj = _proj(x, mod[0], mod[1], _prep_even(w_in, i))
    qn, ckv, dq, dk, dv, kpe, *caches = _ev_post(proj, cos, sin, q_norm[None], kv_norm[None], prev_caches)

    wq = wq_b.reshape(MLA_Q_RANK, MLA_HEADS, MLA_NOPE + MLA_ROPE)
    wq_r = jnp.pad(wq, ((0, 0), (0, 0), (0, MLA_ROPE))).reshape(MLA_Q_RANK, -1).astype(BF16)
    wkv = wkv_b.astype(BF16)
    q = _mm(qn, wq_r, BF16, 512, N_TOK // 512)
    kv = _mm(ckv, wkv, BF16, 512, N_TOK // 512)
    n_even = ctx_ckv.shape[1]
    kv_ctx = _mm(ctx_ckv.reshape(-1, MLA_KV_RANK), wkv, BF16, PAST_LEN, DEC_BATCH,
                 row_block=lambda b: b * n_even + i)

    n_ctx = DEC_BATCH * PAST_LEN
    kpe_ctx = jnp.pad(ctx_kpe[:, i].reshape(n_ctx, MLA_ROPE), ((0, 0), (0, LANES - MLA_ROPE))).astype(BF16)
    dk_ctx = ctx_dk[:, i].reshape(n_ctx, -1).astype(BF16)
    dv_ctx = ctx_dv[:, i].reshape(n_ctx, -1).astype(BF16)

    args = (lam_init, q, cos, sin, kv, kpe, dq, dk, dv, lam, subln[None])
    o1p, o2p = _even_attn(*args, n_seq=BATCH, seq=SEQ, row_off=0)
    o1s, o2s = _even_attn(*args, n_seq=DEC_BATCH, seq=DEC_SEQ, row_off=N_PROMPT,
                          ctx=(kv_ctx, kpe_ctx, dk_ctx, dv_ctx))
    return (o1p, o1s, o2p, o2s), w_out.astype(BF16), caches


def _odd_layer(x, mod, i, p, ctx, tables, prev_caches):
    w_in, sink, conv_w, conv_b, dt_bias, a_log, d_skip, norm_g, w_out = p
    k_ctx, v_ctx, h0_f, h0_b = ctx
    cos, sin = tables

    proj = _proj(x, mod[0], mod[1], _prep_odd(w_in, i))
    cb = conv_b[None]
    dtb = jnp.pad(dt_bias.reshape(1, -1), ((0, 0), (0, LANES - 2 * SSD_HEADS)))
    alog = jnp.pad(a_log.reshape(1, -1), ((0, 0), (0, LANES - 2 * SSD_HEADS)))
    q, k, v, xs, bc, dtp, *caches = _od_post(proj, cos, sin, conv_w, cb, dtb, prev_caches)
    sink_b = jnp.broadcast_to(sink[:, None], (GQA_HEADS, LANES))

    o_att_p = _gqa_prompt(q, k, v, sink_b)
    n_ctx = DEC_BATCH * PAST_LEN
    o_att_s = _gqa_window(q, k, v, k_ctx[:, i].reshape(n_ctx, -1).astype(BF16),
                          v_ctx[:, i].reshape(n_ctx, -1).astype(BF16), sink_b)

    y_p, st_f, st_b = _ssd(xs, bc, dtp, alog, BATCH, SEQ, 0)
    y_s, _, _ = _ssd(xs, bc, dtp, alog, DEC_BATCH, DEC_SEQ, N_PROMPT, h0=(i, h0_f, h0_b))
    dsk = jnp.repeat(d_skip, SSD_HEAD_DIM)[None]
    y_n_p = _ssd_gate(y_p, xs, proj, dsk, norm_g[None], 0)
    y_n_s = _ssd_gate(y_s, xs, proj, dsk, norm_g[None], N_PROMPT)
    return (o_att_p, o_att_s, y_n_p, y_n_s), w_out.astype(BF16), caches, (st_f, st_b)


def kernel(x_prompt, x_sample, cache_mla_ckv, cache_mla_kpe, cache_diff_k, cache_diff_v, cache_gqa_k, cache_gqa_v, state_ssd_fwd, state_ssd_bwd, c, c_ctx, w_mod, b_mod, ln1_g, ln1_b, ln2_g, ln2_b, ev_w_in, mla_q_norm, mla_kv_norm, mla_wq_b, mla_wkv_b, diff_lambda, diff_subln, ev_w_out, od_w_in, gqa_sink, ssd_conv_w, ssd_conv_b, ssd_dt_bias, ssd_a_log, ssd_d, ssd_norm, od_w_out, moe_router_group, moe_router_expert, moe_w_gate, moe_w_up, moe_w_down):
    x = jnp.concatenate([x_prompt.reshape(N_PROMPT, D_MODEL), x_sample.reshape(N_SAMPLE, D_MODEL)], axis=0)
    cs = jnp.concatenate([c_ctx[None], c, jnp.zeros((8 - 1 - DEC_BATCH, D_MODEL), F32)], axis=0)
    mods = _adaln(cs, w_mod, b_mod)[:, :1 + DEC_BATCH].reshape(DEPTH, 1 + DEC_BATCH, 6, 1, D_MODEL)
    tables_64 = _rope_tables(64)
    tables_128 = _rope_tables(GQA_HEAD_DIM)

    ev_caches, od_caches, ssd_states = (), (), []
    for l in range(DEPTH):
        i = l // 2
        mod = [mods[l, :, k] for k in range(6)]
        if l % 2 == 0:
            p = (ev_w_in, mla_q_norm[i], mla_kv_norm[i], mla_wq_b[i], mla_wkv_b[i], diff_lambda[i],
                 diff_subln[i], ev_w_out[i])
            ctx = (cache_mla_ckv, cache_mla_kpe, cache_diff_k, cache_diff_v)
            mix, w_out, ev_caches = _even_layer(x, mod, l, i, p, ctx, tables_64, ev_caches)
        else:
            p = (od_w_in, gqa_sink[i], ssd_conv_w[i], ssd_conv_b[i], ssd_dt_bias[i], ssd_a_log[i], ssd_d[i],
                 ssd_norm[i], od_w_out[i])
            ctx = (cache_gqa_k, cache_gqa_v, state_ssd_fwd, state_ssd_bwd)
            mix, w_out, od_caches, states = _odd_layer(x, mod, i, p, ctx, tables_128, od_caches)
            ssd_states.append(states)
        w_router = jnp.concatenate([moe_router_group[l], moe_router_expert[l],
                                    jnp.zeros((D_MODEL, LANES - MOE_GROUPS - MOE_EXPERTS), F32)], axis=1)
        w_router_hi = w_router.astype(BF16)
        w_router = jnp.stack([w_router_hi, (w_router - w_router_hi.astype(F32)).astype(BF16)])
        x, h2, route = _out_proj(*mix, w_out, x, mod[2], ln1_g[l][None], ln1_b[l][None], mod[3], mod[4], w_router)
        z = _moe_experts(l, h2, _route_meta(route), moe_w_gate, moe_w_up, moe_w_down)
        x = _combine(x, z, route, mod[5], ln2_g[l][None], ln2_b[l][None], split=(l == DEPTH - 1))

    y_prompt = x[0].reshape(BATCH, SEQ, D_MODEL)
    y_sample = x[1].reshape(DEC_BATCH, DEC_SEQ, D_MODEL)
    new_ssd_fwd = jnp.stack([s[0] for s in ssd_states], axis=1)
    new_ssd_bwd = jnp.stack([s[1] for s in ssd_states], axis=1)
    return (y_prompt, y_sample, *ev_caches, *od_caches, new_ssd_fwd, new_ssd_bwd)
```

```python
import functools
import math

import jax
import jax.numpy as jnp
from jax import lax
from jax.experimental import pallas as pl
from jax.experimental.pallas import tpu as pltpu

F32 = jnp.float32
BF16 = jnp.bfloat16

D_MODEL = 2048
BATCH = 16
SEQ = 256
DEPTH = 4
DEC_BATCH = 2
DEC_SEQ = 1024
PAST_LEN = 512
GRID_W = 64
ROPE_THETA = 10000.0
DEEPNORM_ALPHA = (2.0 * DEPTH) ** 0.25
LN_EPS = 1e-5
RMS_EPS = 1e-6

MLA_HEADS = 8
MLA_Q_RANK = 512
MLA_KV_RANK = 512
MLA_NOPE = 128
MLA_ROPE = 64
MLA_V = 128
DIFF_HEADS = 8
DIFF_QK = 64
DIFF_V = 128
GQA_HEADS = 8
GQA_KV_HEADS = 2
GQA_HEAD_DIM = 128
WINDOW = 128
SSD_HEADS = 16
SSD_HEAD_DIM = 64
SSD_D_INNER = SSD_HEADS * SSD_HEAD_DIM
SSD_GROUPS = 2
SSD_STATE = 128
SSD_CHUNK = 128
MOE_GROUPS = 4
MOE_EPG = 4
MOE_EXPERTS = 16
MOE_HIDDEN = 512

N_PROMPT = BATCH * SEQ
N_SAMPLE = DEC_BATCH * DEC_SEQ
N_TOK = N_PROMPT + N_SAMPLE
PROJ_N = 4224
LANES = 128
VMEM_LIMIT = 56 * 1024 * 1024

PROJ_TM, PROJ_TN = 1024, 1408
OUT_TM = 512
ATT_TQ = 256
ATT_HEADS_PER_STEP = 2
ATT_HEADS_PER_STEP_CTX = 4
POST_TM = 256
MOE_TM = 256
MOE_SLOTS = 2 * N_TOK
MOE_TILES = MOE_SLOTS // MOE_TM + MOE_EXPERTS
MOE_ROWS = MOE_TILES * MOE_TM
WEIGHT_DMA_PRIORITY = 1
SLAB = (D_MODEL // LANES, LANES)


def _cparams(sem):
    return pltpu.CompilerParams(dimension_semantics=sem, vmem_limit_bytes=VMEM_LIMIT)


def _mod_group(row0):
    return jnp.maximum(row0 // DEC_SEQ - (N_PROMPT // DEC_SEQ - 1), 0)


def _silu(x):
    return x / (1.0 + jnp.exp(-x))


def _nt(a, b):
    return lax.dot_general(a, b, (((1,), (1,)), ((), ())), preferred_element_type=F32)


def _rms(x, g):
    return x * lax.rsqrt(jnp.mean(x * x, axis=-1, keepdims=True) + RMS_EPS) * g


def _layer_norm(y, g, b):
    mu = jnp.mean(y, axis=-1, keepdims=True)
    yc = y - mu
    return yc * lax.rsqrt(jnp.mean(yc * yc, axis=-1, keepdims=True) + LN_EPS) * g + b


def _adaln_body(c_ref, w_ref, b_ref, o_ref):
    a = _silu(c_ref[...])
    o_ref[0] = jnp.dot(a, w_ref[0], preferred_element_type=F32) + b_ref[0]


def _adaln(cs, w_mod, b_mod):
    tn = 1024
    n = w_mod.shape[-1]
    return pl.pallas_call(
        _adaln_body,
        grid=(DEPTH, n // tn),
        in_specs=[
            pl.BlockSpec((8, D_MODEL), lambda l, j: (0, 0)),
            pl.BlockSpec((1, D_MODEL, tn), lambda l, j: (l, 0, j)),
            pl.BlockSpec((1, 1, tn), lambda l, j: (l, 0, j)),
        ],
        out_specs=pl.BlockSpec((1, 8, tn), lambda l, j: (l, 0, j)),
        out_shape=jax.ShapeDtypeStruct((DEPTH, 8, n), F32),
        compiler_params=_cparams(("parallel", "parallel")),
        name="adaln",
    )(cs, w_mod, b_mod.reshape(DEPTH, 1, n))


def _proj_body(x_ref, sh_ref, sc_ref, w_ref, o_ref, h_scr):
    @pl.when(pl.program_id(1) == 0)
    def _():
        h = x_ref[...] * (1.0 + sc_ref[0]) + sh_ref[0]
        h_scr[...] = h.astype(BF16)

    o_ref[...] = jnp.dot(h_scr[...], w_ref[...], preferred_element_type=F32)


def _proj(x, shift, scale, w):
    tm, tn = PROJ_TM, PROJ_TN
    n = w.shape[1]
    mod_spec = pl.BlockSpec((1, 1, D_MODEL), lambda i, j: (_mod_group(i * tm), 0, 0))
    return pl.pallas_call(
        _proj_body,
        grid=(N_TOK // tm, n // tn),
        in_specs=[
            pl.BlockSpec((tm, D_MODEL), lambda i, j: (i, 0)),
            mod_spec,
            mod_spec,
            pl.BlockSpec((D_MODEL, tn), lambda i, j: (0, j)),
        ],
        out_specs=pl.BlockSpec((tm, tn), lambda i, j: (i, j)),
        out_shape=jax.ShapeDtypeStruct((N_TOK, n), F32),
        scratch_shapes=[pltpu.VMEM((tm, D_MODEL), BF16)],
        compiler_params=_cparams(("parallel", "arbitrary")),
        name="in_proj",
    )(x, shift, scale, w)


PROJ_BLOCKS = PROJ_N // LANES


def _prep_even_body(a_ref, b_ref, o_ref):
    c = pl.program_id(0)
    half = LANES // 2
    aligned = (MLA_Q_RANK + MLA_KV_RANK) // LANES

    @pl.when(c < aligned)
    def _():
        o_ref[...] = a_ref[0].astype(BF16)

    @pl.when((c >= aligned) & (c < PROJ_BLOCKS - 1))
    def _():
        o_ref[...] = jnp.concatenate([a_ref[0][:, half:], b_ref[0][:, :half]], axis=1).astype(BF16)

    @pl.when(c == PROJ_BLOCKS - 1)
    def _():
        k_pe = a_ref[0][:, :half]
        o_ref[...] = jnp.concatenate([k_pe, jnp.zeros_like(k_pe)], axis=1).astype(BF16)


def _prep_even(w_in, layer):
    aligned = (MLA_Q_RANK + MLA_KV_RANK) // LANES
    last = PROJ_BLOCKS - 1
    return pl.pallas_call(
        _prep_even_body,
        grid=(PROJ_BLOCKS,),
        in_specs=[pl.BlockSpec((1, D_MODEL, LANES), lambda c: (layer, 0, jnp.where(c == last, aligned, c))),
                  pl.BlockSpec((1, D_MODEL, LANES), lambda c: (layer, 0, jnp.minimum(c + 1, last)))],
        out_specs=pl.BlockSpec((D_MODEL, LANES), lambda c: (0, c)),
        out_shape=jax.ShapeDtypeStruct((D_MODEL, PROJ_N), BF16),
        compiler_params=_cparams(("parallel",)),
        name="prep_w_even",
    )(w_in, w_in)


def _prep_odd_body(perm_ref, a_ref, dt_ref, o_ref):
    c = pl.program_id(0)

    @pl.when(c < PROJ_BLOCKS - 1)
    def _():
        o_ref[...] = a_ref[0].astype(BF16)

    @pl.when(c == PROJ_BLOCKS - 1)
    def _():
        o_ref[...] = dt_ref[...]


def _prep_odd(w_in, layer):
    order = list(range(0, 8)) + list(range(20, 28)) + list(range(12, 20)) + [8, 9, 10, 11, 28, 29, 30, 31, 31]
    perm = jnp.asarray(order, jnp.int32)
    n_dt = 2 * SSD_HEADS
    w_dt = jnp.pad(w_in[layer, :, w_in.shape[2] - n_dt:], ((0, 0), (0, LANES - n_dt))).astype(BF16)
    return pl.pallas_call(
        _prep_odd_body,
        grid_spec=pltpu.PrefetchScalarGridSpec(
            num_scalar_prefetch=1,
            grid=(PROJ_BLOCKS,),
            in_specs=[pl.BlockSpec((1, D_MODEL, LANES), lambda c, perm_ref: (layer, 0, perm_ref[c])),
                      pl.BlockSpec((D_MODEL, LANES), lambda c, perm_ref: (0, 0))],
            out_specs=pl.BlockSpec((D_MODEL, LANES), lambda c, perm_ref: (0, c)),
        ),
        out_shape=jax.ShapeDtypeStruct((D_MODEL, PROJ_N), BF16),
        compiler_params=_cparams(("parallel",)),
        name="prep_w_odd",
    )(perm, w_in, w_dt)


def _mm_body(a_ref, b_ref, o_ref):
    o_ref[...] = jnp.dot(a_ref[...].astype(BF16), b_ref[...], preferred_element_type=F32).astype(o_ref.dtype)


def _mm(a, b, out_dtype, tm, n_tiles, row_block=lambda i: i):
    k = a.shape[1]
    n = b.shape[1]
    return pl.pallas_call(
        _mm_body,
        grid=(n_tiles,),
        in_specs=[pl.BlockSpec((tm, k), lambda i: (row_block(i), 0)), pl.BlockSpec((k, n), lambda i: (0, 0))],
        out_specs=pl.BlockSpec((tm, n), lambda i: (i, 0)),
        out_shape=jax.ShapeDtypeStruct((n_tiles * tm, n), out_dtype),
        compiler_params=_cparams(("parallel",)),
        name="mm_resident",
    )(a, b)


def _rope(x, cos, sin_signed, quarter):
    rows, w = x.shape
    reps = w // LANES
    if reps > 1:
        cos = jnp.concatenate([cos] * reps, axis=1)
        sin_signed = jnp.concatenate([sin_signed] * reps, axis=1)
    lane = lax.broadcasted_iota(jnp.int32, x.shape, 1)
    first = (lane % (2 * quarter)) < quarter
    up = pltpu.roll(x, w - quarter, 1)
    dn = pltpu.roll(x, quarter, 1)
    return x * cos + jnp.where(first, up, dn) * sin_signed


def _rope_tables(rdim):
    half = rdim // 2
    quarter = half // 2
    pos = jnp.arange(DEC_SEQ)
    row = (pos // GRID_W).astype(F32)
    col = (pos % GRID_W).astype(F32)
    inv = ROPE_THETA ** (-jnp.arange(quarter, dtype=F32) * 2.0 / half)
    lane = jnp.arange(LANES)
    r = lane % rdim
    use_col = (r // half) == 1
    j = r % quarter
    p = jnp.where(use_col[None, :], col[:, None], row[:, None])
    ang = p * inv[j][None, :]
    sign = jnp.where((r % half) < quarter, -1.0, 1.0).astype(F32)
    cos = jnp.cos(ang)
    sin = jnp.sin(ang) * sign[None, :]
    cos = jnp.concatenate([jnp.ones((N_PROMPT, LANES), F32), cos, cos], axis=0)
    sin = jnp.concatenate([jnp.zeros((N_PROMPT, LANES), F32), sin, sin], axis=0)
    return cos, sin


def _ev_post_body(n_prev, ql_ref, kvl_ref, dq_ref, dk_ref, dv_ref, kpe_ref, cos_ref, sin_ref, gq_ref, gkv_ref,
                  wq_ref, wkv_ref, *refs):
    prev = refs[:4 * n_prev]
    q_ref, kv_ref, dqr_ref, dkr_ref, dvb_ref, kper_ref, c_ckv, c_kpe, c_dk, c_dv = refs[4 * n_prev:]
    cos = cos_ref[...]
    sin = sin_ref[...]
    qn = _rms(ql_ref[...], gq_ref[...]).astype(BF16)
    q_ref[...] = jnp.dot(qn, wq_ref[...], preferred_element_type=F32).astype(BF16)
    ckv = _rms(kvl_ref[...], gkv_ref[...])
    kv_ref[...] = jnp.dot(ckv.astype(BF16), wkv_ref[...], preferred_element_type=F32).astype(BF16)
    dqr_ref[...] = _rope(dq_ref[...], cos, sin, DIFF_QK // 4).astype(BF16)
    dk = _rope(dk_ref[...], cos, sin, DIFF_QK // 4)
    dkr_ref[...] = dk.astype(BF16)
    dv = dv_ref[...]
    dvb_ref[...] = dv.astype(BF16)
    kpe = _rope(kpe_ref[...], cos, sin, MLA_ROPE // 4)
    kper_ref[...] = kpe.astype(BF16)

    @pl.when(pl.program_id(0) < BATCH)
    def _():
        if n_prev:
            c_ckv[0, 0] = prev[0][0, 0]
            c_kpe[0, 0] = prev[1][0, 0]
            c_dk[0, 0] = prev[2][0, 0]
            c_dv[0, 0] = prev[3][0, 0]
        c_ckv[0, n_prev] = ckv
        c_kpe[0, n_prev] = kpe[:, :MLA_ROPE]
        c_dk[0, n_prev] = dk.reshape(SEQ, DIFF_HEADS, 2 * DIFF_QK)
        c_dv[0, n_prev] = dv.reshape(SEQ, DIFF_HEADS, DIFF_V)


def _ev_post(proj, cos, sin, gq, gkv, wq, wkv, prev):
    tm = POST_TM
    n_prev = 1 if prev else 0
    nl = n_prev + 1
    row = lambda c: (lambda i: (i, c))
    seq4 = lambda i: (jnp.minimum(i, BATCH - 1), 0, 0, 0)
    seq5 = lambda i: (jnp.minimum(i, BATCH - 1), 0, 0, 0, 0)
    cache_shapes = [(MLA_KV_RANK,), (MLA_ROPE,), (DIFF_HEADS, 2 * DIFF_QK), (DIFF_HEADS, DIFF_V)]

    def cache_spec(layers, tail):
        return pl.BlockSpec((1, layers, SEQ) + tail, seq4 if len(tail) == 1 else seq5)

    return pl.pallas_call(
        functools.partial(_ev_post_body, n_prev),
        grid=(N_TOK // tm,),
        in_specs=[
            pl.BlockSpec((tm, 512), row(0)),
            pl.BlockSpec((tm, 512), row(1)),
            pl.BlockSpec((tm, 1024), row(1)),
            pl.BlockSpec((tm, 1024), row(2)),
            pl.BlockSpec((tm, 1024), row(3)),
            pl.BlockSpec((tm, LANES), row(32)),
            pl.BlockSpec((tm, LANES), row(0)),
            pl.BlockSpec((tm, LANES), row(0)),
            pl.BlockSpec((1, 512), lambda i: (0, 0)),
            pl.BlockSpec((1, 512), lambda i: (0, 0)),
            pl.BlockSpec(wq.shape, lambda i: (0, 0)),
            pl.BlockSpec(wkv.shape, lambda i: (0, 0)),
        ] + [cache_spec(1, t) for t in cache_shapes] * n_prev,
        out_specs=[
            pl.BlockSpec((tm, wq.shape[1]), row(0)),
            pl.BlockSpec((tm, wkv.shape[1]), row(0)),
            pl.BlockSpec((tm, 1024), row(0)),
            pl.BlockSpec((tm, 1024), row(0)),
            pl.BlockSpec((tm, 1024), row(0)),
            pl.BlockSpec((tm, LANES), row(0)),
        ] + [cache_spec(nl, t) for t in cache_shapes],
        out_shape=[
            jax.ShapeDtypeStruct((N_TOK, wq.shape[1]), BF16),
            jax.ShapeDtypeStruct((N_TOK, wkv.shape[1]), BF16),
            jax.ShapeDtypeStruct((N_TOK, 1024), BF16),
            jax.ShapeDtypeStruct((N_TOK, 1024), BF16),
            jax.ShapeDtypeStruct((N_TOK, 1024), BF16),
            jax.ShapeDtypeStruct((N_TOK, LANES), BF16),
        ] + [jax.ShapeDtypeStruct((BATCH, nl, SEQ) + t, F32) for t in cache_shapes],
        compiler_params=_cparams(("arbitrary",)),
        name="even_post",
    )(proj, proj, proj, proj, proj, proj, cos, sin, gq, gkv, wq, wkv, *prev)


def _softmax_av(scores, values):
    m = functools.reduce(jnp.maximum, [jnp.max(s, axis=-1, keepdims=True) for s in scores])
    es = [jnp.exp(s - m) for s in scores]
    l = sum(jnp.sum(e, axis=-1, keepdims=True) for e in es)
    o = sum(jnp.dot(e.astype(BF16), v, preferred_element_type=F32) for e, v in zip(es, values))
    return o / l


def _even_attn_body(lam_init, has_ctx, hps, q_ref, cos_ref, sin_ref, kv_ref, kpe_ref, dq_ref, dk_ref, dv_ref,
                    lam_ref, sub_ref, *refs):
    if has_ctx:
        kvc_ref, kpec_ref, dkc_ref, dvc_ref, omla_ref, odiff_ref = refs
    else:
        omla_ref, odiff_ref = refs
    cos = cos_ref[...]
    sin = sin_ref[...]
    lam = lam_ref[...]
    lam_full = (jnp.exp(jnp.sum(lam[0:1] * lam[1:2], axis=-1, keepdims=True))
                - jnp.exp(jnp.sum(lam[2:3] * lam[3:4], axis=-1, keepdims=True)) + lam_init)
    mla_scale = (MLA_NOPE + MLA_ROPE) ** -0.5
    lane = lax.broadcasted_iota(jnp.int32, (q_ref.shape[0], 2 * DIFF_QK), 1)
    for u in range(hps):
        qc = slice(u * 256, (u + 1) * 256)
        dc = slice(u * LANES, (u + 1) * LANES)
        q = q_ref[:, qc]
        qr = _rope(q[:, MLA_NOPE:].astype(F32), cos, sin, MLA_ROPE // 4).astype(BF16)
        qcat = jnp.concatenate([q[:, :MLA_NOPE], qr], axis=1)
        kv = kv_ref[:, qc]
        scores = [_nt(qcat, jnp.concatenate([kv[:, :MLA_NOPE], kpe_ref[...]], axis=1)) * mla_scale]
        values = [kv[:, MLA_NOPE:]]
        if has_ctx:
            kvc = kvc_ref[:, qc]
            scores.append(_nt(qcat, jnp.concatenate([kvc[:, :MLA_NOPE], kpec_ref[...]], axis=1)) * mla_scale)
            values.append(kvc[:, MLA_NOPE:])
        omla_ref[:, dc] = _softmax_av(scores, values).astype(BF16)

        dq = dq_ref[:, dc] * jnp.asarray(DIFF_QK ** -0.5, BF16)
        zero = jnp.zeros_like(dq)
        keys = [dk_ref[:, dc]]
        values = [dv_ref[:, dc]]
        if has_ctx:
            keys.append(dkc_ref[:, dc])
            values.append(dvc_ref[:, dc])
        a1 = _softmax_av([_nt(jnp.where(lane < DIFF_QK, dq, zero), k) for k in keys], values)
        a2 = _softmax_av([_nt(jnp.where(lane >= DIFF_QK, dq, zero), k) for k in keys], values)
        odiff_ref[:, dc] = (_rms(a1 - lam_full * a2, sub_ref[...]) * (1.0 - lam_init)).astype(BF16)


def _even_attn(lam_init, q, cos, sin, kv, kpe, dq, dk, dv, lam, subln, n_seq, seq, row_off, ctx=None):
    tq = ATT_TQ
    hps = ATT_HEADS_PER_STEP if ctx is not None else ATT_HEADS_PER_STEP_CTX
    nq = seq // tq
    q0 = row_off // tq
    s0 = row_off // seq
    qspec = lambda w: pl.BlockSpec((tq, hps * w), lambda b, j, h: (q0 + b * nq + j, h))
    kspec = lambda w: pl.BlockSpec((seq, hps * w), lambda b, j, h: (s0 + b, h))
    tspec = pl.BlockSpec((tq, LANES), lambda b, j, h: (q0 + b * nq + j, 0))
    ospec = pl.BlockSpec((tq, hps * LANES), lambda b, j, h: (b * nq + j, h))
    in_specs = [qspec(256), tspec, tspec, kspec(256),
                pl.BlockSpec((seq, LANES), lambda b, j, h: (s0 + b, 0)),
                qspec(LANES), kspec(LANES), kspec(LANES),
                pl.BlockSpec((4, DIFF_QK), lambda b, j, h: (0, 0)),
                pl.BlockSpec((1, DIFF_V), lambda b, j, h: (0, 0))]
    args = [q, cos, sin, kv, kpe, dq, dk, dv, lam, subln]
    if ctx is not None:
        cspec = lambda w: pl.BlockSpec((PAST_LEN, hps * w), lambda b, j, h: (b, h))
        in_specs += [cspec(256), pl.BlockSpec((PAST_LEN, LANES), lambda b, j, h: (b, 0)), cspec(LANES), cspec(LANES)]
        args += list(ctx)
    return pl.pallas_call(
        functools.partial(_even_attn_body, lam_init, ctx is not None, hps),
        grid=(n_seq, nq, MLA_HEADS // hps),
        in_specs=in_specs,
        out_specs=[ospec, ospec],
        out_shape=[jax.ShapeDtypeStruct((n_seq * seq, 1024), BF16)] * 2,
        compiler_params=_cparams(("parallel", "parallel", "parallel")),
        name="even_attn",
    )(*args)


def _route(logits):
    lane = lax.broadcasted_iota(jnp.int32, logits.shape, 1).astype(F32)
    neg = -jnp.inf
    none = float(LANES)
    is_g = lane < MOE_GROUPS
    lg = jnp.where(is_g, logits, neg)
    mg = jnp.max(lg, axis=-1, keepdims=True)
    g_val = 1.0 / jnp.sum(jnp.exp(lg - mg), axis=-1, keepdims=True)
    g_idx = jnp.min(jnp.where(is_g & (lg == mg), lane, none), axis=-1, keepdims=True)
    e_lo = MOE_GROUPS + g_idx * MOE_EPG
    is_e = (lane >= e_lo) & (lane < e_lo + MOE_EPG)
    le = jnp.where(is_e, logits, neg)
    p = jnp.exp(le - jnp.max(le, axis=-1, keepdims=True))
    p1 = jnp.max(p, axis=-1, keepdims=True)
    i1 = jnp.min(jnp.where(is_e & (p == p1), lane, none), axis=-1, keepdims=True)
    rest = is_e & (lane != i1)
    p2 = jnp.max(jnp.where(rest, p, neg), axis=-1, keepdims=True)
    i2 = jnp.min(jnp.where(rest & (p == p2), lane, none), axis=-1, keepdims=True)
    s = g_val / (p1 + p2)
    return jnp.where(lane == 0.0, i1 - MOE_GROUPS,
                     jnp.where(lane == 1.0, i2 - MOE_GROUPS,
                               jnp.where(lane == 2.0, p1 * s, jnp.where(lane == 3.0, p2 * s, 0.0))))


def _out_proj_body(a1p_ref, a1s_ref, a2p_ref, a2s_ref, w_ref, x_ref, gate_ref, g_ref, b_ref, sh_ref, sc_ref,
                   wr_ref, xo_ref, h_ref, comb_ref):
    is_prompt = pl.program_id(0) < N_PROMPT // OUT_TM
    a1 = jnp.where(is_prompt, a1p_ref[...], a1s_ref[...])
    a2 = jnp.where(is_prompt, a2p_ref[...], a2s_ref[...])
    k1 = a1.shape[1]
    acc = jnp.dot(a1, w_ref[:k1, :], preferred_element_type=F32)
    acc = acc + jnp.dot(a2, w_ref[k1:, :], preferred_element_type=F32)
    xn = _layer_norm(DEEPNORM_ALPHA * x_ref[...] + gate_ref[0] * acc, g_ref[...], b_ref[...])
    xo_ref[...] = xn
    h = xn * (1.0 + sc_ref[0]) + sh_ref[0]
    h_ref[...] = h.reshape(h_ref.shape).astype(BF16)
    h_hi = h.astype(BF16)
    h_lo = (h - h_hi.astype(F32)).astype(BF16)
    logits = (jnp.dot(h_hi, wr_ref[0], preferred_element_type=F32)
              + jnp.dot(h_lo, wr_ref[0], preferred_element_type=F32)
              + jnp.dot(h_hi, wr_ref[1], preferred_element_type=F32))
    comb_ref[...] = _route(logits)


def _out_proj(a1p, a1s, a2p, a2s, w, x, gate, ln_g, ln_b, shift, scale, w_router):
    tm = OUT_TM
    np_tiles = N_PROMPT // tm
    k1, k2 = a1p.shape[1], a2p.shape[1]
    mod_spec = pl.BlockSpec((1, 1, D_MODEL), lambda i: (_mod_group(i * tm), 0, 0))
    vec_spec = pl.BlockSpec((1, D_MODEL), lambda i: (0, 0))
    row_spec = pl.BlockSpec((tm, D_MODEL), lambda i: (i, 0))
    pspec = lambda k: pl.BlockSpec((tm, k), lambda i: (jnp.minimum(i, np_tiles - 1), 0))
    sspec = lambda k: pl.BlockSpec((tm, k), lambda i: (jnp.maximum(i - np_tiles, 0), 0))
    return pl.pallas_call(
        _out_proj_body,
        grid=(N_TOK // tm,),
        in_specs=[
            pspec(k1), sspec(k1), pspec(k2), sspec(k2),
            pl.BlockSpec((k1 + k2, D_MODEL), lambda i: (0, 0)),
            row_spec, mod_spec, vec_spec, vec_spec, mod_spec, mod_spec,
            pl.BlockSpec((2, D_MODEL, LANES), lambda i: (0, 0, 0)),
        ],
        out_specs=[row_spec, pl.BlockSpec((tm,) + SLAB, lambda i: (i, 0, 0)),
                   pl.BlockSpec((tm, LANES), lambda i: (i, 0))],
        out_shape=[
            jax.ShapeDtypeStruct((N_TOK, D_MODEL), F32),
            jax.ShapeDtypeStruct((N_TOK,) + SLAB, BF16),
            jax.ShapeDtypeStruct((N_TOK, LANES), F32),
        ],
        compiler_params=_cparams(("parallel",)),
        name="out_proj_ln_router",
    )(a1p, a1s, a2p, a2s, w, x, gate, ln_g, ln_b, shift, scale, w_router)


def _route_meta(route):
    e = route[:, :2].astype(jnp.int32).reshape(-1)
    onehot = (e[:, None] == jnp.arange(MOE_EXPERTS, dtype=jnp.int32)[None, :]).astype(jnp.int32)
    csum = jnp.cumsum(onehot, axis=0)
    rank = jnp.sum((csum - onehot) * onehot, axis=1)
    count = csum[-1]
    ntile = (count + MOE_TM - 1) // MOE_TM
    tile_end = jnp.cumsum(ntile)
    tile_off = tile_end - ntile
    pos = jnp.sum(onehot * tile_off[None, :], axis=1) * MOE_TM + rank
    n_used = tile_end[-1:]
    j = jnp.arange(MOE_TILES, dtype=jnp.int32)
    tile_expert = jnp.sum((tile_end[None, :] <= jnp.minimum(j, n_used - 1)[:, None]).astype(jnp.int32), axis=1)
    is_e = tile_expert[:, None] == jnp.arange(MOE_EXPERTS, dtype=jnp.int32)[None, :]
    left = jnp.sum(jnp.where(is_e, (count - (j[:, None] - tile_off[None, :]) * MOE_TM)[...], 0), axis=1)
    n_valid = jnp.where(j < n_used, jnp.clip(left, 0, MOE_TM), 0)
    ids = jnp.arange(MOE_EXPERTS, dtype=jnp.int32)
    later = (ids[None, :] > ids[:, None]) & (ntile[None, :] > 0)
    next_of = jnp.min(jnp.where(later, ids[None, :], MOE_EXPERTS), axis=1)
    next_of = jnp.where(next_of == MOE_EXPERTS, -1, next_of)
    next_expert = jnp.sum(jnp.where(is_e, next_of[None, :], 0), axis=1)
    ordinal = jnp.cumsum((ntile > 0).astype(jnp.int32)) - 1
    w_slot = jnp.sum(jnp.where(is_e, ordinal[None, :], 0), axis=1) % 2
    return pos, tile_expert, n_valid, n_used, next_expert, w_slot


def _moe_body(layer, pos_ref, te_ref, nv_ref, nu_ref, ne_ref, ws_ref, h_ref, wg_ref, wu_ref, wd_ref, z_ref,
              inv, xbuf, zbuf, wgbuf, wubuf, wdbuf, wg_s, wu_s, wd_s, gsem, ssem, wsem):
    j = pl.program_id(0)
    slot = j % 2
    n_used = nu_ref[0]
    group = 8

    def weight_copies(e, s):
        return [pltpu.make_async_copy(wg_ref.at[layer, e], wgbuf.at[s], wsem.at[s]),
                pltpu.make_async_copy(wu_ref.at[layer, e], wubuf.at[s], wsem.at[s]),
                pltpu.make_async_copy(wd_ref.at[layer, e], wdbuf.at[s], wsem.at[s])]

    def gather_row(tile, s, r):
        tok = lax.shift_right_logical(inv[tile * MOE_TM + r], 1)
        return pltpu.make_async_copy(h_ref.at[tok], xbuf.at[s, r], gsem.at[s])

    def scatter_row(tile, s, r):
        sl = inv[tile * MOE_TM + r]
        dst = (sl & 1) * N_TOK + lax.shift_right_logical(sl, 1)
        return pltpu.make_async_copy(zbuf.at[s, r], z_ref.at[dst], ssem.at[s])

    def for_valid_rows(tile, fn):
        n = nv_ref[tile]

        def block(g, c):
            for u in range(group):
                fn(g * group + u)
            return c

        def single(r, c):
            fn(r)
            return c

        full = lax.shift_right_logical(n, 3)
        lax.fori_loop(0, full, block, 0)
        lax.fori_loop(full * group, n, single, 0)

    @pl.when(j == 0)
    def _():
        def fill(g, c):
            for u in range(group):
                sl = g * group + u
                inv[pos_ref[sl]] = sl
            return c
        lax.fori_loop(0, MOE_SLOTS // group, fill, 0)
        xbuf[...] = jnp.zeros_like(xbuf)
        for_valid_rows(0, lambda r: gather_row(0, 0, r).start())
        for cp in weight_copies(te_ref[0], ws_ref[0]):
            cp.start(priority=WEIGHT_DMA_PRIORITY)

    changed = (j == 0) | (te_ref[j] != te_ref[jnp.maximum(j - 1, 0)])

    @pl.when(changed)
    def _():
        ws = ws_ref[j]
        for cp in weight_copies(te_ref[j], ws):
            cp.wait()
        wg_s[...] = wgbuf[ws].astype(BF16)
        wu_s[...] = wubuf[ws].astype(BF16)
        wd_s[...] = wdbuf[ws].astype(BF16)

        @pl.when(ne_ref[j] >= 0)
        def _():
            for cp in weight_copies(ne_ref[j], 1 - ws):
                cp.start(priority=WEIGHT_DMA_PRIORITY)

    @pl.when(j < n_used)
    def _():
        @pl.when(j + 1 < n_used)
        def _():
            for_valid_rows(j + 1, lambda r: gather_row(j + 1, 1 - slot, r).start())

        for_valid_rows(j, lambda r: gather_row(j, slot, r).wait())

        @pl.when(j >= 2)
        def _():
            for_valid_rows(j - 2, lambda r: scatter_row(j - 2, slot, r).wait())

        x = xbuf[slot].reshape(MOE_TM, D_MODEL)
        hg = jnp.dot(x, wg_s[...], preferred_element_type=F32)
        hu = jnp.dot(x, wu_s[...], preferred_element_type=F32)
        act = (_silu(hg) * hu).astype(BF16)
        z = jnp.dot(act, wd_s[...], preferred_element_type=F32)
        zbuf[slot] = z.reshape((MOE_TM,) + SLAB).astype(BF16)
        for_valid_rows(j, lambda r: scatter_row(j, slot, r).start())

        @pl.when(j == n_used - 1)
        def _():
            @pl.when(j >= 1)
            def _():
                for_valid_rows(j - 1, lambda r: scatter_row(j - 1, 1 - slot, r).wait())
            for_valid_rows(j, lambda r: scatter_row(j, slot, r).wait())


def _moe_experts(layer, h, meta, wg, wu, wd):
    any_spec = pl.BlockSpec(memory_space=pl.ANY)
    up_shape, down_shape = (D_MODEL, MOE_HIDDEN), (MOE_HIDDEN, D_MODEL)
    return pl.pallas_call(
        functools.partial(_moe_body, layer),
        grid_spec=pltpu.PrefetchScalarGridSpec(
            num_scalar_prefetch=len(meta),
            grid=(MOE_TILES,),
            in_specs=[any_spec, any_spec, any_spec, any_spec],
            out_specs=any_spec,
            scratch_shapes=[pltpu.SMEM((MOE_ROWS,), jnp.int32),
                            pltpu.VMEM((2, MOE_TM) + SLAB, BF16), pltpu.VMEM((2, MOE_TM) + SLAB, BF16),
                            pltpu.VMEM((2,) + up_shape, F32), pltpu.VMEM((2,) + up_shape, F32),
                            pltpu.VMEM((2,) + down_shape, F32),
                            pltpu.VMEM(up_shape, BF16), pltpu.VMEM(up_shape, BF16), pltpu.VMEM(down_shape, BF16),
                            pltpu.SemaphoreType.DMA((2,)), pltpu.SemaphoreType.DMA((2,)),
                            pltpu.SemaphoreType.DMA((2,))],
        ),
        out_shape=jax.ShapeDtypeStruct((2 * N_TOK,) + SLAB, BF16),
        compiler_params=_cparams(("arbitrary",)),
        name="moe_experts",
    )(*meta, h, wg, wu, wd)


def _combine_body(split, x_ref, r_ref, gate_ref, g_ref, b_ref, z1_ref, z2_ref, *o_refs):
    r = r_ref[...]
    z1 = z1_ref[...].reshape(x_ref.shape).astype(F32)
    z2 = z2_ref[...].reshape(x_ref.shape).astype(F32)
    y = r[:, 2:3] * z1 + r[:, 3:4] * z2
    out = _layer_norm(DEEPNORM_ALPHA * x_ref[...] + gate_ref[0] * y, g_ref[...], b_ref[...])
    if split:
        is_prompt = pl.program_id(0) < N_PROMPT // x_ref.shape[0]

        @pl.when(is_prompt)
        def _():
            o_refs[0][...] = out

        @pl.when(jnp.logical_not(is_prompt))
        def _():
            o_refs[1][...] = out
    else:
        o_refs[0][...] = out


def _combine(x, z, route, gate, ln_g, ln_b, split=False):
    tm = 512
    nt = N_TOK // tm
    npt = N_PROMPT // tm
    row_spec = pl.BlockSpec((tm, D_MODEL), lambda i: (i, 0))
    vec_spec = pl.BlockSpec((1, D_MODEL), lambda i: (0, 0))
    if split:
        out_specs = [pl.BlockSpec((tm, D_MODEL), lambda i: (jnp.minimum(i, npt - 1), 0)),
                     pl.BlockSpec((tm, D_MODEL), lambda i: (jnp.maximum(i - npt, 0), 0))]
        out_shape = [jax.ShapeDtypeStruct((N_PROMPT, D_MODEL), F32), jax.ShapeDtypeStruct((N_SAMPLE, D_MODEL), F32)]
    else:
        out_specs = row_spec
        out_shape = jax.ShapeDtypeStruct((N_TOK, D_MODEL), F32)
    return pl.pallas_call(
        functools.partial(_combine_body, split),
        grid=(nt,),
        in_specs=[row_spec, pl.BlockSpec((tm, LANES), lambda i: (i, 0)),
                  pl.BlockSpec((1, 1, D_MODEL), lambda i: (_mod_group(i * tm), 0, 0)),
                  vec_spec, vec_spec, pl.BlockSpec((tm,) + SLAB, lambda i: (i, 0, 0)),
                  pl.BlockSpec((tm,) + SLAB, lambda i: (nt + i, 0, 0))],
        out_specs=out_specs,
        out_shape=out_shape,
        compiler_params=_cparams(("arbitrary",)),
        name="moe_combine_ln",
    )(x, route, gate, ln_g, ln_b, z, z)


def _od_post_body(n_prev, q_ref, k_ref, v_ref, xs_ref, xs_lo_ref, xs_hi_ref, bc_ref, bc_lo_ref, bc_hi_ref, dt_ref,
                  cos_ref, sin_ref, cw_ref, cb_ref, dtb_ref, *refs):
    prev = refs[:2 * n_prev]
    qr_ref, kr_ref, vb_ref, xc_ref, bcc_ref, dtp_ref, c_k, c_v = refs[2 * n_prev:]
    i = pl.program_id(0)
    cos = cos_ref[...]
    sin = sin_ref[...]
    k = k_ref[...]
    v = v_ref[...]
    qr_ref[...] = _rope(q_ref[...], cos, sin, GQA_HEAD_DIM // 4).astype(BF16)
    kr_ref[...] = _rope(k, cos, sin, GQA_HEAD_DIM // 4).astype(BF16)
    vb_ref[...] = v.astype(BF16)

    is_prompt = i < BATCH
    tile_in_seq = (i - BATCH) % (DEC_SEQ // POST_TM)
    seq_start = is_prompt | (tile_in_seq == 0)
    seq_end = is_prompt | (tile_in_seq == DEC_SEQ // POST_TM - 1)

    def conv_silu(x, lo_ref, hi_ref, w, b):
        rows = x.shape[0]
        r = lax.broadcasted_iota(jnp.int32, x.shape, 0)
        lo = jnp.where(seq_start, 0.0, lo_ref[7:8, :])
        hi = jnp.where(seq_end, 0.0, hi_ref[0:1, :])
        before = jnp.where(r == 0, lo, pltpu.roll(x, 1, 0))
        after = jnp.where(r == rows - 1, hi, pltpu.roll(x, rows - 1, 0))
        return _silu(before * w[0:1] + x * w[1:2] + after * w[2:3] + b)

    xc_ref[...] = conv_silu(xs_ref[...], xs_lo_ref, xs_hi_ref, cw_ref[:, :SSD_D_INNER], cb_ref[:, :SSD_D_INNER])
    bcc_ref[...] = conv_silu(bc_ref[...], bc_lo_ref, bc_hi_ref, cw_ref[:, SSD_D_INNER:], cb_ref[:, SSD_D_INNER:])
    t = dt_ref[...] + dtb_ref[...]
    dtp_ref[...] = jnp.maximum(t, 0.0) + jnp.log1p(jnp.exp(-jnp.abs(t)))

    @pl.when(is_prompt)
    def _():
        if n_prev:
            c_k[0, 0] = prev[0][0, 0]
            c_v[0, 0] = prev[1][0, 0]
        c_k[0, n_prev] = k.reshape(SEQ, GQA_KV_HEADS, GQA_HEAD_DIM)
        c_v[0, n_prev] = v.reshape(SEQ, GQA_KV_HEADS, GQA_HEAD_DIM)


def _od_post(proj, cos, sin, conv_w, conv_b, dt_bias, prev):
    tm = POST_TM
    halo = 8
    per = tm // halo
    n_halo = N_TOK // halo
    n_prev = 1 if prev else 0
    blk = lambda w, c: pl.BlockSpec((tm, w), lambda i: (i, c))
    lo = lambda w, c: pl.BlockSpec((halo, w), lambda i: (jnp.maximum(i * per - 1, 0), c))
    hi = lambda w, c: pl.BlockSpec((halo, w), lambda i: (jnp.minimum((i + 1) * per, n_halo - 1), c))
    full = lambda a: pl.BlockSpec(a.shape, lambda i: (0, 0))
    oblk = lambda w: pl.BlockSpec((tm, w), lambda i: (i, 0))
    cache_spec = lambda layers: pl.BlockSpec((1, layers, SEQ, GQA_KV_HEADS, GQA_HEAD_DIM),
                                             lambda i: (jnp.minimum(i, BATCH - 1), 0, 0, 0, 0))
    cache_shape = jax.ShapeDtypeStruct((BATCH, n_prev + 1, SEQ, GQA_KV_HEADS, GQA_HEAD_DIM), F32)
    return pl.pallas_call(
        functools.partial(_od_post_body, n_prev),
        grid=(N_TOK // tm,),
        in_specs=[blk(1024, 0), blk(256, 12), blk(256, 13),
                  blk(1024, 1), lo(1024, 1), hi(1024, 1), blk(512, 7), lo(512, 7), hi(512, 7), blk(LANES, 32),
                  blk(LANES, 0), blk(LANES, 0), full(conv_w), full(conv_b), full(dt_bias)]
        + [cache_spec(1)] * (2 * n_prev),
        out_specs=[oblk(1024), oblk(256), oblk(256), oblk(1024), oblk(512), oblk(LANES),
                   cache_spec(n_prev + 1), cache_spec(n_prev + 1)],
        out_shape=[
            jax.ShapeDtypeStruct((N_TOK, 1024), BF16),
            jax.ShapeDtypeStruct((N_TOK, 256), BF16),
            jax.ShapeDtypeStruct((N_TOK, 256), BF16),
            jax.ShapeDtypeStruct((N_TOK, 1024), F32),
            jax.ShapeDtypeStruct((N_TOK, 512), F32),
            jax.ShapeDtypeStruct((N_TOK, LANES), F32),
            cache_shape, cache_shape,
        ],
        compiler_params=_cparams(("arbitrary",)),
        name="odd_post",
    )(proj, proj, proj, proj, proj, proj, proj, proj, proj, proj, cos, sin, conv_w, conv_b, dt_bias, *prev)


def _sink_column(sink_ref, kh, rows):
    g = GQA_HEADS // GQA_KV_HEADS
    return jnp.concatenate(
        [jnp.broadcast_to(sink_ref[kh * g + j:kh * g + j + 1, 0:1], (rows, 1)) for j in range(g)], axis=0)


def _gqa_prompt_body(q_ref, k_ref, v_ref, sink_ref, o_ref):
    g = GQA_HEADS // GQA_KV_HEADS
    d = GQA_HEAD_DIM
    rows = q_ref.shape[0]
    scale = d ** -0.5
    outs = []
    for kh in range(GQA_KV_HEADS):
        qg = jnp.concatenate([q_ref[:, (kh * g + j) * d:(kh * g + j + 1) * d] for j in range(g)], axis=0)
        k = k_ref[:, kh * d:(kh + 1) * d]
        v = v_ref[:, kh * d:(kh + 1) * d]
        s = _nt(qg, k) * scale
        snk = _sink_column(sink_ref, kh, rows)
        m = jnp.maximum(jnp.max(s, axis=-1, keepdims=True), snk)
        e = jnp.exp(s - m)
        l = jnp.sum(e, axis=-1, keepdims=True) + jnp.exp(snk - m)
        o = jnp.dot(e.astype(BF16), v, preferred_element_type=F32) / l
        outs.extend(o[j * rows:(j + 1) * rows] for j in range(g))
    o_ref[...] = jnp.concatenate(outs, axis=1).astype(BF16)


def _gqa_prompt(q, k, v, sink):
    spec = lambda w: pl.BlockSpec((SEQ, w), lambda i: (i, 0))
    return pl.pallas_call(
        _gqa_prompt_body,
        grid=(BATCH,),
        in_specs=[spec(1024), spec(256), spec(256), pl.BlockSpec((8, LANES), lambda i: (0, 0))],
        out_specs=spec(1024),
        out_shape=jax.ShapeDtypeStruct((N_PROMPT, 1024), BF16),
        compiler_params=_cparams(("parallel",)),
        name="gqa_dense",
    )(q, k, v, sink)


def _gqa_window_body(q_ref, k_ref, v_ref, kc_ref, vc_ref, sink_ref, o_ref):
    g = GQA_HEADS // GQA_KV_HEADS
    d = GQA_HEAD_DIM
    blk = q_ref.shape[0]
    span = 3 * blk
    n = pl.program_id(1)
    start = pl.multiple_of(jnp.clip((n - 1) * blk, 0, DEC_SEQ - span), blk)
    scale = d ** -0.5
    qpos = n * blk + lax.broadcasted_iota(jnp.int32, (blk, span), 0)
    kpos = start + lax.broadcasted_iota(jnp.int32, (blk, span), 1)
    valid1 = jnp.abs(qpos - kpos) <= WINDOW
    valid = jnp.concatenate([valid1] * g, axis=0)
    outs = []
    for kh in range(GQA_KV_HEADS):
        qg = jnp.concatenate([q_ref[:, (kh * g + j) * d:(kh * g + j + 1) * d] for j in range(g)], axis=0)
        kl = k_ref[pl.ds(start, span), kh * d:(kh + 1) * d]
        vl = v_ref[pl.ds(start, span), kh * d:(kh + 1) * d]
        kc = kc_ref[:, kh * d:(kh + 1) * d]
        vc = vc_ref[:, kh * d:(kh + 1) * d]
        sl = jnp.where(valid, _nt(qg, kl) * scale, -jnp.inf)
        sc = _nt(qg, kc) * scale
        snk = _sink_column(sink_ref, kh, blk)
        m = jnp.maximum(jnp.maximum(jnp.max(sl, axis=-1, keepdims=True), jnp.max(sc, axis=-1, keepdims=True)), snk)
        el = jnp.exp(sl - m)
        ec = jnp.exp(sc - m)
        l = jnp.sum(el, axis=-1, keepdims=True) + jnp.sum(ec, axis=-1, keepdims=True) + jnp.exp(snk - m)
        o = (jnp.dot(el.astype(BF16), vl, preferred_element_type=F32)
             + jnp.dot(ec.astype(BF16), vc, preferred_element_type=F32)) / l
        outs.extend(o[j * blk:(j + 1) * blk] for j in range(g))
    o_ref[...] = jnp.concatenate(outs, axis=1).astype(BF16)


def _gqa_window(q, k, v, cache_k, cache_v, sink):
    blk = WINDOW
    nq = DEC_SEQ // blk
    q0 = N_PROMPT // blk
    s0 = N_PROMPT // DEC_SEQ
    seq_spec = pl.BlockSpec((DEC_SEQ, 256), lambda b, j: (s0 + b, 0))
    ctx_spec = pl.BlockSpec((PAST_LEN, 256), lambda b, j: (b, 0))
    return pl.pallas_call(
        _gqa_window_body,
        grid=(DEC_BATCH, nq),
        in_specs=[pl.BlockSpec((blk, 1024), lambda b, j: (q0 + b * nq + j, 0)), seq_spec, seq_spec, ctx_spec,
                  ctx_spec,
                  pl.BlockSpec((8, LANES), lambda b, j: (0, 0))],
        out_specs=pl.BlockSpec((blk, 1024), lambda b, j: (b * nq + j, 0)),
        out_shape=jax.ShapeDtypeStruct((N_SAMPLE, 1024), BF16),
        compiler_params=_cparams(("parallel", "parallel")),
        name="gqa_window",
    )(q, k, v, cache_k, cache_v, sink)


def _expand(x, sel):
    hi = x.astype(BF16)
    r1 = x - hi.astype(F32)
    mid = r1.astype(BF16)
    lo = (r1 - mid.astype(F32)).astype(BF16)
    return (jnp.dot(hi, sel, preferred_element_type=F32) + jnp.dot(mid, sel, preferred_element_type=F32)
            + jnp.dot(lo, sel, preferred_element_type=F32))


def _ssd_body(has_h0, xs_ref, bc_ref, dt_ref, alog_ref, e1_ref, e2_ref, *refs):
    if has_h0:
        h0f_ref, h0b_ref, y_ref, stf_ref, stb_ref = refs
    else:
        y_ref, stf_ref, stb_ref = refs
    d = pl.program_id(1)
    c = pl.program_id(2)
    q = SSD_CHUNK
    ppg = SSD_HEADS // SSD_GROUPS // 2
    hd = SSD_HEAD_DIM

    row = lax.broadcasted_iota(jnp.int32, (q, LANES), 0)
    a = -jnp.exp(alog_ref[...])
    dta = dt_ref[...] * a

    def scan(x, rev):
        k = 1
        while k < q:
            if rev:
                x = x + jnp.where(row < q - k, pltpu.roll(x, q - k, 0), 0.0)
            else:
                x = x + jnp.where(row >= k, pltpu.roll(x, k, 0), 0.0)
            k *= 2
        return x

    li = lax.broadcasted_iota(jnp.int32, (q, q), 0)
    si = lax.broadcasted_iota(jnp.int32, (q, q), 1)

    def run(rev):
        st_ref = stb_ref if rev else stf_ref

        @pl.when(c == 0)
        def _():
            if has_h0:
                st_ref[0] = (h0b_ref if rev else h0f_ref)[0, 0]
            else:
                st_ref[0] = jnp.zeros(st_ref.shape[1:], F32)

        cum = scan(dta, rev)
        cum_t = cum.T
        e_row = 0 if rev else q - 1
        mask = (li <= si) if rev else (li >= si)
        dt_x = _expand(dt_ref[...], e1_ref[0])
        cum_x = _expand(cum, e1_ref[0])
        cum_b = _expand(cum, e2_ref[0])
        edge_x = cum_x[e_row:e_row + 1, :]
        xdt = xs_ref[...] * dt_x
        xdt16 = xdt.astype(BF16)
        xw16 = (xdt * jnp.exp(edge_x - cum_x)).astype(BF16)
        ecum = jnp.exp(cum_x)
        lower = si < hd
        ys = []
        for g in range(SSD_GROUPS):
            bm16 = bc_ref[:, g * SSD_STATE:(g + 1) * SSD_STATE].astype(BF16)
            cm16 = bc_ref[:, (SSD_GROUPS + g) * SSD_STATE:(SSD_GROUPS + g + 1) * SSD_STATE].astype(BF16)
            cb = _nt(cm16, bm16)
            for pp in range(ppg):
                p = g * ppg + pp
                cols = slice(p * LANES, (p + 1) * LANES)
                y_pair, fac = [], []
                for u in range(2):
                    h = 2 * p + u
                    j = (SSD_HEADS if rev else 0) + h
                    seg = cum_b[:, h * LANES:(h + 1) * LANES] - cum_t[j:j + 1, :]
                    decay = jnp.exp(jnp.where(mask, seg, -jnp.inf))
                    y_pair.append(jnp.dot((cb * decay).astype(BF16), xdt16[:, cols], preferred_element_type=F32))
                    fac.append(jnp.broadcast_to(jnp.exp(cum_t[j:j + 1, e_row:e_row + 1]), (hd, SSD_STATE)))
                y_diag = jnp.where(lower, y_pair[0], y_pair[1])
                state = lax.dot_general(xw16[:, cols], bm16, (((0,), (0,)), ((), ())),
                                        preferred_element_type=F32)
                h_prev = st_ref[0, 2 * p:2 * p + 2].reshape(2 * hd, SSD_STATE)
                y_off = _nt(cm16, h_prev.astype(BF16)) * ecum[:, cols]
                ys.append(y_diag + y_off)
                h_new = h_prev * jnp.concatenate(fac, axis=0) + state
                st_ref[0, 2 * p:2 * p + 2] = h_new.reshape(2, hd, SSD_STATE)
        y_ref[0] = jnp.concatenate(ys, axis=1)

    @pl.when(d == 0)
    def _():
        run(False)

    @pl.when(d == 1)
    def _():
        run(True)


def _ssd(xs, bc, dtp, a_log, n_seq, seq, row_off, h0=None):
    b = n_seq
    t = n_seq * seq
    nc = seq // SSD_CHUNK
    c0 = row_off // SSD_CHUNK

    def rmap(i, d, c):
        return i * nc + jnp.where(d == 0, c, nc - 1 - c)

    blk = lambda w: pl.BlockSpec((SSD_CHUNK, w), lambda i, d, c: (c0 + rmap(i, d, c), 0))
    st_spec = pl.BlockSpec((1, SSD_HEADS, SSD_HEAD_DIM, SSD_STATE), lambda i, d, c: (i, 0, 0, 0))
    st_shape = jax.ShapeDtypeStruct((b, SSD_HEADS, SSD_HEAD_DIM, SSD_STATE), F32)
    def sel(w):
        row = jnp.arange(LANES)[None, :, None]
        head = SSD_HEADS * jnp.arange(2)[:, None, None] + (jnp.arange(SSD_HEADS * w) // w)[None, None, :]
        return (row == head).astype(BF16)

    sel_spec = lambda w: pl.BlockSpec((1, LANES, SSD_HEADS * w), lambda i, d, c: (d, 0, 0))
    in_specs = [blk(1024), blk(512), blk(LANES), pl.BlockSpec((1, LANES), lambda i, d, c: (0, 0)),
                sel_spec(SSD_HEAD_DIM), sel_spec(LANES)]
    args = [xs, bc, dtp, a_log, sel(SSD_HEAD_DIM), sel(LANES)]
    if h0 is not None:
        layer, h0_f, h0_b = h0
        h0_spec = pl.BlockSpec((1, 1, SSD_HEADS, SSD_HEAD_DIM, SSD_STATE), lambda i, d, c: (i, layer, 0, 0, 0))
        in_specs += [h0_spec, h0_spec]
        args += [h0_f, h0_b]
    return pl.pallas_call(
        functools.partial(_ssd_body, h0 is not None),
        grid=(b, 2, nc),
        in_specs=in_specs,
        out_specs=[pl.BlockSpec((1, SSD_CHUNK, 1024), lambda i, d, c: (d, rmap(i, d, c), 0)), st_spec, st_spec],
        out_shape=[jax.ShapeDtypeStruct((2, t, 1024), F32), st_shape, st_shape],
        compiler_params=_cparams(("parallel", "arbitrary", "arbitrary")),
        name="ssd_scan",
    )(*args)


def _ssd_gate_body(y_ref, xs_ref, z_ref, dsk_ref, g_ref, o_ref):
    y = (y_ref[0] + y_ref[1] + xs_ref[...] * dsk_ref[...]) * _silu(z_ref[...])
    w = SSD_D_INNER // SSD_GROUPS
    outs = [_rms(y[:, g * w:(g + 1) * w], g_ref[:, g * w:(g + 1) * w]) for g in range(SSD_GROUPS)]
    o_ref[...] = jnp.concatenate(outs, axis=1).astype(BF16)


def _ssd_gate(y2, xs, proj, d_skip, norm_g, row_off):
    tm = 512
    t = y2.shape[1]
    off = row_off // tm
    return pl.pallas_call(
        _ssd_gate_body,
        grid=(t // tm,),
        in_specs=[pl.BlockSpec((2, tm, 1024), lambda i: (0, i, 0)),
                  pl.BlockSpec((tm, 1024), lambda i: (off + i, 0)),
                  pl.BlockSpec((tm, 1024), lambda i: (off + i, 2)),
                  pl.BlockSpec((1, 1024), lambda i: (0, 0)),
                  pl.BlockSpec((1, 1024), lambda i: (0, 0))],
        out_specs=pl.BlockSpec((tm, 1024), lambda i: (i, 0)),
        out_shape=jax.ShapeDtypeStruct((t, 1024), BF16),
        compiler_params=_cparams(("parallel",)),
        name="ssd_gate_norm",
    )(y2, xs, proj, d_skip, norm_g)


def _even_layer(x, mod, layer_idx, i, p, ctx, tables, prev_caches):
    w_in, q_norm, kv_norm, wq_b, wkv_b, lam, subln, w_out = p
    ctx_ckv, ctx_kpe, ctx_dk, ctx_dv = ctx
    cos, sin = tables
    lam_init = 0.8 - 0.6 * math.exp(-0.3 * layer_idx)

    proj = _proj(x, mod[0], mod[1], _prep_even(w_in, i))
    wq = wq_b.reshape(MLA_Q_RANK, MLA_HEADS, MLA_NOPE + MLA_ROPE)
    wq_r = jnp.pad(wq, ((0, 0), (0, 0), (0, MLA_ROPE))).reshape(MLA_Q_RANK, -1).astype(BF16)
    wkv = wkv_b.astype(BF16)
    q, kv, dq, dk, dv, kpe, *caches = _ev_post(proj, cos, sin, q_norm[None], kv_norm[None], wq_r, wkv, prev_caches)
    n_even = ctx_ckv.shape[1]
    kv_ctx = _mm(ctx_ckv.reshape(-1, MLA_KV_RANK), wkv, BF16, PAST_LEN, DEC_BATCH,
                 row_block=lambda b: b * n_even + i)

    n_ctx = DEC_BATCH * PAST_LEN
    kpe_ctx = jnp.pad(ctx_kpe[:, i].reshape(n_ctx, MLA_ROPE), ((0, 0), (0, LANES - MLA_ROPE))).astype(BF16)
    dk_ctx = ctx_dk[:, i].reshape(n_ctx, -1).astype(BF16)
    dv_ctx = ctx_dv[:, i].reshape(n_ctx, -1).astype(BF16)

    args = (lam_init, q, cos, sin, kv, kpe, dq, dk, dv, lam, subln[None])
    o1p, o2p = _even_attn(*args, n_seq=BATCH, seq=SEQ, row_off=0)
    o1s, o2s = _even_attn(*args, n_seq=DEC_BATCH, seq=DEC_SEQ, row_off=N_PROMPT,
                          ctx=(kv_ctx, kpe_ctx, dk_ctx, dv_ctx))
    return (o1p, o1s, o2p, o2s), w_out.astype(BF16), caches


def _odd_layer(x, mod, i, p, ctx, tables, prev_caches):
    w_in, sink, conv_w, conv_b, dt_bias, a_log, d_skip, norm_g, w_out = p
    k_ctx, v_ctx, h0_f, h0_b = ctx
    cos, sin = tables

    proj = _proj(x, mod[0], mod[1], _prep_odd(w_in, i))
    cb = conv_b[None]
    dtb = jnp.pad(dt_bias.reshape(1, -1), ((0, 0), (0, LANES - 2 * SSD_HEADS)))
    alog = jnp.pad(a_log.reshape(1, -1), ((0, 0), (0, LANES - 2 * SSD_HEADS)))
    q, k, v, xs, bc, dtp, *caches = _od_post(proj, cos, sin, conv_w, cb, dtb, prev_caches)
    sink_b = jnp.broadcast_to(sink[:, None], (GQA_HEADS, LANES))

    o_att_p = _gqa_prompt(q, k, v, sink_b)
    n_ctx = DEC_BATCH * PAST_LEN
    o_att_s = _gqa_window(q, k, v, k_ctx[:, i].reshape(n_ctx, -1).astype(BF16),
                          v_ctx[:, i].reshape(n_ctx, -1).astype(BF16), sink_b)

    y_p, st_f, st_b = _ssd(xs, bc, dtp, alog, BATCH, SEQ, 0)
    y_s, _, _ = _ssd(xs, bc, dtp, alog, DEC_BATCH, DEC_SEQ, N_PROMPT, h0=(i, h0_f, h0_b))
    dsk = jnp.repeat(d_skip, SSD_HEAD_DIM)[None]
    y_n_p = _ssd_gate(y_p, xs, proj, dsk, norm_g[None], 0)
    y_n_s = _ssd_gate(y_s, xs, proj, dsk, norm_g[None], N_PROMPT)
    return (o_att_p, o_att_s, y_n_p, y_n_s), w_out.astype(BF16), caches, (st_f, st_b)


def kernel(x_prompt, x_sample, cache_mla_ckv, cache_mla_kpe, cache_diff_k, cache_diff_v, cache_gqa_k, cache_gqa_v, state_ssd_fwd, state_ssd_bwd, c, c_ctx, w_mod, b_mod, ln1_g, ln1_b, ln2_g, ln2_b, ev_w_in, mla_q_norm, mla_kv_norm, mla_wq_b, mla_wkv_b, diff_lambda, diff_subln, ev_w_out, od_w_in, gqa_sink, ssd_conv_w, ssd_conv_b, ssd_dt_bias, ssd_a_log, ssd_d, ssd_norm, od_w_out, moe_router_group, moe_router_expert, moe_w_gate, moe_w_up, moe_w_down):
    x = jnp.concatenate([x_prompt.reshape(N_PROMPT, D_MODEL), x_sample.reshape(N_SAMPLE, D_MODEL)], axis=0)
    cs = jnp.concatenate([c_ctx[None], c, jnp.zeros((8 - 1 - DEC_BATCH, D_MODEL), F32)], axis=0)
    mods = _adaln(cs, w_mod, b_mod)[:, :1 + DEC_BATCH].reshape(DEPTH, 1 + DEC_BATCH, 6, 1, D_MODEL)
    tables_64 = _rope_tables(64)
    tables_128 = _rope_tables(GQA_HEAD_DIM)

    ev_caches, od_caches, ssd_states = (), (), []
    for l in range(DEPTH):
        i = l // 2
        mod = [mods[l, :, k] for k in range(6)]
        if l % 2 == 0:
            p = (ev_w_in, mla_q_norm[i], mla_kv_norm[i], mla_wq_b[i], mla_wkv_b[i], diff_lambda[i],
                 diff_subln[i], ev_w_out[i])
            ctx = (cache_mla_ckv, cache_mla_kpe, cache_diff_k, cache_diff_v)
            mix, w_out, ev_caches = _even_layer(x, mod, l, i, p, ctx, tables_64, ev_caches)
        else:
            p = (od_w_in, gqa_sink[i], ssd_conv_w[i], ssd_conv_b[i], ssd_dt_bias[i], ssd_a_log[i], ssd_d[i],
                 ssd_norm[i], od_w_out[i])
            ctx = (cache_gqa_k, cache_gqa_v, state_ssd_fwd, state_ssd_bwd)
            mix, w_out, od_caches, states = _odd_layer(x, mod, i, p, ctx, tables_128, od_caches)
            ssd_states.append(states)
        w_router = jnp.concatenate([moe_router_group[l], moe_router_expert[l],
                                    jnp.zeros((D_MODEL, LANES - MOE_GROUPS - MOE_EXPERTS), F32)], axis=1)
        w_router_hi = w_router.astype(BF16)
        w_router = jnp.stack([w_router_hi, (w_router - w_router_hi.astype(F32)).astype(BF16)])
        x, h2, route = _out_proj(*mix, w_out, x, mod[2], ln1_g[l][None], ln1_b[l][None], mod[3], mod[4], w_router)
        z = _moe_experts(l, h2, _route_meta(route), moe_w_gate, moe_w_up, moe_w_down)
        x = _combine(x, z, route, mod[5], ln2_g[l][None], ln2_b[l][None], split=(l == DEPTH - 1))

    y_prompt = x[0].reshape(BATCH, SEQ, D_MODEL)
    y_sample = x[1].reshape(DEC_BATCH, DEC_SEQ, D_MODEL)
    new_ssd_fwd = jnp.stack([s[0] for s in ssd_states], axis=1)
    new_ssd_bwd = jnp.stack([s[1] for s in ssd_states], axis=1)
    return (y_prompt, y_sample, *ev_caches, *od_caches, new_ssd_fwd, new_ssd_bwd)
```

```python
import functools
import math

import jax
import jax.numpy as jnp
from jax import lax
from jax.experimental import pallas as pl
from jax.experimental.pallas import tpu as pltpu

F32 = jnp.float32
BF16 = jnp.bfloat16

D_MODEL = 2048
BATCH = 16
SEQ = 256
DEPTH = 4
DEC_BATCH = 2
DEC_SEQ = 1024
PAST_LEN = 512
GRID_W = 64
ROPE_THETA = 10000.0
DEEPNORM_ALPHA = (2.0 * DEPTH) ** 0.25
LN_EPS = 1e-5
RMS_EPS = 1e-6

MLA_HEADS = 8
MLA_Q_RANK = 512
MLA_KV_RANK = 512
MLA_NOPE = 128
MLA_ROPE = 64
MLA_V = 128
DIFF_HEADS = 8
DIFF_QK = 64
DIFF_V = 128
GQA_HEADS = 8
GQA_KV_HEADS = 2
GQA_HEAD_DIM = 128
WINDOW = 128
SSD_HEADS = 16
SSD_HEAD_DIM = 64
SSD_D_INNER = SSD_HEADS * SSD_HEAD_DIM
SSD_GROUPS = 2
SSD_STATE = 128
SSD_CHUNK = 128
MOE_GROUPS = 4
MOE_EPG = 4
MOE_EXPERTS = 16
MOE_HIDDEN = 512

N_PROMPT = BATCH * SEQ
N_SAMPLE = DEC_BATCH * DEC_SEQ
N_TOK = N_PROMPT + N_SAMPLE
PROJ_N = 4224
LANES = 128
VMEM_LIMIT = 56 * 1024 * 1024

PROJ_TM, PROJ_TN = 1024, 1408
OUT_TM = 512
ATT_TQ = 256
ATT_HEADS_PER_STEP = 2
ATT_HEADS_PER_STEP_CTX = 4
POST_TM = 256
MOE_TM = 256
MOE_SLOTS = 2 * N_TOK
MOE_TILES = MOE_SLOTS // MOE_TM + MOE_EXPERTS
MOE_ROWS = MOE_TILES * MOE_TM
WEIGHT_DMA_PRIORITY = 1
SLAB = (D_MODEL // LANES, LANES)


def _cparams(sem):
    return pltpu.CompilerParams(dimension_semantics=sem, vmem_limit_bytes=VMEM_LIMIT)


def _mod_group(row0):
    return jnp.maximum(row0 // DEC_SEQ - (N_PROMPT // DEC_SEQ - 1), 0)


def _silu(x):
    return x / (1.0 + jnp.exp(-x))


def _nt(a, b):
    return lax.dot_general(a, b, (((1,), (1,)), ((), ())), preferred_element_type=F32)


def _rms(x, g):
    return x * lax.rsqrt(jnp.mean(x * x, axis=-1, keepdims=True) + RMS_EPS) * g


def _layer_norm(y, g, b):
    mu = jnp.mean(y, axis=-1, keepdims=True)
    yc = y - mu
    return yc * lax.rsqrt(jnp.mean(yc * yc, axis=-1, keepdims=True) + LN_EPS) * g + b


def _adaln_body(c_ref, w_ref, b_ref, o_ref):
    a = _silu(c_ref[...])
    o_ref[0] = jnp.dot(a, w_ref[0], preferred_element_type=F32) + b_ref[0]


def _adaln(cs, w_mod, b_mod):
    tn = 1024
    n = w_mod.shape[-1]
    return pl.pallas_call(
        _adaln_body,
        grid=(DEPTH, n // tn),
        in_specs=[
            pl.BlockSpec((8, D_MODEL), lambda l, j: (0, 0)),
            pl.BlockSpec((1, D_MODEL, tn), lambda l, j: (l, 0, j)),
            pl.BlockSpec((1, 1, tn), lambda l, j: (l, 0, j)),
        ],
        out_specs=pl.BlockSpec((1, 8, tn), lambda l, j: (l, 0, j)),
        out_shape=jax.ShapeDtypeStruct((DEPTH, 8, n), F32),
        compiler_params=_cparams(("parallel", "parallel")),
        name="adaln",
    )(cs, w_mod, b_mod.reshape(DEPTH, 1, n))


def _proj_body(x_ref, sh_ref, sc_ref, w_ref, o_ref, h_scr):
    @pl.when(pl.program_id(1) == 0)
    def _():
        h = x_ref[...] * (1.0 + sc_ref[0]) + sh_ref[0]
        h_scr[...] = h.astype(BF16)

    o_ref[...] = jnp.dot(h_scr[...], w_ref[...], preferred_element_type=F32)


def _proj(x, shift, scale, w):
    tm, tn = PROJ_TM, PROJ_TN
    n = w.shape[1]
    mod_spec = pl.BlockSpec((1, 1, D_MODEL), lambda i, j: (_mod_group(i * tm), 0, 0))
    return pl.pallas_call(
        _proj_body,
        grid=(N_TOK // tm, n // tn),
        in_specs=[
            pl.BlockSpec((tm, D_MODEL), lambda i, j: (i, 0)),
            mod_spec,
            mod_spec,
            pl.BlockSpec((D_MODEL, tn), lambda i, j: (0, j)),
        ],
        out_specs=pl.BlockSpec((tm, tn), lambda i, j: (i, j)),
        out_shape=jax.ShapeDtypeStruct((N_TOK, n), F32),
        scratch_shapes=[pltpu.VMEM((tm, D_MODEL), BF16)],
        compiler_params=_cparams(("parallel", "arbitrary")),
        name="in_proj",
    )(x, shift, scale, w)


PROJ_BLOCKS = PROJ_N // LANES


def _prep_even_body(a_ref, b_ref, o_ref):
    c = pl.program_id(0)
    half = LANES // 2
    aligned = (MLA_Q_RANK + MLA_KV_RANK) // LANES

    @pl.when(c < aligned)
    def _():
        o_ref[...] = a_ref[0]

    @pl.when((c >= aligned) & (c < PROJ_BLOCKS - 1))
    def _():
        o_ref[...] = jnp.concatenate([a_ref[0][:, half:], b_ref[0][:, :half]], axis=1)

    @pl.when(c == PROJ_BLOCKS - 1)
    def _():
        k_pe = a_ref[0][:, :half]
        o_ref[...] = jnp.concatenate([k_pe, jnp.zeros_like(k_pe)], axis=1)


def _prep_even(w_in, layer):
    aligned = (MLA_Q_RANK + MLA_KV_RANK) // LANES
    last = PROJ_BLOCKS - 1
    return pl.pallas_call(
        _prep_even_body,
        grid=(PROJ_BLOCKS,),
        in_specs=[pl.BlockSpec((1, D_MODEL, LANES), lambda c: (layer, 0, jnp.where(c == last, aligned, c))),
                  pl.BlockSpec((1, D_MODEL, LANES), lambda c: (layer, 0, jnp.minimum(c + 1, last)))],
        out_specs=pl.BlockSpec((D_MODEL, LANES), lambda c: (0, c)),
        out_shape=jax.ShapeDtypeStruct((D_MODEL, PROJ_N), BF16),
        compiler_params=_cparams(("parallel",)),
        name="prep_w_even",
    )(w_in, w_in)


def _prep_odd_body(perm_ref, a_ref, dt_ref, o_ref):
    c = pl.program_id(0)

    @pl.when(c < PROJ_BLOCKS - 1)
    def _():
        o_ref[...] = a_ref[0]

    @pl.when(c == PROJ_BLOCKS - 1)
    def _():
        o_ref[...] = dt_ref[...]


def _prep_odd(w_in, layer):
    order = list(range(0, 8)) + list(range(20, 28)) + list(range(12, 20)) + [8, 9, 10, 11, 28, 29, 30, 31, 31]
    perm = jnp.asarray(order, jnp.int32)
    n_dt = 2 * SSD_HEADS
    w_dt = jnp.pad(w_in[layer, :, w_in.shape[2] - n_dt:], ((0, 0), (0, LANES - n_dt))).astype(BF16)
    return pl.pallas_call(
        _prep_odd_body,
        grid_spec=pltpu.PrefetchScalarGridSpec(
            num_scalar_prefetch=1,
            grid=(PROJ_BLOCKS,),
            in_specs=[pl.BlockSpec((1, D_MODEL, LANES), lambda c, perm_ref: (layer, 0, perm_ref[c])),
                      pl.BlockSpec((D_MODEL, LANES), lambda c, perm_ref: (0, 0))],
            out_specs=pl.BlockSpec((D_MODEL, LANES), lambda c, perm_ref: (0, c)),
        ),
        out_shape=jax.ShapeDtypeStruct((D_MODEL, PROJ_N), BF16),
        compiler_params=_cparams(("parallel",)),
        name="prep_w_odd",
    )(perm, w_in, w_dt)


def _mm_body(a_ref, b_ref, o_ref):
    o_ref[...] = jnp.dot(a_ref[...].astype(BF16), b_ref[...], preferred_element_type=F32).astype(o_ref.dtype)


def _mm(a, b, out_dtype, tm, n_tiles, row_block=lambda i: i):
    k = a.shape[1]
    n = b.shape[1]
    return pl.pallas_call(
        _mm_body,
        grid=(n_tiles,),
        in_specs=[pl.BlockSpec((tm, k), lambda i: (row_block(i), 0)), pl.BlockSpec((k, n), lambda i: (0, 0))],
        out_specs=pl.BlockSpec((tm, n), lambda i: (i, 0)),
        out_shape=jax.ShapeDtypeStruct((n_tiles * tm, n), out_dtype),
        compiler_params=_cparams(("parallel",)),
        name="mm_resident",
    )(a, b)


def _rope(x, cos, sin_signed, quarter):
    rows, w = x.shape
    reps = w // LANES
    if reps > 1:
        cos = jnp.concatenate([cos] * reps, axis=1)
        sin_signed = jnp.concatenate([sin_signed] * reps, axis=1)
    lane = lax.broadcasted_iota(jnp.int32, x.shape, 1)
    first = (lane % (2 * quarter)) < quarter
    up = pltpu.roll(x, w - quarter, 1)
    dn = pltpu.roll(x, quarter, 1)
    return x * cos + jnp.where(first, up, dn) * sin_signed


def _rope_tables(rdim):
    half = rdim // 2
    quarter = half // 2
    pos = jnp.arange(DEC_SEQ)
    row = (pos // GRID_W).astype(F32)
    col = (pos % GRID_W).astype(F32)
    inv = ROPE_THETA ** (-jnp.arange(quarter, dtype=F32) * 2.0 / half)
    lane = jnp.arange(LANES)
    r = lane % rdim
    use_col = (r // half) == 1
    j = r % quarter
    p = jnp.where(use_col[None, :], col[:, None], row[:, None])
    ang = p * inv[j][None, :]
    sign = jnp.where((r % half) < quarter, -1.0, 1.0).astype(F32)
    cos = jnp.cos(ang)
    sin = jnp.sin(ang) * sign[None, :]
    cos = jnp.concatenate([jnp.ones((N_PROMPT, LANES), F32), cos, cos], axis=0)
    sin = jnp.concatenate([jnp.zeros((N_PROMPT, LANES), F32), sin, sin], axis=0)
    return cos, sin


def _ev_post_body(n_prev, ql_ref, kvl_ref, dq_ref, dk_ref, dv_ref, kpe_ref, cos_ref, sin_ref, gq_ref, gkv_ref,
                  wq_ref, wkv_ref, *refs):
    prev = refs[:4 * n_prev]
    q_ref, kv_ref, dqr_ref, dkr_ref, dvb_ref, kper_ref, c_ckv, c_kpe, c_dk, c_dv = refs[4 * n_prev:]
    cos = cos_ref[...]
    sin = sin_ref[...]
    qn = _rms(ql_ref[...], gq_ref[...]).astype(BF16)
    q_ref[...] = jnp.dot(qn, wq_ref[...], preferred_element_type=F32).astype(BF16)
    ckv = _rms(kvl_ref[...], gkv_ref[...])
    kv_ref[...] = jnp.dot(ckv.astype(BF16), wkv_ref[...], preferred_element_type=F32).astype(BF16)
    dqr_ref[...] = _rope(dq_ref[...], cos, sin, DIFF_QK // 4).astype(BF16)
    dk = _rope(dk_ref[...], cos, sin, DIFF_QK // 4)
    dkr_ref[...] = dk.astype(BF16)
    dv = dv_ref[...]
    dvb_ref[...] = dv.astype(BF16)
    kpe = _rope(kpe_ref[...], cos, sin, MLA_ROPE // 4)
    kper_ref[...] = kpe.astype(BF16)

    @pl.when(pl.program_id(0) < BATCH)
    def _():
        if n_prev:
            c_ckv[0, 0] = prev[0][0, 0]
            c_kpe[0, 0] = prev[1][0, 0]
            c_dk[0, 0] = prev[2][0, 0]
            c_dv[0, 0] = prev[3][0, 0]
        c_ckv[0, n_prev] = ckv
        c_kpe[0, n_prev] = kpe[:, :MLA_ROPE]
        c_dk[0, n_prev] = dk.reshape(SEQ, DIFF_HEADS, 2 * DIFF_QK)
        c_dv[0, n_prev] = dv.reshape(SEQ, DIFF_HEADS, DIFF_V)


def _ev_post(proj, cos, sin, gq, gkv, wq, wkv, prev):
    tm = POST_TM
    n_prev = 1 if prev else 0
    nl = n_prev + 1
    row = lambda c: (lambda i: (i, c))
    seq4 = lambda i: (jnp.minimum(i, BATCH - 1), 0, 0, 0)
    seq5 = lambda i: (jnp.minimum(i, BATCH - 1), 0, 0, 0, 0)
    cache_shapes = [(MLA_KV_RANK,), (MLA_ROPE,), (DIFF_HEADS, 2 * DIFF_QK), (DIFF_HEADS, DIFF_V)]

    def cache_spec(layers, tail):
        return pl.BlockSpec((1, layers, SEQ) + tail, seq4 if len(tail) == 1 else seq5)

    return pl.pallas_call(
        functools.partial(_ev_post_body, n_prev),
        grid=(N_TOK // tm,),
        in_specs=[
            pl.BlockSpec((tm, 512), row(0)),
            pl.BlockSpec((tm, 512), row(1)),
            pl.BlockSpec((tm, 1024), row(1)),
            pl.BlockSpec((tm, 1024), row(2)),
            pl.BlockSpec((tm, 1024), row(3)),
            pl.BlockSpec((tm, LANES), row(32)),
            pl.BlockSpec((tm, LANES), row(0)),
            pl.BlockSpec((tm, LANES), row(0)),
            pl.BlockSpec((1, 512), lambda i: (0, 0)),
            pl.BlockSpec((1, 512), lambda i: (0, 0)),
            pl.BlockSpec(wq.shape, lambda i: (0, 0)),
            pl.BlockSpec(wkv.shape, lambda i: (0, 0)),
        ] + [cache_spec(1, t) for t in cache_shapes] * n_prev,
        out_specs=[
            pl.BlockSpec((tm, wq.shape[1]), row(0)),
            pl.BlockSpec((tm, wkv.shape[1]), row(0)),
            pl.BlockSpec((tm, 1024), row(0)),
            pl.BlockSpec((tm, 1024), row(0)),
            pl.BlockSpec((tm, 1024), row(0)),
            pl.BlockSpec((tm, LANES), row(0)),
        ] + [cache_spec(nl, t) for t in cache_shapes],
        out_shape=[
            jax.ShapeDtypeStruct((N_TOK, wq.shape[1]), BF16),
            jax.ShapeDtypeStruct((N_TOK, wkv.shape[1]), BF16),
            jax.ShapeDtypeStruct((N_TOK, 1024), BF16),
            jax.ShapeDtypeStruct((N_TOK, 1024), BF16),
            jax.ShapeDtypeStruct((N_TOK, 1024), BF16),
            jax.ShapeDtypeStruct((N_TOK, LANES), BF16),
        ] + [jax.ShapeDtypeStruct((BATCH, nl, SEQ) + t, F32) for t in cache_shapes],
        compiler_params=_cparams(("arbitrary",)),
        name="even_post",
    )(proj, proj, proj, proj, proj, proj, cos, sin, gq, gkv, wq, wkv, *prev)


def _softmax_av(scores, values):
    m = functools.reduce(jnp.maximum, [jnp.max(s, axis=-1, keepdims=True) for s in scores])
    es = [jnp.exp(s - m) for s in scores]
    l = sum(jnp.sum(e, axis=-1, keepdims=True) for e in es)
    o = sum(jnp.dot(e.astype(BF16), v, preferred_element_type=F32) for e, v in zip(es, values))
    return o / l


def _even_attn_body(lam_init, has_ctx, hps, q_ref, cos_ref, sin_ref, kv_ref, kpe_ref, dq_ref, dk_ref, dv_ref,
                    lam_ref, sub_ref, *refs):
    if has_ctx:
        kvc_ref, kpec_ref, dkc_ref, dvc_ref, omla_ref, odiff_ref = refs
    else:
        omla_ref, odiff_ref = refs
    cos = cos_ref[...]
    sin = sin_ref[...]
    lam = lam_ref[...]
    lam_full = (jnp.exp(jnp.sum(lam[0:1] * lam[1:2], axis=-1, keepdims=True))
                - jnp.exp(jnp.sum(lam[2:3] * lam[3:4], axis=-1, keepdims=True)) + lam_init)
    mla_scale = (MLA_NOPE + MLA_ROPE) ** -0.5
    lane = lax.broadcasted_iota(jnp.int32, (q_ref.shape[0], 2 * DIFF_QK), 1)
    for u in range(hps):
        qc = slice(u * 256, (u + 1) * 256)
        dc = slice(u * LANES, (u + 1) * LANES)
        q = q_ref[:, qc]
        qr = _rope(q[:, MLA_NOPE:].astype(F32), cos, sin, MLA_ROPE // 4).astype(BF16)
        qcat = jnp.concatenate([q[:, :MLA_NOPE], qr], axis=1)
        kv = kv_ref[:, qc]
        scores = [_nt(qcat, jnp.concatenate([kv[:, :MLA_NOPE], kpe_ref[...]], axis=1)) * mla_scale]
        values = [kv[:, MLA_NOPE:]]
        if has_ctx:
            kvc = kvc_ref[:, qc]
            scores.append(_nt(qcat, jnp.concatenate([kvc[:, :MLA_NOPE], kpec_ref[...]], axis=1)) * mla_scale)
            values.append(kvc[:, MLA_NOPE:])
        omla_ref[:, dc] = _softmax_av(scores, values).astype(BF16)

        dq = dq_ref[:, dc] * jnp.asarray(DIFF_QK ** -0.5, BF16)
        zero = jnp.zeros_like(dq)
        keys = [dk_ref[:, dc]]
        values = [dv_ref[:, dc]]
        if has_ctx:
            keys.append(dkc_ref[:, dc])
            values.append(dvc_ref[:, dc])
        a1 = _softmax_av([_nt(jnp.where(lane < DIFF_QK, dq, zero), k) for k in keys], values)
        a2 = _softmax_av([_nt(jnp.where(lane >= DIFF_QK, dq, zero), k) for k in keys], values)
        odiff_ref[:, dc] = (_rms(a1 - lam_full * a2, sub_ref[...]) * (1.0 - lam_init)).astype(BF16)


def _even_attn(lam_init, q, cos, sin, kv, kpe, dq, dk, dv, lam, subln, n_seq, seq, row_off, ctx=None):
    tq = ATT_TQ
    hps = ATT_HEADS_PER_STEP if ctx is not None else ATT_HEADS_PER_STEP_CTX
    nq = seq // tq
    q0 = row_off // tq
    s0 = row_off // seq
    qspec = lambda w: pl.BlockSpec((tq, hps * w), lambda b, j, h: (q0 + b * nq + j, h))
    kspec = lambda w: pl.BlockSpec((seq, hps * w), lambda b, j, h: (s0 + b, h))
    tspec = pl.BlockSpec((tq, LANES), lambda b, j, h: (q0 + b * nq + j, 0))
    ospec = pl.BlockSpec((tq, hps * LANES), lambda b, j, h: (b * nq + j, h))
    in_specs = [qspec(256), tspec, tspec, kspec(256),
                pl.BlockSpec((seq, LANES), lambda b, j, h: (s0 + b, 0)),
                qspec(LANES), kspec(LANES), kspec(LANES),
                pl.BlockSpec((4, DIFF_QK), lambda b, j, h: (0, 0)),
                pl.BlockSpec((1, DIFF_V), lambda b, j, h: (0, 0))]
    args = [q, cos, sin, kv, kpe, dq, dk, dv, lam, subln]
    if ctx is not None:
        cspec = lambda w: pl.BlockSpec((PAST_LEN, hps * w), lambda b, j, h: (b, h))
        in_specs += [cspec(256), pl.BlockSpec((PAST_LEN, LANES), lambda b, j, h: (b, 0)), cspec(LANES), cspec(LANES)]
        args += list(ctx)
    return pl.pallas_call(
        functools.partial(_even_attn_body, lam_init, ctx is not None, hps),
        grid=(n_seq, nq, MLA_HEADS // hps),
        in_specs=in_specs,
        out_specs=[ospec, ospec],
        out_shape=[jax.ShapeDtypeStruct((n_seq * seq, 1024), BF16)] * 2,
        compiler_params=_cparams(("parallel", "parallel", "parallel")),
        name="even_attn",
    )(*args)


def _route(logits):
    lane = lax.broadcasted_iota(jnp.int32, logits.shape, 1).astype(F32)
    neg = -jnp.inf
    none = float(LANES)
    is_g = lane < MOE_GROUPS
    lg = jnp.where(is_g, logits, neg)
    mg = jnp.max(lg, axis=-1, keepdims=True)
    g_val = 1.0 / jnp.sum(jnp.exp(lg - mg), axis=-1, keepdims=True)
    g_idx = jnp.min(jnp.where(is_g & (lg == mg), lane, none), axis=-1, keepdims=True)
    e_lo = MOE_GROUPS + g_idx * MOE_EPG
    is_e = (lane >= e_lo) & (lane < e_lo + MOE_EPG)
    le = jnp.where(is_e, logits, neg)
    p = jnp.exp(le - jnp.max(le, axis=-1, keepdims=True))
    p1 = jnp.max(p, axis=-1, keepdims=True)
    i1 = jnp.min(jnp.where(is_e & (p == p1), lane, none), axis=-1, keepdims=True)
    rest = is_e & (lane != i1)
    p2 = jnp.max(jnp.where(rest, p, neg), axis=-1, keepdims=True)
    i2 = jnp.min(jnp.where(rest & (p == p2), lane, none), axis=-1, keepdims=True)
    s = g_val / (p1 + p2)
    return jnp.where(lane == 0.0, i1 - MOE_GROUPS,
                     jnp.where(lane == 1.0, i2 - MOE_GROUPS,
                               jnp.where(lane == 2.0, p1 * s, jnp.where(lane == 3.0, p2 * s, 0.0))))


def _out_proj_body(a1p_ref, a1s_ref, a2p_ref, a2s_ref, w_ref, x_ref, gate_ref, g_ref, b_ref, sh_ref, sc_ref,
                   wr_ref, xo_ref, h_ref, comb_ref):
    is_prompt = pl.program_id(0) < N_PROMPT // OUT_TM
    a1 = jnp.where(is_prompt, a1p_ref[...], a1s_ref[...])
    a2 = jnp.where(is_prompt, a2p_ref[...], a2s_ref[...])
    k1 = a1.shape[1]
    acc = jnp.dot(a1, w_ref[:k1, :], preferred_element_type=F32)
    acc = acc + jnp.dot(a2, w_ref[k1:, :], preferred_element_type=F32)
    xn = _layer_norm(DEEPNORM_ALPHA * x_ref[...] + gate_ref[0] * acc, g_ref[...], b_ref[...])
    xo_ref[...] = xn
    h = xn * (1.0 + sc_ref[0]) + sh_ref[0]
    h_ref[...] = h.reshape(h_ref.shape).astype(BF16)
    h_hi = h.astype(BF16)
    h_lo = (h - h_hi.astype(F32)).astype(BF16)
    logits = (jnp.dot(h_hi, wr_ref[0], preferred_element_type=F32)
              + jnp.dot(h_lo, wr_ref[0], preferred_element_type=F32)
              + jnp.dot(h_hi, wr_ref[1], preferred_element_type=F32))
    comb_ref[...] = _route(logits)


def _out_proj(a1p, a1s, a2p, a2s, w, x, gate, ln_g, ln_b, shift, scale, w_router):
    tm = OUT_TM
    np_tiles = N_PROMPT // tm
    k1, k2 = a1p.shape[1], a2p.shape[1]
    mod_spec = pl.BlockSpec((1, 1, D_MODEL), lambda i: (_mod_group(i * tm), 0, 0))
    vec_spec = pl.BlockSpec((1, D_MODEL), lambda i: (0, 0))
    row_spec = pl.BlockSpec((tm, D_MODEL), lambda i: (i, 0))
    pspec = lambda k: pl.BlockSpec((tm, k), lambda i: (jnp.minimum(i, np_tiles - 1), 0))
    sspec = lambda k: pl.BlockSpec((tm, k), lambda i: (jnp.maximum(i - np_tiles, 0), 0))
    return pl.pallas_call(
        _out_proj_body,
        grid=(N_TOK // tm,),
        in_specs=[
            pspec(k1), sspec(k1), pspec(k2), sspec(k2),
            pl.BlockSpec((k1 + k2, D_MODEL), lambda i: (0, 0)),
            row_spec, mod_spec, vec_spec, vec_spec, mod_spec, mod_spec,
            pl.BlockSpec((2, D_MODEL, LANES), lambda i: (0, 0, 0)),
        ],
        out_specs=[row_spec, pl.BlockSpec((tm,) + SLAB, lambda i: (i, 0, 0)),
                   pl.BlockSpec((tm, LANES), lambda i: (i, 0))],
        out_shape=[
            jax.ShapeDtypeStruct((N_TOK, D_MODEL), F32),
            jax.ShapeDtypeStruct((N_TOK,) + SLAB, BF16),
            jax.ShapeDtypeStruct((N_TOK, LANES), F32),
        ],
        compiler_params=_cparams(("parallel",)),
        name="out_proj_ln_router",
    )(a1p, a1s, a2p, a2s, w, x, gate, ln_g, ln_b, shift, scale, w_router)


def _route_meta(route):
    e = route[:, :2].astype(jnp.int32).reshape(-1)
    onehot = (e[:, None] == jnp.arange(MOE_EXPERTS, dtype=jnp.int32)[None, :]).astype(jnp.int32)
    csum = jnp.cumsum(onehot, axis=0)
    rank = jnp.sum((csum - onehot) * onehot, axis=1)
    count = csum[-1]
    ntile = (count + MOE_TM - 1) // MOE_TM
    tile_end = jnp.cumsum(ntile)
    tile_off = tile_end - ntile
    pos = jnp.sum(onehot * tile_off[None, :], axis=1) * MOE_TM + rank
    n_used = tile_end[-1:]
    j = jnp.arange(MOE_TILES, dtype=jnp.int32)
    tile_expert = jnp.sum((tile_end[None, :] <= jnp.minimum(j, n_used - 1)[:, None]).astype(jnp.int32), axis=1)
    is_e = tile_expert[:, None] == jnp.arange(MOE_EXPERTS, dtype=jnp.int32)[None, :]
    left = jnp.sum(jnp.where(is_e, (count - (j[:, None] - tile_off[None, :]) * MOE_TM)[...], 0), axis=1)
    n_valid = jnp.where(j < n_used, jnp.clip(left, 0, MOE_TM), 0)
    ids = jnp.arange(MOE_EXPERTS, dtype=jnp.int32)
    later = (ids[None, :] > ids[:, None]) & (ntile[None, :] > 0)
    next_of = jnp.min(jnp.where(later, ids[None, :], MOE_EXPERTS), axis=1)
    next_of = jnp.where(next_of == MOE_EXPERTS, -1, next_of)
    next_expert = jnp.sum(jnp.where(is_e, next_of[None, :], 0), axis=1)
    ordinal = jnp.cumsum((ntile > 0).astype(jnp.int32)) - 1
    w_slot = jnp.sum(jnp.where(is_e, ordinal[None, :], 0), axis=1) % 2
    return pos, tile_expert, n_valid, n_used, next_expert, w_slot


def _moe_body(layer, pos_ref, te_ref, nv_ref, nu_ref, ne_ref, ws_ref, h_ref, wg_ref, wu_ref, wd_ref, z_ref,
              inv, xbuf, zbuf, wgbuf, wubuf, wdbuf, wg_s, wu_s, wd_s, gsem, ssem, wsem):
    j = pl.program_id(0)
    slot = j % 2
    n_used = nu_ref[0]
    group = 8

    def weight_copies(e, s):
        return [pltpu.make_async_copy(wg_ref.at[layer, e], wgbuf.at[s], wsem.at[s]),
                pltpu.make_async_copy(wu_ref.at[layer, e], wubuf.at[s], wsem.at[s]),
                pltpu.make_async_copy(wd_ref.at[layer, e], wdbuf.at[s], wsem.at[s])]

    def gather_row(tile, s, r):
        tok = lax.shift_right_logical(inv[tile * MOE_TM + r], 1)
        return pltpu.make_async_copy(h_ref.at[tok], xbuf.at[s, r], gsem.at[s])

    def scatter_row(tile, s, r):
        sl = inv[tile * MOE_TM + r]
        dst = (sl & 1) * N_TOK + lax.shift_right_logical(sl, 1)
        return pltpu.make_async_copy(zbuf.at[s, r], z_ref.at[dst], ssem.at[s])

    def for_valid_rows(tile, fn):
        n = nv_ref[tile]

        def block(g, c):
            for u in range(group):
                fn(g * group + u, u % 2)
            return c

        def single(r, c):
            fn(r, 0)
            return c

        full = lax.shift_right_logical(n, 3)
        lax.fori_loop(0, full, block, 0)
        lax.fori_loop(full * group, n, single, 0)

    @pl.when(j == 0)
    def _():
        def fill(g, c):
            for u in range(group):
                sl = g * group + u
                inv[pos_ref[sl]] = sl
            return c
        lax.fori_loop(0, MOE_SLOTS // group, fill, 0)
        xbuf[...] = jnp.zeros_like(xbuf)
        for_valid_rows(0, lambda r, pr: gather_row(0, 0, r).start(priority=pr))
        for cp in weight_copies(te_ref[0], ws_ref[0]):
            cp.start(priority=WEIGHT_DMA_PRIORITY)

    changed = (j == 0) | (te_ref[j] != te_ref[jnp.maximum(j - 1, 0)])

    @pl.when(changed)
    def _():
        ws = ws_ref[j]
        for cp in weight_copies(te_ref[j], ws):
            cp.wait()
        wg_s[...] = wgbuf[ws].astype(BF16)
        wu_s[...] = wubuf[ws].astype(BF16)
        wd_s[...] = wdbuf[ws].astype(BF16)

        @pl.when(ne_ref[j] >= 0)
        def _():
            for cp in weight_copies(ne_ref[j], 1 - ws):
                cp.start(priority=WEIGHT_DMA_PRIORITY)

    @pl.when(j < n_used)
    def _():
        @pl.when(j + 1 < n_used)
        def _():
            for_valid_rows(j + 1, lambda r, pr: gather_row(j + 1, 1 - slot, r).start(priority=pr))

        for_valid_rows(j, lambda r, pr: gather_row(j, slot, r).wait())

        @pl.when(j >= 2)
        def _():
            for_valid_rows(j - 2, lambda r, pr: scatter_row(j - 2, slot, r).wait())

        x = xbuf[slot].reshape(MOE_TM, D_MODEL)
        hg = jnp.dot(x, wg_s[...], preferred_element_type=F32)
        hu = jnp.dot(x, wu_s[...], preferred_element_type=F32)
        act = (_silu(hg) * hu).astype(BF16)
        z = jnp.dot(act, wd_s[...], preferred_element_type=F32)
        zbuf[slot] = z.reshape((MOE_TM,) + SLAB).astype(BF16)
        for_valid_rows(j, lambda r, pr: scatter_row(j, slot, r).start(priority=pr))

        @pl.when(j == n_used - 1)
        def _():
            @pl.when(j >= 1)
            def _():
                for_valid_rows(j - 1, lambda r, pr: scatter_row(j - 1, 1 - slot, r).wait())
            for_valid_rows(j, lambda r, pr: scatter_row(j, slot, r).wait())


def _moe_experts(layer, h, meta, wg, wu, wd):
    any_spec = pl.BlockSpec(memory_space=pl.ANY)
    up_shape, down_shape = (D_MODEL, MOE_HIDDEN), (MOE_HIDDEN, D_MODEL)
    return pl.pallas_call(
        functools.partial(_moe_body, layer),
        grid_spec=pltpu.PrefetchScalarGridSpec(
            num_scalar_prefetch=len(meta),
            grid=(MOE_TILES,),
            in_specs=[any_spec, any_spec, any_spec, any_spec],
            out_specs=any_spec,
            scratch_shapes=[pltpu.SMEM((MOE_ROWS,), jnp.int32),
                            pltpu.VMEM((2, MOE_TM) + SLAB, BF16), pltpu.VMEM((2, MOE_TM) + SLAB, BF16),
                            pltpu.VMEM((2,) + up_shape, F32), pltpu.VMEM((2,) + up_shape, F32),
                            pltpu.VMEM((2,) + down_shape, F32),
                            pltpu.VMEM(up_shape, BF16), pltpu.VMEM(up_shape, BF16), pltpu.VMEM(down_shape, BF16),
                            pltpu.SemaphoreType.DMA((2,)), pltpu.SemaphoreType.DMA((2,)),
                            pltpu.SemaphoreType.DMA((2,))],
        ),
        out_shape=jax.ShapeDtypeStruct((2 * N_TOK,) + SLAB, BF16),
        compiler_params=_cparams(("arbitrary",)),
        name="moe_experts",
    )(*meta, h, wg, wu, wd)


def _combine_body(split, x_ref, r_ref, gate_ref, g_ref, b_ref, z1_ref, z2_ref, *o_refs):
    r = r_ref[...]
    z1 = z1_ref[...].reshape(x_ref.shape).astype(F32)
    z2 = z2_ref[...].reshape(x_ref.shape).astype(F32)
    y = r[:, 2:3] * z1 + r[:, 3:4] * z2
    out = _layer_norm(DEEPNORM_ALPHA * x_ref[...] + gate_ref[0] * y, g_ref[...], b_ref[...])
    if split:
        is_prompt = pl.program_id(0) < N_PROMPT // x_ref.shape[0]

        @pl.when(is_prompt)
        def _():
            o_refs[0][...] = out

        @pl.when(jnp.logical_not(is_prompt))
        def _():
            o_refs[1][...] = out
    else:
        o_refs[0][...] = out


def _combine(x, z, route, gate, ln_g, ln_b, split=False):
    tm = 512
    nt = N_TOK // tm
    npt = N_PROMPT // tm
    row_spec = pl.BlockSpec((tm, D_MODEL), lambda i: (i, 0))
    vec_spec = pl.BlockSpec((1, D_MODEL), lambda i: (0, 0))
    if split:
        out_specs = [pl.BlockSpec((tm, D_MODEL), lambda i: (jnp.minimum(i, npt - 1), 0)),
                     pl.BlockSpec((tm, D_MODEL), lambda i: (jnp.maximum(i - npt, 0), 0))]
        out_shape = [jax.ShapeDtypeStruct((N_PROMPT, D_MODEL), F32), jax.ShapeDtypeStruct((N_SAMPLE, D_MODEL), F32)]
    else:
        out_specs = row_spec
        out_shape = jax.ShapeDtypeStruct((N_TOK, D_MODEL), F32)
    return pl.pallas_call(
        functools.partial(_combine_body, split),
        grid=(nt,),
        in_specs=[row_spec, pl.BlockSpec((tm, LANES), lambda i: (i, 0)),
                  pl.BlockSpec((1, 1, D_MODEL), lambda i: (_mod_group(i * tm), 0, 0)),
                  vec_spec, vec_spec, pl.BlockSpec((tm,) + SLAB, lambda i: (i, 0, 0)),
                  pl.BlockSpec((tm,) + SLAB, lambda i: (nt + i, 0, 0))],
        out_specs=out_specs,
        out_shape=out_shape,
        compiler_params=_cparams(("arbitrary",)),
        name="moe_combine_ln",
    )(x, route, gate, ln_g, ln_b, z, z)


def _od_post_body(n_prev, q_ref, k_ref, v_ref, xs_ref, xs_lo_ref, xs_hi_ref, bc_ref, bc_lo_ref, bc_hi_ref, dt_ref,
                  cos_ref, sin_ref, cw_ref, cb_ref, dtb_ref, *refs):
    prev = refs[:2 * n_prev]
    qr_ref, kr_ref, vb_ref, xc_ref, bcc_ref, dtp_ref, c_k, c_v = refs[2 * n_prev:]
    i = pl.program_id(0)
    cos = cos_ref[...]
    sin = sin_ref[...]
    k = k_ref[...]
    v = v_ref[...]
    qr_ref[...] = _rope(q_ref[...], cos, sin, GQA_HEAD_DIM // 4).astype(BF16)
    kr_ref[...] = _rope(k, cos, sin, GQA_HEAD_DIM // 4).astype(BF16)
    vb_ref[...] = v.astype(BF16)

    is_prompt = i < BATCH
    tile_in_seq = (i - BATCH) % (DEC_SEQ // POST_TM)
    seq_start = is_prompt | (tile_in_seq == 0)
    seq_end = is_prompt | (tile_in_seq == DEC_SEQ // POST_TM - 1)

    def conv_silu(x, lo_ref, hi_ref, w, b):
        rows = x.shape[0]
        r = lax.broadcasted_iota(jnp.int32, x.shape, 0)
        lo = jnp.where(seq_start, 0.0, lo_ref[7:8, :])
        hi = jnp.where(seq_end, 0.0, hi_ref[0:1, :])
        before = jnp.where(r == 0, lo, pltpu.roll(x, 1, 0))
        after = jnp.where(r == rows - 1, hi, pltpu.roll(x, rows - 1, 0))
        return _silu(before * w[0:1] + x * w[1:2] + after * w[2:3] + b)

    xc_ref[...] = conv_silu(xs_ref[...], xs_lo_ref, xs_hi_ref, cw_ref[:, :SSD_D_INNER], cb_ref[:, :SSD_D_INNER])
    bcc_ref[...] = conv_silu(bc_ref[...], bc_lo_ref, bc_hi_ref, cw_ref[:, SSD_D_INNER:], cb_ref[:, SSD_D_INNER:])
    t = dt_ref[...] + dtb_ref[...]
    dtp_ref[...] = jnp.maximum(t, 0.0) + jnp.log1p(jnp.exp(-jnp.abs(t)))

    @pl.when(is_prompt)
    def _():
        if n_prev:
            c_k[0, 0] = prev[0][0, 0]
            c_v[0, 0] = prev[1][0, 0]
        c_k[0, n_prev] = k.reshape(SEQ, GQA_KV_HEADS, GQA_HEAD_DIM)
        c_v[0, n_prev] = v.reshape(SEQ, GQA_KV_HEADS, GQA_HEAD_DIM)


def _od_post(proj, cos, sin, conv_w, conv_b, dt_bias, prev):
    tm = POST_TM
    halo = 8
    per = tm // halo
    n_halo = N_TOK // halo
    n_prev = 1 if prev else 0
    blk = lambda w, c: pl.BlockSpec((tm, w), lambda i: (i, c))
    lo = lambda w, c: pl.BlockSpec((halo, w), lambda i: (jnp.maximum(i * per - 1, 0), c))
    hi = lambda w, c: pl.BlockSpec((halo, w), lambda i: (jnp.minimum((i + 1) * per, n_halo - 1), c))
    full = lambda a: pl.BlockSpec(a.shape, lambda i: (0, 0))
    oblk = lambda w: pl.BlockSpec((tm, w), lambda i: (i, 0))
    cache_spec = lambda layers: pl.BlockSpec((1, layers, SEQ, GQA_KV_HEADS, GQA_HEAD_DIM),
                                             lambda i: (jnp.minimum(i, BATCH - 1), 0, 0, 0, 0))
    cache_shape = jax.ShapeDtypeStruct((BATCH, n_prev + 1, SEQ, GQA_KV_HEADS, GQA_HEAD_DIM), F32)
    return pl.pallas_call(
        functools.partial(_od_post_body, n_prev),
        grid=(N_TOK // tm,),
        in_specs=[blk(1024, 0), blk(256, 12), blk(256, 13),
                  blk(1024, 1), lo(1024, 1), hi(1024, 1), blk(512, 7), lo(512, 7), hi(512, 7), blk(LANES, 32),
                  blk(LANES, 0), blk(LANES, 0), full(conv_w), full(conv_b), full(dt_bias)]
        + [cache_spec(1)] * (2 * n_prev),
        out_specs=[oblk(1024), oblk(256), oblk(256), oblk(1024), oblk(512), oblk(LANES),
                   cache_spec(n_prev + 1), cache_spec(n_prev + 1)],
        out_shape=[
            jax.ShapeDtypeStruct((N_TOK, 1024), BF16),
            jax.ShapeDtypeStruct((N_TOK, 256), BF16),
            jax.ShapeDtypeStruct((N_TOK, 256), BF16),
            jax.ShapeDtypeStruct((N_TOK, 1024), F32),
            jax.ShapeDtypeStruct((N_TOK, 512), F32),
            jax.ShapeDtypeStruct((N_TOK, LANES), F32),
            cache_shape, cache_shape,
        ],
        compiler_params=_cparams(("arbitrary",)),
        name="odd_post",
    )(proj, proj, proj, proj, proj, proj, proj, proj, proj, proj, cos, sin, conv_w, conv_b, dt_bias, *prev)


def _sink_column(sink_ref, kh, rows):
    g = GQA_HEADS // GQA_KV_HEADS
    return jnp.concatenate(
        [jnp.broadcast_to(sink_ref[kh * g + j:kh * g + j + 1, 0:1], (rows, 1)) for j in range(g)], axis=0)


def _gqa_prompt_body(q_ref, k_ref, v_ref, sink_ref, o_ref):
    g = GQA_HEADS // GQA_KV_HEADS
    d = GQA_HEAD_DIM
    rows = q_ref.shape[0]
    scale = d ** -0.5
    outs = []
    for kh in range(GQA_KV_HEADS):
        qg = jnp.concatenate([q_ref[:, (kh * g + j) * d:(kh * g + j + 1) * d] for j in range(g)], axis=0)
        k = k_ref[:, kh * d:(kh + 1) * d]
        v = v_ref[:, kh * d:(kh + 1) * d]
        s = _nt(qg, k) * scale
        snk = _sink_column(sink_ref, kh, rows)
        m = jnp.maximum(jnp.max(s, axis=-1, keepdims=True), snk)
        e = jnp.exp(s - m)
        l = jnp.sum(e, axis=-1, keepdims=True) + jnp.exp(snk - m)
        o = jnp.dot(e.astype(BF16), v, preferred_element_type=F32) / l
        outs.extend(o[j * rows:(j + 1) * rows] for j in range(g))
    o_ref[...] = jnp.concatenate(outs, axis=1).astype(BF16)


def _gqa_prompt(q, k, v, sink):
    spec = lambda w: pl.BlockSpec((SEQ, w), lambda i: (i, 0))
    return pl.pallas_call(
        _gqa_prompt_body,
        grid=(BATCH,),
        in_specs=[spec(1024), spec(256), spec(256), pl.BlockSpec((8, LANES), lambda i: (0, 0))],
        out_specs=spec(1024),
        out_shape=jax.ShapeDtypeStruct((N_PROMPT, 1024), BF16),
        compiler_params=_cparams(("parallel",)),
        name="gqa_dense",
    )(q, k, v, sink)


def _gqa_window_body(q_ref, k_ref, v_ref, kc_ref, vc_ref, sink_ref, o_ref):
    g = GQA_HEADS // GQA_KV_HEADS
    d = GQA_HEAD_DIM
    blk = q_ref.shape[0]
    span = 3 * blk
    n = pl.program_id(1)
    start = pl.multiple_of(jnp.clip((n - 1) * blk, 0, DEC_SEQ - span), blk)
    scale = d ** -0.5
    qpos = n * blk + lax.broadcasted_iota(jnp.int32, (blk, span), 0)
    kpos = start + lax.broadcasted_iota(jnp.int32, (blk, span), 1)
    valid1 = jnp.abs(qpos - kpos) <= WINDOW
    valid = jnp.concatenate([valid1] * g, axis=0)
    outs = []
    for kh in range(GQA_KV_HEADS):
        qg = jnp.concatenate([q_ref[:, (kh * g + j) * d:(kh * g + j + 1) * d] for j in range(g)], axis=0)
        kl = k_ref[pl.ds(start, span), kh * d:(kh + 1) * d]
        vl = v_ref[pl.ds(start, span), kh * d:(kh + 1) * d]
        kc = kc_ref[:, kh * d:(kh + 1) * d]
        vc = vc_ref[:, kh * d:(kh + 1) * d]
        sl = jnp.where(valid, _nt(qg, kl) * scale, -jnp.inf)
        sc = _nt(qg, kc) * scale
        snk = _sink_column(sink_ref, kh, blk)
        m = jnp.maximum(jnp.maximum(jnp.max(sl, axis=-1, keepdims=True), jnp.max(sc, axis=-1, keepdims=True)), snk)
        el = jnp.exp(sl - m)
        ec = jnp.exp(sc - m)
        l = jnp.sum(el, axis=-1, keepdims=True) + jnp.sum(ec, axis=-1, keepdims=True) + jnp.exp(snk - m)
        o = (jnp.dot(el.astype(BF16), vl, preferred_element_type=F32)
             + jnp.dot(ec.astype(BF16), vc, preferred_element_type=F32)) / l
        outs.extend(o[j * blk:(j + 1) * blk] for j in range(g))
    o_ref[...] = jnp.concatenate(outs, axis=1).astype(BF16)


def _gqa_window(q, k, v, cache_k, cache_v, sink):
    blk = WINDOW
    nq = DEC_SEQ // blk
    q0 = N_PROMPT // blk
    s0 = N_PROMPT // DEC_SEQ
    seq_spec = pl.BlockSpec((DEC_SEQ, 256), lambda b, j: (s0 + b, 0))
    ctx_spec = pl.BlockSpec((PAST_LEN, 256), lambda b, j: (b, 0))
    return pl.pallas_call(
        _gqa_window_body,
        grid=(DEC_BATCH, nq),
        in_specs=[pl.BlockSpec((blk, 1024), lambda b, j: (q0 + b * nq + j, 0)), seq_spec, seq_spec, ctx_spec,
                  ctx_spec,
                  pl.BlockSpec((8, LANES), lambda b, j: (0, 0))],
        out_specs=pl.BlockSpec((blk, 1024), lambda b, j: (b * nq + j, 0)),
        out_shape=jax.ShapeDtypeStruct((N_SAMPLE, 1024), BF16),
        compiler_params=_cparams(("parallel", "parallel")),
        name="gqa_window",
    )(q, k, v, cache_k, cache_v, sink)


def _expand(x, sel):
    hi = x.astype(BF16)
    r1 = x - hi.astype(F32)
    mid = r1.astype(BF16)
    lo = (r1 - mid.astype(F32)).astype(BF16)
    return (jnp.dot(hi, sel, preferred_element_type=F32) + jnp.dot(mid, sel, preferred_element_type=F32)
            + jnp.dot(lo, sel, preferred_element_type=F32))


def _ssd_body(has_h0, xs_ref, bc_ref, dt_ref, alog_ref, e1_ref, e2_ref, *refs):
    if has_h0:
        h0f_ref, h0b_ref, y_ref, stf_ref, stb_ref = refs
    else:
        y_ref, stf_ref, stb_ref = refs
    d = pl.program_id(1)
    c = pl.program_id(2)
    q = SSD_CHUNK
    ppg = SSD_HEADS // SSD_GROUPS // 2
    hd = SSD_HEAD_DIM

    row = lax.broadcasted_iota(jnp.int32, (q, LANES), 0)
    a = -jnp.exp(alog_ref[...])
    dta = dt_ref[...] * a

    def scan(x, rev):
        k = 1
        while k < q:
            if rev:
                x = x + jnp.where(row < q - k, pltpu.roll(x, q - k, 0), 0.0)
            else:
                x = x + jnp.where(row >= k, pltpu.roll(x, k, 0), 0.0)
            k *= 2
        return x

    li = lax.broadcasted_iota(jnp.int32, (q, q), 0)
    si = lax.broadcasted_iota(jnp.int32, (q, q), 1)

    def run(rev):
        st_ref = stb_ref if rev else stf_ref

        @pl.when(c == 0)
        def _():
            if has_h0:
                st_ref[0] = (h0b_ref if rev else h0f_ref)[0, 0]
            else:
                st_ref[0] = jnp.zeros(st_ref.shape[1:], F32)

        cum = scan(dta, rev)
        cum_t = cum.T
        e_row = 0 if rev else q - 1
        mask = (li <= si) if rev else (li >= si)
        dt_x = _expand(dt_ref[...], e1_ref[0])
        cum_x = _expand(cum, e1_ref[0])
        cum_b = _expand(cum, e2_ref[0])
        edge_x = cum_x[e_row:e_row + 1, :]
        xdt = xs_ref[...] * dt_x
        xdt16 = xdt.astype(BF16)
        xw16 = (xdt * jnp.exp(edge_x - cum_x)).astype(BF16)
        ecum = jnp.exp(cum_x)
        lower = si < hd
        ys = []
        for g in range(SSD_GROUPS):
            bm16 = bc_ref[:, g * SSD_STATE:(g + 1) * SSD_STATE].astype(BF16)
            cm16 = bc_ref[:, (SSD_GROUPS + g) * SSD_STATE:(SSD_GROUPS + g + 1) * SSD_STATE].astype(BF16)
            cb = _nt(cm16, bm16)
            for pp in range(ppg):
                p = g * ppg + pp
                cols = slice(p * LANES, (p + 1) * LANES)
                y_pair, fac = [], []
                for u in range(2):
                    h = 2 * p + u
                    j = (SSD_HEADS if rev else 0) + h
                    seg = cum_b[:, h * LANES:(h + 1) * LANES] - cum_t[j:j + 1, :]
                    decay = jnp.exp(jnp.where(mask, seg, -jnp.inf))
                    y_pair.append(jnp.dot((cb * decay).astype(BF16), xdt16[:, cols], preferred_element_type=F32))
                    fac.append(jnp.broadcast_to(jnp.exp(cum_t[j:j + 1, e_row:e_row + 1]), (hd, SSD_STATE)))
                y_diag = jnp.where(lower, y_pair[0], y_pair[1])
                state = lax.dot_general(xw16[:, cols], bm16, (((0,), (0,)), ((), ())),
                                        preferred_element_type=F32)
                h_prev = st_ref[0, 2 * p:2 * p + 2].reshape(2 * hd, SSD_STATE)
                y_off = _nt(cm16, h_prev.astype(BF16)) * ecum[:, cols]
                ys.append(y_diag + y_off)
                h_new = h_prev * jnp.concatenate(fac, axis=0) + state
                st_ref[0, 2 * p:2 * p + 2] = h_new.reshape(2, hd, SSD_STATE)
        y_ref[0] = jnp.concatenate(ys, axis=1)

    @pl.when(d == 0)
    def _():
        run(False)

    @pl.when(d == 1)
    def _():
        run(True)


def _ssd(xs, bc, dtp, a_log, n_seq, seq, row_off, h0=None):
    b = n_seq
    t = n_seq * seq
    nc = seq // SSD_CHUNK
    c0 = row_off // SSD_CHUNK

    def rmap(i, d, c):
        return i * nc + jnp.where(d == 0, c, nc - 1 - c)

    blk = lambda w: pl.BlockSpec((SSD_CHUNK, w), lambda i, d, c: (c0 + rmap(i, d, c), 0))
    st_spec = pl.BlockSpec((1, SSD_HEADS, SSD_HEAD_DIM, SSD_STATE), lambda i, d, c: (i, 0, 0, 0))
    st_shape = jax.ShapeDtypeStruct((b, SSD_HEADS, SSD_HEAD_DIM, SSD_STATE), F32)
    def sel(w):
        row = jnp.arange(LANES)[None, :, None]
        head = SSD_HEADS * jnp.arange(2)[:, None, None] + (jnp.arange(SSD_HEADS * w) // w)[None, None, :]
        return (row == head).astype(BF16)

    sel_spec = lambda w: pl.BlockSpec((1, LANES, SSD_HEADS * w), lambda i, d, c: (d, 0, 0))
    in_specs = [blk(1024), blk(512), blk(LANES), pl.BlockSpec((1, LANES), lambda i, d, c: (0, 0)),
                sel_spec(SSD_HEAD_DIM), sel_spec(LANES)]
    args = [xs, bc, dtp, a_log, sel(SSD_HEAD_DIM), sel(LANES)]
    if h0 is not None:
        layer, h0_f, h0_b = h0
        h0_spec = pl.BlockSpec((1, 1, SSD_HEADS, SSD_HEAD_DIM, SSD_STATE), lambda i, d, c: (i, layer, 0, 0, 0))
        in_specs += [h0_spec, h0_spec]
        args += [h0_f, h0_b]
    return pl.pallas_call(
        functools.partial(_ssd_body, h0 is not None),
        grid=(b, 2, nc),
        in_specs=in_specs,
        out_specs=[pl.BlockSpec((1, SSD_CHUNK, 1024), lambda i, d, c: (d, rmap(i, d, c), 0)), st_spec, st_spec],
        out_shape=[jax.ShapeDtypeStruct((2, t, 1024), F32), st_shape, st_shape],
        compiler_params=_cparams(("parallel", "arbitrary", "arbitrary")),
        name="ssd_scan",
    )(*args)


def _ssd_gate_body(y_ref, xs_ref, z_ref, dsk_ref, g_ref, o_ref):
    y = (y_ref[0] + y_ref[1] + xs_ref[...] * dsk_ref[...]) * _silu(z_ref[...])
    w = SSD_D_INNER // SSD_GROUPS
    outs = [_rms(y[:, g * w:(g + 1) * w], g_ref[:, g * w:(g + 1) * w]) for g in range(SSD_GROUPS)]
    o_ref[...] = jnp.concatenate(outs, axis=1).astype(BF16)


def _ssd_gate(y2, xs, proj, d_skip, norm_g, row_off):
    tm = 512
    t = y2.shape[1]
    off = row_off // tm
    return pl.pallas_call(
        _ssd_gate_body,
        grid=(t // tm,),
        in_specs=[pl.BlockSpec((2, tm, 1024), lambda i: (0, i, 0)),
                  pl.BlockSpec((tm, 1024), lambda i: (off + i, 0)),
                  pl.BlockSpec((tm, 1024), lambda i: (off + i, 2)),
                  pl.BlockSpec((1, 1024), lambda i: (0, 0)),
                  pl.BlockSpec((1, 1024), lambda i: (0, 0))],
        out_specs=pl.BlockSpec((tm, 1024), lambda i: (i, 0)),
        out_shape=jax.ShapeDtypeStruct((t, 1024), BF16),
        compiler_params=_cparams(("parallel",)),
        name="ssd_gate_norm",
    )(y2, xs, proj, d_skip, norm_g)


def _even_layer(x, mod, layer_idx, i, p, ctx, tables, prev_caches):
    w_in, q_norm, kv_norm, wq_b, wkv_b, lam, subln, w_out = p
    ctx_ckv, ctx_kpe, ctx_dk, ctx_dv = ctx
    cos, sin = tables
    lam_init = 0.8 - 0.6 * math.exp(-0.3 * layer_idx)

    proj = _proj(x, mod[0], mod[1], _prep_even(w_in, i))
    wq = wq_b.reshape(MLA_Q_RANK, MLA_HEADS, MLA_NOPE + MLA_ROPE)
    wq_r = jnp.pad(wq, ((0, 0), (0, 0), (0, MLA_ROPE))).reshape(MLA_Q_RANK, -1).astype(BF16)
    wkv = wkv_b.astype(BF16)
    q, kv, dq, dk, dv, kpe, *caches = _ev_post(proj, cos, sin, q_norm[None], kv_norm[None], wq_r, wkv, prev_caches)
    n_even = ctx_ckv.shape[1]
    kv_ctx = _mm(ctx_ckv.reshape(-1, MLA_KV_RANK), wkv, BF16, PAST_LEN, DEC_BATCH,
                 row_block=lambda b: b * n_even + i)

    n_ctx = DEC_BATCH * PAST_LEN
    kpe_ctx = jnp.pad(ctx_kpe[:, i].reshape(n_ctx, MLA_ROPE), ((0, 0), (0, LANES - MLA_ROPE))).astype(BF16)
    dk_ctx = ctx_dk[:, i].reshape(n_ctx, -1).astype(BF16)
    dv_ctx = ctx_dv[:, i].reshape(n_ctx, -1).astype(BF16)

    args = (lam_init, q, cos, sin, kv, kpe, dq, dk, dv, lam, subln[None])
    o1p, o2p = _even_attn(*args, n_seq=BATCH, seq=SEQ, row_off=0)
    o1s, o2s = _even_attn(*args, n_seq=DEC_BATCH, seq=DEC_SEQ, row_off=N_PROMPT,
                          ctx=(kv_ctx, kpe_ctx, dk_ctx, dv_ctx))
    return (o1p, o1s, o2p, o2s), w_out.astype(BF16), caches


def _odd_layer(x, mod, i, p, ctx, tables, prev_caches):
    w_in, sink, conv_w, conv_b, dt_bias, a_log, d_skip, norm_g, w_out = p
    k_ctx, v_ctx, h0_f, h0_b = ctx
    cos, sin = tables

    proj = _proj(x, mod[0], mod[1], _prep_odd(w_in, i))
    cb = conv_b[None]
    dtb = jnp.pad(dt_bias.reshape(1, -1), ((0, 0), (0, LANES - 2 * SSD_HEADS)))
    alog = jnp.pad(a_log.reshape(1, -1), ((0, 0), (0, LANES - 2 * SSD_HEADS)))
    q, k, v, xs, bc, dtp, *caches = _od_post(proj, cos, sin, conv_w, cb, dtb, prev_caches)
    sink_b = jnp.broadcast_to(sink[:, None], (GQA_HEADS, LANES))

    o_att_p = _gqa_prompt(q, k, v, sink_b)
    n_ctx = DEC_BATCH * PAST_LEN
    o_att_s = _gqa_window(q, k, v, k_ctx[:, i].reshape(n_ctx, -1).astype(BF16),
                          v_ctx[:, i].reshape(n_ctx, -1).astype(BF16), sink_b)

    y_p, st_f, st_b = _ssd(xs, bc, dtp, alog, BATCH, SEQ, 0)
    y_s, _, _ = _ssd(xs, bc, dtp, alog, DEC_BATCH, DEC_SEQ, N_PROMPT, h0=(i, h0_f, h0_b))
    dsk = jnp.repeat(d_skip, SSD_HEAD_DIM)[None]
    y_n_p = _ssd_gate(y_p, xs, proj, dsk, norm_g[None], 0)
    y_n_s = _ssd_gate(y_s, xs, proj, dsk, norm_g[None], N_PROMPT)
    return (o_att_p, o_att_s, y_n_p, y_n_s), w_out.astype(BF16), caches, (st_f, st_b)


def kernel(x_prompt, x_sample, cache_mla_ckv, cache_mla_kpe, cache_diff_k, cache_diff_v, cache_gqa_k, cache_gqa_v, state_ssd_fwd, state_ssd_bwd, c, c_ctx, w_mod, b_mod, ln1_g, ln1_b, ln2_g, ln2_b, ev_w_in, mla_q_norm, mla_kv_norm, mla_wq_b, mla_wkv_b, diff_lambda, diff_subln, ev_w_out, od_w_in, gqa_sink, ssd_conv_w, ssd_conv_b, ssd_dt_bias, ssd_a_log, ssd_d, ssd_norm, od_w_out, moe_router_group, moe_router_expert, moe_w_gate, moe_w_up, moe_w_down):
    x = jnp.concatenate([x_prompt.reshape(N_PROMPT, D_MODEL), x_sample.reshape(N_SAMPLE, D_MODEL)], axis=0)
    cs = jnp.concatenate([c_ctx[None], c, jnp.zeros((8 - 1 - DEC_BATCH, D_MODEL), F32)], axis=0)
    mods = _adaln(cs, w_mod, b_mod)[:, :1 + DEC_BATCH].reshape(DEPTH, 1 + DEC_BATCH, 6, 1, D_MODEL)
    tables_64 = _rope_tables(64)
    tables_128 = _rope_tables(GQA_HEAD_DIM)

    ev_w_in = ev_w_in.astype(BF16)
    od_w_in = od_w_in.astype(BF16)
    ev_caches, od_caches, ssd_states = (), (), []
    for l in range(DEPTH):
        i = l // 2
        mod = [mods[l, :, k] for k in range(6)]
        if l % 2 == 0:
            p = (ev_w_in, mla_q_norm[i], mla_kv_norm[i], mla_wq_b[i], mla_wkv_b[i], diff_lambda[i],
                 diff_subln[i], ev_w_out[i])
            ctx = (cache_mla_ckv, cache_mla_kpe, cache_diff_k, cache_diff_v)
            mix, w_out, ev_caches = _even_layer(x, mod, l, i, p, ctx, tables_64, ev_caches)
        else:
            p = (od_w_in, gqa_sink[i], ssd_conv_w[i], ssd_conv_b[i], ssd_dt_bias[i], ssd_a_log[i], ssd_d[i],
                 ssd_norm[i], od_w_out[i])
            ctx = (cache_gqa_k, cache_gqa_v, state_ssd_fwd, state_ssd_bwd)
            mix, w_out, od_caches, states = _odd_layer(x, mod, i, p, ctx, tables_128, od_caches)
            ssd_states.append(states)
        w_router = jnp.concatenate([moe_router_group[l], moe_router_expert[l],
                                    jnp.zeros((D_MODEL, LANES - MOE_GROUPS - MOE_EXPERTS), F32)], axis=1)
        w_router_hi = w_router.astype(BF16)
        w_router = jnp.stack([w_router_hi, (w_router - w_router_hi.astype(F32)).astype(BF16)])
        x, h2, route = _out_proj(*mix, w_out, x, mod[2], ln1_g[l][None], ln1_b[l][None], mod[3], mod[4], w_router)
        z = _moe_experts(l, h2, _route_meta(route), moe_w_gate, moe_w_up, moe_w_down)
        x = _combine(x, z, route, mod[5], ln2_g[l][None], ln2_b[l][None], split=(l == DEPTH - 1))

    y_prompt = x[0].reshape(BATCH, SEQ, D_MODEL)
    y_sample = x[1].reshape(DEC_BATCH, DEC_SEQ, D_MODEL)
    new_ssd_fwd = jnp.stack([s[0] for s in ssd_states], axis=1)
    new_ssd_bwd = jnp.stack([s[1] for s in ssd_states], axis=1)
    return (y_prompt, y_sample, *ev_caches, *od_caches, new_ssd_fwd, new_ssd_bwd)
```

```python
import functools
import math

import jax
import jax.numpy as jnp
from jax import lax
from jax.experimental import pallas as pl
from jax.experimental.pallas import tpu as pltpu

F32 = jnp.float32
BF16 = jnp.bfloat16

D_MODEL = 2048
BATCH = 16
SEQ = 256
DEPTH = 4
DEC_BATCH = 2
DEC_SEQ = 1024
PAST_LEN = 512
GRID_W = 64
ROPE_THETA = 10000.0
DEEPNORM_ALPHA = (2.0 * DEPTH) ** 0.25
LN_EPS = 1e-5
RMS_EPS = 1e-6

MLA_HEADS = 8
MLA_Q_RANK = 512
MLA_KV_RANK = 512
MLA_NOPE = 128
MLA_ROPE = 64
MLA_V = 128
DIFF_HEADS = 8
DIFF_QK = 64
DIFF_V = 128
GQA_HEADS = 8
GQA_KV_HEADS = 2
GQA_HEAD_DIM = 128
WINDOW = 128
SSD_HEADS = 16
SSD_HEAD_DIM = 64
SSD_D_INNER = SSD_HEADS * SSD_HEAD_DIM
SSD_GROUPS = 2
SSD_STATE = 128
SSD_CHUNK = 128
MOE_GROUPS = 4
MOE_EPG = 4
MOE_EXPERTS = 16
MOE_HIDDEN = 512

N_PROMPT = BATCH * SEQ
N_SAMPLE = DEC_BATCH * DEC_SEQ
N_TOK = N_PROMPT + N_SAMPLE
PROJ_N = 4224
LANES = 128
VMEM_LIMIT = 56 * 1024 * 1024

PROJ_TM, PROJ_TN = 1024, 1408
OUT_TM = 512
ATT_TQ = 256
ATT_HEADS_PER_STEP = 2
ATT_HEADS_PER_STEP_CTX = 4
POST_TM = 256
MOE_TM = 256
MOE_SLOTS = 2 * N_TOK
MOE_TILES = MOE_SLOTS // MOE_TM + MOE_EXPERTS
MOE_ROWS = MOE_TILES * MOE_TM
WEIGHT_DMA_PRIORITY = 1
SLAB = (D_MODEL // LANES, LANES)


def _cparams(sem):
    return pltpu.CompilerParams(dimension_semantics=sem, vmem_limit_bytes=VMEM_LIMIT)


def _mod_group(row0):
    return jnp.maximum(row0 // DEC_SEQ - (N_PROMPT // DEC_SEQ - 1), 0)


def _silu(x):
    return x / (1.0 + jnp.exp(-x))


def _nt(a, b):
    return lax.dot_general(a, b, (((1,), (1,)), ((), ())), preferred_element_type=F32)


def _rms(x, g):
    return x * lax.rsqrt(jnp.mean(x * x, axis=-1, keepdims=True) + RMS_EPS) * g


def _layer_norm(y, g, b):
    mu = jnp.mean(y, axis=-1, keepdims=True)
    yc = y - mu
    return yc * lax.rsqrt(jnp.mean(yc * yc, axis=-1, keepdims=True) + LN_EPS) * g + b


MOD_N = 6 * D_MODEL


def _adaln_tile(c_ref, w_ref, b_ref, o_ref):
    o_ref[...] = jnp.dot(_silu(c_ref[...]), w_ref[0], preferred_element_type=F32) + b_ref[0]


def _adaln_specs(layer, n_steps, step_of):
    tn = MOD_N // n_steps
    in_specs = [pl.BlockSpec((8, D_MODEL), lambda *g: (0, 0)),
                pl.BlockSpec((1, D_MODEL, tn), lambda *g: (layer, 0, step_of(*g))),
                pl.BlockSpec((1, 1, tn), lambda *g: (layer, 0, step_of(*g)))]
    return in_specs, pl.BlockSpec((8, tn), lambda *g: (0, step_of(*g))), jax.ShapeDtypeStruct((8, MOD_N), F32)


def _adaln(cs, w_mod, b_mod3, layer):
    n_steps = 12
    in_specs, out_spec, out_shape = _adaln_specs(layer, n_steps, lambda j: j)
    return pl.pallas_call(
        _adaln_tile,
        grid=(n_steps,),
        in_specs=in_specs,
        out_specs=out_spec,
        out_shape=out_shape,
        compiler_params=_cparams(("parallel",)),
        name="adaln",
    )(cs, w_mod, b_mod3)


def _proj_body(x_ref, sh_ref, sc_ref, w_ref, o_ref, h_scr):
    @pl.when(pl.program_id(1) == 0)
    def _():
        h = x_ref[...] * (1.0 + sc_ref[0]) + sh_ref[0]
        h_scr[...] = h.astype(BF16)

    o_ref[...] = jnp.dot(h_scr[...], w_ref[...], preferred_element_type=F32)


def _proj(x, shift, scale, w):
    tm, tn = PROJ_TM, PROJ_TN
    n = w.shape[1]
    mod_spec = pl.BlockSpec((1, 1, D_MODEL), lambda i, j: (_mod_group(i * tm), 0, 0))
    return pl.pallas_call(
        _proj_body,
        grid=(N_TOK // tm, n // tn),
        in_specs=[
            pl.BlockSpec((tm, D_MODEL), lambda i, j: (i, 0)),
            mod_spec,
            mod_spec,
            pl.BlockSpec((D_MODEL, tn), lambda i, j: (0, j)),
        ],
        out_specs=pl.BlockSpec((tm, tn), lambda i, j: (i, j)),
        out_shape=jax.ShapeDtypeStruct((N_TOK, n), F32),
        scratch_shapes=[pltpu.VMEM((tm, D_MODEL), BF16)],
        compiler_params=_cparams(("parallel", "arbitrary")),
        name="in_proj",
    )(x, shift, scale, w)


PROJ_BLOCKS = PROJ_N // LANES


def _prep_even_body(a_ref, b_ref, o_ref):
    c = pl.program_id(0)
    half = LANES // 2
    aligned = (MLA_Q_RANK + MLA_KV_RANK) // LANES

    @pl.when(c < aligned)
    def _():
        o_ref[...] = a_ref[0]

    @pl.when((c >= aligned) & (c < PROJ_BLOCKS - 1))
    def _():
        o_ref[...] = jnp.concatenate([a_ref[0][:, half:], b_ref[0][:, :half]], axis=1)

    @pl.when(c == PROJ_BLOCKS - 1)
    def _():
        k_pe = a_ref[0][:, :half]
        o_ref[...] = jnp.concatenate([k_pe, jnp.zeros_like(k_pe)], axis=1)


def _prep_even(w_in, layer):
    aligned = (MLA_Q_RANK + MLA_KV_RANK) // LANES
    last = PROJ_BLOCKS - 1
    return pl.pallas_call(
        _prep_even_body,
        grid=(PROJ_BLOCKS,),
        in_specs=[pl.BlockSpec((1, D_MODEL, LANES), lambda c: (layer, 0, jnp.where(c == last, aligned, c))),
                  pl.BlockSpec((1, D_MODEL, LANES), lambda c: (layer, 0, jnp.minimum(c + 1, last)))],
        out_specs=pl.BlockSpec((D_MODEL, LANES), lambda c: (0, c)),
        out_shape=jax.ShapeDtypeStruct((D_MODEL, PROJ_N), BF16),
        compiler_params=_cparams(("parallel",)),
        name="prep_w_even",
    )(w_in, w_in)


def _prep_odd_body(perm_ref, a_ref, dt_ref, o_ref):
    c = pl.program_id(0)

    @pl.when(c < PROJ_BLOCKS - 1)
    def _():
        o_ref[...] = a_ref[0]

    @pl.when(c == PROJ_BLOCKS - 1)
    def _():
        o_ref[...] = dt_ref[...]


def _prep_odd(w_in, layer):
    order = list(range(0, 8)) + list(range(20, 28)) + list(range(12, 20)) + [8, 9, 10, 11, 28, 29, 30, 31, 31]
    perm = jnp.asarray(order, jnp.int32)
    n_dt = 2 * SSD_HEADS
    w_dt = jnp.pad(w_in[layer, :, w_in.shape[2] - n_dt:], ((0, 0), (0, LANES - n_dt))).astype(BF16)
    return pl.pallas_call(
        _prep_odd_body,
        grid_spec=pltpu.PrefetchScalarGridSpec(
            num_scalar_prefetch=1,
            grid=(PROJ_BLOCKS,),
            in_specs=[pl.BlockSpec((1, D_MODEL, LANES), lambda c, perm_ref: (layer, 0, perm_ref[c])),
                      pl.BlockSpec((D_MODEL, LANES), lambda c, perm_ref: (0, 0))],
            out_specs=pl.BlockSpec((D_MODEL, LANES), lambda c, perm_ref: (0, c)),
        ),
        out_shape=jax.ShapeDtypeStruct((D_MODEL, PROJ_N), BF16),
        compiler_params=_cparams(("parallel",)),
        name="prep_w_odd",
    )(perm, w_in, w_dt)


def _mm_body(a_ref, b_ref, o_ref):
    o_ref[...] = jnp.dot(a_ref[...].astype(BF16), b_ref[...], preferred_element_type=F32).astype(o_ref.dtype)


def _mm(a, b, out_dtype, tm, n_tiles, row_block=lambda i: i):
    k = a.shape[1]
    n = b.shape[1]
    return pl.pallas_call(
        _mm_body,
        grid=(n_tiles,),
        in_specs=[pl.BlockSpec((tm, k), lambda i: (row_block(i), 0)), pl.BlockSpec((k, n), lambda i: (0, 0))],
        out_specs=pl.BlockSpec((tm, n), lambda i: (i, 0)),
        out_shape=jax.ShapeDtypeStruct((n_tiles * tm, n), out_dtype),
        compiler_params=_cparams(("parallel",)),
        name="mm_resident",
    )(a, b)


def _rope(x, cos, sin_signed, quarter):
    rows, w = x.shape
    reps = w // LANES
    if reps > 1:
        cos = jnp.concatenate([cos] * reps, axis=1)
        sin_signed = jnp.concatenate([sin_signed] * reps, axis=1)
    lane = lax.broadcasted_iota(jnp.int32, x.shape, 1)
    first = (lane % (2 * quarter)) < quarter
    up = pltpu.roll(x, w - quarter, 1)
    dn = pltpu.roll(x, quarter, 1)
    return x * cos + jnp.where(first, up, dn) * sin_signed


def _rope_tables(rdim):
    half = rdim // 2
    quarter = half // 2
    pos = jnp.arange(DEC_SEQ)
    row = (pos // GRID_W).astype(F32)
    col = (pos % GRID_W).astype(F32)
    inv = ROPE_THETA ** (-jnp.arange(quarter, dtype=F32) * 2.0 / half)
    lane = jnp.arange(LANES)
    r = lane % rdim
    use_col = (r // half) == 1
    j = r % quarter
    p = jnp.where(use_col[None, :], col[:, None], row[:, None])
    ang = p * inv[j][None, :]
    sign = jnp.where((r % half) < quarter, -1.0, 1.0).astype(F32)
    cos = jnp.cos(ang)
    sin = jnp.sin(ang) * sign[None, :]
    cos = jnp.concatenate([jnp.ones((N_PROMPT, LANES), F32), cos, cos], axis=0)
    sin = jnp.concatenate([jnp.zeros((N_PROMPT, LANES), F32), sin, sin], axis=0)
    return cos, sin


def _ev_post_body(n_prev, ql_ref, kvl_ref, dq_ref, dk_ref, dv_ref, kpe_ref, cos_ref, sin_ref, gq_ref, gkv_ref,
                  wq_ref, wkv_ref, *refs):
    prev = refs[:4 * n_prev]
    q_ref, kv_ref, dqr_ref, dkr_ref, dvb_ref, kper_ref, c_ckv, c_kpe, c_dk, c_dv = refs[4 * n_prev:]
    cos = cos_ref[...]
    sin = sin_ref[...]
    qn = _rms(ql_ref[...], gq_ref[...]).astype(BF16)
    q_ref[...] = jnp.dot(qn, wq_ref[...], preferred_element_type=F32).astype(BF16)
    ckv = _rms(kvl_ref[...], gkv_ref[...])
    kv_ref[...] = jnp.dot(ckv.astype(BF16), wkv_ref[...], preferred_element_type=F32).astype(BF16)
    dqr_ref[...] = _rope(dq_ref[...], cos, sin, DIFF_QK // 4).astype(BF16)
    dk = _rope(dk_ref[...], cos, sin, DIFF_QK // 4)
    dkr_ref[...] = dk.astype(BF16)
    dv = dv_ref[...]
    dvb_ref[...] = dv.astype(BF16)
    kpe = _rope(kpe_ref[...], cos, sin, MLA_ROPE // 4)
    kper_ref[...] = kpe.astype(BF16)

    @pl.when(pl.program_id(0) < BATCH)
    def _():
        if n_prev:
            c_ckv[0, 0] = prev[0][0, 0]
            c_kpe[0, 0] = prev[1][0, 0]
            c_dk[0, 0] = prev[2][0, 0]
            c_dv[0, 0] = prev[3][0, 0]
        c_ckv[0, n_prev] = ckv
        c_kpe[0, n_prev] = kpe[:, :MLA_ROPE]
        c_dk[0, n_prev] = dk.reshape(SEQ, DIFF_HEADS, 2 * DIFF_QK)
        c_dv[0, n_prev] = dv.reshape(SEQ, DIFF_HEADS, DIFF_V)


def _ev_post(proj, cos, sin, gq, gkv, wq, wkv, prev):
    tm = POST_TM
    n_prev = 1 if prev else 0
    nl = n_prev + 1
    row = lambda c: (lambda i: (i, c))
    seq4 = lambda i: (jnp.minimum(i, BATCH - 1), 0, 0, 0)
    seq5 = lambda i: (jnp.minimum(i, BATCH - 1), 0, 0, 0, 0)
    cache_shapes = [(MLA_KV_RANK,), (MLA_ROPE,), (DIFF_HEADS, 2 * DIFF_QK), (DIFF_HEADS, DIFF_V)]

    def cache_spec(layers, tail):
        return pl.BlockSpec((1, layers, SEQ) + tail, seq4 if len(tail) == 1 else seq5)

    return pl.pallas_call(
        functools.partial(_ev_post_body, n_prev),
        grid=(N_TOK // tm,),
        in_specs=[
            pl.BlockSpec((tm, 512), row(0)),
            pl.BlockSpec((tm, 512), row(1)),
            pl.BlockSpec((tm, 1024), row(1)),
            pl.BlockSpec((tm, 1024), row(2)),
            pl.BlockSpec((tm, 1024), row(3)),
            pl.BlockSpec((tm, LANES), row(32)),
            pl.BlockSpec((tm, LANES), row(0)),
            pl.BlockSpec((tm, LANES), row(0)),
            pl.BlockSpec((1, 512), lambda i: (0, 0)),
            pl.BlockSpec((1, 512), lambda i: (0, 0)),
            pl.BlockSpec(wq.shape, lambda i: (0, 0)),
            pl.BlockSpec(wkv.shape, lambda i: (0, 0)),
        ] + [cache_spec(1, t) for t in cache_shapes] * n_prev,
        out_specs=[
            pl.BlockSpec((tm, wq.shape[1]), row(0)),
            pl.BlockSpec((tm, wkv.shape[1]), row(0)),
            pl.BlockSpec((tm, 1024), row(0)),
            pl.BlockSpec((tm, 1024), row(0)),
            pl.BlockSpec((tm, 1024), row(0)),
            pl.BlockSpec((tm, LANES), row(0)),
        ] + [cache_spec(nl, t) for t in cache_shapes],
        out_shape=[
            jax.ShapeDtypeStruct((N_TOK, wq.shape[1]), BF16),
            jax.ShapeDtypeStruct((N_TOK, wkv.shape[1]), BF16),
            jax.ShapeDtypeStruct((N_TOK, 1024), BF16),
            jax.ShapeDtypeStruct((N_TOK, 1024), BF16),
            jax.ShapeDtypeStruct((N_TOK, 1024), BF16),
            jax.ShapeDtypeStruct((N_TOK, LANES), BF16),
        ] + [jax.ShapeDtypeStruct((BATCH, nl, SEQ) + t, F32) for t in cache_shapes],
        compiler_params=_cparams(("arbitrary",)),
        name="even_post",
    )(proj, proj, proj, proj, proj, proj, cos, sin, gq, gkv, wq, wkv, *prev)


def _softmax_av(scores, values):
    m = functools.reduce(jnp.maximum, [jnp.max(s, axis=-1, keepdims=True) for s in scores])
    es = [jnp.exp(s - m) for s in scores]
    l = sum(jnp.sum(e, axis=-1, keepdims=True) for e in es)
    o = sum(jnp.dot(e.astype(BF16), v, preferred_element_type=F32) for e, v in zip(es, values))
    return o / l


def _even_attn_body(lam_init, has_ctx, has_mod, hps, q_ref, cos_ref, sin_ref, kv_ref, kpe_ref, dq_ref, dk_ref, dv_ref,
                    lam_ref, sub_ref, *refs):
    refs = list(refs)
    if has_ctx:
        kvc_ref, kpec_ref, dkc_ref, dvc_ref = refs[:4]
        refs = refs[4:]
    if has_mod:
        _adaln_tile(*refs[:3], refs[-1])
        refs = refs[3:-1]
    omla_ref, odiff_ref = refs
    cos = cos_ref[...]
    sin = sin_ref[...]
    lam = lam_ref[...]
    lam_full = (jnp.exp(jnp.sum(lam[0:1] * lam[1:2], axis=-1, keepdims=True))
                - jnp.exp(jnp.sum(lam[2:3] * lam[3:4], axis=-1, keepdims=True)) + lam_init)
    mla_scale = (MLA_NOPE + MLA_ROPE) ** -0.5
    lane = lax.broadcasted_iota(jnp.int32, (q_ref.shape[0], 2 * DIFF_QK), 1)
    for u in range(hps):
        qc = slice(u * 256, (u + 1) * 256)
        dc = slice(u * LANES, (u + 1) * LANES)
        q = q_ref[:, qc]
        qr = _rope(q[:, MLA_NOPE:].astype(F32), cos, sin, MLA_ROPE // 4).astype(BF16)
        qcat = jnp.concatenate([q[:, :MLA_NOPE], qr], axis=1)
        kv = kv_ref[:, qc]
        scores = [_nt(qcat, jnp.concatenate([kv[:, :MLA_NOPE], kpe_ref[...]], axis=1)) * mla_scale]
        values = [kv[:, MLA_NOPE:]]
        if has_ctx:
            kvc = kvc_ref[:, qc]
            scores.append(_nt(qcat, jnp.concatenate([kvc[:, :MLA_NOPE], kpec_ref[...]], axis=1)) * mla_scale)
            values.append(kvc[:, MLA_NOPE:])
        omla_ref[:, dc] = _softmax_av(scores, values).astype(BF16)

        dq = dq_ref[:, dc] * jnp.asarray(DIFF_QK ** -0.5, BF16)
        zero = jnp.zeros_like(dq)
        keys = [dk_ref[:, dc]]
        values = [dv_ref[:, dc]]
        if has_ctx:
            keys.append(dkc_ref[:, dc])
            values.append(dvc_ref[:, dc])
        a1 = _softmax_av([_nt(jnp.where(lane < DIFF_QK, dq, zero), k) for k in keys], values)
        a2 = _softmax_av([_nt(jnp.where(lane >= DIFF_QK, dq, zero), k) for k in keys], values)
        odiff_ref[:, dc] = (_rms(a1 - lam_full * a2, sub_ref[...]) * (1.0 - lam_init)).astype(BF16)


def _even_attn(lam_init, q, cos, sin, kv, kpe, dq, dk, dv, lam, subln, n_seq, seq, row_off, ctx=None, adaln=None):
    tq = ATT_TQ
    hps = ATT_HEADS_PER_STEP if ctx is not None else ATT_HEADS_PER_STEP_CTX
    nq = seq // tq
    q0 = row_off // tq
    s0 = row_off // seq
    qspec = lambda w: pl.BlockSpec((tq, hps * w), lambda b, j, h: (q0 + b * nq + j, h))
    kspec = lambda w: pl.BlockSpec((seq, hps * w), lambda b, j, h: (s0 + b, h))
    tspec = pl.BlockSpec((tq, LANES), lambda b, j, h: (q0 + b * nq + j, 0))
    ospec = pl.BlockSpec((tq, hps * LANES), lambda b, j, h: (b * nq + j, h))
    in_specs = [qspec(256), tspec, tspec, kspec(256),
                pl.BlockSpec((seq, LANES), lambda b, j, h: (s0 + b, 0)),
                qspec(LANES), kspec(LANES), kspec(LANES),
                pl.BlockSpec((4, DIFF_QK), lambda b, j, h: (0, 0)),
                pl.BlockSpec((1, DIFF_V), lambda b, j, h: (0, 0))]
    args = [q, cos, sin, kv, kpe, dq, dk, dv, lam, subln]
    if ctx is not None:
        cspec = lambda w: pl.BlockSpec((PAST_LEN, hps * w), lambda b, j, h: (b, h))
        in_specs += [cspec(256), pl.BlockSpec((PAST_LEN, LANES), lambda b, j, h: (b, 0)), cspec(LANES), cspec(LANES)]
        args += list(ctx)
    nh = MLA_HEADS // hps
    out_specs = [ospec, ospec]
    out_shape = [jax.ShapeDtypeStruct((n_seq * seq, 1024), BF16)] * 2
    if adaln is not None:
        mod_layer, mod_args = adaln
        m_in, m_out, m_shape = _adaln_specs(mod_layer, n_seq * nq * nh, lambda b, j, h: (b * nq + j) * nh + h)
        in_specs += m_in
        args += list(mod_args)
        out_specs.append(m_out)
        out_shape.append(m_shape)
    return pl.pallas_call(
        functools.partial(_even_attn_body, lam_init, ctx is not None, adaln is not None, hps),
        grid=(n_seq, nq, nh),
        in_specs=in_specs,
        out_specs=out_specs,
        out_shape=out_shape,
        compiler_params=_cparams(("parallel", "parallel", "parallel")),
        name="even_attn",
    )(*args)


def _route(logits):
    lane = lax.broadcasted_iota(jnp.int32, logits.shape, 1).astype(F32)
    neg = -jnp.inf
    none = float(LANES)
    is_g = lane < MOE_GROUPS
    lg = jnp.where(is_g, logits, neg)
    mg = jnp.max(lg, axis=-1, keepdims=True)
    g_val = 1.0 / jnp.sum(jnp.exp(lg - mg), axis=-1, keepdims=True)
    g_idx = jnp.min(jnp.where(is_g & (lg == mg), lane, none), axis=-1, keepdims=True)
    e_lo = MOE_GROUPS + g_idx * MOE_EPG
    is_e = (lane >= e_lo) & (lane < e_lo + MOE_EPG)
    le = jnp.where(is_e, logits, neg)
    p = jnp.exp(le - jnp.max(le, axis=-1, keepdims=True))
    p1 = jnp.max(p, axis=-1, keepdims=True)
    i1 = jnp.min(jnp.where(is_e & (p == p1), lane, none), axis=-1, keepdims=True)
    rest = is_e & (lane != i1)
    p2 = jnp.max(jnp.where(rest, p, neg), axis=-1, keepdims=True)
    i2 = jnp.min(jnp.where(rest & (p == p2), lane, none), axis=-1, keepdims=True)
    s = g_val / (p1 + p2)
    return jnp.where(lane == 0.0, i1 - MOE_GROUPS,
                     jnp.where(lane == 1.0, i2 - MOE_GROUPS,
                               jnp.where(lane == 2.0, p1 * s, jnp.where(lane == 3.0, p2 * s, 0.0))))


def _out_proj_body(a1p_ref, a1s_ref, a2p_ref, a2s_ref, w_ref, x_ref, gate_ref, g_ref, b_ref, sh_ref, sc_ref,
                   wr_ref, xo_ref, h_ref, comb_ref):
    is_prompt = pl.program_id(0) < N_PROMPT // OUT_TM
    a1 = jnp.where(is_prompt, a1p_ref[...], a1s_ref[...])
    a2 = jnp.where(is_prompt, a2p_ref[...], a2s_ref[...])
    k1 = a1.shape[1]
    acc = jnp.dot(a1, w_ref[:k1, :], preferred_element_type=F32)
    acc = acc + jnp.dot(a2, w_ref[k1:, :], preferred_element_type=F32)
    xn = _layer_norm(DEEPNORM_ALPHA * x_ref[...] + gate_ref[0] * acc, g_ref[...], b_ref[...])
    xo_ref[...] = xn
    h = xn * (1.0 + sc_ref[0]) + sh_ref[0]
    h_ref[...] = h.reshape(h_ref.shape).astype(BF16)
    h_hi = h.astype(BF16)
    h_lo = (h - h_hi.astype(F32)).astype(BF16)
    logits = (jnp.dot(h_hi, wr_ref[0], preferred_element_type=F32)
              + jnp.dot(h_lo, wr_ref[0], preferred_element_type=F32)
              + jnp.dot(h_hi, wr_ref[1], preferred_element_type=F32))
    comb_ref[...] = _route(logits)


def _out_proj(a1p, a1s, a2p, a2s, w, x, gate, ln_g, ln_b, shift, scale, w_router):
    tm = OUT_TM
    np_tiles = N_PROMPT // tm
    k1, k2 = a1p.shape[1], a2p.shape[1]
    mod_spec = pl.BlockSpec((1, 1, D_MODEL), lambda i: (_mod_group(i * tm), 0, 0))
    vec_spec = pl.BlockSpec((1, D_MODEL), lambda i: (0, 0))
    row_spec = pl.BlockSpec((tm, D_MODEL), lambda i: (i, 0))
    pspec = lambda k: pl.BlockSpec((tm, k), lambda i: (jnp.minimum(i, np_tiles - 1), 0))
    sspec = lambda k: pl.BlockSpec((tm, k), lambda i: (jnp.maximum(i - np_tiles, 0), 0))
    return pl.pallas_call(
        _out_proj_body,
        grid=(N_TOK // tm,),
        in_specs=[
            pspec(k1), sspec(k1), pspec(k2), sspec(k2),
            pl.BlockSpec((k1 + k2, D_MODEL), lambda i: (0, 0)),
            row_spec, mod_spec, vec_spec, vec_spec, mod_spec, mod_spec,
            pl.BlockSpec((2, D_MODEL, LANES), lambda i: (0, 0, 0)),
        ],
        out_specs=[row_spec, pl.BlockSpec((tm,) + SLAB, lambda i: (i, 0, 0)),
                   pl.BlockSpec((tm, LANES), lambda i: (i, 0))],
        out_shape=[
            jax.ShapeDtypeStruct((N_TOK, D_MODEL), F32),
            jax.ShapeDtypeStruct((N_TOK,) + SLAB, BF16),
            jax.ShapeDtypeStruct((N_TOK, LANES), F32),
        ],
        compiler_params=_cparams(("parallel",)),
        name="out_proj_ln_router",
    )(a1p, a1s, a2p, a2s, w, x, gate, ln_g, ln_b, shift, scale, w_router)


def _route_meta(route):
    e = route[:, :2].astype(jnp.int32).reshape(-1)
    onehot = (e[:, None] == jnp.arange(MOE_EXPERTS, dtype=jnp.int32)[None, :]).astype(jnp.int32)
    csum = jnp.cumsum(onehot, axis=0)
    rank = jnp.sum((csum - onehot) * onehot, axis=1)
    count = csum[-1]
    ntile = (count + MOE_TM - 1) // MOE_TM
    tile_end = jnp.cumsum(ntile)
    tile_off = tile_end - ntile
    pos = jnp.sum(onehot * tile_off[None, :], axis=1) * MOE_TM + rank
    n_used = tile_end[-1:]
    j = jnp.arange(MOE_TILES, dtype=jnp.int32)
    tile_expert = jnp.sum((tile_end[None, :] <= jnp.minimum(j, n_used - 1)[:, None]).astype(jnp.int32), axis=1)
    is_e = tile_expert[:, None] == jnp.arange(MOE_EXPERTS, dtype=jnp.int32)[None, :]
    left = jnp.sum(jnp.where(is_e, (count - (j[:, None] - tile_off[None, :]) * MOE_TM)[...], 0), axis=1)
    n_valid = jnp.where(j < n_used, jnp.clip(left, 0, MOE_TM), 0)
    ids = jnp.arange(MOE_EXPERTS, dtype=jnp.int32)
    later = (ids[None, :] > ids[:, None]) & (ntile[None, :] > 0)
    next_of = jnp.min(jnp.where(later, ids[None, :], MOE_EXPERTS), axis=1)
    next_of = jnp.where(next_of == MOE_EXPERTS, -1, next_of)
    next_expert = jnp.sum(jnp.where(is_e, next_of[None, :], 0), axis=1)
    ordinal = jnp.cumsum((ntile > 0).astype(jnp.int32)) - 1
    w_slot = jnp.sum(jnp.where(is_e, ordinal[None, :], 0), axis=1) % 2
    return pos, tile_expert, n_valid, n_used, next_expert, w_slot


def _moe_body(layer, pos_ref, te_ref, nv_ref, nu_ref, ne_ref, ws_ref, h_ref, wg_ref, wu_ref, wd_ref, z_ref,
              inv, xbuf, zbuf, wgbuf, wubuf, wdbuf, wg_s, wu_s, wd_s, gsem, ssem, wsem):
    j = pl.program_id(0)
    slot = j % 2
    n_used = nu_ref[0]
    group = 8

    def weight_copies(e, s):
        return [pltpu.make_async_copy(wg_ref.at[layer, e], wgbuf.at[s], wsem.at[s]),
                pltpu.make_async_copy(wu_ref.at[layer, e], wubuf.at[s], wsem.at[s]),
                pltpu.make_async_copy(wd_ref.at[layer, e], wdbuf.at[s], wsem.at[s])]

    def gather_row(tile, s, r):
        tok = lax.shift_right_logical(inv[tile * MOE_TM + r], 1)
        return pltpu.make_async_copy(h_ref.at[tok], xbuf.at[s, r], gsem.at[s])

    def scatter_row(tile, s, r):
        sl = inv[tile * MOE_TM + r]
        dst = (sl & 1) * N_TOK + lax.shift_right_logical(sl, 1)
        return pltpu.make_async_copy(zbuf.at[s, r], z_ref.at[dst], ssem.at[s])

    def for_valid_rows(tile, fn):
        n = nv_ref[tile]

        def block(g, c):
            for u in range(group):
                fn(g * group + u, u % 2)
            return c

        def single(r, c):
            fn(r, 0)
            return c

        full = lax.shift_right_logical(n, 3)
        lax.fori_loop(0, full, block, 0)
        lax.fori_loop(full * group, n, single, 0)

    @pl.when(j == 0)
    def _():
        def fill(g, c):
            for u in range(group):
                sl = g * group + u
                inv[pos_ref[sl]] = sl
            return c
        lax.fori_loop(0, MOE_SLOTS // group, fill, 0)
        xbuf[...] = jnp.zeros_like(xbuf)
        for_valid_rows(0, lambda r, pr: gather_row(0, 0, r).start(priority=pr))
        for cp in weight_copies(te_ref[0], ws_ref[0]):
            cp.start(priority=WEIGHT_DMA_PRIORITY)

    changed = (j == 0) | (te_ref[j] != te_ref[jnp.maximum(j - 1, 0)])

    @pl.when(changed)
    def _():
        ws = ws_ref[j]
        for cp in weight_copies(te_ref[j], ws):
            cp.wait()
        wg_s[...] = wgbuf[ws].astype(BF16)
        wu_s[...] = wubuf[ws].astype(BF16)
        wd_s[...] = wdbuf[ws].astype(BF16)

        @pl.when(ne_ref[j] >= 0)
        def _():
            for cp in weight_copies(ne_ref[j], 1 - ws):
                cp.start(priority=WEIGHT_DMA_PRIORITY)

    @pl.when(j < n_used)
    def _():
        @pl.when(j + 1 < n_used)
        def _():
            for_valid_rows(j + 1, lambda r, pr: gather_row(j + 1, 1 - slot, r).start(priority=pr))

        for_valid_rows(j, lambda r, pr: gather_row(j, slot, r).wait())

        @pl.when(j >= 2)
        def _():
            for_valid_rows(j - 2, lambda r, pr: scatter_row(j - 2, slot, r).wait())

        x = xbuf[slot].reshape(MOE_TM, D_MODEL)
        hg = jnp.dot(x, wg_s[...], preferred_element_type=F32)
        hu = jnp.dot(x, wu_s[...], preferred_element_type=F32)
        act = (_silu(hg) * hu).astype(BF16)
        z = jnp.dot(act, wd_s[...], preferred_element_type=F32)
        zbuf[slot] = z.reshape((MOE_TM,) + SLAB).astype(BF16)
        for_valid_rows(j, lambda r, pr: scatter_row(j, slot, r).start(priority=pr))

        @pl.when(j == n_used - 1)
        def _():
            @pl.when(j >= 1)
            def _():
                for_valid_rows(j - 1, lambda r, pr: scatter_row(j - 1, 1 - slot, r).wait())
            for_valid_rows(j, lambda r, pr: scatter_row(j, slot, r).wait())


def _moe_experts(layer, h, meta, wg, wu, wd):
    any_spec = pl.BlockSpec(memory_space=pl.ANY)
    up_shape, down_shape = (D_MODEL, MOE_HIDDEN), (MOE_HIDDEN, D_MODEL)
    return pl.pallas_call(
        functools.partial(_moe_body, layer),
        grid_spec=pltpu.PrefetchScalarGridSpec(
            num_scalar_prefetch=len(meta),
            grid=(MOE_TILES,),
            in_specs=[any_spec, any_spec, any_spec, any_spec],
            out_specs=any_spec,
            scratch_shapes=[pltpu.SMEM((MOE_ROWS,), jnp.int32),
                            pltpu.VMEM((2, MOE_TM) + SLAB, BF16), pltpu.VMEM((2, MOE_TM) + SLAB, BF16),
                            pltpu.VMEM((2,) + up_shape, F32), pltpu.VMEM((2,) + up_shape, F32),
                            pltpu.VMEM((2,) + down_shape, F32),
                            pltpu.VMEM(up_shape, BF16), pltpu.VMEM(up_shape, BF16), pltpu.VMEM(down_shape, BF16),
                            pltpu.SemaphoreType.DMA((2,)), pltpu.SemaphoreType.DMA((2,)),
                            pltpu.SemaphoreType.DMA((2,))],
        ),
        out_shape=jax.ShapeDtypeStruct((2 * N_TOK,) + SLAB, BF16),
        compiler_params=_cparams(("arbitrary",)),
        name="moe_experts",
    )(*meta, h, wg, wu, wd)


def _combine_body(split, x_ref, r_ref, gate_ref, g_ref, b_ref, z1_ref, z2_ref, *o_refs):
    r = r_ref[...]
    z1 = z1_ref[...].reshape(x_ref.shape).astype(F32)
    z2 = z2_ref[...].reshape(x_ref.shape).astype(F32)
    y = r[:, 2:3] * z1 + r[:, 3:4] * z2
    out = _layer_norm(DEEPNORM_ALPHA * x_ref[...] + gate_ref[0] * y, g_ref[...], b_ref[...])
    if split:
        is_prompt = pl.program_id(0) < N_PROMPT // x_ref.shape[0]

        @pl.when(is_prompt)
        def _():
            o_refs[0][...] = out

        @pl.when(jnp.logical_not(is_prompt))
        def _():
            o_refs[1][...] = out
    else:
        o_refs[0][...] = out


def _combine(x, z, route, gate, ln_g, ln_b, split=False):
    tm = 512
    nt = N_TOK // tm
    npt = N_PROMPT // tm
    row_spec = pl.BlockSpec((tm, D_MODEL), lambda i: (i, 0))
    vec_spec = pl.BlockSpec((1, D_MODEL), lambda i: (0, 0))
    if split:
        out_specs = [pl.BlockSpec((tm, D_MODEL), lambda i: (jnp.minimum(i, npt - 1), 0)),
                     pl.BlockSpec((tm, D_MODEL), lambda i: (jnp.maximum(i - npt, 0), 0))]
        out_shape = [jax.ShapeDtypeStruct((N_PROMPT, D_MODEL), F32), jax.ShapeDtypeStruct((N_SAMPLE, D_MODEL), F32)]
    else:
        out_specs = row_spec
        out_shape = jax.ShapeDtypeStruct((N_TOK, D_MODEL), F32)
    return pl.pallas_call(
        functools.partial(_combine_body, split),
        grid=(nt,),
        in_specs=[row_spec, pl.BlockSpec((tm, LANES), lambda i: (i, 0)),
                  pl.BlockSpec((1, 1, D_MODEL), lambda i: (_mod_group(i * tm), 0, 0)),
                  vec_spec, vec_spec, pl.BlockSpec((tm,) + SLAB, lambda i: (i, 0, 0)),
                  pl.BlockSpec((tm,) + SLAB, lambda i: (nt + i, 0, 0))],
        out_specs=out_specs,
        out_shape=out_shape,
        compiler_params=_cparams(("arbitrary",)),
        name="moe_combine_ln",
    )(x, route, gate, ln_g, ln_b, z, z)


def _od_post_body(n_prev, q_ref, k_ref, v_ref, xs_ref, xs_lo_ref, xs_hi_ref, bc_ref, bc_lo_ref, bc_hi_ref, dt_ref,
                  cos_ref, sin_ref, cw_ref, cb_ref, dtb_ref, *refs):
    prev = refs[:2 * n_prev]
    qr_ref, kr_ref, vb_ref, xc_ref, bcc_ref, dtp_ref, c_k, c_v = refs[2 * n_prev:]
    i = pl.program_id(0)
    cos = cos_ref[...]
    sin = sin_ref[...]
    k = k_ref[...]
    v = v_ref[...]
    qr_ref[...] = _rope(q_ref[...], cos, sin, GQA_HEAD_DIM // 4).astype(BF16)
    kr_ref[...] = _rope(k, cos, sin, GQA_HEAD_DIM // 4).astype(BF16)
    vb_ref[...] = v.astype(BF16)

    is_prompt = i < BATCH
    tile_in_seq = (i - BATCH) % (DEC_SEQ // POST_TM)
    seq_start = is_prompt | (tile_in_seq == 0)
    seq_end = is_prompt | (tile_in_seq == DEC_SEQ // POST_TM - 1)

    def conv_silu(x, lo_ref, hi_ref, w, b):
        rows = x.shape[0]
        r = lax.broadcasted_iota(jnp.int32, x.shape, 0)
        lo = jnp.where(seq_start, 0.0, lo_ref[7:8, :])
        hi = jnp.where(seq_end, 0.0, hi_ref[0:1, :])
        before = jnp.where(r == 0, lo, pltpu.roll(x, 1, 0))
        after = jnp.where(r == rows - 1, hi, pltpu.roll(x, rows - 1, 0))
        return _silu(before * w[0:1] + x * w[1:2] + after * w[2:3] + b)

    xc_ref[...] = conv_silu(xs_ref[...], xs_lo_ref, xs_hi_ref, cw_ref[:, :SSD_D_INNER], cb_ref[:, :SSD_D_INNER])
    bcc_ref[...] = conv_silu(bc_ref[...], bc_lo_ref, bc_hi_ref, cw_ref[:, SSD_D_INNER:], cb_ref[:, SSD_D_INNER:])
    t = dt_ref[...] + dtb_ref[...]
    dtp_ref[...] = jnp.maximum(t, 0.0) + jnp.log1p(jnp.exp(-jnp.abs(t)))

    @pl.when(is_prompt)
    def _():
        if n_prev:
            c_k[0, 0] = prev[0][0, 0]
            c_v[0, 0] = prev[1][0, 0]
        c_k[0, n_prev] = k.reshape(SEQ, GQA_KV_HEADS, GQA_HEAD_DIM)
        c_v[0, n_prev] = v.reshape(SEQ, GQA_KV_HEADS, GQA_HEAD_DIM)


def _od_post(proj, cos, sin, conv_w, conv_b, dt_bias, prev):
    tm = POST_TM
    halo = 8
    per = tm // halo
    n_halo = N_TOK // halo
    n_prev = 1 if prev else 0
    blk = lambda w, c: pl.BlockSpec((tm, w), lambda i: (i, c))
    lo = lambda w, c: pl.BlockSpec((halo, w), lambda i: (jnp.maximum(i * per - 1, 0), c))
    hi = lambda w, c: pl.BlockSpec((halo, w), lambda i: (jnp.minimum((i + 1) * per, n_halo - 1), c))
    full = lambda a: pl.BlockSpec(a.shape, lambda i: (0, 0))
    oblk = lambda w: pl.BlockSpec((tm, w), lambda i: (i, 0))
    cache_spec = lambda layers: pl.BlockSpec((1, layers, SEQ, GQA_KV_HEADS, GQA_HEAD_DIM),
                                             lambda i: (jnp.minimum(i, BATCH - 1), 0, 0, 0, 0))
    cache_shape = jax.ShapeDtypeStruct((BATCH, n_prev + 1, SEQ, GQA_KV_HEADS, GQA_HEAD_DIM), F32)
    return pl.pallas_call(
        functools.partial(_od_post_body, n_prev),
        grid=(N_TOK // tm,),
        in_specs=[blk(1024, 0), blk(256, 12), blk(256, 13),
                  blk(1024, 1), lo(1024, 1), hi(1024, 1), blk(512, 7), lo(512, 7), hi(512, 7), blk(LANES, 32),
                  blk(LANES, 0), blk(LANES, 0), full(conv_w), full(conv_b), full(dt_bias)]
        + [cache_spec(1)] * (2 * n_prev),
        out_specs=[oblk(1024), oblk(256), oblk(256), oblk(1024), oblk(512), oblk(LANES),
                   cache_spec(n_prev + 1), cache_spec(n_prev + 1)],
        out_shape=[
            jax.ShapeDtypeStruct((N_TOK, 1024), BF16),
            jax.ShapeDtypeStruct((N_TOK, 256), BF16),
            jax.ShapeDtypeStruct((N_TOK, 256), BF16),
            jax.ShapeDtypeStruct((N_TOK, 1024), F32),
            jax.ShapeDtypeStruct((N_TOK, 512), F32),
            jax.ShapeDtypeStruct((N_TOK, LANES), F32),
            cache_shape, cache_shape,
        ],
        compiler_params=_cparams(("arbitrary",)),
        name="odd_post",
    )(proj, proj, proj, proj, proj, proj, proj, proj, proj, proj, cos, sin, conv_w, conv_b, dt_bias, *prev)


def _sink_column(sink_ref, kh, rows):
    g = GQA_HEADS // GQA_KV_HEADS
    return jnp.concatenate(
        [jnp.broadcast_to(sink_ref[kh * g + j:kh * g + j + 1, 0:1], (rows, 1)) for j in range(g)], axis=0)


def _gqa_prompt_body(q_ref, k_ref, v_ref, sink_ref, o_ref):
    g = GQA_HEADS // GQA_KV_HEADS
    d = GQA_HEAD_DIM
    rows = q_ref.shape[0]
    scale = d ** -0.5
    outs = []
    for kh in range(GQA_KV_HEADS):
        qg = jnp.concatenate([q_ref[:, (kh * g + j) * d:(kh * g + j + 1) * d] for j in range(g)], axis=0)
        k = k_ref[:, kh * d:(kh + 1) * d]
        v = v_ref[:, kh * d:(kh + 1) * d]
        s = _nt(qg, k) * scale
        snk = _sink_column(sink_ref, kh, rows)
        m = jnp.maximum(jnp.max(s, axis=-1, keepdims=True), snk)
        e = jnp.exp(s - m)
        l = jnp.sum(e, axis=-1, keepdims=True) + jnp.exp(snk - m)
        o = jnp.dot(e.astype(BF16), v, preferred_element_type=F32) / l
        outs.extend(o[j * rows:(j + 1) * rows] for j in range(g))
    o_ref[...] = jnp.concatenate(outs, axis=1).astype(BF16)


def _gqa_prompt(q, k, v, sink, adaln=None):
    spec = lambda w: pl.BlockSpec((SEQ, w), lambda i: (i, 0))
    in_specs = [spec(1024), spec(256), spec(256), pl.BlockSpec((8, LANES), lambda i: (0, 0))]
    args = [q, k, v, sink]
    out_specs = [spec(1024)]
    out_shape = [jax.ShapeDtypeStruct((N_PROMPT, 1024), BF16)]
    body = _gqa_prompt_body
    if adaln is not None:
        mod_layer, mod_args = adaln
        m_in, m_out, m_shape = _adaln_specs(mod_layer, BATCH, lambda i: i)
        in_specs += m_in
        args += list(mod_args)
        out_specs.append(m_out)
        out_shape.append(m_shape)

        def body(q_ref, k_ref, v_ref, sink_ref, c_ref, w_ref, b_ref, o_ref, mo_ref):
            _gqa_prompt_body(q_ref, k_ref, v_ref, sink_ref, o_ref)
            _adaln_tile(c_ref, w_ref, b_ref, mo_ref)

    return pl.pallas_call(
        body,
        grid=(BATCH,),
        in_specs=in_specs,
        out_specs=out_specs,
        out_shape=out_shape,
        compiler_params=_cparams(("parallel",)),
        name="gqa_dense",
    )(*args)


def _gqa_window_body(q_ref, k_ref, v_ref, kc_ref, vc_ref, sink_ref, o_ref):
    g = GQA_HEADS // GQA_KV_HEADS
    d = GQA_HEAD_DIM
    blk = q_ref.shape[0]
    span = 3 * blk
    n = pl.program_id(1)
    start = pl.multiple_of(jnp.clip((n - 1) * blk, 0, DEC_SEQ - span), blk)
    scale = d ** -0.5
    qpos = n * blk + lax.broadcasted_iota(jnp.int32, (blk, span), 0)
    kpos = start + lax.broadcasted_iota(jnp.int32, (blk, span), 1)
    valid1 = jnp.abs(qpos - kpos) <= WINDOW
    valid = jnp.concatenate([valid1] * g, axis=0)
    outs = []
    for kh in range(GQA_KV_HEADS):
        qg = jnp.concatenate([q_ref[:, (kh * g + j) * d:(kh * g + j + 1) * d] for j in range(g)], axis=0)
        kl = k_ref[pl.ds(start, span), kh * d:(kh + 1) * d]
        vl = v_ref[pl.ds(start, span), kh * d:(kh + 1) * d]
        kc = kc_ref[:, kh * d:(kh + 1) * d]
        vc = vc_ref[:, kh * d:(kh + 1) * d]
        sl = jnp.where(valid, _nt(qg, kl) * scale, -jnp.inf)
        sc = _nt(qg, kc) * scale
        snk = _sink_column(sink_ref, kh, blk)
        m = jnp.maximum(jnp.maximum(jnp.max(sl, axis=-1, keepdims=True), jnp.max(sc, axis=-1, keepdims=True)), snk)
        el = jnp.exp(sl - m)
        ec = jnp.exp(sc - m)
        l = jnp.sum(el, axis=-1, keepdims=True) + jnp.sum(ec, axis=-1, keepdims=True) + jnp.exp(snk - m)
        o = (jnp.dot(el.astype(BF16), vl, preferred_element_type=F32)
             + jnp.dot(ec.astype(BF16), vc, preferred_element_type=F32)) / l
        outs.extend(o[j * blk:(j + 1) * blk] for j in range(g))
    o_ref[...] = jnp.concatenate(outs, axis=1).astype(BF16)


def _gqa_window(q, k, v, cache_k, cache_v, sink):
    blk = WINDOW
    nq = DEC_SEQ // blk
    q0 = N_PROMPT // blk
    s0 = N_PROMPT // DEC_SEQ
    seq_spec = pl.BlockSpec((DEC_SEQ, 256), lambda b, j: (s0 + b, 0))
    ctx_spec = pl.BlockSpec((PAST_LEN, 256), lambda b, j: (b, 0))
    return pl.pallas_call(
        _gqa_window_body,
        grid=(DEC_BATCH, nq),
        in_specs=[pl.BlockSpec((blk, 1024), lambda b, j: (q0 + b * nq + j, 0)), seq_spec, seq_spec, ctx_spec,
                  ctx_spec,
                  pl.BlockSpec((8, LANES), lambda b, j: (0, 0))],
        out_specs=pl.BlockSpec((blk, 1024), lambda b, j: (b * nq + j, 0)),
        out_shape=jax.ShapeDtypeStruct((N_SAMPLE, 1024), BF16),
        compiler_params=_cparams(("parallel", "parallel")),
        name="gqa_window",
    )(q, k, v, cache_k, cache_v, sink)


def _expand(x, sel):
    hi = x.astype(BF16)
    r1 = x - hi.astype(F32)
    mid = r1.astype(BF16)
    lo = (r1 - mid.astype(F32)).astype(BF16)
    return (jnp.dot(hi, sel, preferred_element_type=F32) + jnp.dot(mid, sel, preferred_element_type=F32)
            + jnp.dot(lo, sel, preferred_element_type=F32))


def _ssd_body(has_h0, xs_ref, bc_ref, dt_ref, alog_ref, e1_ref, e2_ref, *refs):
    if has_h0:
        h0f_ref, h0b_ref, y_ref, stf_ref, stb_ref = refs
    else:
        y_ref, stf_ref, stb_ref = refs
    d = pl.program_id(1)
    c = pl.program_id(2)
    q = SSD_CHUNK
    ppg = SSD_HEADS // SSD_GROUPS // 2
    hd = SSD_HEAD_DIM

    row = lax.broadcasted_iota(jnp.int32, (q, LANES), 0)
    a = -jnp.exp(alog_ref[...])
    dta = dt_ref[...] * a

    def scan(x, rev):
        k = 1
        while k < q:
            if rev:
                x = x + jnp.where(row < q - k, pltpu.roll(x, q - k, 0), 0.0)
            else:
                x = x + jnp.where(row >= k, pltpu.roll(x, k, 0), 0.0)
            k *= 2
        return x

    li = lax.broadcasted_iota(jnp.int32, (q, q), 0)
    si = lax.broadcasted_iota(jnp.int32, (q, q), 1)

    def run(rev):
        st_ref = stb_ref if rev else stf_ref

        @pl.when(c == 0)
        def _():
            if has_h0:
                st_ref[0] = (h0b_ref if rev else h0f_ref)[0, 0]
            else:
                st_ref[0] = jnp.zeros(st_ref.shape[1:], F32)

        cum = scan(dta, rev)
        cum_t = cum.T
        e_row = 0 if rev else q - 1
        mask = (li <= si) if rev else (li >= si)
        dt_x = _expand(dt_ref[...], e1_ref[0])
        cum_x = _expand(cum, e1_ref[0])
        cum_b = _expand(cum, e2_ref[0])
        edge_x = cum_x[e_row:e_row + 1, :]
        xdt = xs_ref[...] * dt_x
        xdt16 = xdt.astype(BF16)
        xw16 = (xdt * jnp.exp(edge_x - cum_x)).astype(BF16)
        ecum = jnp.exp(cum_x)
        lower = si < hd
        ys = []
        for g in range(SSD_GROUPS):
            bm16 = bc_ref[:, g * SSD_STATE:(g + 1) * SSD_STATE].astype(BF16)
            cm16 = bc_ref[:, (SSD_GROUPS + g) * SSD_STATE:(SSD_GROUPS + g + 1) * SSD_STATE].astype(BF16)
            cb = _nt(cm16, bm16)
            for pp in range(ppg):
                p = g * ppg + pp
                cols = slice(p * LANES, (p + 1) * LANES)
                y_pair, fac = [], []
                for u in range(2):
                    h = 2 * p + u
                    j = (SSD_HEADS if rev else 0) + h
                    seg = cum_b[:, h * LANES:(h + 1) * LANES] - cum_t[j:j + 1, :]
                    decay = jnp.exp(jnp.where(mask, seg, -jnp.inf))
                    y_pair.append(jnp.dot((cb * decay).astype(BF16), xdt16[:, cols], preferred_element_type=F32))
                    fac.append(jnp.broadcast_to(jnp.exp(cum_t[j:j + 1, e_row:e_row + 1]), (hd, SSD_STATE)))
                y_diag = jnp.where(lower, y_pair[0], y_pair[1])
                state = lax.dot_general(xw16[:, cols], bm16, (((0,), (0,)), ((), ())),
                                        preferred_element_type=F32)
                h_prev = st_ref[0, 2 * p:2 * p + 2].reshape(2 * hd, SSD_STATE)
                y_off = _nt(cm16, h_prev.astype(BF16)) * ecum[:, cols]
                ys.append(y_diag + y_off)
                h_new = h_prev * jnp.concatenate(fac, axis=0) + state
                st_ref[0, 2 * p:2 * p + 2] = h_new.reshape(2, hd, SSD_STATE)
        y_ref[0] = jnp.concatenate(ys, axis=1)

    @pl.when(d == 0)
    def _():
        run(False)

    @pl.when(d == 1)
    def _():
        run(True)


def _ssd(xs, bc, dtp, a_log, n_seq, seq, row_off, h0=None):
    b = n_seq
    t = n_seq * seq
    nc = seq // SSD_CHUNK
    c0 = row_off // SSD_CHUNK

    def rmap(i, d, c):
        return i * nc + jnp.where(d == 0, c, nc - 1 - c)

    blk = lambda w: pl.BlockSpec((SSD_CHUNK, w), lambda i, d, c: (c0 + rmap(i, d, c), 0))
    st_spec = pl.BlockSpec((1, SSD_HEADS, SSD_HEAD_DIM, SSD_STATE), lambda i, d, c: (i, 0, 0, 0))
    st_shape = jax.ShapeDtypeStruct((b, SSD_HEADS, SSD_HEAD_DIM, SSD_STATE), F32)
    def sel(w):
        row = jnp.arange(LANES)[None, :, None]
        head = SSD_HEADS * jnp.arange(2)[:, None, None] + (jnp.arange(SSD_HEADS * w) // w)[None, None, :]
        return (row == head).astype(BF16)

    sel_spec = lambda w: pl.BlockSpec((1, LANES, SSD_HEADS * w), lambda i, d, c: (d, 0, 0))
    in_specs = [blk(1024), blk(512), blk(LANES), pl.BlockSpec((1, LANES), lambda i, d, c: (0, 0)),
                sel_spec(SSD_HEAD_DIM), sel_spec(LANES)]
    args = [xs, bc, dtp, a_log, sel(SSD_HEAD_DIM), sel(LANES)]
    if h0 is not None:
        layer, h0_f, h0_b = h0
        h0_spec = pl.BlockSpec((1, 1, SSD_HEADS, SSD_HEAD_DIM, SSD_STATE), lambda i, d, c: (i, layer, 0, 0, 0))
        in_specs += [h0_spec, h0_spec]
        args += [h0_f, h0_b]
    return pl.pallas_call(
        functools.partial(_ssd_body, h0 is not None),
        grid=(b, 2, nc),
        in_specs=in_specs,
        out_specs=[pl.BlockSpec((1, SSD_CHUNK, 1024), lambda i, d, c: (d, rmap(i, d, c), 0)), st_spec, st_spec],
        out_shape=[jax.ShapeDtypeStruct((2, t, 1024), F32), st_shape, st_shape],
        compiler_params=_cparams(("parallel", "arbitrary", "arbitrary")),
        name="ssd_scan",
    )(*args)


def _ssd_gate_body(y_ref, xs_ref, z_ref, dsk_ref, g_ref, o_ref):
    y = (y_ref[0] + y_ref[1] + xs_ref[...] * dsk_ref[...]) * _silu(z_ref[...])
    w = SSD_D_INNER // SSD_GROUPS
    outs = [_rms(y[:, g * w:(g + 1) * w], g_ref[:, g * w:(g + 1) * w]) for g in range(SSD_GROUPS)]
    o_ref[...] = jnp.concatenate(outs, axis=1).astype(BF16)


def _ssd_gate(y2, xs, proj, d_skip, norm_g, row_off):
    tm = 512
    t = y2.shape[1]
    off = row_off // tm
    return pl.pallas_call(
        _ssd_gate_body,
        grid=(t // tm,),
        in_specs=[pl.BlockSpec((2, tm, 1024), lambda i: (0, i, 0)),
                  pl.BlockSpec((tm, 1024), lambda i: (off + i, 0)),
                  pl.BlockSpec((tm, 1024), lambda i: (off + i, 2)),
                  pl.BlockSpec((1, 1024), lambda i: (0, 0)),
                  pl.BlockSpec((1, 1024), lambda i: (0, 0))],
        out_specs=pl.BlockSpec((tm, 1024), lambda i: (i, 0)),
        out_shape=jax.ShapeDtypeStruct((t, 1024), BF16),
        compiler_params=_cparams(("parallel",)),
        name="ssd_gate_norm",
    )(y2, xs, proj, d_skip, norm_g)


def _even_layer(x, mod, layer_idx, i, p, ctx, tables, prev_caches, adaln):
    w_in, q_norm, kv_norm, wq_b, wkv_b, lam, subln, w_out = p
    ctx_ckv, ctx_kpe, ctx_dk, ctx_dv = ctx
    cos, sin = tables
    lam_init = 0.8 - 0.6 * math.exp(-0.3 * layer_idx)

    proj = _proj(x, mod[0], mod[1], _prep_even(w_in, i))
    wq = wq_b.reshape(MLA_Q_RANK, MLA_HEADS, MLA_NOPE + MLA_ROPE)
    wq_r = jnp.pad(wq, ((0, 0), (0, 0), (0, MLA_ROPE))).reshape(MLA_Q_RANK, -1).astype(BF16)
    wkv = wkv_b.astype(BF16)
    q, kv, dq, dk, dv, kpe, *caches = _ev_post(proj, cos, sin, q_norm[None], kv_norm[None], wq_r, wkv, prev_caches)
    n_even = ctx_ckv.shape[1]
    kv_ctx = _mm(ctx_ckv.reshape(-1, MLA_KV_RANK), wkv, BF16, PAST_LEN, DEC_BATCH,
                 row_block=lambda b: b * n_even + i)

    n_ctx = DEC_BATCH * PAST_LEN
    kpe_ctx = jnp.pad(ctx_kpe[:, i].reshape(n_ctx, MLA_ROPE), ((0, 0), (0, LANES - MLA_ROPE))).astype(BF16)
    dk_ctx = ctx_dk[:, i].reshape(n_ctx, -1).astype(BF16)
    dv_ctx = ctx_dv[:, i].reshape(n_ctx, -1).astype(BF16)

    args = (lam_init, q, cos, sin, kv, kpe, dq, dk, dv, lam, subln[None])
    o1p, o2p, *mod_next = _even_attn(*args, n_seq=BATCH, seq=SEQ, row_off=0, adaln=adaln)
    o1s, o2s = _even_attn(*args, n_seq=DEC_BATCH, seq=DEC_SEQ, row_off=N_PROMPT,
                          ctx=(kv_ctx, kpe_ctx, dk_ctx, dv_ctx))
    return (o1p, o1s, o2p, o2s), w_out.astype(BF16), caches, mod_next


def _odd_layer(x, mod, i, p, ctx, tables, prev_caches, adaln):
    w_in, sink, conv_w, conv_b, dt_bias, a_log, d_skip, norm_g, w_out = p
    k_ctx, v_ctx, h0_f, h0_b = ctx
    cos, sin = tables

    proj = _proj(x, mod[0], mod[1], _prep_odd(w_in, i))
    cb = conv_b[None]
    dtb = jnp.pad(dt_bias.reshape(1, -1), ((0, 0), (0, LANES - 2 * SSD_HEADS)))
    alog = jnp.pad(a_log.reshape(1, -1), ((0, 0), (0, LANES - 2 * SSD_HEADS)))
    q, k, v, xs, bc, dtp, *caches = _od_post(proj, cos, sin, conv_w, cb, dtb, prev_caches)
    sink_b = jnp.broadcast_to(sink[:, None], (GQA_HEADS, LANES))

    o_att_p, *mod_next = _gqa_prompt(q, k, v, sink_b, adaln=adaln)
    n_ctx = DEC_BATCH * PAST_LEN
    o_att_s = _gqa_window(q, k, v, k_ctx[:, i].reshape(n_ctx, -1).astype(BF16),
                          v_ctx[:, i].reshape(n_ctx, -1).astype(BF16), sink_b)

    y_p, st_f, st_b = _ssd(xs, bc, dtp, alog, BATCH, SEQ, 0)
    y_s, _, _ = _ssd(xs, bc, dtp, alog, DEC_BATCH, DEC_SEQ, N_PROMPT, h0=(i, h0_f, h0_b))
    dsk = jnp.repeat(d_skip, SSD_HEAD_DIM)[None]
    y_n_p = _ssd_gate(y_p, xs, proj, dsk, norm_g[None], 0)
    y_n_s = _ssd_gate(y_s, xs, proj, dsk, norm_g[None], N_PROMPT)
    return (o_att_p, o_att_s, y_n_p, y_n_s), w_out.astype(BF16), caches, (st_f, st_b), mod_next


def kernel(x_prompt, x_sample, cache_mla_ckv, cache_mla_kpe, cache_diff_k, cache_diff_v, cache_gqa_k, cache_gqa_v, state_ssd_fwd, state_ssd_bwd, c, c_ctx, w_mod, b_mod, ln1_g, ln1_b, ln2_g, ln2_b, ev_w_in, mla_q_norm, mla_kv_norm, mla_wq_b, mla_wkv_b, diff_lambda, diff_subln, ev_w_out, od_w_in, gqa_sink, ssd_conv_w, ssd_conv_b, ssd_dt_bias, ssd_a_log, ssd_d, ssd_norm, od_w_out, moe_router_group, moe_router_expert, moe_w_gate, moe_w_up, moe_w_down):
    x = jnp.concatenate([x_prompt.reshape(N_PROMPT, D_MODEL), x_sample.reshape(N_SAMPLE, D_MODEL)], axis=0)
    cs = jnp.concatenate([c_ctx[None], c, jnp.zeros((8 - 1 - DEC_BATCH, D_MODEL), F32)], axis=0)
    mod_args = (cs, w_mod, b_mod.reshape(DEPTH, 1, MOD_N))
    mod_rows = _adaln(*mod_args, 0)
    tables_64 = _rope_tables(64)
    tables_128 = _rope_tables(GQA_HEAD_DIM)

    ev_w_in = ev_w_in.astype(BF16)
    od_w_in = od_w_in.astype(BF16)
    ev_caches, od_caches, ssd_states = (), (), []
    for l in range(DEPTH):
        i = l // 2
        mods = mod_rows[:1 + DEC_BATCH].reshape(1 + DEC_BATCH, 6, 1, D_MODEL)
        mod = [mods[:, k] for k in range(6)]
        adaln = (l + 1, mod_args) if l + 1 < DEPTH else None
        if l % 2 == 0:
            p = (ev_w_in, mla_q_norm[i], mla_kv_norm[i], mla_wq_b[i], mla_wkv_b[i], diff_lambda[i],
                 diff_subln[i], ev_w_out[i])
            ctx = (cache_mla_ckv, cache_mla_kpe, cache_diff_k, cache_diff_v)
            mix, w_out, ev_caches, mod_next = _even_layer(x, mod, l, i, p, ctx, tables_64, ev_caches, adaln)
        else:
            p = (od_w_in, gqa_sink[i], ssd_conv_w[i], ssd_conv_b[i], ssd_dt_bias[i], ssd_a_log[i], ssd_d[i],
                 ssd_norm[i], od_w_out[i])
            ctx = (cache_gqa_k, cache_gqa_v, state_ssd_fwd, state_ssd_bwd)
            mix, w_out, od_caches, states, mod_next = _odd_layer(x, mod, i, p, ctx, tables_128, od_caches, adaln)
            ssd_states.append(states)
        w_router = jnp.concatenate([moe_router_group[l], moe_router_expert[l],
                                    jnp.zeros((D_MODEL, LANES - MOE_GROUPS - MOE_EXPERTS), F32)], axis=1)
        w_router_hi = w_router.astype(BF16)
        w_router = jnp.stack([w_router_hi, (w_router - w_router_hi.astype(F32)).astype(BF16)])
        x, h2, route = _out_proj(*mix, w_out, x, mod[2], ln1_g[l][None], ln1_b[l][None], mod[3], mod[4], w_router)
        z = _moe_experts(l, h2, _route_meta(route), moe_w_gate, moe_w_up, moe_w_down)
        x = _combine(x, z, route, mod[5], ln2_g[l][None], ln2_b[l][None], split=(l == DEPTH - 1))
        if mod_next:
            mod_rows = mod_next[0]

    y_prompt = x[0].reshape(BATCH, SEQ, D_MODEL)
    y_sample = x[1].reshape(DEC_BATCH, DEC_SEQ, D_MODEL)
    new_ssd_fwd = jnp.stack([s[0] for s in ssd_states], axis=1)
    new_ssd_bwd = jnp.stack([s[1] for s in ssd_states], axis=1)
    return (y_prompt, y_sample, *ev_caches, *od_caches, new_ssd_fwd, new_ssd_bwd)
```

```python
import functools
import math

import jax
import jax.numpy as jnp
from jax import lax
from jax.experimental import pallas as pl
from jax.experimental.pallas import tpu as pltpu

F32 = jnp.float32
BF16 = jnp.bfloat16

D_MODEL = 2048
BATCH = 16
SEQ = 256
DEPTH = 4
DEC_BATCH = 2
DEC_SEQ = 1024
PAST_LEN = 512
GRID_W = 64
ROPE_THETA = 10000.0
DEEPNORM_ALPHA = (2.0 * DEPTH) ** 0.25
LN_EPS = 1e-5
RMS_EPS = 1e-6

MLA_HEADS = 8
MLA_Q_RANK = 512
MLA_KV_RANK = 512
MLA_NOPE = 128
MLA_ROPE = 64
MLA_V = 128
DIFF_HEADS = 8
DIFF_QK = 64
DIFF_V = 128
GQA_HEADS = 8
GQA_KV_HEADS = 2
GQA_HEAD_DIM = 128
WINDOW = 128
SSD_HEADS = 16
SSD_HEAD_DIM = 64
SSD_D_INNER = SSD_HEADS * SSD_HEAD_DIM
SSD_GROUPS = 2
SSD_STATE = 128
SSD_CHUNK = 128
MOE_GROUPS = 4
MOE_EPG = 4
MOE_EXPERTS = 16
MOE_HIDDEN = 512

N_PROMPT = BATCH * SEQ
N_SAMPLE = DEC_BATCH * DEC_SEQ
N_TOK = N_PROMPT + N_SAMPLE
PROJ_N = 4224
LANES = 128
VMEM_LIMIT = 56 * 1024 * 1024

PROJ_TM, PROJ_TN = 1024, 1408
OUT_TM = 512
ATT_TQ = 256
ATT_HEADS_PER_STEP = 2
ATT_HEADS_PER_STEP_CTX = 4
POST_TM = 256
MOE_TM = 256
MOE_SLOTS = 2 * N_TOK
MOE_TILES = MOE_SLOTS // MOE_TM + MOE_EXPERTS
MOE_ROWS = MOE_TILES * MOE_TM
WEIGHT_DMA_PRIORITY = 1
SLAB = (D_MODEL // LANES, LANES)


def _cparams(sem):
    return pltpu.CompilerParams(dimension_semantics=sem, vmem_limit_bytes=VMEM_LIMIT)


def _mod_group(row0):
    return jnp.maximum(row0 // DEC_SEQ - (N_PROMPT // DEC_SEQ - 1), 0)


def _silu(x):
    return x / (1.0 + jnp.exp(-x))


def _nt(a, b):
    return lax.dot_general(a, b, (((1,), (1,)), ((), ())), preferred_element_type=F32)


def _rms(x, g):
    return x * lax.rsqrt(jnp.mean(x * x, axis=-1, keepdims=True) + RMS_EPS) * g


def _layer_norm(y, g, b):
    mu = jnp.mean(y, axis=-1, keepdims=True)
    yc = y - mu
    return yc * lax.rsqrt(jnp.mean(yc * yc, axis=-1, keepdims=True) + LN_EPS) * g + b


MOD_N = 6 * D_MODEL


def _adaln_tile(c_ref, w_ref, b_ref, o_ref):
    o_ref[...] = jnp.dot(_silu(c_ref[...]), w_ref[0], preferred_element_type=F32) + b_ref[0]


def _adaln_specs(layer, n_steps, step_of):
    tn = MOD_N // n_steps
    in_specs = [pl.BlockSpec((8, D_MODEL), lambda *g: (0, 0)),
                pl.BlockSpec((1, D_MODEL, tn), lambda *g: (layer, 0, step_of(*g))),
                pl.BlockSpec((1, 1, tn), lambda *g: (layer, 0, step_of(*g)))]
    return in_specs, pl.BlockSpec((8, tn), lambda *g: (0, step_of(*g))), jax.ShapeDtypeStruct((8, MOD_N), F32)


def _adaln(cs, w_mod, b_mod3, layer):
    n_steps = 12
    in_specs, out_spec, out_shape = _adaln_specs(layer, n_steps, lambda j: j)
    return pl.pallas_call(
        _adaln_tile,
        grid=(n_steps,),
        in_specs=in_specs,
        out_specs=out_spec,
        out_shape=out_shape,
        compiler_params=_cparams(("parallel",)),
        name="adaln",
    )(cs, w_mod, b_mod3)


def _proj_body(x_ref, sh_ref, sc_ref, w_ref, o_ref, h_scr):
    @pl.when(pl.program_id(1) == 0)
    def _():
        h = x_ref[...] * (1.0 + sc_ref[0]) + sh_ref[0]
        h_scr[...] = h.astype(BF16)

    o_ref[...] = jnp.dot(h_scr[...], w_ref[...], preferred_element_type=F32)


def _proj(x, shift, scale, w):
    tm, tn = PROJ_TM, PROJ_TN
    n = w.shape[1]
    mod_spec = pl.BlockSpec((1, 1, D_MODEL), lambda i, j: (_mod_group(i * tm), 0, 0))
    return pl.pallas_call(
        _proj_body,
        grid=(N_TOK // tm, n // tn),
        in_specs=[
            pl.BlockSpec((tm, D_MODEL), lambda i, j: (i, 0)),
            mod_spec,
            mod_spec,
            pl.BlockSpec((D_MODEL, tn), lambda i, j: (0, j)),
        ],
        out_specs=pl.BlockSpec((tm, tn), lambda i, j: (i, j)),
        out_shape=jax.ShapeDtypeStruct((N_TOK, n), F32),
        scratch_shapes=[pltpu.VMEM((tm, D_MODEL), BF16)],
        compiler_params=_cparams(("parallel", "arbitrary")),
        name="in_proj",
    )(x, shift, scale, w)


PROJ_BLOCKS = PROJ_N // LANES


def _prep_even_body(a_ref, b_ref, o_ref):
    c = pl.program_id(0)
    half = LANES // 2
    aligned = (MLA_Q_RANK + MLA_KV_RANK) // LANES

    @pl.when(c < aligned)
    def _():
        o_ref[...] = a_ref[0]

    @pl.when((c >= aligned) & (c < PROJ_BLOCKS - 1))
    def _():
        o_ref[...] = jnp.concatenate([a_ref[0][:, half:], b_ref[0][:, :half]], axis=1)

    @pl.when(c == PROJ_BLOCKS - 1)
    def _():
        k_pe = a_ref[0][:, :half]
        o_ref[...] = jnp.concatenate([k_pe, jnp.zeros_like(k_pe)], axis=1)


def _prep_even(w_in, layer):
    aligned = (MLA_Q_RANK + MLA_KV_RANK) // LANES
    last = PROJ_BLOCKS - 1
    return pl.pallas_call(
        _prep_even_body,
        grid=(PROJ_BLOCKS,),
        in_specs=[pl.BlockSpec((1, D_MODEL, LANES), lambda c: (layer, 0, jnp.where(c == last, aligned, c))),
                  pl.BlockSpec((1, D_MODEL, LANES), lambda c: (layer, 0, jnp.minimum(c + 1, last)))],
        out_specs=pl.BlockSpec((D_MODEL, LANES), lambda c: (0, c)),
        out_shape=jax.ShapeDtypeStruct((D_MODEL, PROJ_N), BF16),
        compiler_params=_cparams(("parallel",)),
        name="prep_w_even",
    )(w_in, w_in)


def _prep_odd_body(perm_ref, a_ref, dt_ref, o_ref):
    c = pl.program_id(0)

    @pl.when(c < PROJ_BLOCKS - 1)
    def _():
        o_ref[...] = a_ref[0]

    @pl.when(c == PROJ_BLOCKS - 1)
    def _():
        o_ref[...] = dt_ref[...]


def _prep_odd(w_in, layer):
    order = list(range(0, 8)) + list(range(20, 28)) + list(range(12, 20)) + [8, 9, 10, 11, 28, 29, 30, 31, 31]
    perm = jnp.asarray(order, jnp.int32)
    n_dt = 2 * SSD_HEADS
    w_dt = jnp.pad(w_in[layer, :, w_in.shape[2] - n_dt:], ((0, 0), (0, LANES - n_dt))).astype(BF16)
    return pl.pallas_call(
        _prep_odd_body,
        grid_spec=pltpu.PrefetchScalarGridSpec(
            num_scalar_prefetch=1,
            grid=(PROJ_BLOCKS,),
            in_specs=[pl.BlockSpec((1, D_MODEL, LANES), lambda c, perm_ref: (layer, 0, perm_ref[c])),
                      pl.BlockSpec((D_MODEL, LANES), lambda c, perm_ref: (0, 0))],
            out_specs=pl.BlockSpec((D_MODEL, LANES), lambda c, perm_ref: (0, c)),
        ),
        out_shape=jax.ShapeDtypeStruct((D_MODEL, PROJ_N), BF16),
        compiler_params=_cparams(("parallel",)),
        name="prep_w_odd",
    )(perm, w_in, w_dt)


def _mm_body(a_ref, b_ref, o_ref):
    o_ref[...] = jnp.dot(a_ref[...].astype(BF16), b_ref[...], preferred_element_type=F32).astype(o_ref.dtype)


def _mm(a, b, out_dtype, tm, n_tiles, row_block=lambda i: i):
    k = a.shape[1]
    n = b.shape[1]
    return pl.pallas_call(
        _mm_body,
        grid=(n_tiles,),
        in_specs=[pl.BlockSpec((tm, k), lambda i: (row_block(i), 0)), pl.BlockSpec((k, n), lambda i: (0, 0))],
        out_specs=pl.BlockSpec((tm, n), lambda i: (i, 0)),
        out_shape=jax.ShapeDtypeStruct((n_tiles * tm, n), out_dtype),
        compiler_params=_cparams(("parallel",)),
        name="mm_resident",
    )(a, b)


def _rope(x, cos, sin_signed, quarter):
    rows, w = x.shape
    reps = w // LANES
    if reps > 1:
        cos = jnp.concatenate([cos] * reps, axis=1)
        sin_signed = jnp.concatenate([sin_signed] * reps, axis=1)
    lane = lax.broadcasted_iota(jnp.int32, x.shape, 1)
    first = (lane % (2 * quarter)) < quarter
    up = pltpu.roll(x, w - quarter, 1)
    dn = pltpu.roll(x, quarter, 1)
    return x * cos + jnp.where(first, up, dn) * sin_signed


def _rope_tables(rdim):
    half = rdim // 2
    quarter = half // 2
    pos = jnp.arange(DEC_SEQ)
    row = (pos // GRID_W).astype(F32)
    col = (pos % GRID_W).astype(F32)
    inv = ROPE_THETA ** (-jnp.arange(quarter, dtype=F32) * 2.0 / half)
    lane = jnp.arange(LANES)
    r = lane % rdim
    use_col = (r // half) == 1
    j = r % quarter
    p = jnp.where(use_col[None, :], col[:, None], row[:, None])
    ang = p * inv[j][None, :]
    sign = jnp.where((r % half) < quarter, -1.0, 1.0).astype(F32)
    cos = jnp.cos(ang)
    sin = jnp.sin(ang) * sign[None, :]
    cos = jnp.concatenate([jnp.ones((N_PROMPT, LANES), F32), cos, cos], axis=0)
    sin = jnp.concatenate([jnp.zeros((N_PROMPT, LANES), F32), sin, sin], axis=0)
    return cos, sin


def _ev_post_body(n_prev, ql_ref, kvl_ref, dq_ref, dk_ref, dv_ref, kpe_ref, cos_ref, sin_ref, gq_ref, gkv_ref,
                  wq_ref, wkv_ref, *refs):
    prev = refs[:4 * n_prev]
    q_ref, kv_ref, dqr_ref, dkr_ref, dvb_ref, kper_ref, c_ckv, c_kpe, c_dk, c_dv = refs[4 * n_prev:]
    cos = cos_ref[...]
    sin = sin_ref[...]
    qn = _rms(ql_ref[...], gq_ref[...]).astype(BF16)
    q_ref[...] = jnp.dot(qn, wq_ref[...], preferred_element_type=F32).astype(BF16)
    ckv = _rms(kvl_ref[...], gkv_ref[...])
    kv_ref[...] = jnp.dot(ckv.astype(BF16), wkv_ref[...], preferred_element_type=F32).astype(BF16)
    dqr_ref[...] = _rope(dq_ref[...], cos, sin, DIFF_QK // 4).astype(BF16)
    dk = _rope(dk_ref[...], cos, sin, DIFF_QK // 4)
    dkr_ref[...] = dk.astype(BF16)
    dv = dv_ref[...]
    dvb_ref[...] = dv.astype(BF16)
    kpe = _rope(kpe_ref[...], cos, sin, MLA_ROPE // 4)
    kper_ref[...] = kpe.astype(BF16)

    @pl.when(pl.program_id(0) < BATCH)
    def _():
        if n_prev:
            c_ckv[0, 0] = prev[0][0, 0]
            c_kpe[0, 0] = prev[1][0, 0]
            c_dk[0, 0] = prev[2][0, 0]
            c_dv[0, 0] = prev[3][0, 0]
        c_ckv[0, n_prev] = ckv
        c_kpe[0, n_prev] = kpe[:, :MLA_ROPE]
        c_dk[0, n_prev] = dk.reshape(SEQ, DIFF_HEADS, 2 * DIFF_QK)
        c_dv[0, n_prev] = dv.reshape(SEQ, DIFF_HEADS, DIFF_V)


def _ev_post(proj, cos, sin, gq, gkv, wq, wkv, prev):
    tm = POST_TM
    n_prev = 1 if prev else 0
    nl = n_prev + 1
    row = lambda c: (lambda i: (i, c))
    seq4 = lambda i: (jnp.minimum(i, BATCH - 1), 0, 0, 0)
    seq5 = lambda i: (jnp.minimum(i, BATCH - 1), 0, 0, 0, 0)
    cache_shapes = [(MLA_KV_RANK,), (MLA_ROPE,), (DIFF_HEADS, 2 * DIFF_QK), (DIFF_HEADS, DIFF_V)]

    def cache_spec(layers, tail):
        return pl.BlockSpec((1, layers, SEQ) + tail, seq4 if len(tail) == 1 else seq5)

    return pl.pallas_call(
        functools.partial(_ev_post_body, n_prev),
        grid=(N_TOK // tm,),
        in_specs=[
            pl.BlockSpec((tm, 512), row(0)),
            pl.BlockSpec((tm, 512), row(1)),
            pl.BlockSpec((tm, 1024), row(1)),
            pl.BlockSpec((tm, 1024), row(2)),
            pl.BlockSpec((tm, 1024), row(3)),
            pl.BlockSpec((tm, LANES), row(32)),
            pl.BlockSpec((tm, LANES), row(0)),
            pl.BlockSpec((tm, LANES), row(0)),
            pl.BlockSpec((1, 512), lambda i: (0, 0)),
            pl.BlockSpec((1, 512), lambda i: (0, 0)),
            pl.BlockSpec(wq.shape, lambda i: (0, 0)),
            pl.BlockSpec(wkv.shape, lambda i: (0, 0)),
        ] + [cache_spec(1, t) for t in cache_shapes] * n_prev,
        out_specs=[
            pl.BlockSpec((tm, wq.shape[1]), row(0)),
            pl.BlockSpec((tm, wkv.shape[1]), row(0)),
            pl.BlockSpec((tm, 1024), row(0)),
            pl.BlockSpec((tm, 1024), row(0)),
            pl.BlockSpec((tm, 1024), row(0)),
            pl.BlockSpec((tm, LANES), row(0)),
        ] + [cache_spec(nl, t) for t in cache_shapes],
        out_shape=[
            jax.ShapeDtypeStruct((N_TOK, wq.shape[1]), BF16),
            jax.ShapeDtypeStruct((N_TOK, wkv.shape[1]), BF16),
            jax.ShapeDtypeStruct((N_TOK, 1024), BF16),
            jax.ShapeDtypeStruct((N_TOK, 1024), BF16),
            jax.ShapeDtypeStruct((N_TOK, 1024), BF16),
            jax.ShapeDtypeStruct((N_TOK, LANES), BF16),
        ] + [jax.ShapeDtypeStruct((BATCH, nl, SEQ) + t, F32) for t in cache_shapes],
        compiler_params=_cparams(("arbitrary",)),
        name="even_post",
    )(proj, proj, proj, proj, proj, proj, cos, sin, gq, gkv, wq, wkv, *prev)


def _softmax_av(scores, values):
    m = functools.reduce(jnp.maximum, [jnp.max(s, axis=-1, keepdims=True) for s in scores])
    es = [jnp.exp(s - m) for s in scores]
    l = sum(jnp.sum(e, axis=-1, keepdims=True) for e in es)
    o = sum(jnp.dot(e.astype(BF16), v, preferred_element_type=F32) for e, v in zip(es, values))
    return o / l


def _even_attn_body(lam_init, has_ctx, has_mod, hps, q_ref, cos_ref, sin_ref, kv_ref, kpe_ref, dq_ref, dk_ref, dv_ref,
                    lam_ref, sub_ref, *refs):
    refs = list(refs)
    if has_ctx:
        kvc_ref, kpec_ref, dkc_ref, dvc_ref = refs[:4]
        refs = refs[4:]
    if has_mod:
        _adaln_tile(*refs[:3], refs[-1])
        refs = refs[3:-1]
    omla_ref, odiff_ref = refs
    cos = cos_ref[...]
    sin = sin_ref[...]
    lam = lam_ref[...]
    lam_full = (jnp.exp(jnp.sum(lam[0:1] * lam[1:2], axis=-1, keepdims=True))
                - jnp.exp(jnp.sum(lam[2:3] * lam[3:4], axis=-1, keepdims=True)) + lam_init)
    mla_scale = (MLA_NOPE + MLA_ROPE) ** -0.5
    lane = lax.broadcasted_iota(jnp.int32, (q_ref.shape[0], 2 * DIFF_QK), 1)
    for u in range(hps):
        qc = slice(u * 256, (u + 1) * 256)
        dc = slice(u * LANES, (u + 1) * LANES)
        q = q_ref[:, qc]
        qr = _rope(q[:, MLA_NOPE:].astype(F32), cos, sin, MLA_ROPE // 4).astype(BF16)
        qcat = jnp.concatenate([q[:, :MLA_NOPE], qr], axis=1)
        kv = kv_ref[:, qc]
        scores = [_nt(qcat, jnp.concatenate([kv[:, :MLA_NOPE], kpe_ref[...]], axis=1)) * mla_scale]
        values = [kv[:, MLA_NOPE:]]
        if has_ctx:
            kvc = kvc_ref[:, qc]
            scores.append(_nt(qcat, jnp.concatenate([kvc[:, :MLA_NOPE], kpec_ref[...]], axis=1)) * mla_scale)
            values.append(kvc[:, MLA_NOPE:])
        omla_ref[:, dc] = _softmax_av(scores, values).astype(BF16)

        dq = dq_ref[:, dc] * jnp.asarray(DIFF_QK ** -0.5, BF16)
        zero = jnp.zeros_like(dq)
        keys = [dk_ref[:, dc]]
        values = [dv_ref[:, dc]]
        if has_ctx:
            keys.append(dkc_ref[:, dc])
            values.append(dvc_ref[:, dc])
        a1 = _softmax_av([_nt(jnp.where(lane < DIFF_QK, dq, zero), k) for k in keys], values)
        a2 = _softmax_av([_nt(jnp.where(lane >= DIFF_QK, dq, zero), k) for k in keys], values)
        odiff_ref[:, dc] = (_rms(a1 - lam_full * a2, sub_ref[...]) * (1.0 - lam_init)).astype(BF16)


def _even_attn(lam_init, q, cos, sin, kv, kpe, dq, dk, dv, lam, subln, n_seq, seq, row_off, ctx=None, adaln=None):
    tq = ATT_TQ
    hps = ATT_HEADS_PER_STEP if ctx is not None else ATT_HEADS_PER_STEP_CTX
    nq = seq // tq
    q0 = row_off // tq
    s0 = row_off // seq
    qspec = lambda w: pl.BlockSpec((tq, hps * w), lambda b, j, h: (q0 + b * nq + j, h))
    kspec = lambda w: pl.BlockSpec((seq, hps * w), lambda b, j, h: (s0 + b, h))
    tspec = pl.BlockSpec((tq, LANES), lambda b, j, h: (q0 + b * nq + j, 0))
    ospec = pl.BlockSpec((tq, hps * LANES), lambda b, j, h: (b * nq + j, h))
    in_specs = [qspec(256), tspec, tspec, kspec(256),
                pl.BlockSpec((seq, LANES), lambda b, j, h: (s0 + b, 0)),
                qspec(LANES), kspec(LANES), kspec(LANES),
                pl.BlockSpec((4, DIFF_QK), lambda b, j, h: (0, 0)),
                pl.BlockSpec((1, DIFF_V), lambda b, j, h: (0, 0))]
    args = [q, cos, sin, kv, kpe, dq, dk, dv, lam, subln]
    if ctx is not None:
        cspec = lambda w: pl.BlockSpec((PAST_LEN, hps * w), lambda b, j, h: (b, h))
        in_specs += [cspec(256), pl.BlockSpec((PAST_LEN, LANES), lambda b, j, h: (b, 0)), cspec(LANES), cspec(LANES)]
        args += list(ctx)
    nh = MLA_HEADS // hps
    out_specs = [ospec, ospec]
    out_shape = [jax.ShapeDtypeStruct((n_seq * seq, 1024), BF16)] * 2
    if adaln is not None:
        mod_layer, mod_args = adaln
        m_in, m_out, m_shape = _adaln_specs(mod_layer, n_seq * nq * nh, lambda b, j, h: (b * nq + j) * nh + h)
        in_specs += m_in
        args += list(mod_args)
        out_specs.append(m_out)
        out_shape.append(m_shape)
    return pl.pallas_call(
        functools.partial(_even_attn_body, lam_init, ctx is not None, adaln is not None, hps),
        grid=(n_seq, nq, nh),
        in_specs=in_specs,
        out_specs=out_specs,
        out_shape=out_shape,
        compiler_params=_cparams(("parallel", "parallel", "parallel")),
        name="even_attn",
    )(*args)


def _route(logits):
    lane = lax.broadcasted_iota(jnp.int32, logits.shape, 1).astype(F32)
    neg = -jnp.inf
    none = float(LANES)
    is_g = lane < MOE_GROUPS
    lg = jnp.where(is_g, logits, neg)
    mg = jnp.max(lg, axis=-1, keepdims=True)
    g_val = 1.0 / jnp.sum(jnp.exp(lg - mg), axis=-1, keepdims=True)
    g_idx = jnp.min(jnp.where(is_g & (lg == mg), lane, none), axis=-1, keepdims=True)
    e_lo = MOE_GROUPS + g_idx * MOE_EPG
    is_e = (lane >= e_lo) & (lane < e_lo + MOE_EPG)
    le = jnp.where(is_e, logits, neg)
    p = jnp.exp(le - jnp.max(le, axis=-1, keepdims=True))
    p1 = jnp.max(p, axis=-1, keepdims=True)
    i1 = jnp.min(jnp.where(is_e & (p == p1), lane, none), axis=-1, keepdims=True)
    rest = is_e & (lane != i1)
    p2 = jnp.max(jnp.where(rest, p, neg), axis=-1, keepdims=True)
    i2 = jnp.min(jnp.where(rest & (p == p2), lane, none), axis=-1, keepdims=True)
    s = g_val / (p1 + p2)
    return jnp.where(lane == 0.0, i1 - MOE_GROUPS,
                     jnp.where(lane == 1.0, i2 - MOE_GROUPS,
                               jnp.where(lane == 2.0, p1 * s, jnp.where(lane == 3.0, p2 * s, 0.0))))


def _out_proj_body(a1p_ref, a1s_ref, a2p_ref, a2s_ref, w_ref, x_ref, gate_ref, g_ref, b_ref, sh_ref, sc_ref,
                   wr_ref, xo_ref, h_ref, comb_ref):
    is_prompt = pl.program_id(0) < N_PROMPT // OUT_TM
    a1 = jnp.where(is_prompt, a1p_ref[...], a1s_ref[...])
    a2 = jnp.where(is_prompt, a2p_ref[...], a2s_ref[...])
    k1 = a1.shape[1]
    acc = jnp.dot(a1, w_ref[:k1, :], preferred_element_type=F32)
    acc = acc + jnp.dot(a2, w_ref[k1:, :], preferred_element_type=F32)
    xn = _layer_norm(DEEPNORM_ALPHA * x_ref[...] + gate_ref[0] * acc, g_ref[...], b_ref[...])
    xo_ref[...] = xn
    h = xn * (1.0 + sc_ref[0]) + sh_ref[0]
    h_ref[...] = h.reshape(h_ref.shape).astype(BF16)
    h_hi = h.astype(BF16)
    h_lo = (h - h_hi.astype(F32)).astype(BF16)
    logits = (jnp.dot(h_hi, wr_ref[0], preferred_element_type=F32)
              + jnp.dot(h_lo, wr_ref[0], preferred_element_type=F32)
              + jnp.dot(h_hi, wr_ref[1], preferred_element_type=F32))
    comb_ref[...] = _route(logits)


def _out_proj(a1p, a1s, a2p, a2s, w, x, gate, ln_g, ln_b, shift, scale, w_router):
    tm = OUT_TM
    np_tiles = N_PROMPT // tm
    k1, k2 = a1p.shape[1], a2p.shape[1]
    mod_spec = pl.BlockSpec((1, 1, D_MODEL), lambda i: (_mod_group(i * tm), 0, 0))
    vec_spec = pl.BlockSpec((1, D_MODEL), lambda i: (0, 0))
    row_spec = pl.BlockSpec((tm, D_MODEL), lambda i: (i, 0))
    pspec = lambda k: pl.BlockSpec((tm, k), lambda i: (jnp.minimum(i, np_tiles - 1), 0))
    sspec = lambda k: pl.BlockSpec((tm, k), lambda i: (jnp.maximum(i - np_tiles, 0), 0))
    return pl.pallas_call(
        _out_proj_body,
        grid=(N_TOK // tm,),
        in_specs=[
            pspec(k1), sspec(k1), pspec(k2), sspec(k2),
            pl.BlockSpec((k1 + k2, D_MODEL), lambda i: (0, 0)),
            row_spec, mod_spec, vec_spec, vec_spec, mod_spec, mod_spec,
            pl.BlockSpec((2, D_MODEL, LANES), lambda i: (0, 0, 0)),
        ],
        out_specs=[row_spec, pl.BlockSpec((tm,) + SLAB, lambda i: (i, 0, 0)),
                   pl.BlockSpec((tm, LANES), lambda i: (i, 0))],
        out_shape=[
            jax.ShapeDtypeStruct((N_TOK, D_MODEL), F32),
            jax.ShapeDtypeStruct((N_TOK,) + SLAB, BF16),
            jax.ShapeDtypeStruct((N_TOK, LANES), F32),
        ],
        compiler_params=_cparams(("parallel",)),
        name="out_proj_ln_router",
    )(a1p, a1s, a2p, a2s, w, x, gate, ln_g, ln_b, shift, scale, w_router)


def _route_meta(route):
    e = route[:, :2].astype(jnp.int32).reshape(-1)
    onehot = (e[:, None] == jnp.arange(MOE_EXPERTS, dtype=jnp.int32)[None, :]).astype(jnp.int32)
    csum = jnp.cumsum(onehot, axis=0)
    rank = jnp.sum((csum - onehot) * onehot, axis=1)
    count = csum[-1]
    ntile = (count + MOE_TM - 1) // MOE_TM
    tile_end = jnp.cumsum(ntile)
    tile_off = tile_end - ntile
    pos = jnp.sum(onehot * tile_off[None, :], axis=1) * MOE_TM + rank
    n_used = tile_end[-1:]
    j = jnp.arange(MOE_TILES, dtype=jnp.int32)
    tile_expert = jnp.sum((tile_end[None, :] <= jnp.minimum(j, n_used - 1)[:, None]).astype(jnp.int32), axis=1)
    is_e = tile_expert[:, None] == jnp.arange(MOE_EXPERTS, dtype=jnp.int32)[None, :]
    left = jnp.sum(jnp.where(is_e, (count - (j[:, None] - tile_off[None, :]) * MOE_TM)[...], 0), axis=1)
    n_valid = jnp.where(j < n_used, jnp.clip(left, 0, MOE_TM), 0)
    ids = jnp.arange(MOE_EXPERTS, dtype=jnp.int32)
    later = (ids[None, :] > ids[:, None]) & (ntile[None, :] > 0)
    next_of = jnp.min(jnp.where(later, ids[None, :], MOE_EXPERTS), axis=1)
    next_of = jnp.where(next_of == MOE_EXPERTS, -1, next_of)
    next_expert = jnp.sum(jnp.where(is_e, next_of[None, :], 0), axis=1)
    ordinal = jnp.cumsum((ntile > 0).astype(jnp.int32)) - 1
    w_slot = jnp.sum(jnp.where(is_e, ordinal[None, :], 0), axis=1) % 2
    return pos, tile_expert, n_valid, n_used, next_expert, w_slot


def _moe_body(layer, pos_ref, te_ref, nv_ref, nu_ref, ne_ref, ws_ref, h_ref, wg_ref, wu_ref, wd_ref, z_ref,
              inv, xbuf, zbuf, wgbuf, wubuf, wdbuf, wg_s, wu_s, wd_s, gsem, ssem, wsem):
    j = pl.program_id(0)
    slot = j % 2
    n_used = nu_ref[0]
    group = 8

    def weight_copies(e, s):
        return [pltpu.make_async_copy(wg_ref.at[layer, e], wgbuf.at[s], wsem.at[s]),
                pltpu.make_async_copy(wu_ref.at[layer, e], wubuf.at[s], wsem.at[s]),
                pltpu.make_async_copy(wd_ref.at[layer, e], wdbuf.at[s], wsem.at[s])]

    def gather_row(tile, s, r):
        tok = lax.shift_right_logical(inv[tile * MOE_TM + r], 1)
        return pltpu.make_async_copy(h_ref.at[tok], xbuf.at[s, r], gsem.at[s])

    def scatter_row(tile, s, r):
        sl = inv[tile * MOE_TM + r]
        dst = (sl & 1) * N_TOK + lax.shift_right_logical(sl, 1)
        return pltpu.make_async_copy(zbuf.at[s, r], z_ref.at[dst], ssem.at[s])

    def for_valid_rows(tile, fn):
        n = nv_ref[tile]

        def block(g, c):
            for u in range(group):
                fn(g * group + u, u % 2)
            return c

        def single(r, c):
            fn(r, 0)
            return c

        full = lax.shift_right_logical(n, 3)
        lax.fori_loop(0, full, block, 0)
        lax.fori_loop(full * group, n, single, 0)

    def wait_gather(tile, s):
        @pl.when(nv_ref[tile] == MOE_TM)
        def _():
            pltpu.make_async_copy(h_ref.at[pl.ds(0, MOE_TM)], xbuf.at[s], gsem.at[s]).wait()

        @pl.when(nv_ref[tile] != MOE_TM)
        def _():
            for_valid_rows(tile, lambda r, pr: gather_row(tile, s, r).wait())

    def wait_scatter(tile, s):
        @pl.when(nv_ref[tile] == MOE_TM)
        def _():
            pltpu.make_async_copy(zbuf.at[s], z_ref.at[pl.ds(0, MOE_TM)], ssem.at[s]).wait()

        @pl.when(nv_ref[tile] != MOE_TM)
        def _():
            for_valid_rows(tile, lambda r, pr: scatter_row(tile, s, r).wait())

    @pl.when(j == 0)
    def _():
        def fill(g, c):
            for u in range(group):
                sl = g * group + u
                inv[pos_ref[sl]] = sl
            return c
        lax.fori_loop(0, MOE_SLOTS // group, fill, 0)
        xbuf[...] = jnp.zeros_like(xbuf)
        for_valid_rows(0, lambda r, pr: gather_row(0, 0, r).start(priority=pr))
        for cp in weight_copies(te_ref[0], ws_ref[0]):
            cp.start(priority=WEIGHT_DMA_PRIORITY)

    changed = (j == 0) | (te_ref[j] != te_ref[jnp.maximum(j - 1, 0)])

    @pl.when(changed)
    def _():
        ws = ws_ref[j]
        for cp in weight_copies(te_ref[j], ws):
            cp.wait()
        wg_s[...] = wgbuf[ws].astype(BF16)
        wu_s[...] = wubuf[ws].astype(BF16)
        wd_s[...] = wdbuf[ws].astype(BF16)

        @pl.when(ne_ref[j] >= 0)
        def _():
            for cp in weight_copies(ne_ref[j], 1 - ws):
                cp.start(priority=WEIGHT_DMA_PRIORITY)

    @pl.when(j < n_used)
    def _():
        @pl.when(j + 1 < n_used)
        def _():
            for_valid_rows(j + 1, lambda r, pr: gather_row(j + 1, 1 - slot, r).start(priority=pr))

        wait_gather(j, slot)

        @pl.when(j >= 2)
        def _():
            wait_scatter(j - 2, slot)

        x = xbuf[slot].reshape(MOE_TM, D_MODEL)
        hg = jnp.dot(x, wg_s[...], preferred_element_type=F32)
        hu = jnp.dot(x, wu_s[...], preferred_element_type=F32)
        act = (_silu(hg) * hu).astype(BF16)
        z = jnp.dot(act, wd_s[...], preferred_element_type=F32)
        zbuf[slot] = z.reshape((MOE_TM,) + SLAB).astype(BF16)
        for_valid_rows(j, lambda r, pr: scatter_row(j, slot, r).start(priority=pr))

        @pl.when(j == n_used - 1)
        def _():
            @pl.when(j >= 1)
            def _():
                wait_scatter(j - 1, 1 - slot)
            wait_scatter(j, slot)


def _moe_experts(layer, h, meta, wg, wu, wd):
    any_spec = pl.BlockSpec(memory_space=pl.ANY)
    up_shape, down_shape = (D_MODEL, MOE_HIDDEN), (MOE_HIDDEN, D_MODEL)
    return pl.pallas_call(
        functools.partial(_moe_body, layer),
        grid_spec=pltpu.PrefetchScalarGridSpec(
            num_scalar_prefetch=len(meta),
            grid=(MOE_TILES,),
            in_specs=[any_spec, any_spec, any_spec, any_spec],
            out_specs=any_spec,
            scratch_shapes=[pltpu.SMEM((MOE_ROWS,), jnp.int32),
                            pltpu.VMEM((2, MOE_TM) + SLAB, BF16), pltpu.VMEM((2, MOE_TM) + SLAB, BF16),
                            pltpu.VMEM((2,) + up_shape, F32), pltpu.VMEM((2,) + up_shape, F32),
                            pltpu.VMEM((2,) + down_shape, F32),
                            pltpu.VMEM(up_shape, BF16), pltpu.VMEM(up_shape, BF16), pltpu.VMEM(down_shape, BF16),
                            pltpu.SemaphoreType.DMA((2,)), pltpu.SemaphoreType.DMA((2,)),
                            pltpu.SemaphoreType.DMA((2,))],
        ),
        out_shape=jax.ShapeDtypeStruct((2 * N_TOK,) + SLAB, BF16),
        compiler_params=_cparams(("arbitrary",)),
        name="moe_experts",
    )(*meta, h, wg, wu, wd)


def _combine_body(split, x_ref, r_ref, gate_ref, g_ref, b_ref, z1_ref, z2_ref, *o_refs):
    r = r_ref[...]
    z1 = z1_ref[...].reshape(x_ref.shape).astype(F32)
    z2 = z2_ref[...].reshape(x_ref.shape).astype(F32)
    y = r[:, 2:3] * z1 + r[:, 3:4] * z2
    out = _layer_norm(DEEPNORM_ALPHA * x_ref[...] + gate_ref[0] * y, g_ref[...], b_ref[...])
    if split:
        is_prompt = pl.program_id(0) < N_PROMPT // x_ref.shape[0]

        @pl.when(is_prompt)
        def _():
            o_refs[0][...] = out

        @pl.when(jnp.logical_not(is_prompt))
        def _():
            o_refs[1][...] = out
    else:
        o_refs[0][...] = out


def _combine(x, z, route, gate, ln_g, ln_b, split=False):
    tm = 512
    nt = N_TOK // tm
    npt = N_PROMPT // tm
    row_spec = pl.BlockSpec((tm, D_MODEL), lambda i: (i, 0))
    vec_spec = pl.BlockSpec((1, D_MODEL), lambda i: (0, 0))
    if split:
        out_specs = [pl.BlockSpec((tm, D_MODEL), lambda i: (jnp.minimum(i, npt - 1), 0)),
                     pl.BlockSpec((tm, D_MODEL), lambda i: (jnp.maximum(i - npt, 0), 0))]
        out_shape = [jax.ShapeDtypeStruct((N_PROMPT, D_MODEL), F32), jax.ShapeDtypeStruct((N_SAMPLE, D_MODEL), F32)]
    else:
        out_specs = row_spec
        out_shape = jax.ShapeDtypeStruct((N_TOK, D_MODEL), F32)
    return pl.pallas_call(
        functools.partial(_combine_body, split),
        grid=(nt,),
        in_specs=[row_spec, pl.BlockSpec((tm, LANES), lambda i: (i, 0)),
                  pl.BlockSpec((1, 1, D_MODEL), lambda i: (_mod_group(i * tm), 0, 0)),
                  vec_spec, vec_spec, pl.BlockSpec((tm,) + SLAB, lambda i: (i, 0, 0)),
                  pl.BlockSpec((tm,) + SLAB, lambda i: (nt + i, 0, 0))],
        out_specs=out_specs,
        out_shape=out_shape,
        compiler_params=_cparams(("arbitrary",)),
        name="moe_combine_ln",
    )(x, route, gate, ln_g, ln_b, z, z)


def _od_post_body(n_prev, q_ref, k_ref, v_ref, xs_ref, xs_lo_ref, xs_hi_ref, bc_ref, bc_lo_ref, bc_hi_ref, dt_ref,
                  cos_ref, sin_ref, cw_ref, cb_ref, dtb_ref, *refs):
    prev = refs[:2 * n_prev]
    qr_ref, kr_ref, vb_ref, xc_ref, bcc_ref, dtp_ref, c_k, c_v = refs[2 * n_prev:]
    i = pl.program_id(0)
    cos = cos_ref[...]
    sin = sin_ref[...]
    k = k_ref[...]
    v = v_ref[...]
    qr_ref[...] = _rope(q_ref[...], cos, sin, GQA_HEAD_DIM // 4).astype(BF16)
    kr_ref[...] = _rope(k, cos, sin, GQA_HEAD_DIM // 4).astype(BF16)
    vb_ref[...] = v.astype(BF16)

    is_prompt = i < BATCH
    tile_in_seq = (i - BATCH) % (DEC_SEQ // POST_TM)
    seq_start = is_prompt | (tile_in_seq == 0)
    seq_end = is_prompt | (tile_in_seq == DEC_SEQ // POST_TM - 1)

    def conv_silu(x, lo_ref, hi_ref, w, b):
        rows = x.shape[0]
        r = lax.broadcasted_iota(jnp.int32, x.shape, 0)
        lo = jnp.where(seq_start, 0.0, lo_ref[7:8, :])
        hi = jnp.where(seq_end, 0.0, hi_ref[0:1, :])
        before = jnp.where(r == 0, lo, pltpu.roll(x, 1, 0))
        after = jnp.where(r == rows - 1, hi, pltpu.roll(x, rows - 1, 0))
        return _silu(before * w[0:1] + x * w[1:2] + after * w[2:3] + b)

    xc_ref[...] = conv_silu(xs_ref[...], xs_lo_ref, xs_hi_ref, cw_ref[:, :SSD_D_INNER], cb_ref[:, :SSD_D_INNER])
    bcc_ref[...] = conv_silu(bc_ref[...], bc_lo_ref, bc_hi_ref, cw_ref[:, SSD_D_INNER:], cb_ref[:, SSD_D_INNER:])
    t = dt_ref[...] + dtb_ref[...]
    dtp_ref[...] = jnp.maximum(t, 0.0) + jnp.log1p(jnp.exp(-jnp.abs(t)))

    @pl.when(is_prompt)
    def _():
        if n_prev:
            c_k[0, 0] = prev[0][0, 0]
            c_v[0, 0] = prev[1][0, 0]
        c_k[0, n_prev] = k.reshape(SEQ, GQA_KV_HEADS, GQA_HEAD_DIM)
        c_v[0, n_prev] = v.reshape(SEQ, GQA_KV_HEADS, GQA_HEAD_DIM)


def _od_post(proj, cos, sin, conv_w, conv_b, dt_bias, prev):
    tm = POST_TM
    halo = 8
    per = tm // halo
    n_halo = N_TOK // halo
    n_prev = 1 if prev else 0
    blk = lambda w, c: pl.BlockSpec((tm, w), lambda i: (i, c))
    lo = lambda w, c: pl.BlockSpec((halo, w), lambda i: (jnp.maximum(i * per - 1, 0), c))
    hi = lambda w, c: pl.BlockSpec((halo, w), lambda i: (jnp.minimum((i + 1) * per, n_halo - 1), c))
    full = lambda a: pl.BlockSpec(a.shape, lambda i: (0, 0))
    oblk = lambda w: pl.BlockSpec((tm, w), lambda i: (i, 0))
    cache_spec = lambda layers: pl.BlockSpec((1, layers, SEQ, GQA_KV_HEADS, GQA_HEAD_DIM),
                                             lambda i: (jnp.minimum(i, BATCH - 1), 0, 0, 0, 0))
    cache_shape = jax.ShapeDtypeStruct((BATCH, n_prev + 1, SEQ, GQA_KV_HEADS, GQA_HEAD_DIM), F32)
    return pl.pallas_call(
        functools.partial(_od_post_body, n_prev),
        grid=(N_TOK // tm,),
        in_specs=[blk(1024, 0), blk(256, 12), blk(256, 13),
                  blk(1024, 1), lo(1024, 1), hi(1024, 1), blk(512, 7), lo(512, 7), hi(512, 7), blk(LANES, 32),
                  blk(LANES, 0), blk(LANES, 0), full(conv_w), full(conv_b), full(dt_bias)]
        + [cache_spec(1)] * (2 * n_prev),
        out_specs=[oblk(1024), oblk(256), oblk(256), oblk(1024), oblk(512), oblk(LANES),
                   cache_spec(n_prev + 1), cache_spec(n_prev + 1)],
        out_shape=[
            jax.ShapeDtypeStruct((N_TOK, 1024), BF16),
            jax.ShapeDtypeStruct((N_TOK, 256), BF16),
            jax.ShapeDtypeStruct((N_TOK, 256), BF16),
            jax.ShapeDtypeStruct((N_TOK, 1024), F32),
            jax.ShapeDtypeStruct((N_TOK, 512), F32),
            jax.ShapeDtypeStruct((N_TOK, LANES), F32),
            cache_shape, cache_shape,
        ],
        compiler_params=_cparams(("arbitrary",)),
        name="odd_post",
    )(proj, proj, proj, proj, proj, proj, proj, proj, proj, proj, cos, sin, conv_w, conv_b, dt_bias, *prev)


def _sink_column(sink_ref, kh, rows):
    g = GQA_HEADS // GQA_KV_HEADS
    return jnp.concatenate(
        [jnp.broadcast_to(sink_ref[kh * g + j:kh * g + j + 1, 0:1], (rows, 1)) for j in range(g)], axis=0)


def _gqa_prompt_body(q_ref, k_ref, v_ref, sink_ref, o_ref):
    g = GQA_HEADS // GQA_KV_HEADS
    d = GQA_HEAD_DIM
    rows = q_ref.shape[0]
    scale = d ** -0.5
    outs = []
    for kh in range(GQA_KV_HEADS):
        qg = jnp.concatenate([q_ref[:, (kh * g + j) * d:(kh * g + j + 1) * d] for j in range(g)], axis=0)
        k = k_ref[:, kh * d:(kh + 1) * d]
        v = v_ref[:, kh * d:(kh + 1) * d]
        s = _nt(qg, k) * scale
        snk = _sink_column(sink_ref, kh, rows)
        m = jnp.maximum(jnp.max(s, axis=-1, keepdims=True), snk)
        e = jnp.exp(s - m)
        l = jnp.sum(e, axis=-1, keepdims=True) + jnp.exp(snk - m)
        o = jnp.dot(e.astype(BF16), v, preferred_element_type=F32) / l
        outs.extend(o[j * rows:(j + 1) * rows] for j in range(g))
    o_ref[...] = jnp.concatenate(outs, axis=1).astype(BF16)


def _gqa_prompt(q, k, v, sink, adaln=None):
    spec = lambda w: pl.BlockSpec((SEQ, w), lambda i: (i, 0))
    in_specs = [spec(1024), spec(256), spec(256), pl.BlockSpec((8, LANES), lambda i: (0, 0))]
    args = [q, k, v, sink]
    out_specs = [spec(1024)]
    out_shape = [jax.ShapeDtypeStruct((N_PROMPT, 1024), BF16)]
    body = _gqa_prompt_body
    if adaln is not None:
        mod_layer, mod_args = adaln
        m_in, m_out, m_shape = _adaln_specs(mod_layer, BATCH, lambda i: i)
        in_specs += m_in
        args += list(mod_args)
        out_specs.append(m_out)
        out_shape.append(m_shape)

        def body(q_ref, k_ref, v_ref, sink_ref, c_ref, w_ref, b_ref, o_ref, mo_ref):
            _gqa_prompt_body(q_ref, k_ref, v_ref, sink_ref, o_ref)
            _adaln_tile(c_ref, w_ref, b_ref, mo_ref)

    return pl.pallas_call(
        body,
        grid=(BATCH,),
        in_specs=in_specs,
        out_specs=out_specs,
        out_shape=out_shape,
        compiler_params=_cparams(("parallel",)),
        name="gqa_dense",
    )(*args)


def _gqa_window_body(q_ref, k_ref, v_ref, kc_ref, vc_ref, sink_ref, o_ref):
    g = GQA_HEADS // GQA_KV_HEADS
    d = GQA_HEAD_DIM
    blk = q_ref.shape[0]
    span = 3 * blk
    n = pl.program_id(1)
    start = pl.multiple_of(jnp.clip((n - 1) * blk, 0, DEC_SEQ - span), blk)
    scale = d ** -0.5
    qpos = n * blk + lax.broadcasted_iota(jnp.int32, (blk, span), 0)
    kpos = start + lax.broadcasted_iota(jnp.int32, (blk, span), 1)
    valid1 = jnp.abs(qpos - kpos) <= WINDOW
    valid = jnp.concatenate([valid1] * g, axis=0)
    outs = []
    for kh in range(GQA_KV_HEADS):
        qg = jnp.concatenate([q_ref[:, (kh * g + j) * d:(kh * g + j + 1) * d] for j in range(g)], axis=0)
        kl = k_ref[pl.ds(start, span), kh * d:(kh + 1) * d]
        vl = v_ref[pl.ds(start, span), kh * d:(kh + 1) * d]
        kc = kc_ref[:, kh * d:(kh + 1) * d]
        vc = vc_ref[:, kh * d:(kh + 1) * d]
        sl = jnp.where(valid, _nt(qg, kl) * scale, -jnp.inf)
        sc = _nt(qg, kc) * scale
        snk = _sink_column(sink_ref, kh, blk)
        m = jnp.maximum(jnp.maximum(jnp.max(sl, axis=-1, keepdims=True), jnp.max(sc, axis=-1, keepdims=True)), snk)
        el = jnp.exp(sl - m)
        ec = jnp.exp(sc - m)
        l = jnp.sum(el, axis=-1, keepdims=True) + jnp.sum(ec, axis=-1, keepdims=True) + jnp.exp(snk - m)
        o = (jnp.dot(el.astype(BF16), vl, preferred_element_type=F32)
             + jnp.dot(ec.astype(BF16), vc, preferred_element_type=F32)) / l
        outs.extend(o[j * blk:(j + 1) * blk] for j in range(g))
    o_ref[...] = jnp.concatenate(outs, axis=1).astype(BF16)


def _gqa_window(q, k, v, cache_k, cache_v, sink):
    blk = WINDOW
    nq = DEC_SEQ // blk
    q0 = N_PROMPT // blk
    s0 = N_PROMPT // DEC_SEQ
    seq_spec = pl.BlockSpec((DEC_SEQ, 256), lambda b, j: (s0 + b, 0))
    ctx_spec = pl.BlockSpec((PAST_LEN, 256), lambda b, j: (b, 0))
    return pl.pallas_call(
        _gqa_window_body,
        grid=(DEC_BATCH, nq),
        in_specs=[pl.BlockSpec((blk, 1024), lambda b, j: (q0 + b * nq + j, 0)), seq_spec, seq_spec, ctx_spec,
                  ctx_spec,
                  pl.BlockSpec((8, LANES), lambda b, j: (0, 0))],
        out_specs=pl.BlockSpec((blk, 1024), lambda b, j: (b * nq + j, 0)),
        out_shape=jax.ShapeDtypeStruct((N_SAMPLE, 1024), BF16),
        compiler_params=_cparams(("parallel", "parallel")),
        name="gqa_window",
    )(q, k, v, cache_k, cache_v, sink)


def _expand(x, sel):
    hi = x.astype(BF16)
    r1 = x - hi.astype(F32)
    mid = r1.astype(BF16)
    lo = (r1 - mid.astype(F32)).astype(BF16)
    return (jnp.dot(hi, sel, preferred_element_type=F32) + jnp.dot(mid, sel, preferred_element_type=F32)
            + jnp.dot(lo, sel, preferred_element_type=F32))


def _ssd_body(has_h0, xs_ref, bc_ref, dt_ref, alog_ref, e1_ref, e2_ref, *refs):
    if has_h0:
        h0f_ref, h0b_ref, y_ref, stf_ref, stb_ref = refs
    else:
        y_ref, stf_ref, stb_ref = refs
    d = pl.program_id(1)
    c = pl.program_id(2)
    q = SSD_CHUNK
    ppg = SSD_HEADS // SSD_GROUPS // 2
    hd = SSD_HEAD_DIM

    row = lax.broadcasted_iota(jnp.int32, (q, LANES), 0)
    a = -jnp.exp(alog_ref[...])
    dta = dt_ref[...] * a

    def scan(x, rev):
        k = 1
        while k < q:
            if rev:
                x = x + jnp.where(row < q - k, pltpu.roll(x, q - k, 0), 0.0)
            else:
                x = x + jnp.where(row >= k, pltpu.roll(x, k, 0), 0.0)
            k *= 2
        return x

    li = lax.broadcasted_iota(jnp.int32, (q, q), 0)
    si = lax.broadcasted_iota(jnp.int32, (q, q), 1)

    def run(rev):
        st_ref = stb_ref if rev else stf_ref

        @pl.when(c == 0)
        def _():
            if has_h0:
                st_ref[0] = (h0b_ref if rev else h0f_ref)[0, 0]
            else:
                st_ref[0] = jnp.zeros(st_ref.shape[1:], F32)

        cum = scan(dta, rev)
        cum_t = cum.T
        e_row = 0 if rev else q - 1
        mask = (li <= si) if rev else (li >= si)
        dt_x = _expand(dt_ref[...], e1_ref[0])
        cum_x = _expand(cum, e1_ref[0])
        cum_b = _expand(cum, e2_ref[0])
        edge_x = cum_x[e_row:e_row + 1, :]
        xdt = xs_ref[...] * dt_x
        xdt16 = xdt.astype(BF16)
        xw16 = (xdt * jnp.exp(edge_x - cum_x)).astype(BF16)
        ecum = jnp.exp(cum_x)
        lower = si < hd
        ys = []
        for g in range(SSD_GROUPS):
            bm16 = bc_ref[:, g * SSD_STATE:(g + 1) * SSD_STATE].astype(BF16)
            cm16 = bc_ref[:, (SSD_GROUPS + g) * SSD_STATE:(SSD_GROUPS + g + 1) * SSD_STATE].astype(BF16)
            cb = _nt(cm16, bm16)
            for pp in range(ppg):
                p = g * ppg + pp
                cols = slice(p * LANES, (p + 1) * LANES)
                y_pair, fac = [], []
                for u in range(2):
                    h = 2 * p + u
                    j = (SSD_HEADS if rev else 0) + h
                    seg = cum_b[:, h * LANES:(h + 1) * LANES] - cum_t[j:j + 1, :]
                    decay = jnp.exp(jnp.where(mask, seg, -jnp.inf))
                    y_pair.append(jnp.dot((cb * decay).astype(BF16), xdt16[:, cols], preferred_element_type=F32))
                    fac.append(jnp.broadcast_to(jnp.exp(cum_t[j:j + 1, e_row:e_row + 1]), (hd, SSD_STATE)))
                y_diag = jnp.where(lower, y_pair[0], y_pair[1])
                state = lax.dot_general(xw16[:, cols], bm16, (((0,), (0,)), ((), ())),
                                        preferred_element_type=F32)
                h_prev = st_ref[0, 2 * p:2 * p + 2].reshape(2 * hd, SSD_STATE)
                y_off = _nt(cm16, h_prev.astype(BF16)) * ecum[:, cols]
                ys.append(y_diag + y_off)
                h_new = h_prev * jnp.concatenate(fac, axis=0) + state
                st_ref[0, 2 * p:2 * p + 2] = h_new.reshape(2, hd, SSD_STATE)
        y_ref[0] = jnp.concatenate(ys, axis=1)

    @pl.when(d == 0)
    def _():
        run(False)

    @pl.when(d == 1)
    def _():
        run(True)


def _ssd(xs, bc, dtp, a_log, n_seq, seq, row_off, h0=None):
    b = n_seq
    t = n_seq * seq
    nc = seq // SSD_CHUNK
    c0 = row_off // SSD_CHUNK

    def rmap(i, d, c):
        return i * nc + jnp.where(d == 0, c, nc - 1 - c)

    blk = lambda w: pl.BlockSpec((SSD_CHUNK, w), lambda i, d, c: (c0 + rmap(i, d, c), 0))
    st_spec = pl.BlockSpec((1, SSD_HEADS, SSD_HEAD_DIM, SSD_STATE), lambda i, d, c: (i, 0, 0, 0))
    st_shape = jax.ShapeDtypeStruct((b, SSD_HEADS, SSD_HEAD_DIM, SSD_STATE), F32)
    def sel(w):
        row = jnp.arange(LANES)[None, :, None]
        head = SSD_HEADS * jnp.arange(2)[:, None, None] + (jnp.arange(SSD_HEADS * w) // w)[None, None, :]
        return (row == head).astype(BF16)

    sel_spec = lambda w: pl.BlockSpec((1, LANES, SSD_HEADS * w), lambda i, d, c: (d, 0, 0))
    in_specs = [blk(1024), blk(512), blk(LANES), pl.BlockSpec((1, LANES), lambda i, d, c: (0, 0)),
                sel_spec(SSD_HEAD_DIM), sel_spec(LANES)]
    args = [xs, bc, dtp, a_log, sel(SSD_HEAD_DIM), sel(LANES)]
    if h0 is not None:
        layer, h0_f, h0_b = h0
        h0_spec = pl.BlockSpec((1, 1, SSD_HEADS, SSD_HEAD_DIM, SSD_STATE), lambda i, d, c: (i, layer, 0, 0, 0))
        in_specs += [h0_spec, h0_spec]
        args += [h0_f, h0_b]
    return pl.pallas_call(
        functools.partial(_ssd_body, h0 is not None),
        grid=(b, 2, nc),
        in_specs=in_specs,
        out_specs=[pl.BlockSpec((1, SSD_CHUNK, 1024), lambda i, d, c: (d, rmap(i, d, c), 0)), st_spec, st_spec],
        out_shape=[jax.ShapeDtypeStruct((2, t, 1024), F32), st_shape, st_shape],
        compiler_params=_cparams(("parallel", "arbitrary", "arbitrary")),
        name="ssd_scan",
    )(*args)


def _ssd_gate_body(y_ref, xs_ref, z_ref, dsk_ref, g_ref, o_ref):
    y = (y_ref[0] + y_ref[1] + xs_ref[...] * dsk_ref[...]) * _silu(z_ref[...])
    w = SSD_D_INNER // SSD_GROUPS
    outs = [_rms(y[:, g * w:(g + 1) * w], g_ref[:, g * w:(g + 1) * w]) for g in range(SSD_GROUPS)]
    o_ref[...] = jnp.concatenate(outs, axis=1).astype(BF16)


def _ssd_gate(y2, xs, proj, d_skip, norm_g, row_off):
    tm = 512
    t = y2.shape[1]
    off = row_off // tm
    return pl.pallas_call(
        _ssd_gate_body,
        grid=(t // tm,),
        in_specs=[pl.BlockSpec((2, tm, 1024), lambda i: (0, i, 0)),
                  pl.BlockSpec((tm, 1024), lambda i: (off + i, 0)),
                  pl.BlockSpec((tm, 1024), lambda i: (off + i, 2)),
                  pl.BlockSpec((1, 1024), lambda i: (0, 0)),
                  pl.BlockSpec((1, 1024), lambda i: (0, 0))],
        out_specs=pl.BlockSpec((tm, 1024), lambda i: (i, 0)),
        out_shape=jax.ShapeDtypeStruct((t, 1024), BF16),
        compiler_params=_cparams(("parallel",)),
        name="ssd_gate_norm",
    )(y2, xs, proj, d_skip, norm_g)


def _even_layer(x, mod, layer_idx, i, p, ctx, tables, prev_caches, adaln):
    w_in, q_norm, kv_norm, wq_b, wkv_b, lam, subln, w_out = p
    ctx_ckv, ctx_kpe, ctx_dk, ctx_dv = ctx
    cos, sin = tables
    lam_init = 0.8 - 0.6 * math.exp(-0.3 * layer_idx)

    proj = _proj(x, mod[0], mod[1], _prep_even(w_in, i))
    wq = wq_b.reshape(MLA_Q_RANK, MLA_HEADS, MLA_NOPE + MLA_ROPE)
    wq_r = jnp.pad(wq, ((0, 0), (0, 0), (0, MLA_ROPE))).reshape(MLA_Q_RANK, -1).astype(BF16)
    wkv = wkv_b.astype(BF16)
    q, kv, dq, dk, dv, kpe, *caches = _ev_post(proj, cos, sin, q_norm[None], kv_norm[None], wq_r, wkv, prev_caches)
    n_even = ctx_ckv.shape[1]
    kv_ctx = _mm(ctx_ckv.reshape(-1, MLA_KV_RANK), wkv, BF16, PAST_LEN, DEC_BATCH,
                 row_block=lambda b: b * n_even + i)

    n_ctx = DEC_BATCH * PAST_LEN
    kpe_ctx = jnp.pad(ctx_kpe[:, i].reshape(n_ctx, MLA_ROPE), ((0, 0), (0, LANES - MLA_ROPE))).astype(BF16)
    dk_ctx = ctx_dk[:, i].reshape(n_ctx, -1).astype(BF16)
    dv_ctx = ctx_dv[:, i].reshape(n_ctx, -1).astype(BF16)

    args = (lam_init, q, cos, sin, kv, kpe, dq, dk, dv, lam, subln[None])
    o1p, o2p, *mod_next = _even_attn(*args, n_seq=BATCH, seq=SEQ, row_off=0, adaln=adaln)
    o1s, o2s = _even_attn(*args, n_seq=DEC_BATCH, seq=DEC_SEQ, row_off=N_PROMPT,
                          ctx=(kv_ctx, kpe_ctx, dk_ctx, dv_ctx))
    return (o1p, o1s, o2p, o2s), w_out.astype(BF16), caches, mod_next


def _odd_layer(x, mod, i, p, ctx, tables, prev_caches, adaln):
    w_in, sink, conv_w, conv_b, dt_bias, a_log, d_skip, norm_g, w_out = p
    k_ctx, v_ctx, h0_f, h0_b = ctx
    cos, sin = tables

    proj = _proj(x, mod[0], mod[1], _prep_odd(w_in, i))
    cb = conv_b[None]
    dtb = jnp.pad(dt_bias.reshape(1, -1), ((0, 0), (0, LANES - 2 * SSD_HEADS)))
    alog = jnp.pad(a_log.reshape(1, -1), ((0, 0), (0, LANES - 2 * SSD_HEADS)))
    q, k, v, xs, bc, dtp, *caches = _od_post(proj, cos, sin, conv_w, cb, dtb, prev_caches)
    sink_b = jnp.broadcast_to(sink[:, None], (GQA_HEADS, LANES))

    o_att_p, *mod_next = _gqa_prompt(q, k, v, sink_b, adaln=adaln)
    n_ctx = DEC_BATCH * PAST_LEN
    o_att_s = _gqa_window(q, k, v, k_ctx[:, i].reshape(n_ctx, -1).astype(BF16),
                          v_ctx[:, i].reshape(n_ctx, -1).astype(BF16), sink_b)

    y_p, st_f, st_b = _ssd(xs, bc, dtp, alog, BATCH, SEQ, 0)
    y_s, _, _ = _ssd(xs, bc, dtp, alog, DEC_BATCH, DEC_SEQ, N_PROMPT, h0=(i, h0_f, h0_b))
    dsk = jnp.repeat(d_skip, SSD_HEAD_DIM)[None]
    y_n_p = _ssd_gate(y_p, xs, proj, dsk, norm_g[None], 0)
    y_n_s = _ssd_gate(y_s, xs, proj, dsk, norm_g[None], N_PROMPT)
    return (o_att_p, o_att_s, y_n_p, y_n_s), w_out.astype(BF16), caches, (st_f, st_b), mod_next


def kernel(x_prompt, x_sample, cache_mla_ckv, cache_mla_kpe, cache_diff_k, cache_diff_v, cache_gqa_k, cache_gqa_v, state_ssd_fwd, state_ssd_bwd, c, c_ctx, w_mod, b_mod, ln1_g, ln1_b, ln2_g, ln2_b, ev_w_in, mla_q_norm, mla_kv_norm, mla_wq_b, mla_wkv_b, diff_lambda, diff_subln, ev_w_out, od_w_in, gqa_sink, ssd_conv_w, ssd_conv_b, ssd_dt_bias, ssd_a_log, ssd_d, ssd_norm, od_w_out, moe_router_group, moe_router_expert, moe_w_gate, moe_w_up, moe_w_down):
    x = jnp.concatenate([x_prompt.reshape(N_PROMPT, D_MODEL), x_sample.reshape(N_SAMPLE, D_MODEL)], axis=0)
    cs = jnp.concatenate([c_ctx[None], c, jnp.zeros((8 - 1 - DEC_BATCH, D_MODEL), F32)], axis=0)
    mod_args = (cs, w_mod, b_mod.reshape(DEPTH, 1, MOD_N))
    mod_rows = _adaln(*mod_args, 0)
    tables_64 = _rope_tables(64)
    tables_128 = _rope_tables(GQA_HEAD_DIM)

    ev_w_in = ev_w_in.astype(BF16)
    od_w_in = od_w_in.astype(BF16)
    ev_caches, od_caches, ssd_states = (), (), []
    for l in range(DEPTH):
        i = l // 2
        mods = mod_rows[:1 + DEC_BATCH].reshape(1 + DEC_BATCH, 6, 1, D_MODEL)
        mod = [mods[:, k] for k in range(6)]
        adaln = (l + 1, mod_args) if l + 1 < DEPTH else None
        if l % 2 == 0:
            p = (ev_w_in, mla_q_norm[i], mla_kv_norm[i], mla_wq_b[i], mla_wkv_b[i], diff_lambda[i],
                 diff_subln[i], ev_w_out[i])
            ctx = (cache_mla_ckv, cache_mla_kpe, cache_diff_k, cache_diff_v)
            mix, w_out, ev_caches, mod_next = _even_layer(x, mod, l, i, p, ctx, tables_64, ev_caches, adaln)
        else:
            p = (od_w_in, gqa_sink[i], ssd_conv_w[i], ssd_conv_b[i], ssd_dt_bias[i], ssd_a_log[i], ssd_d[i],
                 ssd_norm[i], od_w_out[i])
            ctx = (cache_gqa_k, cache_gqa_v, state_ssd_fwd, state_ssd_bwd)
            mix, w_out, od_caches, states, mod_next = _odd_layer(x, mod, i, p, ctx, tables_128, od_caches, adaln)
            ssd_states.append(states)
        w_router = jnp.concatenate([moe_router_group[l], moe_router_expert[l],
                                    jnp.zeros((D_MODEL, LANES - MOE_GROUPS - MOE_EXPERTS), F32)], axis=1)
        w_router_hi = w_router.astype(BF16)
        w_router = jnp.stack([w_router_hi, (w_router - w_router_hi.astype(F32)).astype(BF16)])
        x, h2, route = _out_proj(*mix, w_out, x, mod[2], ln1_g[l][None], ln1_b[l][None], mod[3], mod[4], w_router)
        z = _moe_experts(l, h2, _route_meta(route), moe_w_gate, moe_w_up, moe_w_down)
        x = _combine(x, z, route, mod[5], ln2_g[l][None], ln2_b[l][None], split=(l == DEPTH - 1))
        if mod_next:
            mod_rows = mod_next[0]

    y_prompt = x[0].reshape(BATCH, SEQ, D_MODEL)
    y_sample = x[1].reshape(DEC_BATCH, DEC_SEQ, D_MODEL)
    new_ssd_fwd = jnp.stack([s[0] for s in ssd_states], axis=1)
    new_ssd_bwd = jnp.stack([s[1] for s in ssd_states], axis=1)
    return (y_prompt, y_sample, *ev_caches, *od_caches, new_ssd_fwd, new_ssd_bwd)
```

```python
import functools
import math

import jax
import jax.numpy as jnp
from jax import lax
from jax.experimental import pallas as pl
from jax.experimental.pallas import tpu as pltpu

F32 = jnp.float32
BF16 = jnp.bfloat16

D_MODEL = 2048
BATCH = 16
SEQ = 256
DEPTH = 4
DEC_BATCH = 2
DEC_SEQ = 1024
PAST_LEN = 512
GRID_W = 64
ROPE_THETA = 10000.0
DEEPNORM_ALPHA = (2.0 * DEPTH) ** 0.25
LN_EPS = 1e-5
RMS_EPS = 1e-6

MLA_HEADS = 8
MLA_Q_RANK = 512
MLA_KV_RANK = 512
MLA_NOPE = 128
MLA_ROPE = 64
MLA_V = 128
DIFF_HEADS = 8
DIFF_QK = 64
DIFF_V = 128
GQA_HEADS = 8
GQA_KV_HEADS = 2
GQA_HEAD_DIM = 128
WINDOW = 128
SSD_HEADS = 16
SSD_HEAD_DIM = 64
SSD_D_INNER = SSD_HEADS * SSD_HEAD_DIM
SSD_GROUPS = 2
SSD_STATE = 128
SSD_CHUNK = 128
MOE_GROUPS = 4
MOE_EPG = 4
MOE_EXPERTS = 16
MOE_HIDDEN = 512

N_PROMPT = BATCH * SEQ
N_SAMPLE = DEC_BATCH * DEC_SEQ
N_TOK = N_PROMPT + N_SAMPLE
PROJ_N = 4224
LANES = 128
VMEM_LIMIT = 56 * 1024 * 1024

PROJ_TM, PROJ_TN = 1024, 1408
OUT_TM = 512
ATT_TQ = 256
ATT_HEADS_PER_STEP = 2
ATT_HEADS_PER_STEP_CTX = 4
POST_TM = 256
MOE_TM = 256
MOE_SLOTS = 2 * N_TOK
MOE_TILES = MOE_SLOTS // MOE_TM + MOE_EXPERTS
MOE_ROWS = MOE_TILES * MOE_TM
WEIGHT_DMA_PRIORITY = 1
SLAB = (D_MODEL // LANES, LANES)


def _cparams(sem):
    return pltpu.CompilerParams(dimension_semantics=sem, vmem_limit_bytes=VMEM_LIMIT)


def _mod_group(row0):
    return jnp.maximum(row0 // DEC_SEQ - (N_PROMPT // DEC_SEQ - 1), 0)


def _silu(x):
    return x / (1.0 + jnp.exp(-x))


def _nt(a, b):
    return lax.dot_general(a, b, (((1,), (1,)), ((), ())), preferred_element_type=F32)


def _rms(x, g):
    return x * lax.rsqrt(jnp.mean(x * x, axis=-1, keepdims=True) + RMS_EPS) * g


def _layer_norm(y, g, b):
    mu = jnp.mean(y, axis=-1, keepdims=True)
    yc = y - mu
    return yc * lax.rsqrt(jnp.mean(yc * yc, axis=-1, keepdims=True) + LN_EPS) * g + b


MOD_N = 6 * D_MODEL


def _adaln_tile(c_ref, w_ref, b_ref, o_ref):
    o_ref[...] = jnp.dot(_silu(c_ref[...]), w_ref[0], preferred_element_type=F32) + b_ref[0]


def _adaln_specs(layer, n_steps, step_of):
    tn = MOD_N // n_steps
    in_specs = [pl.BlockSpec((8, D_MODEL), lambda *g: (0, 0)),
                pl.BlockSpec((1, D_MODEL, tn), lambda *g: (layer, 0, step_of(*g))),
                pl.BlockSpec((1, 1, tn), lambda *g: (layer, 0, step_of(*g)))]
    return in_specs, pl.BlockSpec((8, tn), lambda *g: (0, step_of(*g))), jax.ShapeDtypeStruct((8, MOD_N), F32)


def _adaln(cs, w_mod, b_mod3, layer):
    n_steps = 12
    in_specs, out_spec, out_shape = _adaln_specs(layer, n_steps, lambda j: j)
    return pl.pallas_call(
        _adaln_tile,
        grid=(n_steps,),
        in_specs=in_specs,
        out_specs=out_spec,
        out_shape=out_shape,
        compiler_params=_cparams(("parallel",)),
        name="adaln",
    )(cs, w_mod, b_mod3)


def _proj_body(x_ref, sh_ref, sc_ref, w_ref, o_ref, h_scr):
    @pl.when(pl.program_id(1) == 0)
    def _():
        h = x_ref[...] * (1.0 + sc_ref[0]) + sh_ref[0]
        h_scr[...] = h.astype(BF16)

    o_ref[...] = jnp.dot(h_scr[...], w_ref[...], preferred_element_type=F32)


def _proj(x, shift, scale, w):
    tm, tn = PROJ_TM, PROJ_TN
    n = w.shape[1]
    mod_spec = pl.BlockSpec((1, 1, D_MODEL), lambda i, j: (_mod_group(i * tm), 0, 0))
    return pl.pallas_call(
        _proj_body,
        grid=(N_TOK // tm, n // tn),
        in_specs=[
            pl.BlockSpec((tm, D_MODEL), lambda i, j: (i, 0)),
            mod_spec,
            mod_spec,
            pl.BlockSpec((D_MODEL, tn), lambda i, j: (0, j)),
        ],
        out_specs=pl.BlockSpec((tm, tn), lambda i, j: (i, j)),
        out_shape=jax.ShapeDtypeStruct((N_TOK, n), F32),
        scratch_shapes=[pltpu.VMEM((tm, D_MODEL), BF16)],
        compiler_params=_cparams(("parallel", "arbitrary")),
        name="in_proj",
    )(x, shift, scale, w)


PROJ_BLOCKS = PROJ_N // LANES


def _prep_even_body(a_ref, b_ref, o_ref):
    c = pl.program_id(0)
    half = LANES // 2
    aligned = (MLA_Q_RANK + MLA_KV_RANK) // LANES

    @pl.when(c < aligned)
    def _():
        o_ref[...] = a_ref[0]

    @pl.when((c >= aligned) & (c < PROJ_BLOCKS - 1))
    def _():
        o_ref[...] = jnp.concatenate([a_ref[0][:, half:], b_ref[0][:, :half]], axis=1)

    @pl.when(c == PROJ_BLOCKS - 1)
    def _():
        k_pe = a_ref[0][:, :half]
        o_ref[...] = jnp.concatenate([k_pe, jnp.zeros_like(k_pe)], axis=1)


def _prep_even(w_in, layer):
    aligned = (MLA_Q_RANK + MLA_KV_RANK) // LANES
    last = PROJ_BLOCKS - 1
    return pl.pallas_call(
        _prep_even_body,
        grid=(PROJ_BLOCKS,),
        in_specs=[pl.BlockSpec((1, D_MODEL, LANES), lambda c: (layer, 0, jnp.where(c == last, aligned, c))),
                  pl.BlockSpec((1, D_MODEL, LANES), lambda c: (layer, 0, jnp.minimum(c + 1, last)))],
        out_specs=pl.BlockSpec((D_MODEL, LANES), lambda c: (0, c)),
        out_shape=jax.ShapeDtypeStruct((D_MODEL, PROJ_N), BF16),
        compiler_params=_cparams(("parallel",)),
        name="prep_w_even",
    )(w_in, w_in)


def _prep_odd_body(perm_ref, a_ref, dt_ref, o_ref):
    c = pl.program_id(0)

    @pl.when(c < PROJ_BLOCKS - 1)
    def _():
        o_ref[...] = a_ref[0]

    @pl.when(c == PROJ_BLOCKS - 1)
    def _():
        o_ref[...] = dt_ref[...]


def _prep_odd(w_in, layer):
    order = list(range(0, 8)) + list(range(20, 28)) + list(range(12, 20)) + [8, 9, 10, 11, 28, 29, 30, 31, 31]
    perm = jnp.asarray(order, jnp.int32)
    n_dt = 2 * SSD_HEADS
    w_dt = jnp.pad(w_in[layer, :, w_in.shape[2] - n_dt:], ((0, 0), (0, LANES - n_dt))).astype(BF16)
    return pl.pallas_call(
        _prep_odd_body,
        grid_spec=pltpu.PrefetchScalarGridSpec(
            num_scalar_prefetch=1,
            grid=(PROJ_BLOCKS,),
            in_specs=[pl.BlockSpec((1, D_MODEL, LANES), lambda c, perm_ref: (layer, 0, perm_ref[c])),
                      pl.BlockSpec((D_MODEL, LANES), lambda c, perm_ref: (0, 0))],
            out_specs=pl.BlockSpec((D_MODEL, LANES), lambda c, perm_ref: (0, c)),
        ),
        out_shape=jax.ShapeDtypeStruct((D_MODEL, PROJ_N), BF16),
        compiler_params=_cparams(("parallel",)),
        name="prep_w_odd",
    )(perm, w_in, w_dt)


def _mm_body(a_ref, b_ref, o_ref):
    o_ref[...] = jnp.dot(a_ref[...].astype(BF16), b_ref[...], preferred_element_type=F32).astype(o_ref.dtype)


def _mm(a, b, out_dtype, tm, n_tiles, row_block=lambda i: i):
    k = a.shape[1]
    n = b.shape[1]
    return pl.pallas_call(
        _mm_body,
        grid=(n_tiles,),
        in_specs=[pl.BlockSpec((tm, k), lambda i: (row_block(i), 0)), pl.BlockSpec((k, n), lambda i: (0, 0))],
        out_specs=pl.BlockSpec((tm, n), lambda i: (i, 0)),
        out_shape=jax.ShapeDtypeStruct((n_tiles * tm, n), out_dtype),
        compiler_params=_cparams(("parallel",)),
        name="mm_resident",
    )(a, b)


def _rope(x, cos, sin_signed, quarter):
    rows, w = x.shape
    reps = w // LANES
    if reps > 1:
        cos = jnp.concatenate([cos] * reps, axis=1)
        sin_signed = jnp.concatenate([sin_signed] * reps, axis=1)
    lane = lax.broadcasted_iota(jnp.int32, x.shape, 1)
    first = (lane % (2 * quarter)) < quarter
    up = pltpu.roll(x, w - quarter, 1)
    dn = pltpu.roll(x, quarter, 1)
    return x * cos + jnp.where(first, up, dn) * sin_signed


def _rope_tables(rdim):
    half = rdim // 2
    quarter = half // 2
    pos = jnp.arange(DEC_SEQ)
    row = (pos // GRID_W).astype(F32)
    col = (pos % GRID_W).astype(F32)
    inv = ROPE_THETA ** (-jnp.arange(quarter, dtype=F32) * 2.0 / half)
    lane = jnp.arange(LANES)
    r = lane % rdim
    use_col = (r // half) == 1
    j = r % quarter
    p = jnp.where(use_col[None, :], col[:, None], row[:, None])
    ang = p * inv[j][None, :]
    sign = jnp.where((r % half) < quarter, -1.0, 1.0).astype(F32)
    cos = jnp.cos(ang)
    sin = jnp.sin(ang) * sign[None, :]
    cos = jnp.concatenate([jnp.ones((N_PROMPT, LANES), F32), cos, cos], axis=0)
    sin = jnp.concatenate([jnp.zeros((N_PROMPT, LANES), F32), sin, sin], axis=0)
    return cos, sin


def _ev_post_body(n_prev, ql_ref, kvl_ref, dq_ref, dk_ref, dv_ref, kpe_ref, cos_ref, sin_ref, gq_ref, gkv_ref,
                  wq_ref, wkv_ref, *refs):
    prev = refs[:4 * n_prev]
    q_ref, kv_ref, dqr_ref, dkr_ref, dvb_ref, kper_ref, c_ckv, c_kpe, c_dk, c_dv = refs[4 * n_prev:]
    cos = cos_ref[...]
    sin = sin_ref[...]
    qn = _rms(ql_ref[...], gq_ref[...]).astype(BF16)
    q_ref[...] = jnp.dot(qn, wq_ref[...], preferred_element_type=F32).astype(BF16)
    ckv = _rms(kvl_ref[...], gkv_ref[...])
    kv_ref[...] = jnp.dot(ckv.astype(BF16), wkv_ref[...], preferred_element_type=F32).astype(BF16)
    dqr_ref[...] = _rope(dq_ref[...], cos, sin, DIFF_QK // 4).astype(BF16)
    dk = _rope(dk_ref[...], cos, sin, DIFF_QK // 4)
    dkr_ref[...] = dk.astype(BF16)
    dv = dv_ref[...]
    dvb_ref[...] = dv.astype(BF16)
    kpe = _rope(kpe_ref[...], cos, sin, MLA_ROPE // 4)
    kper_ref[...] = kpe.astype(BF16)

    @pl.when(pl.program_id(0) < BATCH)
    def _():
        if n_prev:
            c_ckv[0, 0] = prev[0][0, 0]
            c_kpe[0, 0] = prev[1][0, 0]
            c_dk[0, 0] = prev[2][0, 0]
            c_dv[0, 0] = prev[3][0, 0]
        c_ckv[0, n_prev] = ckv
        c_kpe[0, n_prev] = kpe[:, :MLA_ROPE]
        c_dk[0, n_prev] = dk.reshape(SEQ, DIFF_HEADS, 2 * DIFF_QK)
        c_dv[0, n_prev] = dv.reshape(SEQ, DIFF_HEADS, DIFF_V)


def _ev_post(proj, cos, sin, gq, gkv, wq, wkv, prev):
    tm = POST_TM
    n_prev = 1 if prev else 0
    nl = n_prev + 1
    row = lambda c: (lambda i: (i, c))
    seq4 = lambda i: (jnp.minimum(i, BATCH - 1), 0, 0, 0)
    seq5 = lambda i: (jnp.minimum(i, BATCH - 1), 0, 0, 0, 0)
    cache_shapes = [(MLA_KV_RANK,), (MLA_ROPE,), (DIFF_HEADS, 2 * DIFF_QK), (DIFF_HEADS, DIFF_V)]

    def cache_spec(layers, tail):
        return pl.BlockSpec((1, layers, SEQ) + tail, seq4 if len(tail) == 1 else seq5)

    return pl.pallas_call(
        functools.partial(_ev_post_body, n_prev),
        grid=(N_TOK // tm,),
        in_specs=[
            pl.BlockSpec((tm, 512), row(0)),
            pl.BlockSpec((tm, 512), row(1)),
            pl.BlockSpec((tm, 1024), row(1)),
            pl.BlockSpec((tm, 1024), row(2)),
            pl.BlockSpec((tm, 1024), row(3)),
            pl.BlockSpec((tm, LANES), row(32)),
            pl.BlockSpec((tm, LANES), row(0)),
            pl.BlockSpec((tm, LANES), row(0)),
            pl.BlockSpec((1, 512), lambda i: (0, 0)),
            pl.BlockSpec((1, 512), lambda i: (0, 0)),
            pl.BlockSpec(wq.shape, lambda i: (0, 0)),
            pl.BlockSpec(wkv.shape, lambda i: (0, 0)),
        ] + [cache_spec(1, t) for t in cache_shapes] * n_prev,
        out_specs=[
            pl.BlockSpec((tm, wq.shape[1]), row(0)),
            pl.BlockSpec((tm, wkv.shape[1]), row(0)),
            pl.BlockSpec((tm, 1024), row(0)),
            pl.BlockSpec((tm, 1024), row(0)),
            pl.BlockSpec((tm, 1024), row(0)),
            pl.BlockSpec((tm, LANES), row(0)),
        ] + [cache_spec(nl, t) for t in cache_shapes],
        out_shape=[
            jax.ShapeDtypeStruct((N_TOK, wq.shape[1]), BF16),
            jax.ShapeDtypeStruct((N_TOK, wkv.shape[1]), BF16),
            jax.ShapeDtypeStruct((N_TOK, 1024), BF16),
            jax.ShapeDtypeStruct((N_TOK, 1024), BF16),
            jax.ShapeDtypeStruct((N_TOK, 1024), BF16),
            jax.ShapeDtypeStruct((N_TOK, LANES), BF16),
        ] + [jax.ShapeDtypeStruct((BATCH, nl, SEQ) + t, F32) for t in cache_shapes],
        compiler_params=_cparams(("arbitrary",)),
        name="even_post",
    )(proj, proj, proj, proj, proj, proj, cos, sin, gq, gkv, wq, wkv, *prev)


def _softmax_av(scores, values):
    m = functools.reduce(jnp.maximum, [jnp.max(s, axis=-1, keepdims=True) for s in scores])
    es = [jnp.exp(s - m) for s in scores]
    l = sum(jnp.sum(e, axis=-1, keepdims=True) for e in es)
    o = sum(jnp.dot(e.astype(BF16), v, preferred_element_type=F32) for e, v in zip(es, values))
    return o / l


def _even_attn_body(lam_init, has_ctx, has_mod, hps, q_ref, cos_ref, sin_ref, kv_ref, kpe_ref, dq_ref, dk_ref, dv_ref,
                    lam_ref, sub_ref, *refs):
    refs = list(refs)
    if has_ctx:
        kvc_ref, kpec_ref, dkc_ref, dvc_ref = refs[:4]
        refs = refs[4:]
    if has_mod:
        _adaln_tile(*refs[:3], refs[-1])
        refs = refs[3:-1]
    omla_ref, odiff_ref = refs
    cos = cos_ref[...]
    sin = sin_ref[...]
    lam = lam_ref[...]
    lam_full = (jnp.exp(jnp.sum(lam[0:1] * lam[1:2], axis=-1, keepdims=True))
                - jnp.exp(jnp.sum(lam[2:3] * lam[3:4], axis=-1, keepdims=True)) + lam_init)
    mla_scale = (MLA_NOPE + MLA_ROPE) ** -0.5
    lane = lax.broadcasted_iota(jnp.int32, (q_ref.shape[0], 2 * DIFF_QK), 1)
    for u in range(hps):
        qc = slice(u * 256, (u + 1) * 256)
        dc = slice(u * LANES, (u + 1) * LANES)
        q = q_ref[:, qc]
        qr = _rope(q[:, MLA_NOPE:].astype(F32), cos, sin, MLA_ROPE // 4).astype(BF16)
        qcat = jnp.concatenate([q[:, :MLA_NOPE], qr], axis=1)
        kv = kv_ref[:, qc]
        scores = [_nt(qcat, jnp.concatenate([kv[:, :MLA_NOPE], kpe_ref[...]], axis=1)) * mla_scale]
        values = [kv[:, MLA_NOPE:]]
        if has_ctx:
            kvc = kvc_ref[:, qc]
            scores.append(_nt(qcat, jnp.concatenate([kvc[:, :MLA_NOPE], kpec_ref[...]], axis=1)) * mla_scale)
            values.append(kvc[:, MLA_NOPE:])
        omla_ref[:, dc] = _softmax_av(scores, values).astype(BF16)

        dq = dq_ref[:, dc] * jnp.asarray(DIFF_QK ** -0.5, BF16)
        zero = jnp.zeros_like(dq)
        keys = [dk_ref[:, dc]]
        values = [dv_ref[:, dc]]
        if has_ctx:
            keys.append(dkc_ref[:, dc])
            values.append(dvc_ref[:, dc])
        a1 = _softmax_av([_nt(jnp.where(lane < DIFF_QK, dq, zero), k) for k in keys], values)
        a2 = _softmax_av([_nt(jnp.where(lane >= DIFF_QK, dq, zero), k) for k in keys], values)
        odiff_ref[:, dc] = (_rms(a1 - lam_full * a2, sub_ref[...]) * (1.0 - lam_init)).astype(BF16)


def _even_attn(lam_init, q, cos, sin, kv, kpe, dq, dk, dv, lam, subln, n_seq, seq, row_off, ctx=None, adaln=None):
    tq = ATT_TQ
    hps = ATT_HEADS_PER_STEP if ctx is not None else ATT_HEADS_PER_STEP_CTX
    nq = seq // tq
    q0 = row_off // tq
    s0 = row_off // seq
    qspec = lambda w: pl.BlockSpec((tq, hps * w), lambda b, j, h: (q0 + b * nq + j, h))
    kspec = lambda w: pl.BlockSpec((seq, hps * w), lambda b, j, h: (s0 + b, h))
    tspec = pl.BlockSpec((tq, LANES), lambda b, j, h: (q0 + b * nq + j, 0))
    ospec = pl.BlockSpec((tq, hps * LANES), lambda b, j, h: (b * nq + j, h))
    in_specs = [qspec(256), tspec, tspec, kspec(256),
                pl.BlockSpec((seq, LANES), lambda b, j, h: (s0 + b, 0)),
                qspec(LANES), kspec(LANES), kspec(LANES),
                pl.BlockSpec((4, DIFF_QK), lambda b, j, h: (0, 0)),
                pl.BlockSpec((1, DIFF_V), lambda b, j, h: (0, 0))]
    args = [q, cos, sin, kv, kpe, dq, dk, dv, lam, subln]
    if ctx is not None:
        cspec = lambda w: pl.BlockSpec((PAST_LEN, hps * w), lambda b, j, h: (b, h))
        in_specs += [cspec(256), pl.BlockSpec((PAST_LEN, LANES), lambda b, j, h: (b, 0)), cspec(LANES), cspec(LANES)]
        args += list(ctx)
    nh = MLA_HEADS // hps
    out_specs = [ospec, ospec]
    out_shape = [jax.ShapeDtypeStruct((n_seq * seq, 1024), BF16)] * 2
    if adaln is not None:
        mod_layer, mod_args = adaln
        m_in, m_out, m_shape = _adaln_specs(mod_layer, n_seq * nq * nh, lambda b, j, h: (b * nq + j) * nh + h)
        in_specs += m_in
        args += list(mod_args)
        out_specs.append(m_out)
        out_shape.append(m_shape)
    return pl.pallas_call(
        functools.partial(_even_attn_body, lam_init, ctx is not None, adaln is not None, hps),
        grid=(n_seq, nq, nh),
        in_specs=in_specs,
        out_specs=out_specs,
        out_shape=out_shape,
        compiler_params=_cparams(("parallel", "parallel", "parallel")),
        name="even_attn",
    )(*args)


def _route(logits):
    lane = lax.broadcasted_iota(jnp.int32, logits.shape, 1).astype(F32)
    neg = -jnp.inf
    none = float(LANES)
    is_g = lane < MOE_GROUPS
    lg = jnp.where(is_g, logits, neg)
    mg = jnp.max(lg, axis=-1, keepdims=True)
    g_val = 1.0 / jnp.sum(jnp.exp(lg - mg), axis=-1, keepdims=True)
    g_idx = jnp.min(jnp.where(is_g & (lg == mg), lane, none), axis=-1, keepdims=True)
    e_lo = MOE_GROUPS + g_idx * MOE_EPG
    is_e = (lane >= e_lo) & (lane < e_lo + MOE_EPG)
    le = jnp.where(is_e, logits, neg)
    p = jnp.exp(le - jnp.max(le, axis=-1, keepdims=True))
    p1 = jnp.max(p, axis=-1, keepdims=True)
    i1 = jnp.min(jnp.where(is_e & (p == p1), lane, none), axis=-1, keepdims=True)
    rest = is_e & (lane != i1)
    p2 = jnp.max(jnp.where(rest, p, neg), axis=-1, keepdims=True)
    i2 = jnp.min(jnp.where(rest & (p == p2), lane, none), axis=-1, keepdims=True)
    s = g_val / (p1 + p2)
    return jnp.where(lane == 0.0, i1 - MOE_GROUPS,
                     jnp.where(lane == 1.0, i2 - MOE_GROUPS,
                               jnp.where(lane == 2.0, p1 * s, jnp.where(lane == 3.0, p2 * s, 0.0))))


def _out_proj_body(a1p_ref, a1s_ref, a2p_ref, a2s_ref, w_ref, x_ref, gate_ref, g_ref, b_ref, sh_ref, sc_ref,
                   wr_ref, xo_ref, h_ref, comb_ref):
    is_prompt = pl.program_id(0) < N_PROMPT // OUT_TM
    a1 = jnp.where(is_prompt, a1p_ref[...], a1s_ref[...])
    a2 = jnp.where(is_prompt, a2p_ref[...], a2s_ref[...])
    k1 = a1.shape[1]
    acc = jnp.dot(a1, w_ref[:k1, :], preferred_element_type=F32)
    acc = acc + jnp.dot(a2, w_ref[k1:, :], preferred_element_type=F32)
    xn = _layer_norm(DEEPNORM_ALPHA * x_ref[...] + gate_ref[0] * acc, g_ref[...], b_ref[...])
    xo_ref[...] = xn
    h = xn * (1.0 + sc_ref[0]) + sh_ref[0]
    h_ref[...] = h.reshape(h_ref.shape).astype(BF16)
    h_hi = h.astype(BF16)
    h_lo = (h - h_hi.astype(F32)).astype(BF16)
    logits = (jnp.dot(h_hi, wr_ref[0], preferred_element_type=F32)
              + jnp.dot(h_lo, wr_ref[0], preferred_element_type=F32)
              + jnp.dot(h_hi, wr_ref[1], preferred_element_type=F32))
    comb_ref[...] = _route(logits)


def _out_proj(a1p, a1s, a2p, a2s, w, x, gate, ln_g, ln_b, shift, scale, w_router):
    tm = OUT_TM
    np_tiles = N_PROMPT // tm
    k1, k2 = a1p.shape[1], a2p.shape[1]
    mod_spec = pl.BlockSpec((1, 1, D_MODEL), lambda i: (_mod_group(i * tm), 0, 0))
    vec_spec = pl.BlockSpec((1, D_MODEL), lambda i: (0, 0))
    row_spec = pl.BlockSpec((tm, D_MODEL), lambda i: (i, 0))
    pspec = lambda k: pl.BlockSpec((tm, k), lambda i: (jnp.minimum(i, np_tiles - 1), 0))
    sspec = lambda k: pl.BlockSpec((tm, k), lambda i: (jnp.maximum(i - np_tiles, 0), 0))
    return pl.pallas_call(
        _out_proj_body,
        grid=(N_TOK // tm,),
        in_specs=[
            pspec(k1), sspec(k1), pspec(k2), sspec(k2),
            pl.BlockSpec((k1 + k2, D_MODEL), lambda i: (0, 0)),
            row_spec, mod_spec, vec_spec, vec_spec, mod_spec, mod_spec,
            pl.BlockSpec((2, D_MODEL, LANES), lambda i: (0, 0, 0)),
        ],
        out_specs=[row_spec, pl.BlockSpec((tm,) + SLAB, lambda i: (i, 0, 0)),
                   pl.BlockSpec((tm, LANES), lambda i: (i, 0))],
        out_shape=[
            jax.ShapeDtypeStruct((N_TOK, D_MODEL), F32),
            jax.ShapeDtypeStruct((N_TOK,) + SLAB, BF16),
            jax.ShapeDtypeStruct((N_TOK, LANES), F32),
        ],
        compiler_params=_cparams(("parallel",)),
        name="out_proj_ln_router",
    )(a1p, a1s, a2p, a2s, w, x, gate, ln_g, ln_b, shift, scale, w_router)


def _route_meta(route):
    e = route[:, :2].astype(jnp.int32).reshape(-1)
    onehot = (e[:, None] == jnp.arange(MOE_EXPERTS, dtype=jnp.int32)[None, :]).astype(jnp.int32)
    csum = jnp.cumsum(onehot, axis=0)
    rank = jnp.sum((csum - onehot) * onehot, axis=1)
    count = csum[-1]
    ntile = (count + MOE_TM - 1) // MOE_TM
    tile_end = jnp.cumsum(ntile)
    tile_off = tile_end - ntile
    pos = jnp.sum(onehot * tile_off[None, :], axis=1) * MOE_TM + rank
    n_used = tile_end[-1:]
    j = jnp.arange(MOE_TILES, dtype=jnp.int32)
    tile_expert = jnp.sum((tile_end[None, :] <= jnp.minimum(j, n_used - 1)[:, None]).astype(jnp.int32), axis=1)
    is_e = tile_expert[:, None] == jnp.arange(MOE_EXPERTS, dtype=jnp.int32)[None, :]
    left = jnp.sum(jnp.where(is_e, (count - (j[:, None] - tile_off[None, :]) * MOE_TM)[...], 0), axis=1)
    n_valid = jnp.where(j < n_used, jnp.clip(left, 0, MOE_TM), 0)
    ids = jnp.arange(MOE_EXPERTS, dtype=jnp.int32)
    later = (ids[None, :] > ids[:, None]) & (ntile[None, :] > 0)
    next_of = jnp.min(jnp.where(later, ids[None, :], MOE_EXPERTS), axis=1)
    next_of = jnp.where(next_of == MOE_EXPERTS, -1, next_of)
    next_expert = jnp.sum(jnp.where(is_e, next_of[None, :], 0), axis=1)
    ordinal = jnp.cumsum((ntile > 0).astype(jnp.int32)) - 1
    w_slot = jnp.sum(jnp.where(is_e, ordinal[None, :], 0), axis=1) % 2
    return pos, tile_expert, n_valid, n_used, next_expert, w_slot


def _moe_body(layer, pos_ref, te_ref, nv_ref, nu_ref, ne_ref, ws_ref, h_ref, wg_ref, wu_ref, wd_ref, z_ref,
              src_tok, dst_row, xbuf, zbuf, wgbuf, wubuf, wdbuf, wg_s, wu_s, wd_s, gsem, ssem, wsem):
    j = pl.program_id(0)
    slot = j % 2
    n_used = nu_ref[0]
    group = 8

    def weight_copies(e, s):
        return [pltpu.make_async_copy(wg_ref.at[layer, e], wgbuf.at[s], wsem.at[s]),
                pltpu.make_async_copy(wu_ref.at[layer, e], wubuf.at[s], wsem.at[s]),
                pltpu.make_async_copy(wd_ref.at[layer, e], wdbuf.at[s], wsem.at[s])]

    def gather_row(tile, s, r):
        return pltpu.make_async_copy(h_ref.at[src_tok[tile * MOE_TM + r]], xbuf.at[s, r], gsem.at[s])

    def scatter_row(tile, s, r):
        return pltpu.make_async_copy(zbuf.at[s, r], z_ref.at[dst_row[tile * MOE_TM + r]], ssem.at[s])

    def for_valid_rows(tile, fn):
        n = nv_ref[tile]

        def block(g, c):
            for u in range(group):
                fn(g * group + u, u % 2)
            return c

        def single(r, c):
            fn(r, 0)
            return c

        full = lax.shift_right_logical(n, 3)
        lax.fori_loop(0, full, block, 0)
        lax.fori_loop(full * group, n, single, 0)

    def wait_gather(tile, s):
        @pl.when(nv_ref[tile] == MOE_TM)
        def _():
            pltpu.make_async_copy(h_ref.at[pl.ds(0, MOE_TM)], xbuf.at[s], gsem.at[s]).wait()

        @pl.when(nv_ref[tile] != MOE_TM)
        def _():
            for_valid_rows(tile, lambda r, pr: gather_row(tile, s, r).wait())

    def wait_scatter(tile, s):
        @pl.when(nv_ref[tile] == MOE_TM)
        def _():
            pltpu.make_async_copy(zbuf.at[s], z_ref.at[pl.ds(0, MOE_TM)], ssem.at[s]).wait()

        @pl.when(nv_ref[tile] != MOE_TM)
        def _():
            for_valid_rows(tile, lambda r, pr: scatter_row(tile, s, r).wait())

    @pl.when(j == 0)
    def _():
        def fill(g, c):
            for u in range(group):
                row = pos_ref[g * group + u]
                tok = g * (group // 2) + u // 2
                src_tok[row] = tok
                dst_row[row] = (u % 2) * N_TOK + tok
            return c
        lax.fori_loop(0, MOE_SLOTS // group, fill, 0)
        xbuf[...] = jnp.zeros_like(xbuf)
        for_valid_rows(0, lambda r, pr: gather_row(0, 0, r).start(priority=pr))
        for cp in weight_copies(te_ref[0], ws_ref[0]):
            cp.start(priority=WEIGHT_DMA_PRIORITY)

    changed = (j == 0) | (te_ref[j] != te_ref[jnp.maximum(j - 1, 0)])

    @pl.when(changed)
    def _():
        ws = ws_ref[j]
        for cp in weight_copies(te_ref[j], ws):
            cp.wait()
        wg_s[...] = wgbuf[ws].astype(BF16)
        wu_s[...] = wubuf[ws].astype(BF16)
        wd_s[...] = wdbuf[ws].astype(BF16)

        @pl.when(ne_ref[j] >= 0)
        def _():
            for cp in weight_copies(ne_ref[j], 1 - ws):
                cp.start(priority=WEIGHT_DMA_PRIORITY)

    @pl.when(j < n_used)
    def _():
        @pl.when(j + 1 < n_used)
        def _():
            for_valid_rows(j + 1, lambda r, pr: gather_row(j + 1, 1 - slot, r).start(priority=pr))

        wait_gather(j, slot)

        @pl.when(j >= 2)
        def _():
            wait_scatter(j - 2, slot)

        x = xbuf[slot].reshape(MOE_TM, D_MODEL)
        hg = jnp.dot(x, wg_s[...], preferred_element_type=F32)
        hu = jnp.dot(x, wu_s[...], preferred_element_type=F32)
        act = (_silu(hg) * hu).astype(BF16)
        z = jnp.dot(act, wd_s[...], preferred_element_type=F32)
        zbuf[slot] = z.reshape((MOE_TM,) + SLAB).astype(BF16)
        for_valid_rows(j, lambda r, pr: scatter_row(j, slot, r).start(priority=pr))

        @pl.when(j == n_used - 1)
        def _():
            @pl.when(j >= 1)
            def _():
                wait_scatter(j - 1, 1 - slot)
            wait_scatter(j, slot)


def _moe_experts(layer, h, meta, wg, wu, wd):
    any_spec = pl.BlockSpec(memory_space=pl.ANY)
    up_shape, down_shape = (D_MODEL, MOE_HIDDEN), (MOE_HIDDEN, D_MODEL)
    return pl.pallas_call(
        functools.partial(_moe_body, layer),
        grid_spec=pltpu.PrefetchScalarGridSpec(
            num_scalar_prefetch=len(meta),
            grid=(MOE_TILES,),
            in_specs=[any_spec, any_spec, any_spec, any_spec],
            out_specs=any_spec,
            scratch_shapes=[pltpu.SMEM((MOE_ROWS,), jnp.int32), pltpu.SMEM((MOE_ROWS,), jnp.int32),
                            pltpu.VMEM((2, MOE_TM) + SLAB, BF16), pltpu.VMEM((2, MOE_TM) + SLAB, BF16),
                            pltpu.VMEM((2,) + up_shape, F32), pltpu.VMEM((2,) + up_shape, F32),
                            pltpu.VMEM((2,) + down_shape, F32),
                            pltpu.VMEM(up_shape, BF16), pltpu.VMEM(up_shape, BF16), pltpu.VMEM(down_shape, BF16),
                            pltpu.SemaphoreType.DMA((2,)), pltpu.SemaphoreType.DMA((2,)),
                            pltpu.SemaphoreType.DMA((2,))],
        ),
        out_shape=jax.ShapeDtypeStruct((2 * N_TOK,) + SLAB, BF16),
        compiler_params=_cparams(("arbitrary",)),
        name="moe_experts",
    )(*meta, h, wg, wu, wd)


def _combine_body(split, x_ref, r_ref, gate_ref, g_ref, b_ref, z1_ref, z2_ref, *o_refs):
    r = r_ref[...]
    z1 = z1_ref[...].reshape(x_ref.shape).astype(F32)
    z2 = z2_ref[...].reshape(x_ref.shape).astype(F32)
    y = r[:, 2:3] * z1 + r[:, 3:4] * z2
    out = _layer_norm(DEEPNORM_ALPHA * x_ref[...] + gate_ref[0] * y, g_ref[...], b_ref[...])
    if split:
        is_prompt = pl.program_id(0) < N_PROMPT // x_ref.shape[0]

        @pl.when(is_prompt)
        def _():
            o_refs[0][...] = out

        @pl.when(jnp.logical_not(is_prompt))
        def _():
            o_refs[1][...] = out
    else:
        o_refs[0][...] = out


def _combine(x, z, route, gate, ln_g, ln_b, split=False):
    tm = 512
    nt = N_TOK // tm
    npt = N_PROMPT // tm
    row_spec = pl.BlockSpec((tm, D_MODEL), lambda i: (i, 0))
    vec_spec = pl.BlockSpec((1, D_MODEL), lambda i: (0, 0))
    if split:
        out_specs = [pl.BlockSpec((tm, D_MODEL), lambda i: (jnp.minimum(i, npt - 1), 0)),
                     pl.BlockSpec((tm, D_MODEL), lambda i: (jnp.maximum(i - npt, 0), 0))]
        out_shape = [jax.ShapeDtypeStruct((N_PROMPT, D_MODEL), F32), jax.ShapeDtypeStruct((N_SAMPLE, D_MODEL), F32)]
    else:
        out_specs = row_spec
        out_shape = jax.ShapeDtypeStruct((N_TOK, D_MODEL), F32)
    return pl.pallas_call(
        functools.partial(_combine_body, split),
        grid=(nt,),
        in_specs=[row_spec, pl.BlockSpec((tm, LANES), lambda i: (i, 0)),
                  pl.BlockSpec((1, 1, D_MODEL), lambda i: (_mod_group(i * tm), 0, 0)),
                  vec_spec, vec_spec, pl.BlockSpec((tm,) + SLAB, lambda i: (i, 0, 0)),
                  pl.BlockSpec((tm,) + SLAB, lambda i: (nt + i, 0, 0))],
        out_specs=out_specs,
        out_shape=out_shape,
        compiler_params=_cparams(("arbitrary",)),
        name="moe_combine_ln",
    )(x, route, gate, ln_g, ln_b, z, z)


def _od_post_body(n_prev, q_ref, k_ref, v_ref, xs_ref, xs_lo_ref, xs_hi_ref, bc_ref, bc_lo_ref, bc_hi_ref, dt_ref,
                  cos_ref, sin_ref, cw_ref, cb_ref, dtb_ref, *refs):
    prev = refs[:2 * n_prev]
    qr_ref, kr_ref, vb_ref, xc_ref, bcc_ref, dtp_ref, c_k, c_v = refs[2 * n_prev:]
    i = pl.program_id(0)
    cos = cos_ref[...]
    sin = sin_ref[...]
    k = k_ref[...]
    v = v_ref[...]
    qr_ref[...] = _rope(q_ref[...], cos, sin, GQA_HEAD_DIM // 4).astype(BF16)
    kr_ref[...] = _rope(k, cos, sin, GQA_HEAD_DIM // 4).astype(BF16)
    vb_ref[...] = v.astype(BF16)

    is_prompt = i < BATCH
    tile_in_seq = (i - BATCH) % (DEC_SEQ // POST_TM)
    seq_start = is_prompt | (tile_in_seq == 0)
    seq_end = is_prompt | (tile_in_seq == DEC_SEQ // POST_TM - 1)

    def conv_silu(x, lo_ref, hi_ref, w, b):
        rows = x.shape[0]
        r = lax.broadcasted_iota(jnp.int32, x.shape, 0)
        lo = jnp.where(seq_start, 0.0, lo_ref[7:8, :])
        hi = jnp.where(seq_end, 0.0, hi_ref[0:1, :])
        before = jnp.where(r == 0, lo, pltpu.roll(x, 1, 0))
        after = jnp.where(r == rows - 1, hi, pltpu.roll(x, rows - 1, 0))
        return _silu(before * w[0:1] + x * w[1:2] + after * w[2:3] + b)

    xc_ref[...] = conv_silu(xs_ref[...], xs_lo_ref, xs_hi_ref, cw_ref[:, :SSD_D_INNER], cb_ref[:, :SSD_D_INNER])
    bcc_ref[...] = conv_silu(bc_ref[...], bc_lo_ref, bc_hi_ref, cw_ref[:, SSD_D_INNER:], cb_ref[:, SSD_D_INNER:])
    t = dt_ref[...] + dtb_ref[...]
    dtp_ref[...] = jnp.maximum(t, 0.0) + jnp.log1p(jnp.exp(-jnp.abs(t)))

    @pl.when(is_prompt)
    def _():
        if n_prev:
            c_k[0, 0] = prev[0][0, 0]
            c_v[0, 0] = prev[1][0, 0]
        c_k[0, n_prev] = k.reshape(SEQ, GQA_KV_HEADS, GQA_HEAD_DIM)
        c_v[0, n_prev] = v.reshape(SEQ, GQA_KV_HEADS, GQA_HEAD_DIM)


def _od_post(proj, cos, sin, conv_w, conv_b, dt_bias, prev):
    tm = POST_TM
    halo = 8
    per = tm // halo
    n_halo = N_TOK // halo
    n_prev = 1 if prev else 0
    blk = lambda w, c: pl.BlockSpec((tm, w), lambda i: (i, c))
    lo = lambda w, c: pl.BlockSpec((halo, w), lambda i: (jnp.maximum(i * per - 1, 0), c))
    hi = lambda w, c: pl.BlockSpec((halo, w), lambda i: (jnp.minimum((i + 1) * per, n_halo - 1), c))
    full = lambda a: pl.BlockSpec(a.shape, lambda i: (0, 0))
    oblk = lambda w: pl.BlockSpec((tm, w), lambda i: (i, 0))
    cache_spec = lambda layers: pl.BlockSpec((1, layers, SEQ, GQA_KV_HEADS, GQA_HEAD_DIM),
                                             lambda i: (jnp.minimum(i, BATCH - 1), 0, 0, 0, 0))
    cache_shape = jax.ShapeDtypeStruct((BATCH, n_prev + 1, SEQ, GQA_KV_HEADS, GQA_HEAD_DIM), F32)
    return pl.pallas_call(
        functools.partial(_od_post_body, n_prev),
        grid=(N_TOK // tm,),
        in_specs=[blk(1024, 0), blk(256, 12), blk(256, 13),
                  blk(1024, 1), lo(1024, 1), hi(1024, 1), blk(512, 7), lo(512, 7), hi(512, 7), blk(LANES, 32),
                  blk(LANES, 0), blk(LANES, 0), full(conv_w), full(conv_b), full(dt_bias)]
        + [cache_spec(1)] * (2 * n_prev),
        out_specs=[oblk(1024), oblk(256), oblk(256), oblk(1024), oblk(512), oblk(LANES),
                   cache_spec(n_prev + 1), cache_spec(n_prev + 1)],
        out_shape=[
            jax.ShapeDtypeStruct((N_TOK, 1024), BF16),
            jax.ShapeDtypeStruct((N_TOK, 256), BF16),
            jax.ShapeDtypeStruct((N_TOK, 256), BF16),
            jax.ShapeDtypeStruct((N_TOK, 1024), F32),
            jax.ShapeDtypeStruct((N_TOK, 512), F32),
            jax.ShapeDtypeStruct((N_TOK, LANES), F32),
            cache_shape, cache_shape,
        ],
        compiler_params=_cparams(("arbitrary",)),
        name="odd_post",
    )(proj, proj, proj, proj, proj, proj, proj, proj, proj, proj, cos, sin, conv_w, conv_b, dt_bias, *prev)


def _sink_column(sink_ref, kh, rows):
    g = GQA_HEADS // GQA_KV_HEADS
    return jnp.concatenate(
        [jnp.broadcast_to(sink_ref[kh * g + j:kh * g + j + 1, 0:1], (rows, 1)) for j in range(g)], axis=0)


def _gqa_prompt_body(q_ref, k_ref, v_ref, sink_ref, o_ref):
    g = GQA_HEADS // GQA_KV_HEADS
    d = GQA_HEAD_DIM
    rows = q_ref.shape[0]
    scale = d ** -0.5
    outs = []
    for kh in range(GQA_KV_HEADS):
        qg = jnp.concatenate([q_ref[:, (kh * g + j) * d:(kh * g + j + 1) * d] for j in range(g)], axis=0)
        k = k_ref[:, kh * d:(kh + 1) * d]
        v = v_ref[:, kh * d:(kh + 1) * d]
        s = _nt(qg, k) * scale
        snk = _sink_column(sink_ref, kh, rows)
        m = jnp.maximum(jnp.max(s, axis=-1, keepdims=True), snk)
        e = jnp.exp(s - m)
        l = jnp.sum(e, axis=-1, keepdims=True) + jnp.exp(snk - m)
        o = jnp.dot(e.astype(BF16), v, preferred_element_type=F32) / l
        outs.extend(o[j * rows:(j + 1) * rows] for j in range(g))
    o_ref[...] = jnp.concatenate(outs, axis=1).astype(BF16)


def _gqa_prompt(q, k, v, sink, adaln=None):
    spec = lambda w: pl.BlockSpec((SEQ, w), lambda i: (i, 0))
    in_specs = [spec(1024), spec(256), spec(256), pl.BlockSpec((8, LANES), lambda i: (0, 0))]
    args = [q, k, v, sink]
    out_specs = [spec(1024)]
    out_shape = [jax.ShapeDtypeStruct((N_PROMPT, 1024), BF16)]
    body = _gqa_prompt_body
    if adaln is not None:
        mod_layer, mod_args = adaln
        m_in, m_out, m_shape = _adaln_specs(mod_layer, BATCH, lambda i: i)
        in_specs += m_in
        args += list(mod_args)
        out_specs.append(m_out)
        out_shape.append(m_shape)

        def body(q_ref, k_ref, v_ref, sink_ref, c_ref, w_ref, b_ref, o_ref, mo_ref):
            _gqa_prompt_body(q_ref, k_ref, v_ref, sink_ref, o_ref)
            _adaln_tile(c_ref, w_ref, b_ref, mo_ref)

    return pl.pallas_call(
        body,
        grid=(BATCH,),
        in_specs=in_specs,
        out_specs=out_specs,
        out_shape=out_shape,
        compiler_params=_cparams(("parallel",)),
        name="gqa_dense",
    )(*args)


def _gqa_window_body(q_ref, k_ref, v_ref, kc_ref, vc_ref, sink_ref, o_ref):
    g = GQA_HEADS // GQA_KV_HEADS
    d = GQA_HEAD_DIM
    blk = q_ref.shape[0]
    span = 3 * blk
    n = pl.program_id(1)
    start = pl.multiple_of(jnp.clip((n - 1) * blk, 0, DEC_SEQ - span), blk)
    scale = d ** -0.5
    qpos = n * blk + lax.broadcasted_iota(jnp.int32, (blk, span), 0)
    kpos = start + lax.broadcasted_iota(jnp.int32, (blk, span), 1)
    valid1 = jnp.abs(qpos - kpos) <= WINDOW
    valid = jnp.concatenate([valid1] * g, axis=0)
    outs = []
    for kh in range(GQA_KV_HEADS):
        qg = jnp.concatenate([q_ref[:, (kh * g + j) * d:(kh * g + j + 1) * d] for j in range(g)], axis=0)
        kl = k_ref[pl.ds(start, span), kh * d:(kh + 1) * d]
        vl = v_ref[pl.ds(start, span), kh * d:(kh + 1) * d]
        kc = kc_ref[:, kh * d:(kh + 1) * d]
        vc = vc_ref[:, kh * d:(kh + 1) * d]
        sl = jnp.where(valid, _nt(qg, kl) * scale, -jnp.inf)
        sc = _nt(qg, kc) * scale
        snk = _sink_column(sink_ref, kh, blk)
        m = jnp.maximum(jnp.maximum(jnp.max(sl, axis=-1, keepdims=True), jnp.max(sc, axis=-1, keepdims=True)), snk)
        el = jnp.exp(sl - m)
        ec = jnp.exp(sc - m)
        l = jnp.sum(el, axis=-1, keepdims=True) + jnp.sum(ec, axis=-1, keepdims=True) + jnp.exp(snk - m)
        o = (jnp.dot(el.astype(BF16), vl, preferred_element_type=F32)
             + jnp.dot(ec.astype(BF16), vc, preferred_element_type=F32)) / l
        outs.extend(o[j * blk:(j + 1) * blk] for j in range(g))
    o_ref[...] = jnp.concatenate(outs, axis=1).astype(BF16)


def _gqa_window(q, k, v, cache_k, cache_v, sink):
    blk = WINDOW
    nq = DEC_SEQ // blk
    q0 = N_PROMPT // blk
    s0 = N_PROMPT // DEC_SEQ
    seq_spec = pl.BlockSpec((DEC_SEQ, 256), lambda b, j: (s0 + b, 0))
    ctx_spec = pl.BlockSpec((PAST_LEN, 256), lambda b, j: (b, 0))
    return pl.pallas_call(
        _gqa_window_body,
        grid=(DEC_BATCH, nq),
        in_specs=[pl.BlockSpec((blk, 1024), lambda b, j: (q0 + b * nq + j, 0)), seq_spec, seq_spec, ctx_spec,
                  ctx_spec,
                  pl.BlockSpec((8, LANES), lambda b, j: (0, 0))],
        out_specs=pl.BlockSpec((blk, 1024), lambda b, j: (b * nq + j, 0)),
        out_shape=jax.ShapeDtypeStruct((N_SAMPLE, 1024), BF16),
        compiler_params=_cparams(("parallel", "parallel")),
        name="gqa_window",
    )(q, k, v, cache_k, cache_v, sink)


def _expand(x, sel):
    hi = x.astype(BF16)
    r1 = x - hi.astype(F32)
    mid = r1.astype(BF16)
    lo = (r1 - mid.astype(F32)).astype(BF16)
    return (jnp.dot(hi, sel, preferred_element_type=F32) + jnp.dot(mid, sel, preferred_element_type=F32)
            + jnp.dot(lo, sel, preferred_element_type=F32))


def _ssd_body(has_h0, xs_ref, bc_ref, dt_ref, alog_ref, e1_ref, e2_ref, *refs):
    if has_h0:
        h0f_ref, h0b_ref, y_ref, stf_ref, stb_ref = refs
    else:
        y_ref, stf_ref, stb_ref = refs
    d = pl.program_id(1)
    c = pl.program_id(2)
    q = SSD_CHUNK
    ppg = SSD_HEADS // SSD_GROUPS // 2
    hd = SSD_HEAD_DIM

    row = lax.broadcasted_iota(jnp.int32, (q, LANES), 0)
    a = -jnp.exp(alog_ref[...])
    dta = dt_ref[...] * a

    def scan(x, rev):
        k = 1
        while k < q:
            if rev:
                x = x + jnp.where(row < q - k, pltpu.roll(x, q - k, 0), 0.0)
            else:
                x = x + jnp.where(row >= k, pltpu.roll(x, k, 0), 0.0)
            k *= 2
        return x

    li = lax.broadcasted_iota(jnp.int32, (q, q), 0)
    si = lax.broadcasted_iota(jnp.int32, (q, q), 1)

    def run(rev):
        st_ref = stb_ref if rev else stf_ref

        @pl.when(c == 0)
        def _():
            if has_h0:
                st_ref[0] = (h0b_ref if rev else h0f_ref)[0, 0]
            else:
                st_ref[0] = jnp.zeros(st_ref.shape[1:], F32)

        cum = scan(dta, rev)
        cum_t = cum.T
        e_row = 0 if rev else q - 1
        mask = (li <= si) if rev else (li >= si)
        dt_x = _expand(dt_ref[...], e1_ref[0])
        cum_x = _expand(cum, e1_ref[0])
        cum_b = _expand(cum, e2_ref[0])
        edge_x = cum_x[e_row:e_row + 1, :]
        xdt = xs_ref[...] * dt_x
        xdt16 = xdt.astype(BF16)
        xw16 = (xdt * jnp.exp(edge_x - cum_x)).astype(BF16)
        ecum = jnp.exp(cum_x)
        lower = si < hd
        ys = []
        for g in range(SSD_GROUPS):
            bm16 = bc_ref[:, g * SSD_STATE:(g + 1) * SSD_STATE].astype(BF16)
            cm16 = bc_ref[:, (SSD_GROUPS + g) * SSD_STATE:(SSD_GROUPS + g + 1) * SSD_STATE].astype(BF16)
            cb = _nt(cm16, bm16)
            for pp in range(ppg):
                p = g * ppg + pp
                cols = slice(p * LANES, (p + 1) * LANES)
                y_pair, fac = [], []
                for u in range(2):
                    h = 2 * p + u
                    j = (SSD_HEADS if rev else 0) + h
                    seg = cum_b[:, h * LANES:(h + 1) * LANES] - cum_t[j:j + 1, :]
                    decay = jnp.exp(jnp.where(mask, seg, -jnp.inf))
                    y_pair.append(jnp.dot((cb * decay).astype(BF16), xdt16[:, cols], preferred_element_type=F32))
                    fac.append(jnp.broadcast_to(jnp.exp(cum_t[j:j + 1, e_row:e_row + 1]), (hd, SSD_STATE)))
                y_diag = jnp.where(lower, y_pair[0], y_pair[1])
                state = lax.dot_general(xw16[:, cols], bm16, (((0,), (0,)), ((), ())),
                                        preferred_element_type=F32)
                h_prev = st_ref[0, 2 * p:2 * p + 2].reshape(2 * hd, SSD_STATE)
                y_off = _nt(cm16, h_prev.astype(BF16)) * ecum[:, cols]
                ys.append(y_diag + y_off)
                h_new = h_prev * jnp.concatenate(fac, axis=0) + state
                st_ref[0, 2 * p:2 * p + 2] = h_new.reshape(2, hd, SSD_STATE)
        y_ref[0] = jnp.concatenate(ys, axis=1)

    @pl.when(d == 0)
    def _():
        run(False)

    @pl.when(d == 1)
    def _():
        run(True)


def _ssd(xs, bc, dtp, a_log, n_seq, seq, row_off, h0=None):
    b = n_seq
    t = n_seq * seq
    nc = seq // SSD_CHUNK
    c0 = row_off // SSD_CHUNK

    def rmap(i, d, c):
        return i * nc + jnp.where(d == 0, c, nc - 1 - c)

    blk = lambda w: pl.BlockSpec((SSD_CHUNK, w), lambda i, d, c: (c0 + rmap(i, d, c), 0))
    st_spec = pl.BlockSpec((1, SSD_HEADS, SSD_HEAD_DIM, SSD_STATE), lambda i, d, c: (i, 0, 0, 0))
    st_shape = jax.ShapeDtypeStruct((b, SSD_HEADS, SSD_HEAD_DIM, SSD_STATE), F32)
    def sel(w):
        row = jnp.arange(LANES)[None, :, None]
        head = SSD_HEADS * jnp.arange(2)[:, None, None] + (jnp.arange(SSD_HEADS * w) // w)[None, None, :]
        return (row == head).astype(BF16)

    sel_spec = lambda w: pl.BlockSpec((1, LANES, SSD_HEADS * w), lambda i, d, c: (d, 0, 0))
    in_specs = [blk(1024), blk(512), blk(LANES), pl.BlockSpec((1, LANES), lambda i, d, c: (0, 0)),
                sel_spec(SSD_HEAD_DIM), sel_spec(LANES)]
    args = [xs, bc, dtp, a_log, sel(SSD_HEAD_DIM), sel(LANES)]
    if h0 is not None:
        layer, h0_f, h0_b = h0
        h0_spec = pl.BlockSpec((1, 1, SSD_HEADS, SSD_HEAD_DIM, SSD_STATE), lambda i, d, c: (i, layer, 0, 0, 0))
        in_specs += [h0_spec, h0_spec]
        args += [h0_f, h0_b]
    return pl.pallas_call(
        functools.partial(_ssd_body, h0 is not None),
        grid=(b, 2, nc),
        in_specs=in_specs,
        out_specs=[pl.BlockSpec((1, SSD_CHUNK, 1024), lambda i, d, c: (d, rmap(i, d, c), 0)), st_spec, st_spec],
        out_shape=[jax.ShapeDtypeStruct((2, t, 1024), F32), st_shape, st_shape],
        compiler_params=_cparams(("parallel", "arbitrary", "arbitrary")),
        name="ssd_scan",
    )(*args)


def _ssd_gate_body(y_ref, xs_ref, z_ref, dsk_ref, g_ref, o_ref):
    y = (y_ref[0] + y_ref[1] + xs_ref[...] * dsk_ref[...]) * _silu(z_ref[...])
    w = SSD_D_INNER // SSD_GROUPS
    outs = [_rms(y[:, g * w:(g + 1) * w], g_ref[:, g * w:(g + 1) * w]) for g in range(SSD_GROUPS)]
    o_ref[...] = jnp.concatenate(outs, axis=1).astype(BF16)


def _ssd_gate(y2, xs, proj, d_skip, norm_g, row_off):
    tm = 512
    t = y2.shape[1]
    off = row_off // tm
    return pl.pallas_call(
        _ssd_gate_body,
        grid=(t // tm,),
        in_specs=[pl.BlockSpec((2, tm, 1024), lambda i: (0, i, 0)),
                  pl.BlockSpec((tm, 1024), lambda i: (off + i, 0)),
                  pl.BlockSpec((tm, 1024), lambda i: (off + i, 2)),
                  pl.BlockSpec((1, 1024), lambda i: (0, 0)),
                  pl.BlockSpec((1, 1024), lambda i: (0, 0))],
        out_specs=pl.BlockSpec((tm, 1024), lambda i: (i, 0)),
        out_shape=jax.ShapeDtypeStruct((t, 1024), BF16),
        compiler_params=_cparams(("parallel",)),
        name="ssd_gate_norm",
    )(y2, xs, proj, d_skip, norm_g)


def _even_layer(x, mod, layer_idx, i, p, ctx, tables, prev_caches, adaln):
    w_in, q_norm, kv_norm, wq_b, wkv_b, lam, subln, w_out = p
    ctx_ckv, ctx_kpe, ctx_dk, ctx_dv = ctx
    cos, sin = tables
    lam_init = 0.8 - 0.6 * math.exp(-0.3 * layer_idx)

    proj = _proj(x, mod[0], mod[1], _prep_even(w_in, i))
    wq = wq_b.reshape(MLA_Q_RANK, MLA_HEADS, MLA_NOPE + MLA_ROPE)
    wq_r = jnp.pad(wq, ((0, 0), (0, 0), (0, MLA_ROPE))).reshape(MLA_Q_RANK, -1).astype(BF16)
    wkv = wkv_b.astype(BF16)
    q, kv, dq, dk, dv, kpe, *caches = _ev_post(proj, cos, sin, q_norm[None], kv_norm[None], wq_r, wkv, prev_caches)
    n_even = ctx_ckv.shape[1]
    kv_ctx = _mm(ctx_ckv.reshape(-1, MLA_KV_RANK), wkv, BF16, PAST_LEN, DEC_BATCH,
                 row_block=lambda b: b * n_even + i)

    n_ctx = DEC_BATCH * PAST_LEN
    kpe_ctx = jnp.pad(ctx_kpe[:, i].reshape(n_ctx, MLA_ROPE), ((0, 0), (0, LANES - MLA_ROPE))).astype(BF16)
    dk_ctx = ctx_dk[:, i].reshape(n_ctx, -1).astype(BF16)
    dv_ctx = ctx_dv[:, i].reshape(n_ctx, -1).astype(BF16)

    args = (lam_init, q, cos, sin, kv, kpe, dq, dk, dv, lam, subln[None])
    o1p, o2p, *mod_next = _even_attn(*args, n_seq=BATCH, seq=SEQ, row_off=0, adaln=adaln)
    o1s, o2s = _even_attn(*args, n_seq=DEC_BATCH, seq=DEC_SEQ, row_off=N_PROMPT,
                          ctx=(kv_ctx, kpe_ctx, dk_ctx, dv_ctx))
    return (o1p, o1s, o2p, o2s), w_out.astype(BF16), caches, mod_next


def _odd_layer(x, mod, i, p, ctx, tables, prev_caches, adaln):
    w_in, sink, conv_w, conv_b, dt_bias, a_log, d_skip, norm_g, w_out = p
    k_ctx, v_ctx, h0_f, h0_b = ctx
    cos, sin = tables

    proj = _proj(x, mod[0], mod[1], _prep_odd(w_in, i))
    cb = conv_b[None]
    dtb = jnp.pad(dt_bias.reshape(1, -1), ((0, 0), (0, LANES - 2 * SSD_HEADS)))
    alog = jnp.pad(a_log.reshape(1, -1), ((0, 0), (0, LANES - 2 * SSD_HEADS)))
    q, k, v, xs, bc, dtp, *caches = _od_post(proj, cos, sin, conv_w, cb, dtb, prev_caches)
    sink_b = jnp.broadcast_to(sink[:, None], (GQA_HEADS, LANES))

    o_att_p, *mod_next = _gqa_prompt(q, k, v, sink_b, adaln=adaln)
    n_ctx = DEC_BATCH * PAST_LEN
    o_att_s = _gqa_window(q, k, v, k_ctx[:, i].reshape(n_ctx, -1).astype(BF16),
                          v_ctx[:, i].reshape(n_ctx, -1).astype(BF16), sink_b)

    y_p, st_f, st_b = _ssd(xs, bc, dtp, alog, BATCH, SEQ, 0)
    y_s, _, _ = _ssd(xs, bc, dtp, alog, DEC_BATCH, DEC_SEQ, N_PROMPT, h0=(i, h0_f, h0_b))
    dsk = jnp.repeat(d_skip, SSD_HEAD_DIM)[None]
    y_n_p = _ssd_gate(y_p, xs, proj, dsk, norm_g[None], 0)
    y_n_s = _ssd_gate(y_s, xs, proj, dsk, norm_g[None], N_PROMPT)
    return (o_att_p, o_att_s, y_n_p, y_n_s), w_out.astype(BF16), caches, (st_f, st_b), mod_next


def kernel(x_prompt, x_sample, cache_mla_ckv, cache_mla_kpe, cache_diff_k, cache_diff_v, cache_gqa_k, cache_gqa_v, state_ssd_fwd, state_ssd_bwd, c, c_ctx, w_mod, b_mod, ln1_g, ln1_b, ln2_g, ln2_b, ev_w_in, mla_q_norm, mla_kv_norm, mla_wq_b, mla_wkv_b, diff_lambda, diff_subln, ev_w_out, od_w_in, gqa_sink, ssd_conv_w, ssd_conv_b, ssd_dt_bias, ssd_a_log, ssd_d, ssd_norm, od_w_out, moe_router_group, moe_router_expert, moe_w_gate, moe_w_up, moe_w_down):
    x = jnp.concatenate([x_prompt.reshape(N_PROMPT, D_MODEL), x_sample.reshape(N_SAMPLE, D_MODEL)], axis=0)
    cs = jnp.concatenate([c_ctx[None], c, jnp.zeros((8 - 1 - DEC_BATCH, D_MODEL), F32)], axis=0)
    mod_args = (cs, w_mod, b_mod.reshape(DEPTH, 1, MOD_N))
    mod_rows = _adaln(*mod_args, 0)
    tables_64 = _rope_tables(64)
    tables_128 = _rope_tables(GQA_HEAD_DIM)

    ev_w_in = ev_w_in.astype(BF16)
    od_w_in = od_w_in.astype(BF16)
    ev_caches, od_caches, ssd_states = (), (), []
    for l in range(DEPTH):
        i = l // 2
        mods = mod_rows[:1 + DEC_BATCH].reshape(1 + DEC_BATCH, 6, 1, D_MODEL)
        mod = [mods[:, k] for k in range(6)]
        adaln = (l + 1, mod_args) if l + 1 < DEPTH else None
        if l % 2 == 0:
            p = (ev_w_in, mla_q_norm[i], mla_kv_norm[i], mla_wq_b[i], mla_wkv_b[i], diff_lambda[i],
                 diff_subln[i], ev_w_out[i])
            ctx = (cache_mla_ckv, cache_mla_kpe, cache_diff_k, cache_diff_v)
            mix, w_out, ev_caches, mod_next = _even_layer(x, mod, l, i, p, ctx, tables_64, ev_caches, adaln)
        else:
            p = (od_w_in, gqa_sink[i], ssd_conv_w[i], ssd_conv_b[i], ssd_dt_bias[i], ssd_a_log[i], ssd_d[i],
                 ssd_norm[i], od_w_out[i])
            ctx = (cache_gqa_k, cache_gqa_v, state_ssd_fwd, state_ssd_bwd)
            mix, w_out, od_caches, states, mod_next = _odd_layer(x, mod, i, p, ctx, tables_128, od_caches, adaln)
            ssd_states.append(states)
        w_router = jnp.concatenate([moe_router_group[l], moe_router_expert[l],
                                    jnp.zeros((D_MODEL, LANES - MOE_GROUPS - MOE_EXPERTS), F32)], axis=1)
        w_router_hi = w_router.astype(BF16)
        w_router = jnp.stack([w_router_hi, (w_router - w_router_hi.astype(F32)).astype(BF16)])
        x, h2, route = _out_proj(*mix, w_out, x, mod[2], ln1_g[l][None], ln1_b[l][None], mod[3], mod[4], w_router)
        z = _moe_experts(l, h2, _route_meta(route), moe_w_gate, moe_w_up, moe_w_down)
        x = _combine(x, z, route, mod[5], ln2_g[l][None], ln2_b[l][None], split=(l == DEPTH - 1))
        if mod_next:
            mod_rows = mod_next[0]

    y_prompt = x[0].reshape(BATCH, SEQ, D_MODEL)
    y_sample = x[1].reshape(DEC_BATCH, DEC_SEQ, D_MODEL)
    new_ssd_fwd = jnp.stack([s[0] for s in ssd_states], axis=1)
    new_ssd_bwd = jnp.stack([s[1] for s in ssd_states], axis=1)
    return (y_prompt, y_sample, *ev_caches, *od_caches, new_ssd_fwd, new_ssd_bwd)
```

```python
import functools
import math

import jax
import jax.numpy as jnp
from jax import lax
from jax.experimental import pallas as pl
from jax.experimental.pallas import tpu as pltpu

F32 = jnp.float32
BF16 = jnp.bfloat16

D_MODEL = 2048
BATCH = 16
SEQ = 256
DEPTH = 4
DEC_BATCH = 2
DEC_SEQ = 1024
PAST_LEN = 512
GRID_W = 64
ROPE_THETA = 10000.0
DEEPNORM_ALPHA = (2.0 * DEPTH) ** 0.25
LN_EPS = 1e-5
RMS_EPS = 1e-6

MLA_HEADS = 8
MLA_Q_RANK = 512
MLA_KV_RANK = 512
MLA_NOPE = 128
MLA_ROPE = 64
MLA_V = 128
DIFF_HEADS = 8
DIFF_QK = 64
DIFF_V = 128
GQA_HEADS = 8
GQA_KV_HEADS = 2
GQA_HEAD_DIM = 128
WINDOW = 128
SSD_HEADS = 16
SSD_HEAD_DIM = 64
SSD_D_INNER = SSD_HEADS * SSD_HEAD_DIM
SSD_GROUPS = 2
SSD_STATE = 128
SSD_CHUNK = 128
MOE_GROUPS = 4
MOE_EPG = 4
MOE_EXPERTS = 16
MOE_HIDDEN = 512

N_PROMPT = BATCH * SEQ
N_SAMPLE = DEC_BATCH * DEC_SEQ
N_TOK = N_PROMPT + N_SAMPLE
PROJ_N = 4224
LANES = 128
VMEM_LIMIT = 56 * 1024 * 1024

PROJ_TM, PROJ_TN = 1024, 1408
OUT_TM = 512
ATT_TQ = 256
ATT_HEADS_PER_STEP = 4
ATT_HEADS_PER_STEP_CTX = 4
POST_TM = 256
MOE_TM = 256
MOE_SLOTS = 2 * N_TOK
MOE_TILES = MOE_SLOTS // MOE_TM + MOE_EXPERTS
MOE_ROWS = MOE_TILES * MOE_TM
WEIGHT_DMA_PRIORITY = 1
SLAB = (D_MODEL // LANES, LANES)


def _cparams(sem):
    return pltpu.CompilerParams(dimension_semantics=sem, vmem_limit_bytes=VMEM_LIMIT)


def _mod_group(row0):
    return jnp.maximum(row0 // DEC_SEQ - (N_PROMPT // DEC_SEQ - 1), 0)


def _silu(x):
    return x / (1.0 + jnp.exp(-x))


def _nt(a, b):
    return lax.dot_general(a, b, (((1,), (1,)), ((), ())), preferred_element_type=F32)


def _rms(x, g):
    return x * lax.rsqrt(jnp.mean(x * x, axis=-1, keepdims=True) + RMS_EPS) * g


def _layer_norm(y, g, b):
    mu = jnp.mean(y, axis=-1, keepdims=True)
    yc = y - mu
    return yc * lax.rsqrt(jnp.mean(yc * yc, axis=-1, keepdims=True) + LN_EPS) * g + b


MOD_N = 6 * D_MODEL


def _adaln_tile(c_ref, w_ref, b_ref, o_ref):
    o_ref[...] = jnp.dot(_silu(c_ref[...]), w_ref[0], preferred_element_type=F32) + b_ref[0]


def _adaln_specs(layer, n_steps, step_of):
    tn = MOD_N // n_steps
    in_specs = [pl.BlockSpec((8, D_MODEL), lambda *g: (0, 0)),
                pl.BlockSpec((1, D_MODEL, tn), lambda *g: (layer, 0, step_of(*g))),
                pl.BlockSpec((1, 1, tn), lambda *g: (layer, 0, step_of(*g)))]
    return in_specs, pl.BlockSpec((8, tn), lambda *g: (0, step_of(*g))), jax.ShapeDtypeStruct((8, MOD_N), F32)


def _adaln(cs, w_mod, b_mod3, layer):
    n_steps = 12
    in_specs, out_spec, out_shape = _adaln_specs(layer, n_steps, lambda j: j)
    return pl.pallas_call(
        _adaln_tile,
        grid=(n_steps,),
        in_specs=in_specs,
        out_specs=out_spec,
        out_shape=out_shape,
        compiler_params=_cparams(("parallel",)),
        name="adaln",
    )(cs, w_mod, b_mod3)


def _proj_body(x_ref, sh_ref, sc_ref, w_ref, o_ref, h_scr):
    @pl.when(pl.program_id(1) == 0)
    def _():
        h = x_ref[...] * (1.0 + sc_ref[0]) + sh_ref[0]
        h_scr[...] = h.astype(BF16)

    o_ref[...] = jnp.dot(h_scr[...], w_ref[...], preferred_element_type=F32)


def _proj(x, shift, scale, w):
    tm, tn = PROJ_TM, PROJ_TN
    n = w.shape[1]
    mod_spec = pl.BlockSpec((1, 1, D_MODEL), lambda i, j: (_mod_group(i * tm), 0, 0))
    return pl.pallas_call(
        _proj_body,
        grid=(N_TOK // tm, n // tn),
        in_specs=[
            pl.BlockSpec((tm, D_MODEL), lambda i, j: (i, 0)),
            mod_spec,
            mod_spec,
            pl.BlockSpec((D_MODEL, tn), lambda i, j: (0, j)),
        ],
        out_specs=pl.BlockSpec((tm, tn), lambda i, j: (i, j)),
        out_shape=jax.ShapeDtypeStruct((N_TOK, n), F32),
        scratch_shapes=[pltpu.VMEM((tm, D_MODEL), BF16)],
        compiler_params=_cparams(("parallel", "arbitrary")),
        name="in_proj",
    )(x, shift, scale, w)


PROJ_BLOCKS = PROJ_N // LANES


def _prep_even_body(a_ref, b_ref, o_ref):
    c = pl.program_id(0)
    half = LANES // 2
    aligned = (MLA_Q_RANK + MLA_KV_RANK) // LANES

    @pl.when(c < aligned)
    def _():
        o_ref[...] = a_ref[0]

    @pl.when((c >= aligned) & (c < PROJ_BLOCKS - 1))
    def _():
        o_ref[...] = jnp.concatenate([a_ref[0][:, half:], b_ref[0][:, :half]], axis=1)

    @pl.when(c == PROJ_BLOCKS - 1)
    def _():
        k_pe = a_ref[0][:, :half]
        o_ref[...] = jnp.concatenate([k_pe, jnp.zeros_like(k_pe)], axis=1)


def _prep_even(w_in, layer):
    aligned = (MLA_Q_RANK + MLA_KV_RANK) // LANES
    last = PROJ_BLOCKS - 1
    return pl.pallas_call(
        _prep_even_body,
        grid=(PROJ_BLOCKS,),
        in_specs=[pl.BlockSpec((1, D_MODEL, LANES), lambda c: (layer, 0, jnp.where(c == last, aligned, c))),
                  pl.BlockSpec((1, D_MODEL, LANES), lambda c: (layer, 0, jnp.minimum(c + 1, last)))],
        out_specs=pl.BlockSpec((D_MODEL, LANES), lambda c: (0, c)),
        out_shape=jax.ShapeDtypeStruct((D_MODEL, PROJ_N), BF16),
        compiler_params=_cparams(("parallel",)),
        name="prep_w_even",
    )(w_in, w_in)


def _prep_odd_body(perm_ref, a_ref, dt_ref, o_ref):
    c = pl.program_id(0)

    @pl.when(c < PROJ_BLOCKS - 1)
    def _():
        o_ref[...] = a_ref[0]

    @pl.when(c == PROJ_BLOCKS - 1)
    def _():
        o_ref[...] = dt_ref[...]


def _prep_odd(w_in, layer):
    order = list(range(0, 8)) + list(range(20, 28)) + list(range(12, 20)) + [8, 9, 10, 11, 28, 29, 30, 31, 31]
    perm = jnp.asarray(order, jnp.int32)
    n_dt = 2 * SSD_HEADS
    w_dt = jnp.pad(w_in[layer, :, w_in.shape[2] - n_dt:], ((0, 0), (0, LANES - n_dt))).astype(BF16)
    return pl.pallas_call(
        _prep_odd_body,
        grid_spec=pltpu.PrefetchScalarGridSpec(
            num_scalar_prefetch=1,
            grid=(PROJ_BLOCKS,),
            in_specs=[pl.BlockSpec((1, D_MODEL, LANES), lambda c, perm_ref: (layer, 0, perm_ref[c])),
                      pl.BlockSpec((D_MODEL, LANES), lambda c, perm_ref: (0, 0))],
            out_specs=pl.BlockSpec((D_MODEL, LANES), lambda c, perm_ref: (0, c)),
        ),
        out_shape=jax.ShapeDtypeStruct((D_MODEL, PROJ_N), BF16),
        compiler_params=_cparams(("parallel",)),
        name="prep_w_odd",
    )(perm, w_in, w_dt)


def _mm_body(a_ref, b_ref, o_ref):
    o_ref[...] = jnp.dot(a_ref[...].astype(BF16), b_ref[...], preferred_element_type=F32).astype(o_ref.dtype)


def _mm(a, b, out_dtype, tm, n_tiles, row_block=lambda i: i):
    k = a.shape[1]
    n = b.shape[1]
    return pl.pallas_call(
        _mm_body,
        grid=(n_tiles,),
        in_specs=[pl.BlockSpec((tm, k), lambda i: (row_block(i), 0)), pl.BlockSpec((k, n), lambda i: (0, 0))],
        out_specs=pl.BlockSpec((tm, n), lambda i: (i, 0)),
        out_shape=jax.ShapeDtypeStruct((n_tiles * tm, n), out_dtype),
        compiler_params=_cparams(("parallel",)),
        name="mm_resident",
    )(a, b)


def _rope(x, cos, sin_signed, quarter):
    rows, w = x.shape
    reps = w // LANES
    if reps > 1:
        cos = jnp.concatenate([cos] * reps, axis=1)
        sin_signed = jnp.concatenate([sin_signed] * reps, axis=1)
    lane = lax.broadcasted_iota(jnp.int32, x.shape, 1)
    first = (lane % (2 * quarter)) < quarter
    up = pltpu.roll(x, w - quarter, 1)
    dn = pltpu.roll(x, quarter, 1)
    return x * cos + jnp.where(first, up, dn) * sin_signed


def _rope_tables(rdim):
    half = rdim // 2
    quarter = half // 2
    pos = jnp.arange(DEC_SEQ)
    row = (pos // GRID_W).astype(F32)
    col = (pos % GRID_W).astype(F32)
    inv = ROPE_THETA ** (-jnp.arange(quarter, dtype=F32) * 2.0 / half)
    lane = jnp.arange(LANES)
    r = lane % rdim
    use_col = (r // half) == 1
    j = r % quarter
    p = jnp.where(use_col[None, :], col[:, None], row[:, None])
    ang = p * inv[j][None, :]
    sign = jnp.where((r % half) < quarter, -1.0, 1.0).astype(F32)
    cos = jnp.cos(ang)
    sin = jnp.sin(ang) * sign[None, :]
    cos = jnp.concatenate([jnp.ones((N_PROMPT, LANES), F32), cos, cos], axis=0)
    sin = jnp.concatenate([jnp.zeros((N_PROMPT, LANES), F32), sin, sin], axis=0)
    return cos, sin


def _ev_post_body(n_prev, ql_ref, kvl_ref, dq_ref, dk_ref, dv_ref, kpe_ref, cos_ref, sin_ref, gq_ref, gkv_ref,
                  wq_ref, wkv_ref, *refs):
    prev = refs[:4 * n_prev]
    q_ref, kv_ref, dqr_ref, dkr_ref, dvb_ref, kper_ref, c_ckv, c_kpe, c_dk, c_dv = refs[4 * n_prev:]
    cos = cos_ref[...]
    sin = sin_ref[...]
    qn = _rms(ql_ref[...], gq_ref[...]).astype(BF16)
    q_ref[...] = jnp.dot(qn, wq_ref[...], preferred_element_type=F32).astype(BF16)
    ckv = _rms(kvl_ref[...], gkv_ref[...])
    kv_ref[...] = jnp.dot(ckv.astype(BF16), wkv_ref[...], preferred_element_type=F32).astype(BF16)
    dqr_ref[...] = _rope(dq_ref[...], cos, sin, DIFF_QK // 4).astype(BF16)
    dk = _rope(dk_ref[...], cos, sin, DIFF_QK // 4)
    dkr_ref[...] = dk.astype(BF16)
    dv = dv_ref[...]
    dvb_ref[...] = dv.astype(BF16)
    kpe = _rope(kpe_ref[...], cos, sin, MLA_ROPE // 4)
    kper_ref[...] = kpe.astype(BF16)

    @pl.when(pl.program_id(0) < BATCH)
    def _():
        if n_prev:
            c_ckv[0, 0] = prev[0][0, 0]
            c_kpe[0, 0] = prev[1][0, 0]
            c_dk[0, 0] = prev[2][0, 0]
            c_dv[0, 0] = prev[3][0, 0]
        c_ckv[0, n_prev] = ckv
        c_kpe[0, n_prev] = kpe[:, :MLA_ROPE]
        c_dk[0, n_prev] = dk.reshape(SEQ, DIFF_HEADS, 2 * DIFF_QK)
        c_dv[0, n_prev] = dv.reshape(SEQ, DIFF_HEADS, DIFF_V)


def _ev_post(proj, cos, sin, gq, gkv, wq, wkv, prev):
    tm = POST_TM
    n_prev = 1 if prev else 0
    nl = n_prev + 1
    row = lambda c: (lambda i: (i, c))
    seq4 = lambda i: (jnp.minimum(i, BATCH - 1), 0, 0, 0)
    seq5 = lambda i: (jnp.minimum(i, BATCH - 1), 0, 0, 0, 0)
    cache_shapes = [(MLA_KV_RANK,), (MLA_ROPE,), (DIFF_HEADS, 2 * DIFF_QK), (DIFF_HEADS, DIFF_V)]

    def cache_spec(layers, tail):
        return pl.BlockSpec((1, layers, SEQ) + tail, seq4 if len(tail) == 1 else seq5)

    return pl.pallas_call(
        functools.partial(_ev_post_body, n_prev),
        grid=(N_TOK // tm,),
        in_specs=[
            pl.BlockSpec((tm, 512), row(0)),
            pl.BlockSpec((tm, 512), row(1)),
            pl.BlockSpec((tm, 1024), row(1)),
            pl.BlockSpec((tm, 1024), row(2)),
            pl.BlockSpec((tm, 1024), row(3)),
            pl.BlockSpec((tm, LANES), row(32)),
            pl.BlockSpec((tm, LANES), row(0)),
            pl.BlockSpec((tm, LANES), row(0)),
            pl.BlockSpec((1, 512), lambda i: (0, 0)),
            pl.BlockSpec((1, 512), lambda i: (0, 0)),
            pl.BlockSpec(wq.shape, lambda i: (0, 0)),
            pl.BlockSpec(wkv.shape, lambda i: (0, 0)),
        ] + [cache_spec(1, t) for t in cache_shapes] * n_prev,
        out_specs=[
            pl.BlockSpec((tm, wq.shape[1]), row(0)),
            pl.BlockSpec((tm, wkv.shape[1]), row(0)),
            pl.BlockSpec((tm, 1024), row(0)),
            pl.BlockSpec((tm, 1024), row(0)),
            pl.BlockSpec((tm, 1024), row(0)),
            pl.BlockSpec((tm, LANES), row(0)),
        ] + [cache_spec(nl, t) for t in cache_shapes],
        out_shape=[
            jax.ShapeDtypeStruct((N_TOK, wq.shape[1]), BF16),
            jax.ShapeDtypeStruct((N_TOK, wkv.shape[1]), BF16),
            jax.ShapeDtypeStruct((N_TOK, 1024), BF16),
            jax.ShapeDtypeStruct((N_TOK, 1024), BF16),
            jax.ShapeDtypeStruct((N_TOK, 1024), BF16),
            jax.ShapeDtypeStruct((N_TOK, LANES), BF16),
        ] + [jax.ShapeDtypeStruct((BATCH, nl, SEQ) + t, F32) for t in cache_shapes],
        compiler_params=_cparams(("arbitrary",)),
        name="even_post",
    )(proj, proj, proj, proj, proj, proj, cos, sin, gq, gkv, wq, wkv, *prev)


def _softmax_av(scores, values):
    m = functools.reduce(jnp.maximum, [jnp.max(s, axis=-1, keepdims=True) for s in scores])
    es = [jnp.exp(s - m) for s in scores]
    l = sum(jnp.sum(e, axis=-1, keepdims=True) for e in es)
    o = sum(jnp.dot(e.astype(BF16), v, preferred_element_type=F32) for e, v in zip(es, values))
    return o / l


def _even_attn_body(lam_init, has_ctx, has_mod, hps, q_ref, cos_ref, sin_ref, kv_ref, kpe_ref, dq_ref, dk_ref, dv_ref,
                    lam_ref, sub_ref, *refs):
    refs = list(refs)
    if has_ctx:
        kvc_ref, kpec_ref, dkc_ref, dvc_ref = refs[:4]
        refs = refs[4:]
    if has_mod:
        _adaln_tile(*refs[:3], refs[-1])
        refs = refs[3:-1]
    omla_ref, odiff_ref = refs
    cos = cos_ref[...]
    sin = sin_ref[...]
    lam = lam_ref[...]
    lam_full = (jnp.exp(jnp.sum(lam[0:1] * lam[1:2], axis=-1, keepdims=True))
                - jnp.exp(jnp.sum(lam[2:3] * lam[3:4], axis=-1, keepdims=True)) + lam_init)
    mla_scale = (MLA_NOPE + MLA_ROPE) ** -0.5
    lane = lax.broadcasted_iota(jnp.int32, (q_ref.shape[0], 2 * DIFF_QK), 1)
    for u in range(hps):
        qc = slice(u * 256, (u + 1) * 256)
        dc = slice(u * LANES, (u + 1) * LANES)
        q = q_ref[:, qc]
        qr = _rope(q[:, MLA_NOPE:].astype(F32), cos, sin, MLA_ROPE // 4).astype(BF16)
        qcat = jnp.concatenate([q[:, :MLA_NOPE], qr], axis=1)
        kv = kv_ref[:, qc]
        scores = [_nt(qcat, jnp.concatenate([kv[:, :MLA_NOPE], kpe_ref[...]], axis=1)) * mla_scale]
        values = [kv[:, MLA_NOPE:]]
        if has_ctx:
            kvc = kvc_ref[:, qc]
            scores.append(_nt(qcat, jnp.concatenate([kvc[:, :MLA_NOPE], kpec_ref[...]], axis=1)) * mla_scale)
            values.append(kvc[:, MLA_NOPE:])
        omla_ref[:, dc] = _softmax_av(scores, values).astype(BF16)

        dq = dq_ref[:, dc] * jnp.asarray(DIFF_QK ** -0.5, BF16)
        zero = jnp.zeros_like(dq)
        keys = [dk_ref[:, dc]]
        values = [dv_ref[:, dc]]
        if has_ctx:
            keys.append(dkc_ref[:, dc])
            values.append(dvc_ref[:, dc])
        a1 = _softmax_av([_nt(jnp.where(lane < DIFF_QK, dq, zero), k) for k in keys], values)
        a2 = _softmax_av([_nt(jnp.where(lane >= DIFF_QK, dq, zero), k) for k in keys], values)
        odiff_ref[:, dc] = (_rms(a1 - lam_full * a2, sub_ref[...]) * (1.0 - lam_init)).astype(BF16)


def _even_attn(lam_init, q, cos, sin, kv, kpe, dq, dk, dv, lam, subln, n_seq, seq, row_off, ctx=None, adaln=None):
    tq = ATT_TQ
    hps = ATT_HEADS_PER_STEP if ctx is not None else ATT_HEADS_PER_STEP_CTX
    nq = seq // tq
    q0 = row_off // tq
    s0 = row_off // seq
    qspec = lambda w: pl.BlockSpec((tq, hps * w), lambda b, j, h: (q0 + b * nq + j, h))
    kspec = lambda w: pl.BlockSpec((seq, hps * w), lambda b, j, h: (s0 + b, h))
    tspec = pl.BlockSpec((tq, LANES), lambda b, j, h: (q0 + b * nq + j, 0))
    ospec = pl.BlockSpec((tq, hps * LANES), lambda b, j, h: (b * nq + j, h))
    in_specs = [qspec(256), tspec, tspec, kspec(256),
                pl.BlockSpec((seq, LANES), lambda b, j, h: (s0 + b, 0)),
                qspec(LANES), kspec(LANES), kspec(LANES),
                pl.BlockSpec((4, DIFF_QK), lambda b, j, h: (0, 0)),
                pl.BlockSpec((1, DIFF_V), lambda b, j, h: (0, 0))]
    args = [q, cos, sin, kv, kpe, dq, dk, dv, lam, subln]
    if ctx is not None:
        cspec = lambda w: pl.BlockSpec((PAST_LEN, hps * w), lambda b, j, h: (b, h))
        in_specs += [cspec(256), pl.BlockSpec((PAST_LEN, LANES), lambda b, j, h: (b, 0)), cspec(LANES), cspec(LANES)]
        args += list(ctx)
    nh = MLA_HEADS // hps
    out_specs = [ospec, ospec]
    out_shape = [jax.ShapeDtypeStruct((n_seq * seq, 1024), BF16)] * 2
    if adaln is not None:
        mod_layer, mod_args = adaln
        m_in, m_out, m_shape = _adaln_specs(mod_layer, n_seq * nq * nh, lambda b, j, h: (b * nq + j) * nh + h)
        in_specs += m_in
        args += list(mod_args)
        out_specs.append(m_out)
        out_shape.append(m_shape)
    return pl.pallas_call(
        functools.partial(_even_attn_body, lam_init, ctx is not None, adaln is not None, hps),
        grid=(n_seq, nq, nh),
        in_specs=in_specs,
        out_specs=out_specs,
        out_shape=out_shape,
        compiler_params=_cparams(("parallel", "parallel", "parallel")),
        name="even_attn",
    )(*args)


def _route(logits):
    lane = lax.broadcasted_iota(jnp.int32, logits.shape, 1).astype(F32)
    neg = -jnp.inf
    none = float(LANES)
    is_g = lane < MOE_GROUPS
    lg = jnp.where(is_g, logits, neg)
    mg = jnp.max(lg, axis=-1, keepdims=True)
    g_val = 1.0 / jnp.sum(jnp.exp(lg - mg), axis=-1, keepdims=True)
    g_idx = jnp.min(jnp.where(is_g & (lg == mg), lane, none), axis=-1, keepdims=True)
    e_lo = MOE_GROUPS + g_idx * MOE_EPG
    is_e = (lane >= e_lo) & (lane < e_lo + MOE_EPG)
    le = jnp.where(is_e, logits, neg)
    p = jnp.exp(le - jnp.max(le, axis=-1, keepdims=True))
    p1 = jnp.max(p, axis=-1, keepdims=True)
    i1 = jnp.min(jnp.where(is_e & (p == p1), lane, none), axis=-1, keepdims=True)
    rest = is_e & (lane != i1)
    p2 = jnp.max(jnp.where(rest, p, neg), axis=-1, keepdims=True)
    i2 = jnp.min(jnp.where(rest & (p == p2), lane, none), axis=-1, keepdims=True)
    s = g_val / (p1 + p2)
    return jnp.where(lane == 0.0, i1 - MOE_GROUPS,
                     jnp.where(lane == 1.0, i2 - MOE_GROUPS,
                               jnp.where(lane == 2.0, p1 * s, jnp.where(lane == 3.0, p2 * s, 0.0))))


def _out_proj_body(a1p_ref, a1s_ref, a2p_ref, a2s_ref, w_ref, x_ref, gate_ref, g_ref, b_ref, sh_ref, sc_ref,
                   wr_ref, xo_ref, h_ref, comb_ref):
    is_prompt = pl.program_id(0) < N_PROMPT // OUT_TM
    a1 = jnp.where(is_prompt, a1p_ref[...], a1s_ref[...])
    a2 = jnp.where(is_prompt, a2p_ref[...], a2s_ref[...])
    k1 = a1.shape[1]
    acc = jnp.dot(a1, w_ref[:k1, :], preferred_element_type=F32)
    acc = acc + jnp.dot(a2, w_ref[k1:, :], preferred_element_type=F32)
    xn = _layer_norm(DEEPNORM_ALPHA * x_ref[...] + gate_ref[0] * acc, g_ref[...], b_ref[...])
    xo_ref[...] = xn
    h = xn * (1.0 + sc_ref[0]) + sh_ref[0]
    h_ref[...] = h.reshape(h_ref.shape).astype(BF16)
    h_hi = h.astype(BF16)
    h_lo = (h - h_hi.astype(F32)).astype(BF16)
    logits = (jnp.dot(h_hi, wr_ref[0], preferred_element_type=F32)
              + jnp.dot(h_lo, wr_ref[0], preferred_element_type=F32)
              + jnp.dot(h_hi, wr_ref[1], preferred_element_type=F32))
    comb_ref[...] = _route(logits)


def _out_proj(a1p, a1s, a2p, a2s, w, x, gate, ln_g, ln_b, shift, scale, w_router):
    tm = OUT_TM
    np_tiles = N_PROMPT // tm
    k1, k2 = a1p.shape[1], a2p.shape[1]
    mod_spec = pl.BlockSpec((1, 1, D_MODEL), lambda i: (_mod_group(i * tm), 0, 0))
    vec_spec = pl.BlockSpec((1, D_MODEL), lambda i: (0, 0))
    row_spec = pl.BlockSpec((tm, D_MODEL), lambda i: (i, 0))
    pspec = lambda k: pl.BlockSpec((tm, k), lambda i: (jnp.minimum(i, np_tiles - 1), 0))
    sspec = lambda k: pl.BlockSpec((tm, k), lambda i: (jnp.maximum(i - np_tiles, 0), 0))
    return pl.pallas_call(
        _out_proj_body,
        grid=(N_TOK // tm,),
        in_specs=[
            pspec(k1), sspec(k1), pspec(k2), sspec(k2),
            pl.BlockSpec((k1 + k2, D_MODEL), lambda i: (0, 0)),
            row_spec, mod_spec, vec_spec, vec_spec, mod_spec, mod_spec,
            pl.BlockSpec((2, D_MODEL, LANES), lambda i: (0, 0, 0)),
        ],
        out_specs=[row_spec, pl.BlockSpec((tm,) + SLAB, lambda i: (i, 0, 0)),
                   pl.BlockSpec((tm, LANES), lambda i: (i, 0))],
        out_shape=[
            jax.ShapeDtypeStruct((N_TOK, D_MODEL), F32),
            jax.ShapeDtypeStruct((N_TOK,) + SLAB, BF16),
            jax.ShapeDtypeStruct((N_TOK, LANES), F32),
        ],
        compiler_params=_cparams(("parallel",)),
        name="out_proj_ln_router",
    )(a1p, a1s, a2p, a2s, w, x, gate, ln_g, ln_b, shift, scale, w_router)


def _route_meta(route):
    e = route[:, :2].astype(jnp.int32).reshape(-1)
    onehot = (e[:, None] == jnp.arange(MOE_EXPERTS, dtype=jnp.int32)[None, :]).astype(jnp.int32)
    csum = jnp.cumsum(onehot, axis=0)
    rank = jnp.sum((csum - onehot) * onehot, axis=1)
    count = csum[-1]
    ntile = (count + MOE_TM - 1) // MOE_TM
    tile_end = jnp.cumsum(ntile)
    tile_off = tile_end - ntile
    pos = jnp.sum(onehot * tile_off[None, :], axis=1) * MOE_TM + rank
    n_used = tile_end[-1:]
    j = jnp.arange(MOE_TILES, dtype=jnp.int32)
    tile_expert = jnp.sum((tile_end[None, :] <= jnp.minimum(j, n_used - 1)[:, None]).astype(jnp.int32), axis=1)
    is_e = tile_expert[:, None] == jnp.arange(MOE_EXPERTS, dtype=jnp.int32)[None, :]
    left = jnp.sum(jnp.where(is_e, (count - (j[:, None] - tile_off[None, :]) * MOE_TM)[...], 0), axis=1)
    n_valid = jnp.where(j < n_used, jnp.clip(left, 0, MOE_TM), 0)
    ids = jnp.arange(MOE_EXPERTS, dtype=jnp.int32)
    later = (ids[None, :] > ids[:, None]) & (ntile[None, :] > 0)
    next_of = jnp.min(jnp.where(later, ids[None, :], MOE_EXPERTS), axis=1)
    next_of = jnp.where(next_of == MOE_EXPERTS, -1, next_of)
    next_expert = jnp.sum(jnp.where(is_e, next_of[None, :], 0), axis=1)
    ordinal = jnp.cumsum((ntile > 0).astype(jnp.int32)) - 1
    w_slot = jnp.sum(jnp.where(is_e, ordinal[None, :], 0), axis=1) % 2
    return pos, tile_expert, n_valid, n_used, next_expert, w_slot


def _moe_body(layer, pos_ref, te_ref, nv_ref, nu_ref, ne_ref, ws_ref, h_ref, wg_ref, wu_ref, wd_ref, z_ref,
              src_tok, dst_row, xbuf, zbuf, wgbuf, wubuf, wdbuf, wg_s, wu_s, wd_s, gsem, ssem, wsem):
    j = pl.program_id(0)
    slot = j % 2
    n_used = nu_ref[0]
    group = 16

    def weight_copies(e, s):
        return [pltpu.make_async_copy(wg_ref.at[layer, e], wgbuf.at[s], wsem.at[s]),
                pltpu.make_async_copy(wu_ref.at[layer, e], wubuf.at[s], wsem.at[s]),
                pltpu.make_async_copy(wd_ref.at[layer, e], wdbuf.at[s], wsem.at[s])]

    def gather_row(tile, s, r):
        return pltpu.make_async_copy(h_ref.at[src_tok[tile * MOE_TM + r]], xbuf.at[s, r], gsem.at[s])

    def scatter_row(tile, s, r):
        return pltpu.make_async_copy(zbuf.at[s, r], z_ref.at[dst_row[tile * MOE_TM + r]], ssem.at[s])

    def for_valid_rows(tile, fn):
        n = nv_ref[tile]

        def block(g, c):
            for u in range(group):
                fn(g * group + u, u % 2)
            return c

        def single(r, c):
            fn(r, 0)
            return c

        full = lax.shift_right_logical(n, group.bit_length() - 1)
        lax.fori_loop(0, full, block, 0)
        lax.fori_loop(full * group, n, single, 0)

    def wait_gather(tile, s):
        @pl.when(nv_ref[tile] == MOE_TM)
        def _():
            pltpu.make_async_copy(h_ref.at[pl.ds(0, MOE_TM)], xbuf.at[s], gsem.at[s]).wait()

        @pl.when(nv_ref[tile] != MOE_TM)
        def _():
            for_valid_rows(tile, lambda r, pr: gather_row(tile, s, r).wait())

    def wait_scatter(tile, s):
        @pl.when(nv_ref[tile] == MOE_TM)
        def _():
            pltpu.make_async_copy(zbuf.at[s], z_ref.at[pl.ds(0, MOE_TM)], ssem.at[s]).wait()

        @pl.when(nv_ref[tile] != MOE_TM)
        def _():
            for_valid_rows(tile, lambda r, pr: scatter_row(tile, s, r).wait())

    @pl.when(j == 0)
    def _():
        def fill(g, c):
            for u in range(group):
                row = pos_ref[g * group + u]
                tok = g * (group // 2) + u // 2
                src_tok[row] = tok
                dst_row[row] = (u % 2) * N_TOK + tok
            return c
        lax.fori_loop(0, MOE_SLOTS // group, fill, 0)
        xbuf[...] = jnp.zeros_like(xbuf)
        for_valid_rows(0, lambda r, pr: gather_row(0, 0, r).start(priority=pr))
        for cp in weight_copies(te_ref[0], ws_ref[0]):
            cp.start(priority=WEIGHT_DMA_PRIORITY)

    changed = (j == 0) | (te_ref[j] != te_ref[jnp.maximum(j - 1, 0)])

    @pl.when(changed)
    def _():
        ws = ws_ref[j]
        for cp in weight_copies(te_ref[j], ws):
            cp.wait()
        wg_s[...] = wgbuf[ws].astype(BF16)
        wu_s[...] = wubuf[ws].astype(BF16)
        wd_s[...] = wdbuf[ws].astype(BF16)

        @pl.when(ne_ref[j] >= 0)
        def _():
            for cp in weight_copies(ne_ref[j], 1 - ws):
                cp.start(priority=WEIGHT_DMA_PRIORITY)

    @pl.when(j < n_used)
    def _():
        @pl.when(j + 1 < n_used)
        def _():
            for_valid_rows(j + 1, lambda r, pr: gather_row(j + 1, 1 - slot, r).start(priority=pr))

        wait_gather(j, slot)

        @pl.when(j >= 2)
        def _():
            wait_scatter(j - 2, slot)

        x = xbuf[slot].reshape(MOE_TM, D_MODEL)
        hg = jnp.dot(x, wg_s[...], preferred_element_type=F32)
        hu = jnp.dot(x, wu_s[...], preferred_element_type=F32)
        act = (_silu(hg) * hu).astype(BF16)
        z = jnp.dot(act, wd_s[...], preferred_element_type=F32)
        zbuf[slot] = z.reshape((MOE_TM,) + SLAB).astype(BF16)
        for_valid_rows(j, lambda r, pr: scatter_row(j, slot, r).start(priority=pr))

        @pl.when(j == n_used - 1)
        def _():
            @pl.when(j >= 1)
            def _():
                wait_scatter(j - 1, 1 - slot)
            wait_scatter(j, slot)


def _moe_experts(layer, h, meta, wg, wu, wd):
    any_spec = pl.BlockSpec(memory_space=pl.ANY)
    up_shape, down_shape = (D_MODEL, MOE_HIDDEN), (MOE_HIDDEN, D_MODEL)
    return pl.pallas_call(
        functools.partial(_moe_body, layer),
        grid_spec=pltpu.PrefetchScalarGridSpec(
            num_scalar_prefetch=len(meta),
            grid=(MOE_TILES,),
            in_specs=[any_spec, any_spec, any_spec, any_spec],
            out_specs=any_spec,
            scratch_shapes=[pltpu.SMEM((MOE_ROWS,), jnp.int32), pltpu.SMEM((MOE_ROWS,), jnp.int32),
                            pltpu.VMEM((2, MOE_TM) + SLAB, BF16), pltpu.VMEM((2, MOE_TM) + SLAB, BF16),
                            pltpu.VMEM((2,) + up_shape, F32), pltpu.VMEM((2,) + up_shape, F32),
                            pltpu.VMEM((2,) + down_shape, F32),
                            pltpu.VMEM(up_shape, BF16), pltpu.VMEM(up_shape, BF16), pltpu.VMEM(down_shape, BF16),
                            pltpu.SemaphoreType.DMA((2,)), pltpu.SemaphoreType.DMA((2,)),
                            pltpu.SemaphoreType.DMA((2,))],
        ),
        out_shape=jax.ShapeDtypeStruct((2 * N_TOK,) + SLAB, BF16),
        compiler_params=_cparams(("arbitrary",)),
        name="moe_experts",
    )(*meta, h, wg, wu, wd)


def _combine_body(split, x_ref, r_ref, gate_ref, g_ref, b_ref, z1_ref, z2_ref, *o_refs):
    r = r_ref[...]
    z1 = z1_ref[...].reshape(x_ref.shape).astype(F32)
    z2 = z2_ref[...].reshape(x_ref.shape).astype(F32)
    y = r[:, 2:3] * z1 + r[:, 3:4] * z2
    out = _layer_norm(DEEPNORM_ALPHA * x_ref[...] + gate_ref[0] * y, g_ref[...], b_ref[...])
    if split:
        is_prompt = pl.program_id(0) < N_PROMPT // x_ref.shape[0]

        @pl.when(is_prompt)
        def _():
            o_refs[0][...] = out

        @pl.when(jnp.logical_not(is_prompt))
        def _():
            o_refs[1][...] = out
    else:
        o_refs[0][...] = out


def _combine(x, z, route, gate, ln_g, ln_b, split=False):
    tm = 512
    nt = N_TOK // tm
    npt = N_PROMPT // tm
    row_spec = pl.BlockSpec((tm, D_MODEL), lambda i: (i, 0))
    vec_spec = pl.BlockSpec((1, D_MODEL), lambda i: (0, 0))
    if split:
        out_specs = [pl.BlockSpec((tm, D_MODEL), lambda i: (jnp.minimum(i, npt - 1), 0)),
                     pl.BlockSpec((tm, D_MODEL), lambda i: (jnp.maximum(i - npt, 0), 0))]
        out_shape = [jax.ShapeDtypeStruct((N_PROMPT, D_MODEL), F32), jax.ShapeDtypeStruct((N_SAMPLE, D_MODEL), F32)]
    else:
        out_specs = row_spec
        out_shape = jax.ShapeDtypeStruct((N_TOK, D_MODEL), F32)
    return pl.pallas_call(
        functools.partial(_combine_body, split),
        grid=(nt,),
        in_specs=[row_spec, pl.BlockSpec((tm, LANES), lambda i: (i, 0)),
                  pl.BlockSpec((1, 1, D_MODEL), lambda i: (_mod_group(i * tm), 0, 0)),
                  vec_spec, vec_spec, pl.BlockSpec((tm,) + SLAB, lambda i: (i, 0, 0)),
                  pl.BlockSpec((tm,) + SLAB, lambda i: (nt + i, 0, 0))],
        out_specs=out_specs,
        out_shape=out_shape,
        compiler_params=_cparams(("arbitrary",)),
        name="moe_combine_ln",
    )(x, route, gate, ln_g, ln_b, z, z)


def _od_post_body(n_prev, q_ref, k_ref, v_ref, xs_ref, xs_lo_ref, xs_hi_ref, bc_ref, bc_lo_ref, bc_hi_ref, dt_ref,
                  cos_ref, sin_ref, cw_ref, cb_ref, dtb_ref, *refs):
    prev = refs[:2 * n_prev]
    qr_ref, kr_ref, vb_ref, xc_ref, bcc_ref, dtp_ref, c_k, c_v = refs[2 * n_prev:]
    i = pl.program_id(0)
    cos = cos_ref[...]
    sin = sin_ref[...]
    k = k_ref[...]
    v = v_ref[...]
    qr_ref[...] = _rope(q_ref[...], cos, sin, GQA_HEAD_DIM // 4).astype(BF16)
    kr_ref[...] = _rope(k, cos, sin, GQA_HEAD_DIM // 4).astype(BF16)
    vb_ref[...] = v.astype(BF16)

    is_prompt = i < BATCH
    tile_in_seq = (i - BATCH) % (DEC_SEQ // POST_TM)
    seq_start = is_prompt | (tile_in_seq == 0)
    seq_end = is_prompt | (tile_in_seq == DEC_SEQ // POST_TM - 1)

    def conv_silu(x, lo_ref, hi_ref, w, b):
        rows = x.shape[0]
        r = lax.broadcasted_iota(jnp.int32, x.shape, 0)
        lo = jnp.where(seq_start, 0.0, lo_ref[7:8, :])
        hi = jnp.where(seq_end, 0.0, hi_ref[0:1, :])
        before = jnp.where(r == 0, lo, pltpu.roll(x, 1, 0))
        after = jnp.where(r == rows - 1, hi, pltpu.roll(x, rows - 1, 0))
        return _silu(before * w[0:1] + x * w[1:2] + after * w[2:3] + b)

    xc_ref[...] = conv_silu(xs_ref[...], xs_lo_ref, xs_hi_ref, cw_ref[:, :SSD_D_INNER], cb_ref[:, :SSD_D_INNER])
    bcc_ref[...] = conv_silu(bc_ref[...], bc_lo_ref, bc_hi_ref, cw_ref[:, SSD_D_INNER:], cb_ref[:, SSD_D_INNER:])
    t = dt_ref[...] + dtb_ref[...]
    dtp_ref[...] = jnp.maximum(t, 0.0) + jnp.log1p(jnp.exp(-jnp.abs(t)))

    @pl.when(is_prompt)
    def _():
        if n_prev:
            c_k[0, 0] = prev[0][0, 0]
            c_v[0, 0] = prev[1][0, 0]
        c_k[0, n_prev] = k.reshape(SEQ, GQA_KV_HEADS, GQA_HEAD_DIM)
        c_v[0, n_prev] = v.reshape(SEQ, GQA_KV_HEADS, GQA_HEAD_DIM)


def _od_post(proj, cos, sin, conv_w, conv_b, dt_bias, prev):
    tm = POST_TM
    halo = 8
    per = tm // halo
    n_halo = N_TOK // halo
    n_prev = 1 if prev else 0
    blk = lambda w, c: pl.BlockSpec((tm, w), lambda i: (i, c))
    lo = lambda w, c: pl.BlockSpec((halo, w), lambda i: (jnp.maximum(i * per - 1, 0), c))
    hi = lambda w, c: pl.BlockSpec((halo, w), lambda i: (jnp.minimum((i + 1) * per, n_halo - 1), c))
    full = lambda a: pl.BlockSpec(a.shape, lambda i: (0, 0))
    oblk = lambda w: pl.BlockSpec((tm, w), lambda i: (i, 0))
    cache_spec = lambda layers: pl.BlockSpec((1, layers, SEQ, GQA_KV_HEADS, GQA_HEAD_DIM),
                                             lambda i: (jnp.minimum(i, BATCH - 1), 0, 0, 0, 0))
    cache_shape = jax.ShapeDtypeStruct((BATCH, n_prev + 1, SEQ, GQA_KV_HEADS, GQA_HEAD_DIM), F32)
    return pl.pallas_call(
        functools.partial(_od_post_body, n_prev),
        grid=(N_TOK // tm,),
        in_specs=[blk(1024, 0), blk(256, 12), blk(256, 13),
                  blk(1024, 1), lo(1024, 1), hi(1024, 1), blk(512, 7), lo(512, 7), hi(512, 7), blk(LANES, 32),
                  blk(LANES, 0), blk(LANES, 0), full(conv_w), full(conv_b), full(dt_bias)]
        + [cache_spec(1)] * (2 * n_prev),
        out_specs=[oblk(1024), oblk(256), oblk(256), oblk(1024), oblk(512), oblk(LANES),
                   cache_spec(n_prev + 1), cache_spec(n_prev + 1)],
        out_shape=[
            jax.ShapeDtypeStruct((N_TOK, 1024), BF16),
            jax.ShapeDtypeStruct((N_TOK, 256), BF16),
            jax.ShapeDtypeStruct((N_TOK, 256), BF16),
            jax.ShapeDtypeStruct((N_TOK, 1024), F32),
            jax.ShapeDtypeStruct((N_TOK, 512), F32),
            jax.ShapeDtypeStruct((N_TOK, LANES), F32),
            cache_shape, cache_shape,
        ],
        compiler_params=_cparams(("arbitrary",)),
        name="odd_post",
    )(proj, proj, proj, proj, proj, proj, proj, proj, proj, proj, cos, sin, conv_w, conv_b, dt_bias, *prev)


def _sink_column(sink_ref, kh, rows):
    g = GQA_HEADS // GQA_KV_HEADS
    return jnp.concatenate(
        [jnp.broadcast_to(sink_ref[kh * g + j:kh * g + j + 1, 0:1], (rows, 1)) for j in range(g)], axis=0)


def _gqa_prompt_body(q_ref, k_ref, v_ref, sink_ref, o_ref):
    g = GQA_HEADS // GQA_KV_HEADS
    d = GQA_HEAD_DIM
    rows = q_ref.shape[0]
    scale = d ** -0.5
    outs = []
    for kh in range(GQA_KV_HEADS):
        qg = jnp.concatenate([q_ref[:, (kh * g + j) * d:(kh * g + j + 1) * d] for j in range(g)], axis=0)
        k = k_ref[:, kh * d:(kh + 1) * d]
        v = v_ref[:, kh * d:(kh + 1) * d]
        s = _nt(qg, k) * scale
        snk = _sink_column(sink_ref, kh, rows)
        m = jnp.maximum(jnp.max(s, axis=-1, keepdims=True), snk)
        e = jnp.exp(s - m)
        l = jnp.sum(e, axis=-1, keepdims=True) + jnp.exp(snk - m)
        o = jnp.dot(e.astype(BF16), v, preferred_element_type=F32) / l
        outs.extend(o[j * rows:(j + 1) * rows] for j in range(g))
    o_ref[...] = jnp.concatenate(outs, axis=1).astype(BF16)


def _gqa_prompt(q, k, v, sink, adaln=None):
    spec = lambda w: pl.BlockSpec((SEQ, w), lambda i: (i, 0))
    in_specs = [spec(1024), spec(256), spec(256), pl.BlockSpec((8, LANES), lambda i: (0, 0))]
    args = [q, k, v, sink]
    out_specs = [spec(1024)]
    out_shape = [jax.ShapeDtypeStruct((N_PROMPT, 1024), BF16)]
    body = _gqa_prompt_body
    if adaln is not None:
        mod_layer, mod_args = adaln
        m_in, m_out, m_shape = _adaln_specs(mod_layer, BATCH, lambda i: i)
        in_specs += m_in
        args += list(mod_args)
        out_specs.append(m_out)
        out_shape.append(m_shape)

        def body(q_ref, k_ref, v_ref, sink_ref, c_ref, w_ref, b_ref, o_ref, mo_ref):
            _gqa_prompt_body(q_ref, k_ref, v_ref, sink_ref, o_ref)
            _adaln_tile(c_ref, w_ref, b_ref, mo_ref)

    return pl.pallas_call(
        body,
        grid=(BATCH,),
        in_specs=in_specs,
        out_specs=out_specs,
        out_shape=out_shape,
        compiler_params=_cparams(("parallel",)),
        name="gqa_dense",
    )(*args)


def _gqa_window_body(q_ref, k_ref, v_ref, kc_ref, vc_ref, sink_ref, o_ref):
    g = GQA_HEADS // GQA_KV_HEADS
    d = GQA_HEAD_DIM
    blk = q_ref.shape[0]
    span = 3 * blk
    n = pl.program_id(1)
    start = pl.multiple_of(jnp.clip((n - 1) * blk, 0, DEC_SEQ - span), blk)
    scale = d ** -0.5
    qpos = n * blk + lax.broadcasted_iota(jnp.int32, (blk, span), 0)
    kpos = start + lax.broadcasted_iota(jnp.int32, (blk, span), 1)
    valid1 = jnp.abs(qpos - kpos) <= WINDOW
    valid = jnp.concatenate([valid1] * g, axis=0)
    outs = []
    for kh in range(GQA_KV_HEADS):
        qg = jnp.concatenate([q_ref[:, (kh * g + j) * d:(kh * g + j + 1) * d] for j in range(g)], axis=0)
        kl = k_ref[pl.ds(start, span), kh * d:(kh + 1) * d]
        vl = v_ref[pl.ds(start, span), kh * d:(kh + 1) * d]
        kc = kc_ref[:, kh * d:(kh + 1) * d]
        vc = vc_ref[:, kh * d:(kh + 1) * d]
        sl = jnp.where(valid, _nt(qg, kl) * scale, -jnp.inf)
        sc = _nt(qg, kc) * scale
        snk = _sink_column(sink_ref, kh, blk)
        m = jnp.maximum(jnp.maximum(jnp.max(sl, axis=-1, keepdims=True), jnp.max(sc, axis=-1, keepdims=True)), snk)
        el = jnp.exp(sl - m)
        ec = jnp.exp(sc - m)
        l = jnp.sum(el, axis=-1, keepdims=True) + jnp.sum(ec, axis=-1, keepdims=True) + jnp.exp(snk - m)
        o = (jnp.dot(el.astype(BF16), vl, preferred_element_type=F32)
             + jnp.dot(ec.astype(BF16), vc, preferred_element_type=F32)) / l
        outs.extend(o[j * blk:(j + 1) * blk] for j in range(g))
    o_ref[...] = jnp.concatenate(outs, axis=1).astype(BF16)


def _gqa_window(q, k, v, cache_k, cache_v, sink):
    blk = WINDOW
    nq = DEC_SEQ // blk
    q0 = N_PROMPT // blk
    s0 = N_PROMPT // DEC_SEQ
    seq_spec = pl.BlockSpec((DEC_SEQ, 256), lambda b, j: (s0 + b, 0))
    ctx_spec = pl.BlockSpec((PAST_LEN, 256), lambda b, j: (b, 0))
    return pl.pallas_call(
        _gqa_window_body,
        grid=(DEC_BATCH, nq),
        in_specs=[pl.BlockSpec((blk, 1024), lambda b, j: (q0 + b * nq + j, 0)), seq_spec, seq_spec, ctx_spec,
                  ctx_spec,
                  pl.BlockSpec((8, LANES), lambda b, j: (0, 0))],
        out_specs=pl.BlockSpec((blk, 1024), lambda b, j: (b * nq + j, 0)),
        out_shape=jax.ShapeDtypeStruct((N_SAMPLE, 1024), BF16),
        compiler_params=_cparams(("parallel", "parallel")),
        name="gqa_window",
    )(q, k, v, cache_k, cache_v, sink)


def _expand(x, sel):
    hi = x.astype(BF16)
    r1 = x - hi.astype(F32)
    mid = r1.astype(BF16)
    lo = (r1 - mid.astype(F32)).astype(BF16)
    return (jnp.dot(hi, sel, preferred_element_type=F32) + jnp.dot(mid, sel, preferred_element_type=F32)
            + jnp.dot(lo, sel, preferred_element_type=F32))


def _ssd_body(has_h0, xs_ref, bc_ref, dt_ref, alog_ref, e1_ref, e2_ref, *refs):
    if has_h0:
        h0f_ref, h0b_ref, y_ref, stf_ref, stb_ref = refs
    else:
        y_ref, stf_ref, stb_ref = refs
    d = pl.program_id(1)
    c = pl.program_id(2)
    q = SSD_CHUNK
    ppg = SSD_HEADS // SSD_GROUPS // 2
    hd = SSD_HEAD_DIM

    row = lax.broadcasted_iota(jnp.int32, (q, LANES), 0)
    a = -jnp.exp(alog_ref[...])
    dta = dt_ref[...] * a

    def scan(x, rev):
        k = 1
        while k < q:
            if rev:
                x = x + jnp.where(row < q - k, pltpu.roll(x, q - k, 0), 0.0)
            else:
                x = x + jnp.where(row >= k, pltpu.roll(x, k, 0), 0.0)
            k *= 2
        return x

    li = lax.broadcasted_iota(jnp.int32, (q, q), 0)
    si = lax.broadcasted_iota(jnp.int32, (q, q), 1)

    def run(rev):
        st_ref = stb_ref if rev else stf_ref

        @pl.when(c == 0)
        def _():
            if has_h0:
                st_ref[0] = (h0b_ref if rev else h0f_ref)[0, 0]
            else:
                st_ref[0] = jnp.zeros(st_ref.shape[1:], F32)

        cum = scan(dta, rev)
        cum_t = cum.T
        e_row = 0 if rev else q - 1
        mask = (li <= si) if rev else (li >= si)
        dt_x = _expand(dt_ref[...], e1_ref[0])
        cum_x = _expand(cum, e1_ref[0])
        cum_b = _expand(cum, e2_ref[0])
        edge_x = cum_x[e_row:e_row + 1, :]
        xdt = xs_ref[...] * dt_x
        xdt16 = xdt.astype(BF16)
        xw16 = (xdt * jnp.exp(edge_x - cum_x)).astype(BF16)
        ecum = jnp.exp(cum_x)
        lower = si < hd
        ys = []
        for g in range(SSD_GROUPS):
            bm16 = bc_ref[:, g * SSD_STATE:(g + 1) * SSD_STATE].astype(BF16)
            cm16 = bc_ref[:, (SSD_GROUPS + g) * SSD_STATE:(SSD_GROUPS + g + 1) * SSD_STATE].astype(BF16)
            cb = _nt(cm16, bm16)
            for pp in range(ppg):
                p = g * ppg + pp
                cols = slice(p * LANES, (p + 1) * LANES)
                y_pair, fac = [], []
                for u in range(2):
                    h = 2 * p + u
                    j = (SSD_HEADS if rev else 0) + h
                    seg = cum_b[:, h * LANES:(h + 1) * LANES] - cum_t[j:j + 1, :]
                    decay = jnp.exp(jnp.where(mask, seg, -jnp.inf))
                    y_pair.append(jnp.dot((cb * decay).astype(BF16), xdt16[:, cols], preferred_element_type=F32))
                    fac.append(jnp.broadcast_to(jnp.exp(cum_t[j:j + 1, e_row:e_row + 1]), (hd, SSD_STATE)))
                y_diag = jnp.where(lower, y_pair[0], y_pair[1])
                state = lax.dot_general(xw16[:, cols], bm16, (((0,), (0,)), ((), ())),
                                        preferred_element_type=F32)
                h_prev = st_ref[0, 2 * p:2 * p + 2].reshape(2 * hd, SSD_STATE)
                y_off = _nt(cm16, h_prev.astype(BF16)) * ecum[:, cols]
                ys.append(y_diag + y_off)
                h_new = h_prev * jnp.concatenate(fac, axis=0) + state
                st_ref[0, 2 * p:2 * p + 2] = h_new.reshape(2, hd, SSD_STATE)
        y_ref[0] = jnp.concatenate(ys, axis=1)

    @pl.when(d == 0)
    def _():
        run(False)

    @pl.when(d == 1)
    def _():
        run(True)


def _ssd(xs, bc, dtp, a_log, n_seq, seq, row_off, h0=None):
    b = n_seq
    t = n_seq * seq
    nc = seq // SSD_CHUNK
    c0 = row_off // SSD_CHUNK

    def rmap(i, d, c):
        return i * nc + jnp.where(d == 0, c, nc - 1 - c)

    blk = lambda w: pl.BlockSpec((SSD_CHUNK, w), lambda i, d, c: (c0 + rmap(i, d, c), 0))
    st_spec = pl.BlockSpec((1, SSD_HEADS, SSD_HEAD_DIM, SSD_STATE), lambda i, d, c: (i, 0, 0, 0))
    st_shape = jax.ShapeDtypeStruct((b, SSD_HEADS, SSD_HEAD_DIM, SSD_STATE), F32)
    def sel(w):
        row = jnp.arange(LANES)[None, :, None]
        head = SSD_HEADS * jnp.arange(2)[:, None, None] + (jnp.arange(SSD_HEADS * w) // w)[None, None, :]
        return (row == head).astype(BF16)

    sel_spec = lambda w: pl.BlockSpec((1, LANES, SSD_HEADS * w), lambda i, d, c: (d, 0, 0))
    in_specs = [blk(1024), blk(512), blk(LANES), pl.BlockSpec((1, LANES), lambda i, d, c: (0, 0)),
                sel_spec(SSD_HEAD_DIM), sel_spec(LANES)]
    args = [xs, bc, dtp, a_log, sel(SSD_HEAD_DIM), sel(LANES)]
    if h0 is not None:
        layer, h0_f, h0_b = h0
        h0_spec = pl.BlockSpec((1, 1, SSD_HEADS, SSD_HEAD_DIM, SSD_STATE), lambda i, d, c: (i, layer, 0, 0, 0))
        in_specs += [h0_spec, h0_spec]
        args += [h0_f, h0_b]
    return pl.pallas_call(
        functools.partial(_ssd_body, h0 is not None),
        grid=(b, 2, nc),
        in_specs=in_specs,
        out_specs=[pl.BlockSpec((1, SSD_CHUNK, 1024), lambda i, d, c: (d, rmap(i, d, c), 0)), st_spec, st_spec],
        out_shape=[jax.ShapeDtypeStruct((2, t, 1024), F32), st_shape, st_shape],
        compiler_params=_cparams(("parallel", "arbitrary", "arbitrary")),
        name="ssd_scan",
    )(*args)


def _ssd_gate_body(y_ref, xs_ref, z_ref, dsk_ref, g_ref, o_ref):
    y = (y_ref[0] + y_ref[1] + xs_ref[...] * dsk_ref[...]) * _silu(z_ref[...])
    w = SSD_D_INNER // SSD_GROUPS
    outs = [_rms(y[:, g * w:(g + 1) * w], g_ref[:, g * w:(g + 1) * w]) for g in range(SSD_GROUPS)]
    o_ref[...] = jnp.concatenate(outs, axis=1).astype(BF16)


def _ssd_gate(y2, xs, proj, d_skip, norm_g, row_off):
    tm = 512
    t = y2.shape[1]
    off = row_off // tm
    return pl.pallas_call(
        _ssd_gate_body,
        grid=(t // tm,),
        in_specs=[pl.BlockSpec((2, tm, 1024), lambda i: (0, i, 0)),
                  pl.BlockSpec((tm, 1024), lambda i: (off + i, 0)),
                  pl.BlockSpec((tm, 1024), lambda i: (off + i, 2)),
                  pl.BlockSpec((1, 1024), lambda i: (0, 0)),
                  pl.BlockSpec((1, 1024), lambda i: (0, 0))],
        out_specs=pl.BlockSpec((tm, 1024), lambda i: (i, 0)),
        out_shape=jax.ShapeDtypeStruct((t, 1024), BF16),
        compiler_params=_cparams(("parallel",)),
        name="ssd_gate_norm",
    )(y2, xs, proj, d_skip, norm_g)


def _even_layer(x, mod, layer_idx, i, p, ctx, tables, prev_caches, adaln):
    w_in, q_norm, kv_norm, wq_b, wkv_b, lam, subln, w_out = p
    ctx_ckv, ctx_kpe, ctx_dk, ctx_dv = ctx
    cos, sin = tables
    lam_init = 0.8 - 0.6 * math.exp(-0.3 * layer_idx)

    proj = _proj(x, mod[0], mod[1], _prep_even(w_in, i))
    wq = wq_b.reshape(MLA_Q_RANK, MLA_HEADS, MLA_NOPE + MLA_ROPE)
    wq_r = jnp.pad(wq, ((0, 0), (0, 0), (0, MLA_ROPE))).reshape(MLA_Q_RANK, -1).astype(BF16)
    wkv = wkv_b.astype(BF16)
    q, kv, dq, dk, dv, kpe, *caches = _ev_post(proj, cos, sin, q_norm[None], kv_norm[None], wq_r, wkv, prev_caches)
    n_even = ctx_ckv.shape[1]
    kv_ctx = _mm(ctx_ckv.reshape(-1, MLA_KV_RANK), wkv, BF16, PAST_LEN, DEC_BATCH,
                 row_block=lambda b: b * n_even + i)

    n_ctx = DEC_BATCH * PAST_LEN
    kpe_ctx = jnp.pad(ctx_kpe[:, i].reshape(n_ctx, MLA_ROPE), ((0, 0), (0, LANES - MLA_ROPE))).astype(BF16)
    dk_ctx = ctx_dk[:, i].reshape(n_ctx, -1).astype(BF16)
    dv_ctx = ctx_dv[:, i].reshape(n_ctx, -1).astype(BF16)

    args = (lam_init, q, cos, sin, kv, kpe, dq, dk, dv, lam, subln[None])
    o1p, o2p, *mod_next = _even_attn(*args, n_seq=BATCH, seq=SEQ, row_off=0, adaln=adaln)
    o1s, o2s = _even_attn(*args, n_seq=DEC_BATCH, seq=DEC_SEQ, row_off=N_PROMPT,
                          ctx=(kv_ctx, kpe_ctx, dk_ctx, dv_ctx))
    return (o1p, o1s, o2p, o2s), w_out.astype(BF16), caches, mod_next


def _odd_layer(x, mod, i, p, ctx, tables, prev_caches, adaln):
    w_in, sink, conv_w, conv_b, dt_bias, a_log, d_skip, norm_g, w_out = p
    k_ctx, v_ctx, h0_f, h0_b = ctx
    cos, sin = tables

    proj = _proj(x, mod[0], mod[1], _prep_odd(w_in, i))
    cb = conv_b[None]
    dtb = jnp.pad(dt_bias.reshape(1, -1), ((0, 0), (0, LANES - 2 * SSD_HEADS)))
    alog = jnp.pad(a_log.reshape(1, -1), ((0, 0), (0, LANES - 2 * SSD_HEADS)))
    q, k, v, xs, bc, dtp, *caches = _od_post(proj, cos, sin, conv_w, cb, dtb, prev_caches)
    sink_b = jnp.broadcast_to(sink[:, None], (GQA_HEADS, LANES))

    o_att_p, *mod_next = _gqa_prompt(q, k, v, sink_b, adaln=adaln)
    n_ctx = DEC_BATCH * PAST_LEN
    o_att_s = _gqa_window(q, k, v, k_ctx[:, i].reshape(n_ctx, -1).astype(BF16),
                          v_ctx[:, i].reshape(n_ctx, -1).astype(BF16), sink_b)

    y_p, st_f, st_b = _ssd(xs, bc, dtp, alog, BATCH, SEQ, 0)
    y_s, _, _ = _ssd(xs, bc, dtp, alog, DEC_BATCH, DEC_SEQ, N_PROMPT, h0=(i, h0_f, h0_b))
    dsk = jnp.repeat(d_skip, SSD_HEAD_DIM)[None]
    y_n_p = _ssd_gate(y_p, xs, proj, dsk, norm_g[None], 0)
    y_n_s = _ssd_gate(y_s, xs, proj, dsk, norm_g[None], N_PROMPT)
    return (o_att_p, o_att_s, y_n_p, y_n_s), w_out.astype(BF16), caches, (st_f, st_b), mod_next


def kernel(x_prompt, x_sample, cache_mla_ckv, cache_mla_kpe, cache_diff_k, cache_diff_v, cache_gqa_k, cache_gqa_v, state_ssd_fwd, state_ssd_bwd, c, c_ctx, w_mod, b_mod, ln1_g, ln1_b, ln2_g, ln2_b, ev_w_in, mla_q_norm, mla_kv_norm, mla_wq_b, mla_wkv_b, diff_lambda, diff_subln, ev_w_out, od_w_in, gqa_sink, ssd_conv_w, ssd_conv_b, ssd_dt_bias, ssd_a_log, ssd_d, ssd_norm, od_w_out, moe_router_group, moe_router_expert, moe_w_gate, moe_w_up, moe_w_down):
    x = jnp.concatenate([x_prompt.reshape(N_PROMPT, D_MODEL), x_sample.reshape(N_SAMPLE, D_MODEL)], axis=0)
    cs = jnp.concatenate([c_ctx[None], c, jnp.zeros((8 - 1 - DEC_BATCH, D_MODEL), F32)], axis=0)
    mod_args = (cs, w_mod, b_mod.reshape(DEPTH, 1, MOD_N))
    mod_rows = _adaln(*mod_args, 0)
    tables_64 = _rope_tables(64)
    tables_128 = _rope_tables(GQA_HEAD_DIM)

    ev_w_in = ev_w_in.astype(BF16)
    od_w_in = od_w_in.astype(BF16)
    ev_caches, od_caches, ssd_states = (), (), []
    for l in range(DEPTH):
        i = l // 2
        mods = mod_rows[:1 + DEC_BATCH].reshape(1 + DEC_BATCH, 6, 1, D_MODEL)
        mod = [mods[:, k] for k in range(6)]
        adaln = (l + 1, mod_args) if l + 1 < DEPTH else None
        if l % 2 == 0:
            p = (ev_w_in, mla_q_norm[i], mla_kv_norm[i], mla_wq_b[i], mla_wkv_b[i], diff_lambda[i],
                 diff_subln[i], ev_w_out[i])
            ctx = (cache_mla_ckv, cache_mla_kpe, cache_diff_k, cache_diff_v)
            mix, w_out, ev_caches, mod_next = _even_layer(x, mod, l, i, p, ctx, tables_64, ev_caches, adaln)
        else:
            p = (od_w_in, gqa_sink[i], ssd_conv_w[i], ssd_conv_b[i], ssd_dt_bias[i], ssd_a_log[i], ssd_d[i],
                 ssd_norm[i], od_w_out[i])
            ctx = (cache_gqa_k, cache_gqa_v, state_ssd_fwd, state_ssd_bwd)
            mix, w_out, od_caches, states, mod_next = _odd_layer(x, mod, i, p, ctx, tables_128, od_caches, adaln)
            ssd_states.append(states)
        w_router = jnp.concatenate([moe_router_group[l], moe_router_expert[l],
                                    jnp.zeros((D_MODEL, LANES - MOE_GROUPS - MOE_EXPERTS), F32)], axis=1)
        w_router_hi = w_router.astype(BF16)
        w_router = jnp.stack([w_router_hi, (w_router - w_router_hi.astype(F32)).astype(BF16)])
        x, h2, route = _out_proj(*mix, w_out, x, mod[2], ln1_g[l][None], ln1_b[l][None], mod[3], mod[4], w_router)
        z = _moe_experts(l, h2, _route_meta(route), moe_w_gate, moe_w_up, moe_w_down)
        x = _combine(x, z, route, mod[5], ln2_g[l][None], ln2_b[l][None], split=(l == DEPTH - 1))
        if mod_next:
            mod_rows = mod_next[0]

    y_prompt = x[0].reshape(BATCH, SEQ, D_MODEL)
    y_sample = x[1].reshape(DEC_BATCH, DEC_SEQ, D_MODEL)
    new_ssd_fwd = jnp.stack([s[0] for s in ssd_states], axis=1)
    new_ssd_bwd = jnp.stack([s[1] for s in ssd_states], axis=1)
    return (y_prompt, y_sample, *ev_caches, *od_caches, new_ssd_fwd, new_ssd_bwd)
```
